```python
import math
import jax, jax.numpy as jnp
from jax import lax
import numpy as np

D_MODEL = 2048
BATCH = 4
SEQ = 2048
DEPTH = 2
DEC_BATCH = 16
DEC_SEQ = 16
PAST_LEN = 1024

CHUNK = 64
N_BRANCH = 4
BRANCH = D_MODEL // N_BRANCH
EPS = 1e-6
QBLOCK = 128
RW_HEAD = 64
RW_HEADS = BRANCH // RW_HEAD
RW_LORA_W = 64
RW_LORA_A = 64
RW_IN = 3 * BRANCH + RW_LORA_W + RW_LORA_A
RW_GN_EPS = 64e-5
SSM_GROUP = 16
SSM_GROUPS = BRANCH // SSM_GROUP
SSM_STATE = 64
MLA_HEADS = 8
MLA_NOPE = 64
MLA_ROPE = 32
MLA_V = BRANCH // MLA_HEADS
MLA_Q_LORA = 384
MLA_KV_LORA = 256
MLA_IN = MLA_Q_LORA + MLA_KV_LORA + MLA_ROPE
MLA_SCALE = 1.0 / math.sqrt(MLA_NOPE + MLA_ROPE)
ROPE_BASE = 10000.0
SB_HEAD = 64
SB_HEADS = BRANCH // SB_HEAD
SB_IN = 3 * BRANCH
SB_SCALE = 1.0 / math.sqrt(SB_HEAD)
GATE_OFF = RW_IN + BRANCH + MLA_IN + SB_IN
IN_WIDTH = GATE_OFF + N_BRANCH * BRANCH + N_BRANCH * D_MODEL
SPLITS = (RW_IN, RW_IN + BRANCH, RW_IN + BRANCH + MLA_IN, GATE_OFF, GATE_OFF + N_BRANCH * BRANCH)

kernel_name = "hybrid_streaming_gated_branches_step"


def rms_norm(x, g):
    xf = x.astype(jnp.float32)
    y = xf * lax.rsqrt(jnp.mean(xf * xf, -1, keepdims=True) + EPS)
    return (y * g.astype(jnp.float32)).astype(x.dtype)


def rope(x, pos):
    half = x.shape[-1] // 2
    inv = ROPE_BASE ** (-jnp.arange(half, dtype=jnp.float32) / half)
    ang = pos.astype(jnp.float32)[:, None] * inv
    cos = jnp.cos(ang)[None, :, None, :]
    sin = jnp.sin(ang)[None, :, None, :]
    x1 = x[..., :half].astype(jnp.float32)
    x2 = x[..., half:].astype(jnp.float32)
    return jnp.concatenate([x1 * cos - x2 * sin, x1 * sin + x2 * cos], -1).astype(x.dtype)


def sweep_query_blocks(attend, qs, q_pos):
    T = q_pos.shape[0]
    if T <= QBLOCK or T % QBLOCK:
        return attend(qs, q_pos)
    nb = T // QBLOCK
    qs_b = tuple(jnp.moveaxis(q.reshape(q.shape[0], nb, QBLOCK, *q.shape[2:]), 1, 0) for q in qs)
    pos_b = q_pos.reshape(nb, QBLOCK)
    out = lax.map(lambda a: attend(a[0], a[1]), (qs_b, pos_b))
    out = jnp.moveaxis(out, 0, 1)
    return out.reshape(out.shape[0], T, *out.shape[3:])


def rwkv7_mixer(p_rw, shift_prev, wkv0, mu, w0, w2, a0, a2, k_k, k_a, r_k, lnx_g, lnx_b):
    Bn, T, _ = p_rw.shape
    f32 = jnp.float32
    prev = jnp.concatenate([shift_prev.astype(p_rw.dtype), p_rw[:, :-1]], axis=1)
    ps = (p_rw + (prev - p_rw) * mu).astype(f32)
    r, k, v, wd, ad = jnp.split(ps, [BRANCH, 2 * BRANCH, 3 * BRANCH, 3 * BRANCH + RW_LORA_W], axis=-1)
    log_w = -jax.nn.softplus(-(w0 + jnp.tanh(wd) @ w2)) - 0.5
    decay = jnp.exp(-jnp.exp(log_w))
    a = jax.nn.sigmoid(a0 + ad @ a2)
    heads = lambda t: t.reshape(Bn, T, RW_HEADS, RW_HEAD)
    kk = heads(k * k_k)
    kk = kk / jnp.maximum(jnp.sqrt(jnp.sum(kk * kk, -1, keepdims=True)), 1e-12)
    k = k * (1.0 + (a - 1.0) * k_a)
    r_h, w_h, k_h, v_h, a_h = (heads(t) for t in (r, decay, k, v, a))
    rem = -kk
    wr = kk * a_h

    def step(S, inp):
        r_t, w_t, k_t, v_t, rem_t, wr_t = inp
        S = (S * w_t[:, :, None, :]
             + jnp.einsum('bhvk,bhk->bhv', S, rem_t)[..., None] * wr_t[:, :, None, :]
             + v_t[..., None] * k_t[:, :, None, :])
        return S, jnp.einsum('bhvk,bhk->bhv', S, r_t)

    tm = lambda t: jnp.moveaxis(t, 1, 0)
    S_fin, ys = lax.scan(step, wkv0.astype(f32), tuple(tm(t) for t in (r_h, w_h, k_h, v_h, rem, wr)))
    y = jnp.moveaxis(ys, 0, 1)
    yc = y - jnp.mean(y, -1, keepdims=True)
    y = yc * lax.rsqrt(jnp.mean(yc * yc, -1, keepdims=True) + RW_GN_EPS)
    y = y.reshape(Bn, T, BRANCH) * lnx_g + lnx_b
    bonus = jnp.sum(r_h * k_h * r_k, -1, keepdims=True) * v_h
    y = y + bonus.reshape(Bn, T, BRANCH)
    return y.astype(p_rw.dtype), p_rw[:, -1:], S_fin


def s5_mixer(u, h0_re, h0_im, lam_re, lam_im, log_dt, b_re, b_im, c_re, c_im, d_skip, w_glu, b_glu):
    Bn, T, _ = u.shape
    f32 = jnp.float32
    lam_re, lam_im, b_re, b_im, c_re, c_im = (t.astype(f32) for t in (lam_re, lam_im, b_re, b_im, c_re, c_im))
    uf = u.astype(f32).reshape(Bn, T, SSM_GROUPS, SSM_GROUP)
    dt = jnp.exp(log_dt.astype(f32))[:, None]
    mag = jnp.exp(lam_re * dt)
    ang = lam_im * dt
    lb_re, lb_im = mag * jnp.cos(ang), mag * jnp.sin(ang)
    nr, ni = lb_re - 1.0, lb_im
    den = lam_re * lam_re + lam_im * lam_im
    f_re = (nr * lam_re + ni * lam_im) / den
    f_im = (ni * lam_re - nr * lam_im) / den
    bb_re = f_re[..., None] * b_re - f_im[..., None] * b_im
    bb_im = f_re[..., None] * b_im + f_im[..., None] * b_re
    x_re = jnp.einsum('btgc,gpc->btgp', uf, bb_re)
    x_im = jnp.einsum('btgc,gpc->btgp', uf, bb_im)
    a_re = jnp.broadcast_to(lb_re, x_re.shape)
    a_im = jnp.broadcast_to(lb_im, x_im.shape)

    def combine(e1, e2):
        a1r, a1i, b1r, b1i = e1
        a2r, a2i, b2r, b2i = e2
        return (a1r * a2r - a1i * a2i, a1r * a2i + a1i * a2r,
                a2r * b1r - a2i * b1i + b2r, a2r * b1i + a2i * b1r + b2i)

    Ar, Ai, Hr, Hi = lax.associative_scan(combine, (a_re, a_im, x_re, x_im), axis=1)
    h0r = h0_re.astype(f32)[:, None]
    h0i = h0_im.astype(f32)[:, None]
    hr = Ar * h0r - Ai * h0i + Hr
    hi = Ar * h0i + Ai * h0r + Hi
    y = (jnp.einsum('btgp,gcp->btgc', hr, c_re) - jnp.einsum('btgp,gcp->btgc', hi, c_im)
         + d_skip.astype(f32) * uf).reshape(Bn, T, BRANCH)
    g = jax.nn.gelu(y)
    out = g * jax.nn.sigmoid(g @ w_glu.astype(f32) + b_glu.astype(f32))
    return out.astype(u.dtype), hr[:, -1], hi[:, -1]


def mla_mixer(p_mla, pos, k_pos, ckv_past, kpe_past, q_norm_g, w_q_up, kv_norm_g, w_kv_up):
    Bn, T, _ = p_mla.shape
    f32 = jnp.float32
    q_lat, kv_lat, k_pe = jnp.split(p_mla, [MLA_Q_LORA, MLA_Q_LORA + MLA_KV_LORA], axis=-1)
    q = (rms_norm(q_lat, q_norm_g) @ w_q_up).reshape(Bn, T, MLA_HEADS, MLA_NOPE + MLA_ROPE)
    q_nope = q[..., :MLA_NOPE]
    q_pe = rope(q[..., MLA_NOPE:], pos)
    c_kv = rms_norm(kv_lat, kv_norm_g)
    k_pe = rope(k_pe[:, :, None, :], pos)[:, :, 0, :]
    if ckv_past is None:
        ckv_all, kpe_all = c_kv, k_pe
    else:
        ckv_all = jnp.concatenate([ckv_past.astype(c_kv.dtype), c_kv], axis=1)
        kpe_all = jnp.concatenate([kpe_past.astype(k_pe.dtype), k_pe], axis=1)
    S = ckv_all.shape[1]
    kv = (ckv_all @ w_kv_up).reshape(Bn, S, MLA_HEADS, MLA_NOPE + MLA_V)
    k_nope = kv[..., :MLA_NOPE].astype(f32)
    v = kv[..., MLA_NOPE:].astype(f32)
    kpe_f = kpe_all.astype(f32)
    k_chunk = k_pos // CHUNK

    def attend(qs, qp):
        qn, qr = qs
        s = (jnp.einsum('bqhd,bkhd->bhqk', qn.astype(f32), k_nope)
             + jnp.einsum('bqhr,bkr->bhqk', qr.astype(f32), kpe_f)) * MLA_SCALE
        mask = k_chunk[None, :] <= (qp // CHUNK)[:, None]
        s = jnp.where(mask, s, -1e30)
        w = jax.nn.softmax(s, axis=-1)
        return jnp.einsum('bhqk,bkhd->bqhd', w, v).astype(p_mla.dtype)

    o = sweep_query_blocks(attend, (q_nope, q_pe), pos)
    return o.reshape(Bn, T, MLA_HEADS * MLA_V), c_kv, k_pe


def sb_mixer(p_sb, pos, k_pos, k_past, v_past):
    Bn, T, _ = p_sb.shape
    f32 = jnp.float32
    q, k, v = (t.reshape(Bn, T, SB_HEADS, SB_HEAD) for t in jnp.split(p_sb, 3, axis=-1))
    if k_past is None:
        k_all, v_all = k, v
    else:
        k_all = jnp.concatenate([k_past.astype(k.dtype), k], axis=1)
        v_all = jnp.concatenate([v_past.astype(v.dtype), v], axis=1)
    kf = k_all.astype(f32)
    vf = v_all.astype(f32)

    def attend(qs, qp):
        (qb,) = qs
        z = jnp.einsum('bqhd,bkhd->bhqk', qb.astype(f32), kf) * SB_SCALE
        vis = k_pos[None, :] < qp[:, None]
        log_1m = jnp.where(vis, jax.nn.log_sigmoid(-z), 0.0)
        later = lax.cumsum(log_1m, axis=3, reverse=True) - log_1m
        A = jnp.where(vis, jnp.exp(jax.nn.log_sigmoid(z) + later), 0.0)
        return jnp.einsum('bhqk,bkhd->bqhd', A, vf).astype(p_sb.dtype)

    o = sweep_query_blocks(attend, (q,), pos)
    return o.reshape(Bn, T, BRANCH), k, v


def mixer_layer(x, pos, shift_prev, wkv0, ssm_re0, ssm_im0, ckv_past, kpe_past, sbk_past, sbv_past, p):
    Bn, T, _ = x.shape
    h = rms_norm(x, p["norm_g"])
    proj = h @ p["w_in"]
    p_rw, p_ssm, p_mla, p_sb, p_gate, p_merge = jnp.split(proj, SPLITS, axis=-1)
    if ckv_past is None:
        k_pos = pos
    else:
        k_pos = jnp.concatenate([jnp.arange(ckv_past.shape[1], dtype=jnp.int32), pos])
    y_rw, shift_new, wkv_new = rwkv7_mixer(p_rw, shift_prev, wkv0, p["rw_mu"], p["rw_w0"], p["rw_w2"],
                                           p["rw_a0"], p["rw_a2"], p["rw_k_k"], p["rw_k_a"], p["rw_r_k"],
                                           p["rw_lnx_g"], p["rw_lnx_b"])
    y_ssm, ssm_re_new, ssm_im_new = s5_mixer(p_ssm, ssm_re0, ssm_im0, p["ssm_lam_re"], p["ssm_lam_im"],
                                             p["ssm_log_dt"], p["ssm_b_re"], p["ssm_b_im"], p["ssm_c_re"],
                                             p["ssm_c_im"], p["ssm_d"], p["ssm_w_glu"], p["ssm_b_glu"])
    y_mla, ckv_new, kpe_new = mla_mixer(p_mla, pos, k_pos, ckv_past, kpe_past, p["mla_q_norm"],
                                        p["mla_w_q_up"], p["mla_kv_norm"], p["mla_w_kv_up"])
    y_sb, sbk_new, sbv_new = sb_mixer(p_sb, pos, k_pos, sbk_past, sbv_past)
    branches = jnp.stack([y_rw, y_ssm, y_mla, y_sb], axis=2)
    gated = branches * jax.nn.silu(p_gate.reshape(Bn, T, N_BRANCH, BRANCH))
    up = jnp.einsum('btnc,ncd->btnd', gated, p["w_branch"])
    merge = jax.nn.sigmoid(p_merge.reshape(Bn, T, N_BRANCH, D_MODEL) + p["b_merge"])
    out = jnp.sum(merge * up, axis=2) @ p["w_out"]
    new_state = (shift_new, wkv_new, ssm_re_new, ssm_im_new, ckv_new, kpe_new, sbk_new, sbv_new)
    return x + out.astype(x.dtype), new_state


def setup_inputs(seed: int = 0) -> dict:
    key = jax.random.key(seed)
    keys = iter(jax.random.split(key, 64))
    f32 = jnp.float32
    L = DEPTH

    def nrm(shape, scale=1.0):
        return jax.random.normal(next(keys), shape, f32) * scale

    def unif(shape, lo, hi):
        return jax.random.uniform(next(keys), shape, f32, lo, hi)

    return {
        "x_prompt": nrm((BATCH, SEQ, D_MODEL)),
        "x_sample": nrm((DEC_BATCH, DEC_SEQ, D_MODEL)),
        "state_rwkv_shift": nrm((L, DEC_BATCH, 1, RW_IN)),
        "state_rwkv_wkv": nrm((L, DEC_BATCH, RW_HEADS, RW_HEAD, RW_HEAD), 0.5),
        "state_ssm_re": nrm((L, DEC_BATCH, SSM_GROUPS, SSM_STATE), 0.5),
        "state_ssm_im": nrm((L, DEC_BATCH, SSM_GROUPS, SSM_STATE), 0.5),
        "cache_mla_ckv": nrm((L, DEC_BATCH, PAST_LEN, MLA_KV_LORA)),
        "cache_mla_kpe": nrm((L, DEC_BATCH, PAST_LEN, MLA_ROPE)),
        "cache_sb_k": nrm((L, DEC_BATCH, PAST_LEN, SB_HEADS, SB_HEAD)),
        "cache_sb_v": nrm((L, DEC_BATCH, PAST_LEN, SB_HEADS, SB_HEAD)),
        "norm_g": 1.0 + nrm((L, D_MODEL), 0.02),
        "w_in": nrm((L, D_MODEL, IN_WIDTH), D_MODEL ** -0.5),
        "rw_mu": unif((L, RW_IN), 0.0, 1.0),
        "rw_w0": unif((L, BRANCH), -6.0, 1.0),
        "rw_w2": nrm((L, RW_LORA_W, BRANCH), 0.1 * RW_LORA_W ** -0.5),
        "rw_a0": nrm((L, BRANCH), 0.1),
        "rw_a2": nrm((L, RW_LORA_A, BRANCH), 0.1 * RW_LORA_A ** -0.5),
        "rw_k_k": 0.85 + nrm((L, BRANCH), 0.02),
        "rw_k_a": 1.0 + nrm((L, BRANCH), 0.02),
        "rw_r_k": nrm((L, RW_HEADS, RW_HEAD), 0.1),
        "rw_lnx_g": 1.0 + nrm((L, BRANCH), 0.02),
        "rw_lnx_b": nrm((L, BRANCH), 0.02),
        "ssm_lam_re": -0.5 + nrm((L, SSM_GROUPS, SSM_STATE), 0.01),
        "ssm_lam_im": jnp.pi * jnp.arange(SSM_STATE, dtype=f32) + nrm((L, SSM_GROUPS, SSM_STATE), 0.01),
        "ssm_log_dt": unif((L, SSM_GROUPS), math.log(0.001), math.log(0.1)),
        "ssm_b_re": nrm((L, SSM_GROUPS, SSM_STATE, SSM_GROUP), (2 * SSM_GROUP) ** -0.5),
        "ssm_b_im": nrm((L, SSM_GROUPS, SSM_STATE, SSM_GROUP), (2 * SSM_GROUP) ** -0.5),
        "ssm_c_re": nrm((L, SSM_GROUPS, SSM_GROUP, SSM_STATE), (2 * SSM_STATE) ** -0.5),
        "ssm_c_im": nrm((L, SSM_GROUPS, SSM_GROUP, SSM_STATE), (2 * SSM_STATE) ** -0.5),
        "ssm_d": nrm((L, SSM_GROUPS, SSM_GROUP), 0.5),
        "ssm_w_glu": nrm((L, BRANCH, BRANCH), BRANCH ** -0.5),
        "ssm_b_glu": nrm((L, BRANCH), 0.02),
        "mla_q_norm": 1.0 + nrm((L, MLA_Q_LORA), 0.02),
        "mla_w_q_up": nrm((L, MLA_Q_LORA, MLA_HEADS * (MLA_NOPE + MLA_ROPE)), MLA_Q_LORA ** -0.5),
        "mla_kv_norm": 1.0 + nrm((L, MLA_KV_LORA), 0.02),
        "mla_w_kv_up": nrm((L, MLA_KV_LORA, MLA_HEADS * (MLA_NOPE + MLA_V)), MLA_KV_LORA ** -0.5),
        "w_branch": nrm((L, N_BRANCH, BRANCH, D_MODEL), BRANCH ** -0.5),
        "b_merge": nrm((L, N_BRANCH, D_MODEL), 0.1),
        "w_out": nrm((L, D_MODEL, D_MODEL), D_MODEL ** -0.5),
        "final_norm_g": 1.0 + nrm((D_MODEL,), 0.02),
    }


def reference(x_prompt, x_sample, state_rwkv_shift, state_rwkv_wkv, state_ssm_re, state_ssm_im,
              cache_mla_ckv, cache_mla_kpe, cache_sb_k, cache_sb_v,
              norm_g, w_in, rw_mu, rw_w0, rw_w2, rw_a0, rw_a2, rw_k_k, rw_k_a, rw_r_k, rw_lnx_g, rw_lnx_b,
              ssm_lam_re, ssm_lam_im, ssm_log_dt, ssm_b_re, ssm_b_im, ssm_c_re, ssm_c_im, ssm_d,
              ssm_w_glu, ssm_b_glu, mla_q_norm, mla_w_q_up, mla_kv_norm, mla_w_kv_up,
              w_branch, b_merge, w_out, final_norm_g):
    f32 = jnp.float32
    Bp = x_prompt.shape[0]
    T_p = x_prompt.shape[1]
    T_s = x_sample.shape[1]
    past = cache_mla_ckv.shape[2]
    pos_p = jnp.arange(T_p, dtype=jnp.int32)
    pos_s = past + jnp.arange(T_s, dtype=jnp.int32)
    xp, xs = x_prompt, x_sample
    new_p, new_s = [], []
    for l in range(DEPTH):
        p = {
            "norm_g": norm_g[l], "w_in": w_in[l],
            "rw_mu": rw_mu[l], "rw_w0": rw_w0[l], "rw_w2": rw_w2[l], "rw_a0": rw_a0[l], "rw_a2": rw_a2[l],
            "rw_k_k": rw_k_k[l], "rw_k_a": rw_k_a[l], "rw_r_k": rw_r_k[l],
            "rw_lnx_g": rw_lnx_g[l], "rw_lnx_b": rw_lnx_b[l],
            "ssm_lam_re": ssm_lam_re[l], "ssm_lam_im": ssm_lam_im[l], "ssm_log_dt": ssm_log_dt[l],
            "ssm_b_re": ssm_b_re[l], "ssm_b_im": ssm_b_im[l], "ssm_c_re": ssm_c_re[l], "ssm_c_im": ssm_c_im[l],
            "ssm_d": ssm_d[l], "ssm_w_glu": ssm_w_glu[l], "ssm_b_glu": ssm_b_glu[l],
            "mla_q_norm": mla_q_norm[l], "mla_w_q_up": mla_w_q_up[l],
            "mla_kv_norm": mla_kv_norm[l], "mla_w_kv_up": mla_w_kv_up[l],
            "w_branch": w_branch[l], "b_merge": b_merge[l], "w_out": w_out[l],
        }
        xp, st_p = mixer_layer(
            xp, pos_p,
            jnp.zeros((Bp, 1, RW_IN), xp.dtype),
            jnp.zeros((Bp, RW_HEADS, RW_HEAD, RW_HEAD), f32),
            jnp.zeros((Bp, SSM_GROUPS, SSM_STATE), f32),
            jnp.zeros((Bp, SSM_GROUPS, SSM_STATE), f32),
            None, None, None, None, p)
        xs, st_s = mixer_layer(
            xs, pos_s, state_rwkv_shift[l], state_rwkv_wkv[l], state_ssm_re[l], state_ssm_im[l],
            cache_mla_ckv[l], cache_mla_kpe[l], cache_sb_k[l], cache_sb_v[l], p)
        new_p.append(st_p)
        new_s.append(st_s)
    y_prompt = rms_norm(xp, final_norm_g)
    y_sample = rms_norm(xs, final_norm_g)
    stk = lambda lst, i: jnp.stack([s[i] for s in lst], axis=0)
    shift_p, wkv_p, ssm_re_p, ssm_im_p = stk(new_p, 0), stk(new_p, 1), stk(new_p, 2), stk(new_p, 3)
    ckv_p, kpe_p, sbk_p, sbv_p = stk(new_p, 4), stk(new_p, 5), stk(new_p, 6), stk(new_p, 7)
    shift_s, wkv_s, ssm_re_s, ssm_im_s = stk(new_s, 0), stk(new_s, 1), stk(new_s, 2), stk(new_s, 3)
    ckv_s, kpe_s, sbk_s, sbv_s = stk(new_s, 4), stk(new_s, 5), stk(new_s, 6), stk(new_s, 7)
    return (y_prompt, y_sample,
            shift_p, wkv_p, ssm_re_p, ssm_im_p, ckv_p, kpe_p, sbk_p, sbv_p,
            shift_s, wkv_s, ssm_re_s, ssm_im_s, ckv_s, kpe_s, sbk_s, sbv_s)
```

```python
import functools
import math

import jax
import jax.numpy as jnp
from jax import lax
from jax.experimental import pallas as pl
from jax.experimental.pallas import tpu as pltpu

F32 = jnp.float32
BF16 = jnp.bfloat16

D_MODEL = 2048
BRANCH = 512
N_BRANCH = 4
EPS = 1e-6
CHUNK = 64
HEAD = 64
HEADS = 8
RW_LORA = 64
RW_IN = 3 * BRANCH + 2 * RW_LORA
RW_GN_EPS = 64e-5
SSM_GROUP = 16
SSM_GROUPS = 32
SSM_STATE = 64
SSM_W = SSM_GROUPS * SSM_STATE
MLA_NOPE = 64
MLA_ROPE = 32
MLA_Q_LORA = 384
MLA_KV_LORA = 256
MLA_SCALE = 1.0 / math.sqrt(MLA_NOPE + MLA_ROPE)
ROPE_BASE = 10000.0
SB_SCALE = 1.0 / math.sqrt(HEAD)
LANES = 128
MIB = 1024 * 1024

O_MERGE = 0
O_GATE = 8192
O_SBQ = 10240
O_SBK = 10752
O_SBV = 11264
O_RWR = 11776
O_RWK = 12288
O_RWV = 12800
O_SSM = 13312
O_QLAT = 13824
O_LORA = 14208
O_KVLAT = 14336
O_KPE = 14592
O_KPESW = 14720
PROJ_W = 14848


def _cparams(sem, vmem_mib):
    return pltpu.CompilerParams(dimension_semantics=sem, vmem_limit_bytes=vmem_mib * MIB)


def _dot(a, b):
    return jnp.dot(a, b, preferred_element_type=F32)


def _dot_nt(a, b):
    return lax.dot_general(a, b, (((1,), (1,)), ((), ())), preferred_element_type=F32)


def _split3(x):
    h1 = x.astype(BF16)
    r1 = x - h1.astype(F32)
    h2 = r1.astype(BF16)
    h3 = (r1 - h2.astype(F32)).astype(BF16)
    return h1, h2, h3


def _softplus(x):
    return jnp.maximum(x, 0.0) + jnp.log1p(jnp.exp(-jnp.abs(x)))


def _norm_proj_kernel(x_ref, g_ref, w_ref, o_ref, h_ref):
    @pl.when(pl.program_id(1) == 0)
    def _():
        x = x_ref[...]
        ms = jnp.mean(x * x, axis=-1, keepdims=True)
        h_ref[...] = (x * lax.rsqrt(ms + EPS) * g_ref[...]).astype(BF16)

    o_ref[...] = _dot(h_ref[...], w_ref[...])


def _norm_proj(x2, g, w):
    n, d = x2.shape
    width = w.shape[1]
    tm = min(n, 1024)
    tn = 512
    return pl.pallas_call(
        _norm_proj_kernel,
        out_shape=jax.ShapeDtypeStruct((n, width), F32),
        grid=(n // tm, width // tn),
        in_specs=[
            pl.BlockSpec((tm, d), lambda i, j: (i, 0)),
            pl.BlockSpec((1, d), lambda i, j: (0, 0)),
            pl.BlockSpec((d, tn), lambda i, j: (0, j)),
        ],
        out_specs=pl.BlockSpec((tm, tn), lambda i, j: (i, j)),
        scratch_shapes=[pltpu.VMEM((tm, d), BF16)],
        compiler_params=_cparams(("parallel", "arbitrary"), 40),
        name="norm_proj",
    )(x2, g, w)


def _segsum(x, g2):
    outs = []
    for j in range(BRANCH // LANES):
        h1, h2, h3 = _split3(x[:, LANES * j:LANES * (j + 1)])
        outs.append(_dot(h1, g2) + _dot(h2, g2) + _dot(h3, g2))
    return jnp.concatenate(outs, axis=1)


def _rwkv_kernel(pr_ref, pk_ref, pv_ref, plo_ref, spr_ref, spk_ref, spv_ref, spl_ref, s0_ref,
                 pvec_ref, mul_ref, w2_ref, a2_ref, g2_ref,
                 y_ref, sout_ref,
                 st_ref, cr_ref, ck_ref, cv_ref, cl_ref, *, C):
    c = pl.program_id(1)

    @pl.when(c == 0)
    def _():
        st_ref[...] = s0_ref[0]
        cr_ref[...] = spr_ref[0]
        ck_ref[...] = spk_ref[0]
        cv_ref[...] = spv_ref[0]
        cl_ref[...] = spl_ref[0]

    pvec = pvec_ref[...]
    prow = lambda i: pvec[i:i + 1, :]
    g2 = g2_ref[...]

    def tshift(p, carry_ref, mu):
        rolled = pltpu.roll(p, 1, 0)
        rid = lax.broadcasted_iota(jnp.int32, p.shape, 0)
        prev = jnp.where(rid == 0, carry_ref[...], rolled)
        carry_ref[...] = p[C - 1:C, :]
        return p + (prev - p) * mu

    r = tshift(pr_ref[0], cr_ref, prow(0))
    k = tshift(pk_ref[0], ck_ref, prow(1))
    v = tshift(pv_ref[0], cv_ref, prow(2))
    lo = tshift(plo_ref[0], cl_ref, mul_ref[...])

    log_w = -_softplus(-(prow(3) + _dot(jnp.tanh(lo).astype(BF16), w2_ref[...]))) - 0.5
    ld = -jnp.exp(log_w)
    a_icl = jax.nn.sigmoid(prow(4) + _dot(lo.astype(BF16), a2_ref[...]))
    kkr = k * prow(5)
    kk = kkr / jnp.maximum(jnp.sqrt(_segsum(kkr * kkr, g2)), 1e-12)
    k2 = k * (1.0 + (a_icl - 1.0) * prow(6))
    av = -kk
    bv = kk * a_icl

    ri = lax.broadcasted_iota(jnp.int32, (C, C), 0)
    ci = lax.broadcasted_iota(jnp.int32, (C, C), 1)
    lincl = (ci <= ri).astype(BF16)
    h1, h2, h3 = _split3(ld)
    g = _dot(lincl, h1) + _dot(lincl, h2) + _dot(lincl, h3)
    g_last = g[C - 1:C, :]
    eg = jnp.exp(g)
    eng = jnp.exp(-g)
    at = av * jnp.exp(g - ld)
    rt = r * eg
    bt = bv * eng
    kt = k2 * eng
    e_c = jnp.exp(g_last - g)
    b_end = bv * e_c
    k_end = k2 * e_c
    eg_last = jnp.exp(g_last)

    c2 = 2 * C
    r_i = lax.broadcasted_iota(jnp.int32, (c2, c2), 0)
    c_i = lax.broadcasted_iota(jnp.int32, (c2, c2), 1)
    c_m = jnp.where(c_i >= C, c_i - C, c_i)
    mfull = c_m < jnp.where(r_i < C, r_i, r_i - C + 1)
    lane = lax.broadcasted_iota(jnp.int32, (1, LANES), 1)
    hmask = ((lane < HEAD).astype(F32), (lane >= HEAD).astype(F32))
    v_i = lax.broadcasted_iota(jnp.int32, (LANES, LANES), 0)
    k_i = lax.broadcasted_iota(jnp.int32, (LANES, LANES), 1)
    bdmask = (v_i >> 6) == (k_i >> 6)
    eye = (ri == ci).astype(F32)
    zeros_c = jnp.zeros((C, LANES), BF16)

    ys = []
    for j in range(BRANCH // LANES):
        sl = slice(LANES * j, LANES * (j + 1))
        at2, rt2, v2 = at[:, sl], rt[:, sl], v[:, sl]
        s2 = st_ref[j]
        s2h, s2l, _ = _split3(s2)
        rhs_bk = jnp.concatenate([bt[:, sl], kt[:, sl]], axis=0).astype(BF16)
        u2 = jnp.zeros((C, LANES), F32)
        y2 = jnp.zeros((C, LANES), F32)
        for i in range(2):
            m = hmask[i]
            lhs = jnp.concatenate([at2 * m, rt2 * m], axis=0).astype(BF16)
            w = jnp.where(mfull, _dot_nt(lhs, rhs_bk), 0.0)
            n_mat = w[:C, :C]
            t_mat = eye + n_mat
            n_pow = n_mat
            lv = 2
            while lv < C:
                npb = n_pow.astype(BF16)
                n_pow = _dot(npb, npb)
                t_mat = t_mat + _dot(t_mat.astype(BF16), n_pow.astype(BF16))
                lv *= 2
            a_s = _dot_nt(lhs, s2h) + _dot_nt(lhs, s2l)
            vi = (v2 * m).astype(BF16)
            wb = w.astype(BF16)
            x_mat = a_s[:C] + _dot(wb[:C], jnp.concatenate([zeros_c, vi], axis=0))
            ui = _dot(t_mat.astype(BF16), x_mat.astype(BF16))
            yi = a_s[C:] + _dot(wb[C:], jnp.concatenate([ui.astype(BF16), vi], axis=0))
            u2 = u2 + ui
            y2 = y2 + yi
        uv = jnp.concatenate([u2, v2], axis=0)
        bk = jnp.concatenate([b_end[:, sl], k_end[:, sl]], axis=0).astype(BF16)
        upd = _dot(uv.T.astype(BF16), bk)
        st_ref[j] = s2 * eg_last[:, sl] + jnp.where(bdmask, upd, 0.0)
        ys.append(y2)
    y = jnp.concatenate(ys, axis=1)

    inv_n = 1.0 / HEAD
    yc = y - _segsum(y, g2) * inv_n
    yn = yc * lax.rsqrt(_segsum(yc * yc, g2) * inv_n + RW_GN_EPS) * prow(8) + prow(9)
    bonus = _segsum(r * k2 * prow(7), g2) * v
    y_ref[0] = (yn + bonus).astype(y_ref.dtype)

    @pl.when(c == pl.num_programs(1) - 1)
    def _():
        sout_ref[0] = st_ref[...]


def _rwkv(proj3, sp_r, sp_k, sp_v, sp_l, s0, pvec, mu_l, w2p, a2p, g2, chunk):
    b, t, _ = proj3.shape
    nc = t // chunk
    cb = lambda off, wdt: off // wdt
    bspec = lambda off, wdt: pl.BlockSpec((1, chunk, wdt), lambda i, c, o=cb(off, wdt): (i, c, o))
    row3 = lambda wdt: pl.BlockSpec((1, 1, wdt), lambda i, c: (i, 0, 0))
    full2 = lambda a: pl.BlockSpec(a.shape, lambda i, c: (0, 0))
    return pl.pallas_call(
        functools.partial(_rwkv_kernel, C=chunk),
        out_shape=(jax.ShapeDtypeStruct((b, t, BRANCH), BF16),
                   jax.ShapeDtypeStruct((b, 4, LANES, LANES), F32)),
        grid=(b, nc),
        in_specs=[
            bspec(O_RWR, BRANCH), bspec(O_RWK, BRANCH), bspec(O_RWV, BRANCH), bspec(O_LORA, LANES),
            row3(BRANCH), row3(BRANCH), row3(BRANCH), row3(LANES),
            pl.BlockSpec((1, 4, LANES, LANES), lambda i, c: (i, 0, 0, 0)),
            full2(pvec), full2(mu_l), full2(w2p), full2(a2p), full2(g2),
        ],
        out_specs=(pl.BlockSpec((1, chunk, BRANCH), lambda i, c: (i, c, 0)),
                   pl.BlockSpec((1, 4, LANES, LANES), lambda i, c: (i, 0, 0, 0))),
        scratch_shapes=[pltpu.VMEM((4, LANES, LANES), F32),
                        pltpu.VMEM((1, BRANCH), F32), pltpu.VMEM((1, BRANCH), F32),
                        pltpu.VMEM((1, BRANCH), F32), pltpu.VMEM((1, LANES), F32)],
        compiler_params=_cparams(("parallel", "arbitrary"), 32),
        name="rwkv7",
    )(proj3, proj3, proj3, proj3, sp_r, sp_k, sp_v, sp_l, s0, pvec, mu_l, w2p, a2p, g2)


def _s5_kernel(u_ref, h0r_ref, h0i_ref, bblk_ref, cblk_ref, lam_ref, pw_ref, dsk_ref, wglu_ref, bglu_ref,
               y_ref, hro_ref, hio_ref, x_ref, cr_ref, ci_ref, *, tb):
    @pl.when(pl.program_id(1) == 0)
    def _():
        cr_ref[...] = h0r_ref[0]
        ci_ref[...] = h0i_ref[0]

    u = u_ref[0]
    x = _dot(u.astype(BF16), bblk_ref[...])
    xr = x[:, :SSM_W]
    xi = x[:, SSM_W:]
    rid = lax.broadcasted_iota(jnp.int32, (tb, 1), 0) & 7
    for lvl, s in enumerate((1, 2, 4)):
        ar = lam_ref[2 * lvl:2 * lvl + 1, :]
        ai = lam_ref[2 * lvl + 1:2 * lvl + 2, :]
        sr = pltpu.roll(xr, s, 0)
        si = pltpu.roll(xi, s, 0)
        valid = rid >= s
        nr = xr + jnp.where(valid, ar * sr - ai * si, 0.0)
        ni = xi + jnp.where(valid, ar * si + ai * sr, 0.0)
        xr, xi = nr, ni
    x_ref[:, :SSM_W] = xr
    x_ref[:, SSM_W:] = xi
    pwr = pw_ref[0:8, :]
    pwi = pw_ref[8:16, :]

    def body(gi, carry):
        cr, ci = carry
        o = pl.multiple_of(gi * 8, 8)
        br = x_ref[pl.ds(o, 8), 0:SSM_W]
        bi = x_ref[pl.ds(o, 8), SSM_W:2 * SSM_W]
        br = br + pwr * cr - pwi * ci
        bi = bi + pwr * ci + pwi * cr
        x_ref[pl.ds(o, 8), 0:SSM_W] = br
        x_ref[pl.ds(o, 8), SSM_W:2 * SSM_W] = bi
        return br[7:8, :], bi[7:8, :]

    cr, ci = lax.fori_loop(0, tb // 8, body, (cr_ref[...], ci_ref[...]))
    cr_ref[...] = cr
    ci_ref[...] = ci
    hro_ref[0] = cr
    hio_ref[0] = ci
    y = _dot(x_ref[...].astype(BF16), cblk_ref[...]) + dsk_ref[...] * u
    g = jax.nn.gelu(y)
    out = g * jax.nn.sigmoid(_dot(g.astype(BF16), wglu_ref[...]) + bglu_ref[...])
    y_ref[0] = out.astype(y_ref.dtype)


def _s5(proj3, h0r, h0i, bblk, cblk, lam, pw, dsk, wglu, bglu):
    b, t, _ = proj3.shape
    tb = min(t, 128)
    full2 = lambda a: pl.BlockSpec(a.shape, lambda i, c: (0, 0))
    row3 = pl.BlockSpec((1, 1, SSM_W), lambda i, c: (i, 0, 0))
    return pl.pallas_call(
        functools.partial(_s5_kernel, tb=tb),
        out_shape=(jax.ShapeDtypeStruct((b, t, BRANCH), BF16),
                   jax.ShapeDtypeStruct((b, 1, SSM_W), F32),
                   jax.ShapeDtypeStruct((b, 1, SSM_W), F32)),
        grid=(b, t // tb),
        in_specs=[pl.BlockSpec((1, tb, BRANCH), lambda i, c: (i, c, O_SSM // BRANCH)),
                  row3, row3, full2(bblk), full2(cblk), full2(lam), full2(pw), full2(dsk),
                  full2(wglu), full2(bglu)],
        out_specs=(pl.BlockSpec((1, tb, BRANCH), lambda i, c: (i, c, 0)), row3, row3),
        scratch_shapes=[pltpu.VMEM((tb, 2 * SSM_W), F32),
                        pltpu.VMEM((1, SSM_W), F32), pltpu.VMEM((1, SSM_W), F32)],
        compiler_params=_cparams(("parallel", "arbitrary"), 48),
        name="s5",
    )(proj3, h0r, h0i, bblk, cblk, lam, pw, dsk, wglu, bglu)


def _mla_prep_kernel(ql_ref, kvl_ref, kpe_ref, kpesw_ref, tq_ref, tc_ref, ts_ref,
                     gq_ref, gkv_ref, wq_ref, q_ref, ckv_ref, kr_ref):
    ql = ql_ref[...]
    qn = ql * lax.rsqrt(jnp.mean(ql * ql, axis=-1, keepdims=True) + EPS) * gq_ref[...]
    q = _dot(qn.astype(BF16), wq_ref[...])
    tq = tq_ref[...]
    q_ref[...] = jnp.concatenate(
        [q[:, LANES * h:LANES * (h + 1)] * tq for h in range(HEADS)], axis=1).astype(BF16)
    kvl = kvl_ref[...]
    ckv_ref[...] = kvl * lax.rsqrt(jnp.mean(kvl * kvl, axis=-1, keepdims=True) + EPS) * gkv_ref[...]
    kr = kpe_ref[...] * tc_ref[...] + kpesw_ref[...] * ts_ref[...]
    kr_ref[...] = kr[:, :MLA_ROPE]


def _mla_prep(proj, t, tq_tab, tc_tab, ts_tab, gq, gkv, wq):
    n = proj.shape[0]
    tm = min(t, 512)
    nt = t // tm
    tab = pl.BlockSpec((tm, LANES), lambda i: (i % nt, 0))
    full2 = lambda a: pl.BlockSpec(a.shape, lambda i: (0, 0))
    return pl.pallas_call(
        _mla_prep_kernel,
        out_shape=(jax.ShapeDtypeStruct((n, HEADS * LANES), BF16),
                   jax.ShapeDtypeStruct((n, MLA_KV_LORA), F32),
                   jax.ShapeDtypeStruct((n, MLA_ROPE), F32)),
        grid=(n // tm,),
        in_specs=[pl.BlockSpec((tm, MLA_Q_LORA), lambda i: (i, O_QLAT // MLA_Q_LORA)),
                  pl.BlockSpec((tm, MLA_KV_LORA), lambda i: (i, O_KVLAT // MLA_KV_LORA)),
                  pl.BlockSpec((tm, LANES), lambda i: (i, O_KPE // LANES)),
                  pl.BlockSpec((tm, LANES), lambda i: (i, O_KPESW // LANES)),
                  tab, tab, tab, full2(gq), full2(gkv), full2(wq)],
        out_specs=(pl.BlockSpec((tm, HEADS * LANES), lambda i: (i, 0)),
                   pl.BlockSpec((tm, MLA_KV_LORA), lambda i: (i, 0)),
                   pl.BlockSpec((tm, MLA_ROPE), lambda i: (i, 0))),
        compiler_params=_cparams(("parallel",), 32),
        name="mla_prep",
    )(proj, proj, proj, proj, tq_tab, tc_tab, ts_tab, gq, gkv, wq)


def _mla_kv_kernel(ckv_ref, kr_ref, wk_ref, wv_ref, dup_ref, k_ref, v_ref):
    cb = ckv_ref[...].astype(BF16)
    kn = _dot(cb, wk_ref[...])
    krd = _dot(kr_ref[...].astype(BF16), dup_ref[...])
    k_ref[...] = jnp.concatenate(
        [kn[:, LANES * h:LANES * (h + 1)] + krd for h in range(HEADS)], axis=1).astype(BF16)
    v_ref[...] = _dot(cb, wv_ref[...]).astype(BF16)


def _mla_kv(ckv, kr, wk, wv, dup):
    n = ckv.shape[0]
    tm = 512 if n % 512 == 0 else 128
    full2 = lambda a: pl.BlockSpec(a.shape, lambda i: (0, 0))
    return pl.pallas_call(
        _mla_kv_kernel,
        out_shape=(jax.ShapeDtypeStruct((n, HEADS * LANES), BF16),
                   jax.ShapeDtypeStruct((n, BRANCH), BF16)),
        grid=(n // tm,),
        in_specs=[pl.BlockSpec((tm, MLA_KV_LORA), lambda i: (i, 0)),
                  pl.BlockSpec((tm, MLA_ROPE), lambda i: (i, 0)),
                  full2(wk), full2(wv), full2(dup)],
        out_specs=(pl.BlockSpec((tm, HEADS * LANES), lambda i: (i, 0)),
                   pl.BlockSpec((tm, BRANCH), lambda i: (i, 0))),
        compiler_params=_cparams(("parallel",), 32),
        name="mla_kv",
    )(ckv, kr, wk, wv, dup)


def _mla_attn_kernel(q_ref, k_ref, v_ref, o_ref, *, tq, tk, q_off, s_valid):
    i = pl.program_id(2)
    q_lo = q_off + i * tq
    limit = jnp.minimum(q_lo + tq, s_valid)
    nkb = (limit + tk - 1) // tk
    qidx = q_lo + lax.broadcasted_iota(jnp.int32, (tq, tk), 0)
    kloc = lax.broadcasted_iota(jnp.int32, (tq, tk), 1)
    q2 = q_ref[0]

    def body(kb, carry):
        o = pl.multiple_of(kb * tk, tk)
        kidx = kloc + kb * tk
        vis = jnp.logical_and((kidx >> 6) <= (qidx >> 6), kidx < s_valid)
        kblk = k_ref[0, pl.ds(o, tk), :]
        vblk = v_ref[0, pl.ds(o, tk), :]
        new = []
        for h in range(2):
            m_prev, l_prev, acc = carry[h]
            s = _dot_nt(q2[:, LANES * h:LANES * (h + 1)], kblk[:, LANES * h:LANES * (h + 1)])
            s = jnp.where(vis, s, -1e30)
            m_new = jnp.maximum(m_prev, jnp.max(s, axis=-1, keepdims=True))
            alpha = jnp.exp(m_prev - m_new)
            p = jnp.exp(s - m_new)
            l_new = alpha * l_prev + jnp.sum(p, axis=-1, keepdims=True)
            acc = alpha * acc + _dot(p.astype(BF16), vblk)
            new.append((m_new, l_new, acc))
        return tuple(new)

    init = tuple((jnp.full((tq, 1), -1e30, F32), jnp.zeros((tq, 1), F32), jnp.zeros((tq, LANES), F32))
                 for _ in range(2))
    (m0, l0, a0), (m1, l1, a1) = lax.fori_loop(0, nkb, body, init)
    lane = lax.broadcasted_iota(jnp.int32, (tq, LANES), 1)
    o_ref[0] = jnp.where(lane < HEAD, a0 / l0, a1 / l1).astype(o_ref.dtype)


def _mla_attn(q3, k3, v3, q_off, s_valid, tq, tk):
    b, t, _ = q3.shape
    s_pad = k3.shape[1]
    return pl.pallas_call(
        functools.partial(_mla_attn_kernel, tq=tq, tk=tk, q_off=q_off, s_valid=s_valid),
        out_shape=jax.ShapeDtypeStruct((b, t, BRANCH), BF16),
        grid=(b, 4, t // tq),
        in_specs=[pl.BlockSpec((1, tq, 2 * LANES), lambda bi, j, i: (bi, i, j)),
                  pl.BlockSpec((1, s_pad, 2 * LANES), lambda bi, j, i: (bi, 0, j)),
                  pl.BlockSpec((1, s_pad, LANES), lambda bi, j, i: (bi, 0, j))],
        out_specs=pl.BlockSpec((1, tq, LANES), lambda bi, j, i: (bi, i, j)),
        compiler_params=_cparams(("parallel", "parallel", "arbitrary"), 32),
        name="mla_attn",
    )(q3, k3, v3)


def _sb_attn_kernel(q_ref, k_ref, v_ref, us_ref, o_ref, *, tq, tk, q_off):
    i = pl.program_id(2)
    q_lo = q_off + i * tq
    nkb = (q_lo + tq - 2) // tk + 1
    qidx = q_lo + lax.broadcasted_iota(jnp.int32, (tq, tk), 0)
    kloc = lax.broadcasted_iota(jnp.int32, (tq, tk), 1)
    lane1 = lax.broadcasted_iota(jnp.int32, (1, LANES), 1)
    q2 = q_ref[0]
    qh = ((q2 * (lane1 < HEAD).astype(F32)).astype(BF16), (q2 * (lane1 >= HEAD).astype(F32)).astype(BF16))
    us = us_ref[...]

    def body(step, carry):
        kb = nkb - 1 - step
        o = pl.multiple_of(kb * tk, tk)
        vis = (kloc + kb * tk) < qidx
        kblk = k_ref[0, pl.ds(o, tk), :].astype(BF16)
        vblk = v_ref[0, pl.ds(o, tk), :].astype(BF16)
        new = []
        for h in range(2):
            csum, acc = carry[h]
            z = _dot_nt(qh[h], kblk) * SB_SCALE
            lg = jnp.where(vis, -_softplus(z), 0.0)
            hi = lg.astype(BF16)
            lo = (lg - hi.astype(F32)).astype(BF16)
            later = _dot(hi, us) + _dot(lo, us) + csum
            a = jnp.where(vis, jnp.exp(z + lg + later), 0.0)
            acc = acc + _dot(a.astype(BF16), vblk)
            csum = csum + jnp.sum(lg, axis=-1, keepdims=True)
            new.append((csum, acc))
        return tuple(new)

    init = tuple((jnp.zeros((tq, 1), F32), jnp.zeros((tq, LANES), F32)) for _ in range(2))
    (_, a0), (_, a1) = lax.fori_loop(0, nkb, body, init)
    lane = lax.broadcasted_iota(jnp.int32, (tq, LANES), 1)
    o_ref[0] = jnp.where(lane < HEAD, a0, a1).astype(o_ref.dtype)


def _sb_attn(q3, qcol, k3, kcol, v3, vcol, us, q_off, tq, tk):
    b, t, _ = q3.shape
    s_pad = k3.shape[1]
    return pl.pallas_call(
        functools.partial(_sb_attn_kernel, tq=tq, tk=tk, q_off=q_off),
        out_shape=jax.ShapeDtypeStruct((b, t, BRANCH), BF16),
        grid=(b, 4, t // tq),
        in_specs=[pl.BlockSpec((1, tq, LANES), lambda bi, j, i: (bi, i, qcol + j)),
                  pl.BlockSpec((1, s_pad, LANES), lambda bi, j, i: (bi, 0, kcol + j)),
                  pl.BlockSpec((1, s_pad, LANES), lambda bi, j, i: (bi, 0, vcol + j)),
                  pl.BlockSpec(us.shape, lambda bi, j, i: (0, 0))],
        out_specs=pl.BlockSpec((1, tq, LANES), lambda bi, j, i: (bi, i, j)),
        compiler_params=_cparams(("parallel", "parallel", "arbitrary"), 32),
        name="sb_attn",
    )(q3, k3, v3, us)


def _merge_kernel(yrw_ref, yssm_ref, ymla_ref, ysb_ref, gate_ref, pm_ref, x_ref, wb_ref, bm_ref, wo_ref,
                  fg_ref, o_ref, acc_ref, *, final):
    n = pl.program_id(1)

    @pl.when(n == 0)
    def _():
        acc_ref[...] = jnp.zeros_like(acc_ref)

    def branch(y_ref):
        g = gate_ref[...]
        gated = y_ref[...].astype(F32) * (g * jax.nn.sigmoid(g))
        up = _dot(gated.astype(BF16), wb_ref[0])
        acc_ref[...] += jax.nn.sigmoid(pm_ref[...] + bm_ref[0]) * up

    for idx, y_ref in enumerate((yrw_ref, yssm_ref, ymla_ref, ysb_ref)):
        pl.when(n == idx)(functools.partial(branch, y_ref))

    @pl.when(n == N_BRANCH - 1)
    def _():
        xn = x_ref[...] + _dot(acc_ref[...].astype(BF16), wo_ref[...])
        if final:
            xn = xn * lax.rsqrt(jnp.mean(xn * xn, axis=-1, keepdims=True) + EPS) * fg_ref[...]
        o_ref[...] = xn


def _merge(ys, proj, x2, wb, bm, wo, fg, final):
    n, d = x2.shape
    tm = min(n, 512)
    ysp = pl.BlockSpec((tm, BRANCH), lambda i, k: (i, 0))
    return pl.pallas_call(
        functools.partial(_merge_kernel, final=final),
        out_shape=jax.ShapeDtypeStruct((n, d), F32),
        grid=(n // tm, N_BRANCH),
        in_specs=[ysp, ysp, ysp, ysp,
                  pl.BlockSpec((tm, BRANCH), lambda i, k: (i, O_GATE // BRANCH + k)),
                  pl.BlockSpec((tm, d), lambda i, k: (i, O_MERGE // d + k)),
                  pl.BlockSpec((tm, d), lambda i, k: (i, 0)),
                  pl.BlockSpec((1, BRANCH, d), lambda i, k: (k, 0, 0)),
                  pl.BlockSpec((1, 1, d), lambda i, k: (k, 0, 0)),
                  pl.BlockSpec((d, d), lambda i, k: (0, 0)),
                  pl.BlockSpec((1, d), lambda i, k: (0, 0))],
        out_specs=pl.BlockSpec((tm, d), lambda i, k: (i, 0)),
        scratch_shapes=[pltpu.VMEM((tm, d), F32)],
        compiler_params=_cparams(("parallel", "arbitrary"), 56),
        name="merge_out",
    )(*ys, proj, proj, x2, wb, bm, wo, fg)


def _permute_w_in(w_in):
    o_ssm = RW_IN
    o_mla = o_ssm + BRANCH
    o_sb = o_mla + MLA_Q_LORA + MLA_KV_LORA + MLA_ROPE
    o_gate = o_sb + 3 * BRANCH
    o_merge = o_gate + N_BRANCH * BRANCH
    kpe = o_mla + MLA_Q_LORA + MLA_KV_LORA
    half = MLA_ROPE // 2
    sl = lambda a, b: w_in[:, :, a:b]
    zeros = lambda n: jnp.zeros(w_in.shape[:2] + (n,), w_in.dtype)
    parts = [
        sl(o_merge, o_merge + N_BRANCH * D_MODEL),
        sl(o_gate, o_merge),
        sl(o_sb, o_gate),
        sl(0, 3 * BRANCH),
        sl(o_ssm, o_mla),
        sl(o_mla, o_mla + MLA_Q_LORA),
        sl(3 * BRANCH, RW_IN),
        sl(o_mla + MLA_Q_LORA, kpe),
        sl(kpe, kpe + MLA_ROPE), zeros(LANES - MLA_ROPE),
        -sl(kpe + half, kpe + MLA_ROPE), sl(kpe, kpe + half), zeros(LANES - MLA_ROPE),
    ]
    return jnp.concatenate(parts, axis=-1).astype(BF16)


def _rope_tables(pos):
    half = MLA_ROPE // 2
    inv = ROPE_BASE ** (-jnp.arange(half, dtype=F32) / half)
    ang = pos.astype(F32)[:, None] * inv
    cos, sin = jnp.cos(ang), jnp.sin(ang)
    t = pos.shape[0]
    cc = jnp.concatenate([cos, cos], axis=1)
    ss = jnp.concatenate([sin, sin], axis=1)
    tq = jnp.concatenate([jnp.ones((t, MLA_NOPE), F32), cc, ss], axis=1) * MLA_SCALE
    pad = jnp.zeros((t, LANES - MLA_ROPE), F32)
    return tq, jnp.concatenate([cc, pad], axis=1), jnp.concatenate([ss, pad], axis=1)


def _mla_weights(w_q_up, w_kv_up):
    half = MLA_ROPE // 2
    wq = w_q_up.reshape(MLA_Q_LORA, HEADS, MLA_NOPE + MLA_ROPE)
    x1 = wq[:, :, MLA_NOPE:MLA_NOPE + half]
    x2 = wq[:, :, MLA_NOPE + half:]
    wq_p = jnp.concatenate([wq, -x2, x1], axis=2).reshape(MLA_Q_LORA, HEADS * LANES).astype(BF16)
    wkv = w_kv_up.reshape(MLA_KV_LORA, HEADS, 2 * HEAD)
    wk_p = jnp.concatenate([wkv[:, :, :HEAD], jnp.zeros((MLA_KV_LORA, HEADS, HEAD), F32)], axis=2)
    wk_p = wk_p.reshape(MLA_KV_LORA, HEADS * LANES).astype(BF16)
    wv_p = wkv[:, :, HEAD:].reshape(MLA_KV_LORA, BRANCH).astype(BF16)
    return wq_p, wk_p, wv_p


def _s5_tables(lam_re, lam_im, log_dt, b_re, b_im, c_re, c_im):
    dt = jnp.exp(log_dt)[:, None]
    mag = jnp.exp(lam_re * dt)
    ang = lam_im * dt
    lb_re, lb_im = mag * jnp.cos(ang), mag * jnp.sin(ang)
    nr, ni = lb_re - 1.0, lb_im
    den = lam_re * lam_re + lam_im * lam_im
    f_re = (nr * lam_re + ni * lam_im) / den
    f_im = (ni * lam_re - nr * lam_im) / den
    bb_re = f_re[..., None] * b_re - f_im[..., None] * b_im
    bb_im = f_re[..., None] * b_im + f_im[..., None] * b_re
    eye = jnp.eye(SSM_GROUPS, dtype=F32)
    blk_in = lambda m: jnp.einsum('gpc,gh->gchp', m, eye).reshape(BRANCH, SSM_W)
    blk_out = lambda m: jnp.einsum('gcp,gh->gphc', m, eye).reshape(SSM_W, BRANCH)
    bblk = jnp.concatenate([blk_in(bb_re), blk_in(bb_im)], axis=1).astype(BF16)
    cblk = jnp.concatenate([blk_out(c_re), blk_out(-c_im)], axis=0).astype(BF16)

    def power(j):
        m = jnp.exp(lam_re * dt * j)
        return (m * jnp.cos(ang * j)).reshape(1, SSM_W), (m * jnp.sin(ang * j)).reshape(1, SSM_W)

    lam = jnp.concatenate([p for j in (1, 2, 4) for p in power(j)] + [jnp.zeros((2, SSM_W), F32)], axis=0)
    pws = [power(j) for j in range(1, 9)]
    pw = jnp.concatenate([p[0] for p in pws] + [p[1] for p in pws], axis=0)
    return bblk, cblk, lam, pw


def _pair_states(s):
    b = s.shape[0]
    s = s.reshape(b, 4, 2, HEAD, HEAD)
    return jnp.einsum('bjivk,ih->bjivhk', s, jnp.eye(2, dtype=s.dtype)).reshape(b, 4, LANES, LANES)


def _unpair_states(s):
    b = s.shape[0]
    s = s.reshape(b, 4, 2, HEAD, 2, HEAD)
    return jnp.stack([s[:, :, 0, :, 0, :], s[:, :, 1, :, 1, :]], axis=2).reshape(b, HEADS, HEAD, HEAD)


def _pad_rows(a, rows):
    return jnp.pad(a, ((0, 0), (0, rows - a.shape[1]), (0, 0)))


def _layer(x, pos0, st, lw, final_g, final):
    b, t, d = x.shape
    n = b * t
    x2 = x.reshape(n, d)
    proj = _norm_proj(x2, lw["norm_g"], lw["w_in"])
    proj3 = proj.reshape(b, t, PROJ_W)
    past = 0 if st["ckv"] is None else st["ckv"].shape[1]

    chunk = min(t, CHUNK)
    y_rw, s_new = _rwkv(proj3, st["sp_r"], st["sp_k"], st["sp_v"], st["sp_l"], _pair_states(st["wkv"]),
                        lw["rw_pvec"], lw["rw_mu_l"], lw["rw_w2p"], lw["rw_a2p"], lw["g2"], chunk)
    shift_new = jnp.concatenate([proj3[:, t - 1:, O_RWR:O_RWR + 3 * BRANCH],
                                 proj3[:, t - 1:, O_LORA:O_LORA + 2 * RW_LORA]], axis=-1)
    wkv_new = _unpair_states(s_new)

    y_ssm, hr, hi = _s5(proj3, st["ssm_re"], st["ssm_im"], lw["ssm_bblk"], lw["ssm_cblk"], lw["ssm_lam"],
                        lw["ssm_pw"], lw["ssm_d"], lw["ssm_wglu"], lw["ssm_bglu"])
    ssm_re_new = hr.reshape(b, SSM_GROUPS, SSM_STATE)
    ssm_im_new = hi.reshape(b, SSM_GROUPS, SSM_STATE)

    pos = pos0 + jnp.arange(t, dtype=jnp.int32)
    tq_tab, tc_tab, ts_tab = _rope_tables(pos)
    q, ckv, kr = _mla_prep(proj, t, tq_tab, tc_tab, ts_tab, lw["mla_gq"], lw["mla_gkv"], lw["mla_wq"])
    ckv3 = ckv.reshape(b, t, MLA_KV_LORA)
    kr3 = kr.reshape(b, t, MLA_ROPE)
    s_valid = past + t
    if past:
        s_pad = -(-s_valid // LANES) * LANES
        ckv_all = _pad_rows(jnp.concatenate([st["ckv"], ckv3], axis=1), s_pad)
        kr_all = _pad_rows(jnp.concatenate([st["kpe"], kr3], axis=1), s_pad)
    else:
        s_pad, ckv_all, kr_all = t, ckv3, kr3
    kc, vv = _mla_kv(ckv_all.reshape(b * s_pad, MLA_KV_LORA), kr_all.reshape(b * s_pad, MLA_ROPE),
                     lw["mla_wk"], lw["mla_wv"], lw["dup"])
    tq = min(t, 256)
    tk = 512 if s_pad % 512 == 0 else LANES
    y_mla = _mla_attn(q.reshape(b, t, HEADS * LANES), kc.reshape(b, s_pad, HEADS * LANES),
                      vv.reshape(b, s_pad, BRANCH), past, s_valid, tq, tk)

    sbk = proj3[:, :, O_SBK:O_SBK + BRANCH]
    sbv = proj3[:, :, O_SBV:O_SBV + BRANCH]
    if past:
        k_all = _pad_rows(jnp.concatenate([st["sbk"], sbk], axis=1), s_pad)
        v_all = _pad_rows(jnp.concatenate([st["sbv"], sbv], axis=1), s_pad)
        y_sb = _sb_attn(proj3, O_SBQ // LANES, k_all, 0, v_all, 0, lw["us"], past, tq, LANES)
    else:
        y_sb = _sb_attn(proj3, O_SBQ // LANES, proj3, O_SBK // LANES, proj3, O_SBV // LANES, lw["us"], 0, tq,
                        LANES)

    ys = [y.reshape(n, BRANCH) for y in (y_rw, y_ssm, y_mla, y_sb)]
    x_new = _merge(ys, proj, x2, lw["w_branch"], lw["b_merge"], lw["w_out"], final_g, final).reshape(b, t, d)
    new_state = (shift_new, wkv_new, ssm_re_new, ssm_im_new, ckv3, kr3,
                 sbk.reshape(b, t, HEADS, HEAD), sbv.reshape(b, t, HEADS, HEAD))
    return x_new, new_state


def kernel(x_prompt, x_sample, state_rwkv_shift, state_rwkv_wkv, state_ssm_re, state_ssm_im, cache_mla_ckv, cache_mla_kpe, cache_sb_k, cache_sb_v, norm_g, w_in, rw_mu, rw_w0, rw_w2, rw_a0, rw_a2, rw_k_k, rw_k_a, rw_r_k, rw_lnx_g, rw_lnx_b, ssm_lam_re, ssm_lam_im, ssm_log_dt, ssm_b_re, ssm_b_im, ssm_c_re, ssm_c_im, ssm_d, ssm_w_glu, ssm_b_glu, mla_q_norm, mla_w_q_up, mla_kv_norm, mla_w_kv_up, w_branch, b_merge, w_out, final_norm_g):
    depth = w_in.shape[0]
    bp, tp, _ = x_prompt.shape
    bs = x_sample.shape[0]
    past = cache_mla_ckv.shape[2]

    w_in_p = _permute_w_in(w_in)
    w_branch_b = w_branch.astype(BF16)
    w_out_b = w_out.astype(BF16)
    lane_i = jnp.arange(LANES)
    g2 = ((lane_i[:, None] // HEAD) == (lane_i[None, :] // HEAD)).astype(BF16)
    us = (lane_i[:, None] > lane_i[None, :]).astype(BF16)
    rope_i = jnp.arange(MLA_ROPE)
    dup = ((lane_i[None, :] == rope_i[:, None] + MLA_NOPE)
           | (lane_i[None, :] == rope_i[:, None] + MLA_NOPE + MLA_ROPE)).astype(BF16)
    final_g = final_norm_g.reshape(1, D_MODEL)
    zpad = jnp.zeros((RW_LORA, BRANCH), F32)

    layers = []
    for l in range(depth):
        wq_p, wk_p, wv_p = _mla_weights(mla_w_q_up[l], mla_w_kv_up[l])
        bblk, cblk, lam, pw = _s5_tables(ssm_lam_re[l], ssm_lam_im[l], ssm_log_dt[l], ssm_b_re[l], ssm_b_im[l],
                                         ssm_c_re[l], ssm_c_im[l])
        mu = rw_mu[l]
        rows = [mu[:BRANCH], mu[BRANCH:2 * BRANCH], mu[2 * BRANCH:3 * BRANCH], rw_w0[l], rw_a0[l], rw_k_k[l],
                rw_k_a[l], rw_r_k[l].reshape(BRANCH), rw_lnx_g[l], rw_lnx_b[l]]
        pvec = jnp.concatenate([jnp.stack(rows), jnp.zeros((16 - len(rows), BRANCH), F32)], axis=0)
        layers.append(dict(
            norm_g=norm_g[l].reshape(1, D_MODEL), w_in=w_in_p[l],
            rw_pvec=pvec, rw_mu_l=mu[3 * BRANCH:].reshape(1, 2 * RW_LORA),
            rw_w2p=jnp.concatenate([rw_w2[l], zpad], axis=0).astype(BF16),
            rw_a2p=jnp.concatenate([zpad, rw_a2[l]], axis=0).astype(BF16),
            g2=g2, us=us, dup=dup,
            ssm_bblk=bblk, ssm_cblk=cblk, ssm_lam=lam, ssm_pw=pw,
            ssm_d=ssm_d[l].reshape(1, BRANCH), ssm_wglu=ssm_w_glu[l].astype(BF16),
            ssm_bglu=ssm_b_glu[l].reshape(1, BRANCH),
            mla_gq=mla_q_norm[l].reshape(1, MLA_Q_LORA), mla_gkv=mla_kv_norm[l].reshape(1, MLA_KV_LORA),
            mla_wq=wq_p, mla_wk=wk_p, mla_wv=wv_p,
            w_branch=w_branch_b[l], b_merge=b_merge[l].reshape(N_BRANCH, 1, D_MODEL), w_out=w_out_b[l]))

    def fresh(bn):
        return dict(sp_r=jnp.zeros((bn, 1, BRANCH), F32), sp_k=jnp.zeros((bn, 1, BRANCH), F32),
                    sp_v=jnp.zeros((bn, 1, BRANCH), F32), sp_l=jnp.zeros((bn, 1, 2 * RW_LORA), F32),
                    wkv=jnp.zeros((bn, HEADS, HEAD, HEAD), F32),
                    ssm_re=jnp.zeros((bn, 1, SSM_W), F32), ssm_im=jnp.zeros((bn, 1, SSM_W), F32),
                    ckv=None, kpe=None, sbk=None, sbv=None)

    def carried(l):
        sh = state_rwkv_shift[l]
        return dict(sp_r=sh[:, :, :BRANCH], sp_k=sh[:, :, BRANCH:2 * BRANCH], sp_v=sh[:, :, 2 * BRANCH:3 * BRANCH],
                    sp_l=sh[:, :, 3 * BRANCH:], wkv=state_rwkv_wkv[l],
                    ssm_re=state_ssm_re[l].reshape(bs, 1, SSM_W), ssm_im=state_ssm_im[l].reshape(bs, 1, SSM_W),
                    ckv=cache_mla_ckv[l], kpe=cache_mla_kpe[l],
                    sbk=cache_sb_k[l].reshape(bs, past, BRANCH), sbv=cache_sb_v[l].reshape(bs, past, BRANCH))

    xp, xs = x_prompt, x_sample
    new_p, new_s = [], []
    for l in range(depth):
        last = l == depth - 1
        xp, st_p = _layer(xp, 0, fresh(bp), layers[l], final_g, last)
        xs, st_s = _layer(xs, past, carried(l), layers[l], final_g, last)
        new_p.append(st_p)
        new_s.append(st_s)
    stk = lambda lst, i: jnp.stack([s[i] for s in lst], axis=0)
    return (xp, xs) + tuple(stk(new_p, i) for i in range(8)) + tuple(stk(new_s, i) for i in range(8))
```

```python
import functools
import math

import jax
import jax.numpy as jnp
from jax import lax
from jax.experimental import pallas as pl
from jax.experimental.pallas import tpu as pltpu

F32 = jnp.float32
BF16 = jnp.bfloat16

D_MODEL = 2048
BRANCH = 512
N_BRANCH = 4
EPS = 1e-6
CHUNK = 64
HEAD = 64
HEADS = 8
RW_LORA = 64
RW_IN = 3 * BRANCH + 2 * RW_LORA
RW_GN_EPS = 64e-5
SSM_GROUP = 16
SSM_GROUPS = 32
SSM_STATE = 64
SSM_W = SSM_GROUPS * SSM_STATE
MLA_NOPE = 64
MLA_ROPE = 32
MLA_Q_LORA = 384
MLA_KV_LORA = 256
MLA_SCALE = 1.0 / math.sqrt(MLA_NOPE + MLA_ROPE)
ROPE_BASE = 10000.0
SB_SCALE = 1.0 / math.sqrt(HEAD)
LANES = 128
MIB = 1024 * 1024

O_MERGE = 0
O_GATE = 8192
O_SBQ = 10240
O_SBK = 10752
O_SBV = 11264
O_RWR = 11776
O_RWK = 12288
O_RWV = 12800
O_SSM = 13312
O_QLAT = 13824
O_LORA = 14208
O_KVLAT = 14336
O_KPE = 14592
O_KPESW = 14720
PROJ_W = 14848


def _cparams(sem, vmem_mib):
    return pltpu.CompilerParams(dimension_semantics=sem, vmem_limit_bytes=vmem_mib * MIB)


def _dot(a, b):
    return jnp.dot(a, b, preferred_element_type=F32)


def _dot_nt(a, b):
    return lax.dot_general(a, b, (((1,), (1,)), ((), ())), preferred_element_type=F32)


def _split3(x):
    h1 = x.astype(BF16)
    r1 = x - h1.astype(F32)
    h2 = r1.astype(BF16)
    h3 = (r1 - h2.astype(F32)).astype(BF16)
    return h1, h2, h3


def _softplus(x):
    return jnp.maximum(x, 0.0) + jnp.log1p(jnp.exp(-jnp.abs(x)))


def _norm_proj_kernel(x_ref, g_ref, w_ref, o_ref, h_ref):
    @pl.when(pl.program_id(1) == 0)
    def _():
        x = x_ref[...]
        ms = jnp.mean(x * x, axis=-1, keepdims=True)
        h_ref[...] = (x * lax.rsqrt(ms + EPS) * g_ref[...]).astype(BF16)

    o_ref[...] = _dot(h_ref[...], w_ref[...])


def _norm_proj(x2, g, w):
    n, d = x2.shape
    width = w.shape[1]
    tm = min(n, 1024)
    tn = 512
    return pl.pallas_call(
        _norm_proj_kernel,
        out_shape=jax.ShapeDtypeStruct((n, width), F32),
        grid=(n // tm, width // tn),
        in_specs=[
            pl.BlockSpec((tm, d), lambda i, j: (i, 0)),
            pl.BlockSpec((1, d), lambda i, j: (0, 0)),
            pl.BlockSpec((d, tn), lambda i, j: (0, j)),
        ],
        out_specs=pl.BlockSpec((tm, tn), lambda i, j: (i, j)),
        scratch_shapes=[pltpu.VMEM((tm, d), BF16)],
        compiler_params=_cparams(("parallel", "arbitrary"), 40),
        name="norm_proj",
    )(x2, g, w)


def _segsum(x, g2):
    outs = []
    for j in range(BRANCH // LANES):
        h1, h2, h3 = _split3(x[:, LANES * j:LANES * (j + 1)])
        outs.append(_dot(h1, g2) + _dot(h2, g2) + _dot(h3, g2))
    return jnp.concatenate(outs, axis=1)


def _rwkv_kernel(pr_ref, pk_ref, pv_ref, plo_ref, spr_ref, spk_ref, spv_ref, spl_ref, s0_ref,
                 pvec_ref, mul_ref, w2_ref, a2_ref, g2_ref,
                 y_ref, sout_ref,
                 st_ref, cr_ref, ck_ref, cv_ref, cl_ref, *, C):
    c = pl.program_id(1)

    @pl.when(c == 0)
    def _():
        st_ref[...] = s0_ref[0]
        cr_ref[...] = spr_ref[0]
        ck_ref[...] = spk_ref[0]
        cv_ref[...] = spv_ref[0]
        cl_ref[...] = spl_ref[0]

    pvec = pvec_ref[...]
    prow = lambda i: pvec[i:i + 1, :]
    g2 = g2_ref[...]

    def tshift(p, carry_ref, mu):
        rolled = pltpu.roll(p, 1, 0)
        rid = lax.broadcasted_iota(jnp.int32, p.shape, 0)
        prev = jnp.where(rid == 0, carry_ref[...], rolled)
        carry_ref[...] = p[C - 1:C, :]
        return p + (prev - p) * mu

    r = tshift(pr_ref[0], cr_ref, prow(0))
    k = tshift(pk_ref[0], ck_ref, prow(1))
    v = tshift(pv_ref[0], cv_ref, prow(2))
    lo = tshift(plo_ref[0], cl_ref, mul_ref[...])

    log_w = -_softplus(-(prow(3) + _dot(jnp.tanh(lo).astype(BF16), w2_ref[...]))) - 0.5
    ld = -jnp.exp(log_w)
    a_icl = jax.nn.sigmoid(prow(4) + _dot(lo.astype(BF16), a2_ref[...]))
    kkr = k * prow(5)
    kk = kkr / jnp.maximum(jnp.sqrt(_segsum(kkr * kkr, g2)), 1e-12)
    k2 = k * (1.0 + (a_icl - 1.0) * prow(6))
    av = -kk
    bv = kk * a_icl

    ri = lax.broadcasted_iota(jnp.int32, (C, C), 0)
    ci = lax.broadcasted_iota(jnp.int32, (C, C), 1)
    lincl = (ci <= ri).astype(BF16)
    h1, h2, h3 = _split3(ld)
    g = _dot(lincl, h1) + _dot(lincl, h2) + _dot(lincl, h3)
    g_last = g[C - 1:C, :]
    eg = jnp.exp(g)
    eng = jnp.exp(-g)
    at = av * jnp.exp(g - ld)
    rt = r * eg
    bt = bv * eng
    kt = k2 * eng
    e_c = jnp.exp(g_last - g)
    b_end = bv * e_c
    k_end = k2 * e_c
    eg_last = jnp.exp(g_last)

    c2 = 2 * C
    npair = BRANCH // LANES
    sh = C.bit_length() - 1
    r_i = lax.broadcasted_iota(jnp.int32, (c2, c2), 0)
    c_i = lax.broadcasted_iota(jnp.int32, (c2, c2), 1)
    c_m = jnp.where(c_i >= C, c_i - C, c_i)
    mhalf = c_m < jnp.where(r_i < C, r_i, r_i - C + 1)
    mfull = jnp.concatenate([mhalf, mhalf], axis=0)
    bd2 = (r_i >> sh) == (c_i >> sh)
    eye2 = (r_i == c_i).astype(F32)
    lane = lax.broadcasted_iota(jnp.int32, (1, LANES), 1)
    m0 = (lane < HEAD).astype(F32)
    m1 = (lane >= HEAD).astype(F32)
    v_i = lax.broadcasted_iota(jnp.int32, (LANES, LANES), 0)
    k_i = lax.broadcasted_iota(jnp.int32, (LANES, LANES), 1)
    bdmask = (v_i >> 6) == (k_i >> 6)
    zf = jnp.zeros((C, LANES), F32)
    zc = jnp.zeros((C, LANES), BF16)
    sls = [slice(LANES * j, LANES * (j + 1)) for j in range(npair)]

    ws, a_ss, s2s = [], [], []
    for sl, j in zip(sls, range(npair)):
        a0, r0, a1, r1 = at[:, sl] * m0, rt[:, sl] * m0, at[:, sl] * m1, rt[:, sl] * m1
        lhs_w = jnp.concatenate([jnp.concatenate([a0, zf], axis=1), jnp.concatenate([r0, zf], axis=1),
                                 jnp.concatenate([zf, a1], axis=1), jnp.concatenate([zf, r1], axis=1)],
                                axis=0).astype(BF16)
        btj, ktj = bt[:, sl], kt[:, sl]
        rhs_w = jnp.concatenate([jnp.concatenate([btj, ktj], axis=0),
                                 jnp.concatenate([ktj, btj], axis=0)], axis=1).astype(BF16)
        ws.append(jnp.where(mfull, _dot_nt(lhs_w, rhs_w), 0.0))
        lhs_s = jnp.concatenate([a0, r0, a1, r1], axis=0).astype(BF16)
        s2 = st_ref[j]
        s2h, s2l, _ = _split3(s2)
        a_ss.append(_dot_nt(lhs_s, s2h) + _dot_nt(lhs_s, s2l))
        s2s.append(s2)

    n_pows = [jnp.where(bd2, jnp.concatenate([w[:C], w[c2:c2 + C]], axis=0), 0.0) for w in ws]
    t_mats = [eye2 + n for n in n_pows]
    lv = 2
    while lv < C:
        n_pows = [_dot(n.astype(BF16), n.astype(BF16)) for n in n_pows]
        t_mats = [t + _dot(t.astype(BF16), n.astype(BF16)) for t, n in zip(t_mats, n_pows)]
        lv *= 2

    wbs = [w.astype(BF16) for w in ws]
    v0s = [(v[:, sl] * m0).astype(BF16) for sl in sls]
    v1s = [(v[:, sl] * m1).astype(BF16) for sl in sls]
    xs = [jnp.concatenate([a_s[:C] + _dot(wb[:C], jnp.concatenate([zc, v0], axis=0)),
                           a_s[c2:c2 + C] + _dot(wb[c2:c2 + C], jnp.concatenate([v1, zc], axis=0))], axis=0)
          for a_s, wb, v0, v1 in zip(a_ss, wbs, v0s, v1s)]
    us = [_dot(t.astype(BF16), x.astype(BF16)) for t, x in zip(t_mats, xs)]
    ys = []
    for j, sl in enumerate(sls):
        a_s, wb, u = a_ss[j], wbs[j], us[j]
        u0, u1 = u[:C], u[C:]
        y0 = a_s[C:c2] + _dot(wb[C:c2], jnp.concatenate([u0.astype(BF16), v0s[j]], axis=0))
        y1 = a_s[c2 + C:] + _dot(wb[c2 + C:], jnp.concatenate([v1s[j], u1.astype(BF16)], axis=0))
        ys.append(y0 + y1)
        uv = jnp.concatenate([u0 + u1, v[:, sl]], axis=0)
        bk = jnp.concatenate([b_end[:, sl], k_end[:, sl]], axis=0).astype(BF16)
        upd = _dot(uv.T.astype(BF16), bk)
        st_ref[j] = s2s[j] * eg_last[:, sl] + jnp.where(bdmask, upd, 0.0)
    y = jnp.concatenate(ys, axis=1)

    inv_n = 1.0 / HEAD
    yc = y - _segsum(y, g2) * inv_n
    yn = yc * lax.rsqrt(_segsum(yc * yc, g2) * inv_n + RW_GN_EPS) * prow(8) + prow(9)
    bonus = _segsum(r * k2 * prow(7), g2) * v
    y_ref[0] = (yn + bonus).astype(y_ref.dtype)

    @pl.when(c == pl.num_programs(1) - 1)
    def _():
        sout_ref[0] = st_ref[...]


def _rwkv(proj3, sp_r, sp_k, sp_v, sp_l, s0, pvec, mu_l, w2p, a2p, g2, chunk):
    b, t, _ = proj3.shape
    nc = t // chunk
    cb = lambda off, wdt: off // wdt
    bspec = lambda off, wdt: pl.BlockSpec((1, chunk, wdt), lambda i, c, o=cb(off, wdt): (i, c, o))
    row3 = lambda wdt: pl.BlockSpec((1, 1, wdt), lambda i, c: (i, 0, 0))
    full2 = lambda a: pl.BlockSpec(a.shape, lambda i, c: (0, 0))
    return pl.pallas_call(
        functools.partial(_rwkv_kernel, C=chunk),
        out_shape=(jax.ShapeDtypeStruct((b, t, BRANCH), BF16),
                   jax.ShapeDtypeStruct((b, 4, LANES, LANES), F32)),
        grid=(b, nc),
        in_specs=[
            bspec(O_RWR, BRANCH), bspec(O_RWK, BRANCH), bspec(O_RWV, BRANCH), bspec(O_LORA, LANES),
            row3(BRANCH), row3(BRANCH), row3(BRANCH), row3(LANES),
            pl.BlockSpec((1, 4, LANES, LANES), lambda i, c: (i, 0, 0, 0)),
            full2(pvec), full2(mu_l), full2(w2p), full2(a2p), full2(g2),
        ],
        out_specs=(pl.BlockSpec((1, chunk, BRANCH), lambda i, c: (i, c, 0)),
                   pl.BlockSpec((1, 4, LANES, LANES), lambda i, c: (i, 0, 0, 0))),
        scratch_shapes=[pltpu.VMEM((4, LANES, LANES), F32),
                        pltpu.VMEM((1, BRANCH), F32), pltpu.VMEM((1, BRANCH), F32),
                        pltpu.VMEM((1, BRANCH), F32), pltpu.VMEM((1, LANES), F32)],
        compiler_params=_cparams(("parallel", "arbitrary"), 32),
        name="rwkv7",
    )(proj3, proj3, proj3, proj3, sp_r, sp_k, sp_v, sp_l, s0, pvec, mu_l, w2p, a2p, g2)


def _s5_kernel(u_ref, h0r_ref, h0i_ref, bblk_ref, cblk_ref, lam_ref, pw_ref, dsk_ref, wglu_ref, bglu_ref,
               y_ref, hro_ref, hio_ref, x_ref, cr_ref, ci_ref, *, tb):
    @pl.when(pl.program_id(1) == 0)
    def _():
        cr_ref[...] = h0r_ref[0]
        ci_ref[...] = h0i_ref[0]

    u = u_ref[0]
    x = _dot(u.astype(BF16), bblk_ref[...])
    xr = x[:, :SSM_W]
    xi = x[:, SSM_W:]
    rid = lax.broadcasted_iota(jnp.int32, (tb, 1), 0) & 7
    for lvl, s in enumerate((1, 2, 4)):
        ar = lam_ref[2 * lvl:2 * lvl + 1, :]
        ai = lam_ref[2 * lvl + 1:2 * lvl + 2, :]
        sr = pltpu.roll(xr, s, 0)
        si = pltpu.roll(xi, s, 0)
        valid = rid >= s
        nr = xr + jnp.where(valid, ar * sr - ai * si, 0.0)
        ni = xi + jnp.where(valid, ar * si + ai * sr, 0.0)
        xr, xi = nr, ni
    x_ref[:, :SSM_W] = xr
    x_ref[:, SSM_W:] = xi
    pwr = pw_ref[0:8, :]
    pwi = pw_ref[8:16, :]

    def body(gi, carry):
        cr, ci = carry
        o = pl.multiple_of(gi * 8, 8)
        br = x_ref[pl.ds(o, 8), 0:SSM_W]
        bi = x_ref[pl.ds(o, 8), SSM_W:2 * SSM_W]
        br = br + pwr * cr - pwi * ci
        bi = bi + pwr * ci + pwi * cr
        x_ref[pl.ds(o, 8), 0:SSM_W] = br
        x_ref[pl.ds(o, 8), SSM_W:2 * SSM_W] = bi
        return br[7:8, :], bi[7:8, :]

    cr, ci = lax.fori_loop(0, tb // 8, body, (cr_ref[...], ci_ref[...]))
    cr_ref[...] = cr
    ci_ref[...] = ci
    hro_ref[0] = cr
    hio_ref[0] = ci
    y = _dot(x_ref[...].astype(BF16), cblk_ref[...]) + dsk_ref[...] * u
    g = jax.nn.gelu(y)
    out = g * jax.nn.sigmoid(_dot(g.astype(BF16), wglu_ref[...]) + bglu_ref[...])
    y_ref[0] = out.astype(y_ref.dtype)


def _s5(proj3, h0r, h0i, bblk, cblk, lam, pw, dsk, wglu, bglu):
    b, t, _ = proj3.shape
    tb = min(t, 128)
    full2 = lambda a: pl.BlockSpec(a.shape, lambda i, c: (0, 0))
    row3 = pl.BlockSpec((1, 1, SSM_W), lambda i, c: (i, 0, 0))
    return pl.pallas_call(
        functools.partial(_s5_kernel, tb=tb),
        out_shape=(jax.ShapeDtypeStruct((b, t, BRANCH), BF16),
                   jax.ShapeDtypeStruct((b, 1, SSM_W), F32),
                   jax.ShapeDtypeStruct((b, 1, SSM_W), F32)),
        grid=(b, t // tb),
        in_specs=[pl.BlockSpec((1, tb, BRANCH), lambda i, c: (i, c, O_SSM // BRANCH)),
                  row3, row3, full2(bblk), full2(cblk), full2(lam), full2(pw), full2(dsk),
                  full2(wglu), full2(bglu)],
        out_specs=(pl.BlockSpec((1, tb, BRANCH), lambda i, c: (i, c, 0)), row3, row3),
        scratch_shapes=[pltpu.VMEM((tb, 2 * SSM_W), F32),
                        pltpu.VMEM((1, SSM_W), F32), pltpu.VMEM((1, SSM_W), F32)],
        compiler_params=_cparams(("parallel", "arbitrary"), 48),
        name="s5",
    )(proj3, h0r, h0i, bblk, cblk, lam, pw, dsk, wglu, bglu)


def _mla_prep_kernel(ql_ref, kvl_ref, kpe_ref, kpesw_ref, tq_ref, tc_ref, ts_ref,
                     gq_ref, gkv_ref, wq_ref, q_ref, ckv_ref, kr_ref):
    ql = ql_ref[...]
    qn = ql * lax.rsqrt(jnp.mean(ql * ql, axis=-1, keepdims=True) + EPS) * gq_ref[...]
    q = _dot(qn.astype(BF16), wq_ref[...])
    tq = tq_ref[...]
    q_ref[...] = jnp.concatenate(
        [q[:, LANES * h:LANES * (h + 1)] * tq for h in range(HEADS)], axis=1).astype(BF16)
    kvl = kvl_ref[...]
    ckv_ref[...] = kvl * lax.rsqrt(jnp.mean(kvl * kvl, axis=-1, keepdims=True) + EPS) * gkv_ref[...]
    kr = kpe_ref[...] * tc_ref[...] + kpesw_ref[...] * ts_ref[...]
    kr_ref[...] = kr[:, :MLA_ROPE]


def _mla_prep(proj, t, tq_tab, tc_tab, ts_tab, gq, gkv, wq):
    n = proj.shape[0]
    tm = min(t, 512)
    nt = t // tm
    tab = pl.BlockSpec((tm, LANES), lambda i: (i % nt, 0))
    full2 = lambda a: pl.BlockSpec(a.shape, lambda i: (0, 0))
    return pl.pallas_call(
        _mla_prep_kernel,
        out_shape=(jax.ShapeDtypeStruct((n, HEADS * LANES), BF16),
                   jax.ShapeDtypeStruct((n, MLA_KV_LORA), F32),
                   jax.ShapeDtypeStruct((n, MLA_ROPE), F32)),
        grid=(n // tm,),
        in_specs=[pl.BlockSpec((tm, MLA_Q_LORA), lambda i: (i, O_QLAT // MLA_Q_LORA)),
                  pl.BlockSpec((tm, MLA_KV_LORA), lambda i: (i, O_KVLAT // MLA_KV_LORA)),
                  pl.BlockSpec((tm, LANES), lambda i: (i, O_KPE // LANES)),
                  pl.BlockSpec((tm, LANES), lambda i: (i, O_KPESW // LANES)),
                  tab, tab, tab, full2(gq), full2(gkv), full2(wq)],
        out_specs=(pl.BlockSpec((tm, HEADS * LANES), lambda i: (i, 0)),
                   pl.BlockSpec((tm, MLA_KV_LORA), lambda i: (i, 0)),
                   pl.BlockSpec((tm, MLA_ROPE), lambda i: (i, 0))),
        compiler_params=_cparams(("parallel",), 32),
        name="mla_prep",
    )(proj, proj, proj, proj, tq_tab, tc_tab, ts_tab, gq, gkv, wq)


def _mla_kv_kernel(ckv_ref, kr_ref, wk_ref, wv_ref, dup_ref, k_ref, v_ref):
    cb = ckv_ref[...].astype(BF16)
    kn = _dot(cb, wk_ref[...])
    krd = _dot(kr_ref[...].astype(BF16), dup_ref[...])
    k_ref[...] = jnp.concatenate(
        [kn[:, LANES * h:LANES * (h + 1)] + krd for h in range(HEADS)], axis=1).astype(BF16)
    v_ref[...] = _dot(cb, wv_ref[...]).astype(BF16)


def _mla_kv(ckv, kr, wk, wv, dup):
    n = ckv.shape[0]
    tm = 512 if n % 512 == 0 else 128
    full2 = lambda a: pl.BlockSpec(a.shape, lambda i: (0, 0))
    return pl.pallas_call(
        _mla_kv_kernel,
        out_shape=(jax.ShapeDtypeStruct((n, HEADS * LANES), BF16),
                   jax.ShapeDtypeStruct((n, BRANCH), BF16)),
        grid=(n // tm,),
        in_specs=[pl.BlockSpec((tm, MLA_KV_LORA), lambda i: (i, 0)),
                  pl.BlockSpec((tm, MLA_ROPE), lambda i: (i, 0)),
                  full2(wk), full2(wv), full2(dup)],
        out_specs=(pl.BlockSpec((tm, HEADS * LANES), lambda i: (i, 0)),
                   pl.BlockSpec((tm, BRANCH), lambda i: (i, 0))),
        compiler_params=_cparams(("parallel",), 32),
        name="mla_kv",
    )(ckv, kr, wk, wv, dup)


def _mla_attn_kernel(q_ref, k_ref, v_ref, o_ref, *, tq, tk, q_off, s_valid, hp):
    i = pl.program_id(2)
    q_lo = q_off + i * tq
    limit = jnp.minimum(q_lo + tq, s_valid)
    nkb = (limit + tk - 1) // tk
    qidx = q_lo + lax.broadcasted_iota(jnp.int32, (tq, tk), 0)
    kloc = lax.broadcasted_iota(jnp.int32, (tq, tk), 1)
    nh = 2 * hp
    q2 = q_ref[0]
    qs = [q2[:, LANES * h:LANES * (h + 1)] for h in range(nh)]

    def body(kb, carry):
        o = pl.multiple_of(kb * tk, tk)
        kidx = kloc + kb * tk
        vis = jnp.logical_and((kidx >> 6) <= (qidx >> 6), kidx < s_valid)
        kblk = k_ref[0, pl.ds(o, tk), :]
        vblk = v_ref[0, pl.ds(o, tk), :]
        ss = [jnp.where(vis, _dot_nt(qs[h], kblk[:, LANES * h:LANES * (h + 1)]), -1e30) for h in range(nh)]
        m_new = [jnp.maximum(carry[h][0], jnp.max(ss[h], axis=-1, keepdims=True)) for h in range(nh)]
        ps = [jnp.exp(ss[h] - m_new[h]) for h in range(nh)]
        pv = [_dot(ps[h].astype(BF16), vblk[:, LANES * (h // 2):LANES * (h // 2 + 1)]) for h in range(nh)]
        new = []
        for h in range(nh):
            m_prev, l_prev, acc = carry[h]
            alpha = jnp.exp(m_prev - m_new[h])
            new.append((m_new[h], alpha * l_prev + jnp.sum(ps[h], axis=-1, keepdims=True), alpha * acc + pv[h]))
        return tuple(new)

    init = tuple((jnp.full((tq, 1), -1e30, F32), jnp.zeros((tq, 1), F32), jnp.zeros((tq, LANES), F32))
                 for _ in range(nh))
    res = lax.fori_loop(0, nkb, body, init)
    lane = lax.broadcasted_iota(jnp.int32, (tq, LANES), 1)
    outs = [jnp.where(lane < HEAD, res[2 * p][2] / res[2 * p][1], res[2 * p + 1][2] / res[2 * p + 1][1])
            for p in range(hp)]
    o_ref[0] = jnp.concatenate(outs, axis=1).astype(o_ref.dtype)


def _mla_attn(q3, k3, v3, q_off, s_valid, tq, tk, hp):
    b, t, _ = q3.shape
    s_pad = k3.shape[1]
    return pl.pallas_call(
        functools.partial(_mla_attn_kernel, tq=tq, tk=tk, q_off=q_off, s_valid=s_valid, hp=hp),
        out_shape=jax.ShapeDtypeStruct((b, t, BRANCH), BF16),
        grid=(b, 4 // hp, t // tq),
        in_specs=[pl.BlockSpec((1, tq, 2 * LANES * hp), lambda bi, j, i: (bi, i, j)),
                  pl.BlockSpec((1, s_pad, 2 * LANES * hp), lambda bi, j, i: (bi, 0, j)),
                  pl.BlockSpec((1, s_pad, LANES * hp), lambda bi, j, i: (bi, 0, j))],
        out_specs=pl.BlockSpec((1, tq, LANES * hp), lambda bi, j, i: (bi, i, j)),
        compiler_params=_cparams(("parallel", "parallel", "arbitrary"), 40),
        name="mla_attn",
    )(q3, k3, v3)


def _sb_attn_kernel(q_ref, k_ref, v_ref, us_ref, o_ref, *, tq, tk, q_off, hp):
    i = pl.program_id(2)
    q_lo = q_off + i * tq
    nkb = (q_lo + tq - 2) // tk + 1
    qidx = q_lo + lax.broadcasted_iota(jnp.int32, (tq, tk), 0)
    kloc = lax.broadcasted_iota(jnp.int32, (tq, tk), 1)
    lane1 = lax.broadcasted_iota(jnp.int32, (1, LANES), 1)
    hm = ((lane1 < HEAD).astype(F32), (lane1 >= HEAD).astype(F32))
    nh = 2 * hp
    q2 = q_ref[0]
    qh = [(q2[:, LANES * (h // 2):LANES * (h // 2 + 1)] * hm[h % 2]).astype(BF16) for h in range(nh)]
    us = us_ref[...]

    def body(step, carry):
        kb = nkb - 1 - step
        o = pl.multiple_of(kb * tk, tk)
        vis = (kloc + kb * tk) < qidx
        kblk = k_ref[0, pl.ds(o, tk), :].astype(BF16)
        vblk = v_ref[0, pl.ds(o, tk), :].astype(BF16)
        zs = [_dot_nt(qh[h], kblk[:, LANES * (h // 2):LANES * (h // 2 + 1)]) * SB_SCALE for h in range(nh)]
        lgs = [jnp.where(vis, -_softplus(z), 0.0) for z in zs]
        his = [lg.astype(BF16) for lg in lgs]
        los = [(lg - hi.astype(F32)).astype(BF16) for lg, hi in zip(lgs, his)]
        later = _dot(jnp.concatenate(his + los, axis=0), us)
        new = []
        for h in range(nh):
            csum, acc = carry[h]
            tot = later[h * tq:(h + 1) * tq] + later[(nh + h) * tq:(nh + h + 1) * tq] + csum
            a = jnp.where(vis, jnp.exp(zs[h] + lgs[h] + tot), 0.0)
            acc = acc + _dot(a.astype(BF16), vblk[:, LANES * (h // 2):LANES * (h // 2 + 1)])
            new.append((csum + jnp.sum(lgs[h], axis=-1, keepdims=True), acc))
        return tuple(new)

    init = tuple((jnp.zeros((tq, 1), F32), jnp.zeros((tq, LANES), F32)) for _ in range(nh))
    res = lax.fori_loop(0, nkb, body, init)
    lane = lax.broadcasted_iota(jnp.int32, (tq, LANES), 1)
    outs = [jnp.where(lane < HEAD, res[2 * p][1], res[2 * p + 1][1]) for p in range(hp)]
    o_ref[0] = jnp.concatenate(outs, axis=1).astype(o_ref.dtype)


def _sb_attn(q3, qcol, k3, kcol, v3, vcol, us, q_off, tq, tk, hp):
    b, t, _ = q3.shape
    s_pad = k3.shape[1]
    wdt = LANES * hp
    return pl.pallas_call(
        functools.partial(_sb_attn_kernel, tq=tq, tk=tk, q_off=q_off, hp=hp),
        out_shape=jax.ShapeDtypeStruct((b, t, BRANCH), BF16),
        grid=(b, 4 // hp, t // tq),
        in_specs=[pl.BlockSpec((1, tq, wdt), lambda bi, j, i: (bi, i, qcol // hp + j)),
                  pl.BlockSpec((1, s_pad, wdt), lambda bi, j, i: (bi, 0, kcol // hp + j)),
                  pl.BlockSpec((1, s_pad, wdt), lambda bi, j, i: (bi, 0, vcol // hp + j)),
                  pl.BlockSpec(us.shape, lambda bi, j, i: (0, 0))],
        out_specs=pl.BlockSpec((1, tq, wdt), lambda bi, j, i: (bi, i, j)),
        compiler_params=_cparams(("parallel", "parallel", "arbitrary"), 48),
        name="sb_attn",
    )(q3, k3, v3, us)


def _merge_kernel(yrw_ref, yssm_ref, ymla_ref, ysb_ref, gate_ref, pm_ref, x_ref, wb_ref, bm_ref, wo_ref,
                  fg_ref, o_ref, acc_ref, *, final):
    n = pl.program_id(1)

    @pl.when(n == 0)
    def _():
        acc_ref[...] = jnp.zeros_like(acc_ref)

    def branch(y_ref):
        g = gate_ref[...]
        gated = y_ref[...].astype(F32) * (g * jax.nn.sigmoid(g))
        up = _dot(gated.astype(BF16), wb_ref[0])
        acc_ref[...] += jax.nn.sigmoid(pm_ref[...] + bm_ref[0]) * up

    for idx, y_ref in enumerate((yrw_ref, yssm_ref, ymla_ref, ysb_ref)):
        pl.when(n == idx)(functools.partial(branch, y_ref))

    @pl.when(n == N_BRANCH - 1)
    def _():
        xn = x_ref[...] + _dot(acc_ref[...].astype(BF16), wo_ref[...])
        if final:
            xn = xn * lax.rsqrt(jnp.mean(xn * xn, axis=-1, keepdims=True) + EPS) * fg_ref[...]
        o_ref[...] = xn


def _merge(ys, proj, x2, wb, bm, wo, fg, final):
    n, d = x2.shape
    tm = min(n, 512)
    ysp = pl.BlockSpec((tm, BRANCH), lambda i, k: (i, 0))
    return pl.pallas_call(
        functools.partial(_merge_kernel, final=final),
        out_shape=jax.ShapeDtypeStruct((n, d), F32),
        grid=(n // tm, N_BRANCH),
        in_specs=[ysp, ysp, ysp, ysp,
                  pl.BlockSpec((tm, BRANCH), lambda i, k: (i, O_GATE // BRANCH + k)),
                  pl.BlockSpec((tm, d), lambda i, k: (i, O_MERGE // d + k)),
                  pl.BlockSpec((tm, d), lambda i, k: (i, 0)),
                  pl.BlockSpec((1, BRANCH, d), lambda i, k: (k, 0, 0)),
                  pl.BlockSpec((1, 1, d), lambda i, k: (k, 0, 0)),
                  pl.BlockSpec((d, d), lambda i, k: (0, 0)),
                  pl.BlockSpec((1, d), lambda i, k: (0, 0))],
        out_specs=pl.BlockSpec((tm, d), lambda i, k: (i, 0)),
        scratch_shapes=[pltpu.VMEM((tm, d), F32)],
        compiler_params=_cparams(("parallel", "arbitrary"), 56),
        name="merge_out",
    )(*ys, proj, proj, x2, wb, bm, wo, fg)


def _permute_w_in(w_in):
    o_ssm = RW_IN
    o_mla = o_ssm + BRANCH
    o_sb = o_mla + MLA_Q_LORA + MLA_KV_LORA + MLA_ROPE
    o_gate = o_sb + 3 * BRANCH
    o_merge = o_gate + N_BRANCH * BRANCH
    kpe = o_mla + MLA_Q_LORA + MLA_KV_LORA
    half = MLA_ROPE // 2
    sl = lambda a, b: w_in[:, :, a:b]
    zeros = lambda n: jnp.zeros(w_in.shape[:2] + (n,), w_in.dtype)
    parts = [
        sl(o_merge, o_merge + N_BRANCH * D_MODEL),
        sl(o_gate, o_merge),
        sl(o_sb, o_gate),
        sl(0, 3 * BRANCH),
        sl(o_ssm, o_mla),
        sl(o_mla, o_mla + MLA_Q_LORA),
        sl(3 * BRANCH, RW_IN),
        sl(o_mla + MLA_Q_LORA, kpe),
        sl(kpe, kpe + MLA_ROPE), zeros(LANES - MLA_ROPE),
        -sl(kpe + half, kpe + MLA_ROPE), sl(kpe, kpe + half), zeros(LANES - MLA_ROPE),
    ]
    return jnp.concatenate(parts, axis=-1).astype(BF16)


def _rope_tables(pos):
    half = MLA_ROPE // 2
    inv = ROPE_BASE ** (-jnp.arange(half, dtype=F32) / half)
    ang = pos.astype(F32)[:, None] * inv
    cos, sin = jnp.cos(ang), jnp.sin(ang)
    t = pos.shape[0]
    cc = jnp.concatenate([cos, cos], axis=1)
    ss = jnp.concatenate([sin, sin], axis=1)
    tq = jnp.concatenate([jnp.ones((t, MLA_NOPE), F32), cc, ss], axis=1) * MLA_SCALE
    pad = jnp.zeros((t, LANES - MLA_ROPE), F32)
    return tq, jnp.concatenate([cc, pad], axis=1), jnp.concatenate([ss, pad], axis=1)


def _mla_weights(w_q_up, w_kv_up):
    half = MLA_ROPE // 2
    wq = w_q_up.reshape(MLA_Q_LORA, HEADS, MLA_NOPE + MLA_ROPE)
    x1 = wq[:, :, MLA_NOPE:MLA_NOPE + half]
    x2 = wq[:, :, MLA_NOPE + half:]
    wq_p = jnp.concatenate([wq, -x2, x1], axis=2).reshape(MLA_Q_LORA, HEADS * LANES).astype(BF16)
    wkv = w_kv_up.reshape(MLA_KV_LORA, HEADS, 2 * HEAD)
    wk_p = jnp.concatenate([wkv[:, :, :HEAD], jnp.zeros((MLA_KV_LORA, HEADS, HEAD), F32)], axis=2)
    wk_p = wk_p.reshape(MLA_KV_LORA, HEADS * LANES).astype(BF16)
    wv_p = wkv[:, :, HEAD:].reshape(MLA_KV_LORA, BRANCH).astype(BF16)
    return wq_p, wk_p, wv_p


def _s5_tables(lam_re, lam_im, log_dt, b_re, b_im, c_re, c_im):
    dt = jnp.exp(log_dt)[:, None]
    mag = jnp.exp(lam_re * dt)
    ang = lam_im * dt
    lb_re, lb_im = mag * jnp.cos(ang), mag * jnp.sin(ang)
    nr, ni = lb_re - 1.0, lb_im
    den = lam_re * lam_re + lam_im * lam_im
    f_re = (nr * lam_re + ni * lam_im) / den
    f_im = (ni * lam_re - nr * lam_im) / den
    bb_re = f_re[..., None] * b_re - f_im[..., None] * b_im
    bb_im = f_re[..., None] * b_im + f_im[..., None] * b_re
    eye = jnp.eye(SSM_GROUPS, dtype=F32)
    blk_in = lambda m: jnp.einsum('gpc,gh->gchp', m, eye).reshape(BRANCH, SSM_W)
    blk_out = lambda m: jnp.einsum('gcp,gh->gphc', m, eye).reshape(SSM_W, BRANCH)
    bblk = jnp.concatenate([blk_in(bb_re), blk_in(bb_im)], axis=1).astype(BF16)
    cblk = jnp.concatenate([blk_out(c_re), blk_out(-c_im)], axis=0).astype(BF16)

    def power(j):
        m = jnp.exp(lam_re * dt * j)
        return (m * jnp.cos(ang * j)).reshape(1, SSM_W), (m * jnp.sin(ang * j)).reshape(1, SSM_W)

    lam = jnp.concatenate([p for j in (1, 2, 4) for p in power(j)] + [jnp.zeros((2, SSM_W), F32)], axis=0)
    pws = [power(j) for j in range(1, 9)]
    pw = jnp.concatenate([p[0] for p in pws] + [p[1] for p in pws], axis=0)
    return bblk, cblk, lam, pw


def _pair_states(s):
    b = s.shape[0]
    s = s.reshape(b, 4, 2, HEAD, HEAD)
    return jnp.einsum('bjivk,ih->bjivhk', s, jnp.eye(2, dtype=s.dtype)).reshape(b, 4, LANES, LANES)


def _unpair_states(s):
    b = s.shape[0]
    s = s.reshape(b, 4, 2, HEAD, 2, HEAD)
    return jnp.stack([s[:, :, 0, :, 0, :], s[:, :, 1, :, 1, :]], axis=2).reshape(b, HEADS, HEAD, HEAD)


def _pad_rows(a, rows):
    return jnp.pad(a, ((0, 0), (0, rows - a.shape[1]), (0, 0)))


def _layer(x, pos0, st, lw, final_g, final):
    b, t, d = x.shape
    n = b * t
    x2 = x.reshape(n, d)
    proj = _norm_proj(x2, lw["norm_g"], lw["w_in"])
    proj3 = proj.reshape(b, t, PROJ_W)
    past = 0 if st["ckv"] is None else st["ckv"].shape[1]

    chunk = min(t, CHUNK)
    y_rw, s_new = _rwkv(proj3, st["sp_r"], st["sp_k"], st["sp_v"], st["sp_l"], _pair_states(st["wkv"]),
                        lw["rw_pvec"], lw["rw_mu_l"], lw["rw_w2p"], lw["rw_a2p"], lw["g2"], chunk)
    shift_new = jnp.concatenate([proj3[:, t - 1:, O_RWR:O_RWR + 3 * BRANCH],
                                 proj3[:, t - 1:, O_LORA:O_LORA + 2 * RW_LORA]], axis=-1)
    wkv_new = _unpair_states(s_new)

    y_ssm, hr, hi = _s5(proj3, st["ssm_re"], st["ssm_im"], lw["ssm_bblk"], lw["ssm_cblk"], lw["ssm_lam"],
                        lw["ssm_pw"], lw["ssm_d"], lw["ssm_wglu"], lw["ssm_bglu"])
    ssm_re_new = hr.reshape(b, SSM_GROUPS, SSM_STATE)
    ssm_im_new = hi.reshape(b, SSM_GROUPS, SSM_STATE)

    pos = pos0 + jnp.arange(t, dtype=jnp.int32)
    tq_tab, tc_tab, ts_tab = _rope_tables(pos)
    q, ckv, kr = _mla_prep(proj, t, tq_tab, tc_tab, ts_tab, lw["mla_gq"], lw["mla_gkv"], lw["mla_wq"])
    ckv3 = ckv.reshape(b, t, MLA_KV_LORA)
    kr3 = kr.reshape(b, t, MLA_ROPE)
    s_valid = past + t
    if past:
        s_pad = -(-s_valid // LANES) * LANES
        ckv_all = _pad_rows(jnp.concatenate([st["ckv"], ckv3], axis=1), s_pad)
        kr_all = _pad_rows(jnp.concatenate([st["kpe"], kr3], axis=1), s_pad)
    else:
        s_pad, ckv_all, kr_all = t, ckv3, kr3
    kc, vv = _mla_kv(ckv_all.reshape(b * s_pad, MLA_KV_LORA), kr_all.reshape(b * s_pad, MLA_ROPE),
                     lw["mla_wk"], lw["mla_wv"], lw["dup"])
    small = t <= 64
    tq = min(t, 256)
    hp = 4 if small else 1
    y_mla = _mla_attn(q.reshape(b, t, HEADS * LANES), kc.reshape(b, s_pad, HEADS * LANES),
                      vv.reshape(b, s_pad, BRANCH), past, s_valid, tq, s_pad if small else min(512, s_pad), hp)

    sbk = proj3[:, :, O_SBK:O_SBK + BRANCH]
    sbv = proj3[:, :, O_SBV:O_SBV + BRANCH]
    tk_sb = s_pad if small else min(256, s_pad)
    idx = jnp.arange(tk_sb)
    us = (idx[:, None] > idx[None, :]).astype(BF16)
    if past:
        k_all = _pad_rows(jnp.concatenate([st["sbk"], sbk], axis=1), s_pad)
        v_all = _pad_rows(jnp.concatenate([st["sbv"], sbv], axis=1), s_pad)
        y_sb = _sb_attn(proj3, O_SBQ // LANES, k_all, 0, v_all, 0, us, past, tq, tk_sb, hp)
    else:
        y_sb = _sb_attn(proj3, O_SBQ // LANES, proj3, O_SBK // LANES, proj3, O_SBV // LANES, us, 0, tq,
                        tk_sb, hp)

    ys = [y.reshape(n, BRANCH) for y in (y_rw, y_ssm, y_mla, y_sb)]
    x_new = _merge(ys, proj, x2, lw["w_branch"], lw["b_merge"], lw["w_out"], final_g, final).reshape(b, t, d)
    new_state = (shift_new, wkv_new, ssm_re_new, ssm_im_new, ckv3, kr3,
                 sbk.reshape(b, t, HEADS, HEAD), sbv.reshape(b, t, HEADS, HEAD))
    return x_new, new_state


def kernel(x_prompt, x_sample, state_rwkv_shift, state_rwkv_wkv, state_ssm_re, state_ssm_im, cache_mla_ckv, cache_mla_kpe, cache_sb_k, cache_sb_v, norm_g, w_in, rw_mu, rw_w0, rw_w2, rw_a0, rw_a2, rw_k_k, rw_k_a, rw_r_k, rw_lnx_g, rw_lnx_b, ssm_lam_re, ssm_lam_im, ssm_log_dt, ssm_b_re, ssm_b_im, ssm_c_re, ssm_c_im, ssm_d, ssm_w_glu, ssm_b_glu, mla_q_norm, mla_w_q_up, mla_kv_norm, mla_w_kv_up, w_branch, b_merge, w_out, final_norm_g):
    depth = w_in.shape[0]
    bp, tp, _ = x_prompt.shape
    bs = x_sample.shape[0]
    past = cache_mla_ckv.shape[2]

    w_in_p = _permute_w_in(w_in)
    w_branch_b = w_branch.astype(BF16)
    w_out_b = w_out.astype(BF16)
    lane_i = jnp.arange(LANES)
    g2 = ((lane_i[:, None] // HEAD) == (lane_i[None, :] // HEAD)).astype(BF16)
    us = (lane_i[:, None] > lane_i[None, :]).astype(BF16)
    rope_i = jnp.arange(MLA_ROPE)
    dup = ((lane_i[None, :] == rope_i[:, None] + MLA_NOPE)
           | (lane_i[None, :] == rope_i[:, None] + MLA_NOPE + MLA_ROPE)).astype(BF16)
    final_g = final_norm_g.reshape(1, D_MODEL)
    zpad = jnp.zeros((RW_LORA, BRANCH), F32)

    layers = []
    for l in range(depth):
        wq_p, wk_p, wv_p = _mla_weights(mla_w_q_up[l], mla_w_kv_up[l])
        bblk, cblk, lam, pw = _s5_tables(ssm_lam_re[l], ssm_lam_im[l], ssm_log_dt[l], ssm_b_re[l], ssm_b_im[l],
                                         ssm_c_re[l], ssm_c_im[l])
        mu = rw_mu[l]
        rows = [mu[:BRANCH], mu[BRANCH:2 * BRANCH], mu[2 * BRANCH:3 * BRANCH], rw_w0[l], rw_a0[l], rw_k_k[l],
                rw_k_a[l], rw_r_k[l].reshape(BRANCH), rw_lnx_g[l], rw_lnx_b[l]]
        pvec = jnp.concatenate([jnp.stack(rows), jnp.zeros((16 - len(rows), BRANCH), F32)], axis=0)
        layers.append(dict(
            norm_g=norm_g[l].reshape(1, D_MODEL), w_in=w_in_p[l],
            rw_pvec=pvec, rw_mu_l=mu[3 * BRANCH:].reshape(1, 2 * RW_LORA),
            rw_w2p=jnp.concatenate([rw_w2[l], zpad], axis=0).astype(BF16),
            rw_a2p=jnp.concatenate([zpad, rw_a2[l]], axis=0).astype(BF16),
            g2=g2, us=us, dup=dup,
            ssm_bblk=bblk, ssm_cblk=cblk, ssm_lam=lam, ssm_pw=pw,
            ssm_d=ssm_d[l].reshape(1, BRANCH), ssm_wglu=ssm_w_glu[l].astype(BF16),
            ssm_bglu=ssm_b_glu[l].reshape(1, BRANCH),
            mla_gq=mla_q_norm[l].reshape(1, MLA_Q_LORA), mla_gkv=mla_kv_norm[l].reshape(1, MLA_KV_LORA),
            mla_wq=wq_p, mla_wk=wk_p, mla_wv=wv_p,
            w_branch=w_branch_b[l], b_merge=b_merge[l].reshape(N_BRANCH, 1, D_MODEL), w_out=w_out_b[l]))

    def fresh(bn):
        return dict(sp_r=jnp.zeros((bn, 1, BRANCH), F32), sp_k=jnp.zeros((bn, 1, BRANCH), F32),
                    sp_v=jnp.zeros((bn, 1, BRANCH), F32), sp_l=jnp.zeros((bn, 1, 2 * RW_LORA), F32),
                    wkv=jnp.zeros((bn, HEADS, HEAD, HEAD), F32),
                    ssm_re=jnp.zeros((bn, 1, SSM_W), F32), ssm_im=jnp.zeros((bn, 1, SSM_W), F32),
                    ckv=None, kpe=None, sbk=None, sbv=None)

    def carried(l):
        sh = state_rwkv_shift[l]
        return dict(sp_r=sh[:, :, :BRANCH], sp_k=sh[:, :, BRANCH:2 * BRANCH], sp_v=sh[:, :, 2 * BRANCH:3 * BRANCH],
                    sp_l=sh[:, :, 3 * BRANCH:], wkv=state_rwkv_wkv[l],
                    ssm_re=state_ssm_re[l].reshape(bs, 1, SSM_W), ssm_im=state_ssm_im[l].reshape(bs, 1, SSM_W),
                    ckv=cache_mla_ckv[l], kpe=cache_mla_kpe[l],
                    sbk=cache_sb_k[l].reshape(bs, past, BRANCH), sbv=cache_sb_v[l].reshape(bs, past, BRANCH))

    xp, xs = x_prompt, x_sample
    new_p, new_s = [], []
    for l in range(depth):
        last = l == depth - 1
        xp, st_p = _layer(xp, 0, fresh(bp), layers[l], final_g, last)
        xs, st_s = _layer(xs, past, carried(l), layers[l], final_g, last)
        new_p.append(st_p)
        new_s.append(st_s)
    stk = lambda lst, i: jnp.stack([s[i] for s in lst], axis=0)
    return (xp, xs) + tuple(stk(new_p, i) for i in range(8)) + tuple(stk(new_s, i) for i in range(8))
```

```python
import functools
import math

import jax
import jax.numpy as jnp
from jax import lax
from jax.experimental import pallas as pl
from jax.experimental.pallas import tpu as pltpu

F32 = jnp.float32
BF16 = jnp.bfloat16

D_MODEL = 2048
BRANCH = 512
N_BRANCH = 4
EPS = 1e-6
CHUNK = 64
HEAD = 64
HEADS = 8
RW_LORA = 64
RW_IN = 3 * BRANCH + 2 * RW_LORA
RW_GN_EPS = 64e-5
SSM_GROUP = 16
SSM_GROUPS = 32
SSM_STATE = 64
SSM_W = SSM_GROUPS * SSM_STATE
MLA_NOPE = 64
MLA_ROPE = 32
MLA_Q_LORA = 384
MLA_KV_LORA = 256
MLA_SCALE = 1.0 / math.sqrt(MLA_NOPE + MLA_ROPE)
ROPE_BASE = 10000.0
SB_SCALE = 1.0 / math.sqrt(HEAD)
LANES = 128
MIB = 1024 * 1024

O_MERGE = 0
O_GATE = 8192
O_SBQ = 10240
O_SBK = 10752
O_SBV = 11264
O_RWR = 11776
O_RWK = 12288
O_RWV = 12800
O_SSM = 13312
O_QLAT = 13824
O_LORA = 14208
O_KVLAT = 14336
O_KPE = 14592
O_KPESW = 14720
PROJ_W = 14848


def _cparams(sem, vmem_mib):
    return pltpu.CompilerParams(dimension_semantics=sem, vmem_limit_bytes=vmem_mib * MIB)


def _dot(a, b):
    return jnp.dot(a, b, preferred_element_type=F32)


def _dot_nt(a, b):
    return lax.dot_general(a, b, (((1,), (1,)), ((), ())), preferred_element_type=F32)


def _split3(x):
    h1 = x.astype(BF16)
    r1 = x - h1.astype(F32)
    h2 = r1.astype(BF16)
    h3 = (r1 - h2.astype(F32)).astype(BF16)
    return h1, h2, h3


def _softplus(x):
    return jnp.maximum(x, 0.0) + jnp.log1p(jnp.exp(-jnp.abs(x)))


def _norm_proj_kernel(x_ref, g_ref, w_ref, o_ref, h_ref):
    @pl.when(pl.program_id(1) == 0)
    def _():
        x = x_ref[...]
        ms = jnp.mean(x * x, axis=-1, keepdims=True)
        h_ref[...] = (x * lax.rsqrt(ms + EPS) * g_ref[...]).astype(BF16)

    o_ref[...] = _dot(h_ref[...], w_ref[0])


def _norm_proj(x2, g, w_all, layer):
    n, d = x2.shape
    width = w_all.shape[2]
    tm = min(n, 1024)
    tn = 512
    return pl.pallas_call(
        _norm_proj_kernel,
        out_shape=jax.ShapeDtypeStruct((n, width), F32),
        grid=(n // tm, width // tn),
        in_specs=[
            pl.BlockSpec((tm, d), lambda i, j: (i, 0)),
            pl.BlockSpec((1, d), lambda i, j: (0, 0)),
            pl.BlockSpec((1, d, tn), lambda i, j: (layer, 0, j)),
        ],
        out_specs=pl.BlockSpec((tm, tn), lambda i, j: (i, j)),
        scratch_shapes=[pltpu.VMEM((tm, d), BF16)],
        compiler_params=_cparams(("parallel", "arbitrary"), 40),
        name="norm_proj",
    )(x2, g, w_all)


def _segsum(x, g2):
    outs = []
    for j in range(BRANCH // LANES):
        h1, h2, h3 = _split3(x[:, LANES * j:LANES * (j + 1)])
        outs.append(_dot(h1, g2) + _dot(h2, g2) + _dot(h3, g2))
    return jnp.concatenate(outs, axis=1)


def _rwkv_kernel(pr_ref, pk_ref, pv_ref, plo_ref, spr_ref, spk_ref, spv_ref, spl_ref, s0_ref,
                 pvec_ref, mul_ref, w2_ref, a2_ref, g2_ref,
                 y_ref, sout_ref,
                 st_ref, cr_ref, ck_ref, cv_ref, cl_ref, *, C):
    c = pl.program_id(1)

    @pl.when(c == 0)
    def _():
        st_ref[...] = s0_ref[0]
        cr_ref[...] = spr_ref[0]
        ck_ref[...] = spk_ref[0]
        cv_ref[...] = spv_ref[0]
        cl_ref[...] = spl_ref[0]

    pvec = pvec_ref[...]
    prow = lambda i: pvec[i:i + 1, :]
    g2 = g2_ref[...]

    def tshift(p, carry_ref, mu):
        rolled = pltpu.roll(p, 1, 0)
        rid = lax.broadcasted_iota(jnp.int32, p.shape, 0)
        prev = jnp.where(rid == 0, carry_ref[...], rolled)
        carry_ref[...] = p[C - 1:C, :]
        return p + (prev - p) * mu

    r = tshift(pr_ref[0], cr_ref, prow(0))
    k = tshift(pk_ref[0], ck_ref, prow(1))
    v = tshift(pv_ref[0], cv_ref, prow(2))
    lo = tshift(plo_ref[0], cl_ref, mul_ref[...])

    log_w = -_softplus(-(prow(3) + _dot(jnp.tanh(lo).astype(BF16), w2_ref[...]))) - 0.5
    ld = -jnp.exp(log_w)
    a_icl = jax.nn.sigmoid(prow(4) + _dot(lo.astype(BF16), a2_ref[...]))
    kkr = k * prow(5)
    kk = kkr / jnp.maximum(jnp.sqrt(_segsum(kkr * kkr, g2)), 1e-12)
    k2 = k * (1.0 + (a_icl - 1.0) * prow(6))
    av = -kk
    bv = kk * a_icl

    ri = lax.broadcasted_iota(jnp.int32, (C, C), 0)
    ci = lax.broadcasted_iota(jnp.int32, (C, C), 1)
    lincl = (ci <= ri).astype(BF16)
    h1, h2, h3 = _split3(ld)
    g = _dot(lincl, h1) + _dot(lincl, h2) + _dot(lincl, h3)
    g_last = g[C - 1:C, :]
    eg = jnp.exp(g)
    eng = jnp.exp(-g)
    at = av * jnp.exp(g - ld)
    rt = r * eg
    bt = bv * eng
    kt = k2 * eng
    e_c = jnp.exp(g_last - g)
    b_end = bv * e_c
    k_end = k2 * e_c
    eg_last = jnp.exp(g_last)

    c2 = 2 * C
    npair = BRANCH // LANES
    sh = C.bit_length() - 1
    r_i = lax.broadcasted_iota(jnp.int32, (c2, c2), 0)
    c_i = lax.broadcasted_iota(jnp.int32, (c2, c2), 1)
    c_m = jnp.where(c_i >= C, c_i - C, c_i)
    mhalf = c_m < jnp.where(r_i < C, r_i, r_i - C + 1)
    mfull = jnp.concatenate([mhalf, mhalf], axis=0)
    bd2 = (r_i >> sh) == (c_i >> sh)
    eye2 = (r_i == c_i).astype(F32)
    lane = lax.broadcasted_iota(jnp.int32, (1, LANES), 1)
    m0 = (lane < HEAD).astype(F32)
    m1 = (lane >= HEAD).astype(F32)
    v_i = lax.broadcasted_iota(jnp.int32, (LANES, LANES), 0)
    k_i = lax.broadcasted_iota(jnp.int32, (LANES, LANES), 1)
    bdmask = (v_i >> 6) == (k_i >> 6)
    zf = jnp.zeros((C, LANES), F32)
    zc = jnp.zeros((C, LANES), BF16)
    sls = [slice(LANES * j, LANES * (j + 1)) for j in range(npair)]

    ws, a_ss, s2s = [], [], []
    for sl, j in zip(sls, range(npair)):
        a0, r0, a1, r1 = at[:, sl] * m0, rt[:, sl] * m0, at[:, sl] * m1, rt[:, sl] * m1
        lhs_w = jnp.concatenate([jnp.concatenate([a0, zf], axis=1), jnp.concatenate([r0, zf], axis=1),
                                 jnp.concatenate([zf, a1], axis=1), jnp.concatenate([zf, r1], axis=1)],
                                axis=0).astype(BF16)
        btj, ktj = bt[:, sl], kt[:, sl]
        rhs_w = jnp.concatenate([jnp.concatenate([btj, ktj], axis=0),
                                 jnp.concatenate([ktj, btj], axis=0)], axis=1).astype(BF16)
        ws.append(jnp.where(mfull, _dot_nt(lhs_w, rhs_w), 0.0))
        lhs_s = jnp.concatenate([a0, r0, a1, r1], axis=0).astype(BF16)
        s2 = st_ref[j]
        s2h, s2l, _ = _split3(s2)
        a_ss.append(_dot_nt(lhs_s, s2h) + _dot_nt(lhs_s, s2l))
        s2s.append(s2)

    n_pows = [jnp.where(bd2, jnp.concatenate([w[:C], w[c2:c2 + C]], axis=0), 0.0) for w in ws]
    t_mats = [eye2 + n for n in n_pows]
    lv = 2
    while lv < C:
        n_pows = [_dot(n.astype(BF16), n.astype(BF16)) for n in n_pows]
        t_mats = [t + _dot(t.astype(BF16), n.astype(BF16)) for t, n in zip(t_mats, n_pows)]
        lv *= 2

    wbs = [w.astype(BF16) for w in ws]
    v0s = [(v[:, sl] * m0).astype(BF16) for sl in sls]
    v1s = [(v[:, sl] * m1).astype(BF16) for sl in sls]
    xs = [jnp.concatenate([a_s[:C] + _dot(wb[:C], jnp.concatenate([zc, v0], axis=0)),
                           a_s[c2:c2 + C] + _dot(wb[c2:c2 + C], jnp.concatenate([v1, zc], axis=0))], axis=0)
          for a_s, wb, v0, v1 in zip(a_ss, wbs, v0s, v1s)]
    us = [_dot(t.astype(BF16), x.astype(BF16)) for t, x in zip(t_mats, xs)]
    ys = []
    for j, sl in enumerate(sls):
        a_s, wb, u = a_ss[j], wbs[j], us[j]
        u0, u1 = u[:C], u[C:]
        y0 = a_s[C:c2] + _dot(wb[C:c2], jnp.concatenate([u0.astype(BF16), v0s[j]], axis=0))
        y1 = a_s[c2 + C:] + _dot(wb[c2 + C:], jnp.concatenate([v1s[j], u1.astype(BF16)], axis=0))
        ys.append(y0 + y1)
        uv = jnp.concatenate([u0 + u1, v[:, sl]], axis=0)
        bk = jnp.concatenate([b_end[:, sl], k_end[:, sl]], axis=0).astype(BF16)
        upd = _dot(uv.T.astype(BF16), bk)
        st_ref[j] = s2s[j] * eg_last[:, sl] + jnp.where(bdmask, upd, 0.0)
    y = jnp.concatenate(ys, axis=1)

    inv_n = 1.0 / HEAD
    yc = y - _segsum(y, g2) * inv_n
    yn = yc * lax.rsqrt(_segsum(yc * yc, g2) * inv_n + RW_GN_EPS) * prow(8) + prow(9)
    bonus = _segsum(r * k2 * prow(7), g2) * v
    y_ref[0] = (yn + bonus).astype(y_ref.dtype)

    @pl.when(c == pl.num_programs(1) - 1)
    def _():
        sout_ref[0] = st_ref[...]


def _rwkv(proj3, sp_r, sp_k, sp_v, sp_l, s0, pvec, mu_l, w2p, a2p, g2, chunk):
    b, t, _ = proj3.shape
    nc = t // chunk
    cb = lambda off, wdt: off // wdt
    bspec = lambda off, wdt: pl.BlockSpec((1, chunk, wdt), lambda i, c, o=cb(off, wdt): (i, c, o))
    row3 = lambda wdt: pl.BlockSpec((1, 1, wdt), lambda i, c: (i, 0, 0))
    full2 = lambda a: pl.BlockSpec(a.shape, lambda i, c: (0, 0))
    return pl.pallas_call(
        functools.partial(_rwkv_kernel, C=chunk),
        out_shape=(jax.ShapeDtypeStruct((b, t, BRANCH), BF16),
                   jax.ShapeDtypeStruct((b, 4, LANES, LANES), F32)),
        grid=(b, nc),
        in_specs=[
            bspec(O_RWR, BRANCH), bspec(O_RWK, BRANCH), bspec(O_RWV, BRANCH), bspec(O_LORA, LANES),
            row3(BRANCH), row3(BRANCH), row3(BRANCH), row3(LANES),
            pl.BlockSpec((1, 4, LANES, LANES), lambda i, c: (i, 0, 0, 0)),
            full2(pvec), full2(mu_l), full2(w2p), full2(a2p), full2(g2),
        ],
        out_specs=(pl.BlockSpec((1, chunk, BRANCH), lambda i, c: (i, c, 0)),
                   pl.BlockSpec((1, 4, LANES, LANES), lambda i, c: (i, 0, 0, 0))),
        scratch_shapes=[pltpu.VMEM((4, LANES, LANES), F32),
                        pltpu.VMEM((1, BRANCH), F32), pltpu.VMEM((1, BRANCH), F32),
                        pltpu.VMEM((1, BRANCH), F32), pltpu.VMEM((1, LANES), F32)],
        compiler_params=_cparams(("parallel", "arbitrary"), 32),
        name="rwkv7",
    )(proj3, proj3, proj3, proj3, sp_r, sp_k, sp_v, sp_l, s0, pvec, mu_l, w2p, a2p, g2)


def _s5_kernel(u_ref, h0r_ref, h0i_ref, bblk_ref, cblk_ref, lam_ref, pw_ref, dsk_ref, wglu_ref, bglu_ref,
               y_ref, hro_ref, hio_ref, x_ref, cr_ref, ci_ref, *, tb):
    @pl.when(pl.program_id(1) == 0)
    def _():
        cr_ref[...] = h0r_ref[0]
        ci_ref[...] = h0i_ref[0]

    u = u_ref[0]
    x = _dot(u.astype(BF16), bblk_ref[...])
    xr = x[:, :SSM_W]
    xi = x[:, SSM_W:]
    rid = lax.broadcasted_iota(jnp.int32, (tb, 1), 0) & 7
    for lvl, s in enumerate((1, 2, 4)):
        ar = lam_ref[2 * lvl:2 * lvl + 1, :]
        ai = lam_ref[2 * lvl + 1:2 * lvl + 2, :]
        sr = pltpu.roll(xr, s, 0)
        si = pltpu.roll(xi, s, 0)
        valid = rid >= s
        nr = xr + jnp.where(valid, ar * sr - ai * si, 0.0)
        ni = xi + jnp.where(valid, ar * si + ai * sr, 0.0)
        xr, xi = nr, ni
    x_ref[:, :SSM_W] = xr
    x_ref[:, SSM_W:] = xi
    pwr = pw_ref[0:8, :]
    pwi = pw_ref[8:16, :]

    def body(gi, carry):
        cr, ci = carry
        o = pl.multiple_of(gi * 8, 8)
        br = x_ref[pl.ds(o, 8), 0:SSM_W]
        bi = x_ref[pl.ds(o, 8), SSM_W:2 * SSM_W]
        br = br + pwr * cr - pwi * ci
        bi = bi + pwr * ci + pwi * cr
        x_ref[pl.ds(o, 8), 0:SSM_W] = br
        x_ref[pl.ds(o, 8), SSM_W:2 * SSM_W] = bi
        return br[7:8, :], bi[7:8, :]

    cr, ci = lax.fori_loop(0, tb // 8, body, (cr_ref[...], ci_ref[...]))
    cr_ref[...] = cr
    ci_ref[...] = ci
    hro_ref[0] = cr
    hio_ref[0] = ci
    y = _dot(x_ref[...].astype(BF16), cblk_ref[...]) + dsk_ref[...] * u
    g = jax.nn.gelu(y)
    out = g * jax.nn.sigmoid(_dot(g.astype(BF16), wglu_ref[...]) + bglu_ref[...])
    y_ref[0] = out.astype(y_ref.dtype)


def _s5(proj3, h0r, h0i, bblk, cblk, lam, pw, dsk, wglu, bglu):
    b, t, _ = proj3.shape
    tb = min(t, 128)
    full2 = lambda a: pl.BlockSpec(a.shape, lambda i, c: (0, 0))
    row3 = pl.BlockSpec((1, 1, SSM_W), lambda i, c: (i, 0, 0))
    return pl.pallas_call(
        functools.partial(_s5_kernel, tb=tb),
        out_shape=(jax.ShapeDtypeStruct((b, t, BRANCH), BF16),
                   jax.ShapeDtypeStruct((b, 1, SSM_W), F32),
                   jax.ShapeDtypeStruct((b, 1, SSM_W), F32)),
        grid=(b, t // tb),
        in_specs=[pl.BlockSpec((1, tb, BRANCH), lambda i, c: (i, c, O_SSM // BRANCH)),
                  row3, row3, full2(bblk), full2(cblk), full2(lam), full2(pw), full2(dsk),
                  full2(wglu), full2(bglu)],
        out_specs=(pl.BlockSpec((1, tb, BRANCH), lambda i, c: (i, c, 0)), row3, row3),
        scratch_shapes=[pltpu.VMEM((tb, 2 * SSM_W), F32),
                        pltpu.VMEM((1, SSM_W), F32), pltpu.VMEM((1, SSM_W), F32)],
        compiler_params=_cparams(("parallel", "arbitrary"), 48),
        name="s5",
    )(proj3, h0r, h0i, bblk, cblk, lam, pw, dsk, wglu, bglu)


def _mla_prep_kernel(ql_ref, kvl_ref, kpe_ref, kpesw_ref, tq_ref, tc_ref, ts_ref,
                     gq_ref, gkv_ref, wq_ref, q_ref, ckv_ref, kr_ref):
    ql = ql_ref[...]
    qn = ql * lax.rsqrt(jnp.mean(ql * ql, axis=-1, keepdims=True) + EPS) * gq_ref[...]
    q = _dot(qn.astype(BF16), wq_ref[...])
    tq = tq_ref[...]
    q_ref[...] = jnp.concatenate(
        [q[:, LANES * h:LANES * (h + 1)] * tq for h in range(HEADS)], axis=1).astype(BF16)
    kvl = kvl_ref[...]
    ckv_ref[...] = kvl * lax.rsqrt(jnp.mean(kvl * kvl, axis=-1, keepdims=True) + EPS) * gkv_ref[...]
    kr = kpe_ref[...] * tc_ref[...] + kpesw_ref[...] * ts_ref[...]
    kr_ref[...] = kr[:, :MLA_ROPE]


def _mla_prep(proj, t, tq_tab, tc_tab, ts_tab, gq, gkv, wq):
    n = proj.shape[0]
    tm = min(t, 512)
    nt = t // tm
    tab = pl.BlockSpec((tm, LANES), lambda i: (i % nt, 0))
    full2 = lambda a: pl.BlockSpec(a.shape, lambda i: (0, 0))
    return pl.pallas_call(
        _mla_prep_kernel,
        out_shape=(jax.ShapeDtypeStruct((n, HEADS * LANES), BF16),
                   jax.ShapeDtypeStruct((n, MLA_KV_LORA), F32),
                   jax.ShapeDtypeStruct((n, MLA_ROPE), F32)),
        grid=(n // tm,),
        in_specs=[pl.BlockSpec((tm, MLA_Q_LORA), lambda i: (i, O_QLAT // MLA_Q_LORA)),
                  pl.BlockSpec((tm, MLA_KV_LORA), lambda i: (i, O_KVLAT // MLA_KV_LORA)),
                  pl.BlockSpec((tm, LANES), lambda i: (i, O_KPE // LANES)),
                  pl.BlockSpec((tm, LANES), lambda i: (i, O_KPESW // LANES)),
                  tab, tab, tab, full2(gq), full2(gkv), full2(wq)],
        out_specs=(pl.BlockSpec((tm, HEADS * LANES), lambda i: (i, 0)),
                   pl.BlockSpec((tm, MLA_KV_LORA), lambda i: (i, 0)),
                   pl.BlockSpec((tm, MLA_ROPE), lambda i: (i, 0))),
        compiler_params=_cparams(("parallel",), 32),
        name="mla_prep",
    )(proj, proj, proj, proj, tq_tab, tc_tab, ts_tab, gq, gkv, wq)


def _mla_kv_kernel(ckv_ref, kr_ref, wk_ref, wv_ref, dup_ref, k_ref, v_ref):
    cb = ckv_ref[...].astype(BF16)
    kn = _dot(cb, wk_ref[...])
    krd = _dot(kr_ref[...].astype(BF16), dup_ref[...])
    k_ref[...] = jnp.concatenate(
        [kn[:, LANES * h:LANES * (h + 1)] + krd for h in range(HEADS)], axis=1).astype(BF16)
    v_ref[...] = _dot(cb, wv_ref[...]).astype(BF16)


def _mla_kv(ckv, kr, wk, wv, dup):
    n = ckv.shape[0]
    tm = next((c for c in (512, LANES) if n % c == 0), n)
    full2 = lambda a: pl.BlockSpec(a.shape, lambda i: (0, 0))
    return pl.pallas_call(
        _mla_kv_kernel,
        out_shape=(jax.ShapeDtypeStruct((n, HEADS * LANES), BF16),
                   jax.ShapeDtypeStruct((n, BRANCH), BF16)),
        grid=(n // tm,),
        in_specs=[pl.BlockSpec((tm, MLA_KV_LORA), lambda i: (i, 0)),
                  pl.BlockSpec((tm, MLA_ROPE), lambda i: (i, 0)),
                  full2(wk), full2(wv), full2(dup)],
        out_specs=(pl.BlockSpec((tm, HEADS * LANES), lambda i: (i, 0)),
                   pl.BlockSpec((tm, BRANCH), lambda i: (i, 0))),
        compiler_params=_cparams(("parallel",), 32),
        name="mla_kv",
    )(ckv, kr, wk, wv, dup)


def _assemble_keys(past_ref, new_ref, all_ref, past, t):
    all_ref[0:past, :] = past_ref[0].astype(BF16)
    all_ref[past:past + t, :] = new_ref[0].astype(BF16)
    pad = all_ref.shape[0] - past - t
    if pad:
        all_ref[past + t:, :] = jnp.zeros((pad, all_ref.shape[1]), BF16)


def _mla_attn_kernel(q_ref, k_ref, v_ref, *rest, tq, tk, q_off, s_valid, hp, past):
    if past:
        kp_ref, vp_ref, o_ref, kall_ref, vall_ref = rest
        _assemble_keys(kp_ref, k_ref, kall_ref, past, tq)
        _assemble_keys(vp_ref, v_ref, vall_ref, past, tq)
        kload = lambda o: kall_ref[pl.ds(o, tk), :]
        vload = lambda o: vall_ref[pl.ds(o, tk), :]
    else:
        (o_ref,) = rest
        kload = lambda o: k_ref[0, pl.ds(o, tk), :]
        vload = lambda o: v_ref[0, pl.ds(o, tk), :]
    i = pl.program_id(2)
    q_lo = q_off + i * tq
    limit = jnp.minimum(q_lo + tq, s_valid)
    nkb = (limit + tk - 1) // tk
    qidx = q_lo + lax.broadcasted_iota(jnp.int32, (tq, tk), 0)
    kloc = lax.broadcasted_iota(jnp.int32, (tq, tk), 1)
    nh = 2 * hp
    q2 = q_ref[0]
    qs = [q2[:, LANES * h:LANES * (h + 1)] for h in range(nh)]

    def body(kb, carry):
        o = pl.multiple_of(kb * tk, tk)
        kidx = kloc + kb * tk
        vis = jnp.logical_and((kidx >> 6) <= (qidx >> 6), kidx < s_valid)
        kblk = kload(o)
        vblk = vload(o)
        ss = [jnp.where(vis, _dot_nt(qs[h], kblk[:, LANES * h:LANES * (h + 1)]), -1e30) for h in range(nh)]
        m_new = [jnp.maximum(carry[h][0], jnp.max(ss[h], axis=-1, keepdims=True)) for h in range(nh)]
        ps = [jnp.exp(ss[h] - m_new[h]) for h in range(nh)]
        pv = [_dot(ps[h].astype(BF16), vblk[:, LANES * (h // 2):LANES * (h // 2 + 1)]) for h in range(nh)]
        new = []
        for h in range(nh):
            m_prev, l_prev, acc = carry[h]
            alpha = jnp.exp(m_prev - m_new[h])
            new.append((m_new[h], alpha * l_prev + jnp.sum(ps[h], axis=-1, keepdims=True), alpha * acc + pv[h]))
        return tuple(new)

    init = tuple((jnp.full((tq, 1), -1e30, F32), jnp.zeros((tq, 1), F32), jnp.zeros((tq, LANES), F32))
                 for _ in range(nh))
    res = lax.fori_loop(0, nkb, body, init)
    lane = lax.broadcasted_iota(jnp.int32, (tq, LANES), 1)
    outs = [jnp.where(lane < HEAD, res[2 * p][2] / res[2 * p][1], res[2 * p + 1][2] / res[2 * p + 1][1])
            for p in range(hp)]
    o_ref[0] = jnp.concatenate(outs, axis=1).astype(o_ref.dtype)


def _mla_attn(q3, k3, v3, kp3, vp3, s_pad, tq, tk, hp):
    b, t, _ = q3.shape
    past = 0 if kp3 is None else kp3.shape[1]
    rows = k3.shape[1]
    in_specs = [pl.BlockSpec((1, tq, 2 * LANES * hp), lambda bi, j, i: (bi, i, j)),
                pl.BlockSpec((1, rows, 2 * LANES * hp), lambda bi, j, i: (bi, 0, j)),
                pl.BlockSpec((1, rows, LANES * hp), lambda bi, j, i: (bi, 0, j))]
    args = [q3, k3, v3]
    scratch = []
    if past:
        in_specs += [pl.BlockSpec((1, past, 2 * LANES * hp), lambda bi, j, i: (bi, 0, j)),
                     pl.BlockSpec((1, past, LANES * hp), lambda bi, j, i: (bi, 0, j))]
        args += [kp3, vp3]
        scratch = [pltpu.VMEM((s_pad, 2 * LANES * hp), BF16), pltpu.VMEM((s_pad, LANES * hp), BF16)]
    return pl.pallas_call(
        functools.partial(_mla_attn_kernel, tq=tq, tk=tk, q_off=past, s_valid=past + t, hp=hp, past=past),
        out_shape=jax.ShapeDtypeStruct((b, t, BRANCH), BF16),
        grid=(b, 4 // hp, t // tq),
        in_specs=in_specs,
        out_specs=pl.BlockSpec((1, tq, LANES * hp), lambda bi, j, i: (bi, i, j)),
        scratch_shapes=scratch,
        compiler_params=_cparams(("parallel", "parallel", "arbitrary"), 40),
        name="mla_attn",
    )(*args)


def _sb_attn_kernel(q_ref, k_ref, v_ref, *rest, tq, tk, q_off, hp, past):
    if past:
        kp_ref, vp_ref, us_ref, o_ref, kall_ref, vall_ref = rest
        _assemble_keys(kp_ref, k_ref, kall_ref, past, tq)
        _assemble_keys(vp_ref, v_ref, vall_ref, past, tq)
        kload = lambda o: kall_ref[pl.ds(o, tk), :]
        vload = lambda o: vall_ref[pl.ds(o, tk), :]
    else:
        us_ref, o_ref = rest
        kload = lambda o: k_ref[0, pl.ds(o, tk), :].astype(BF16)
        vload = lambda o: v_ref[0, pl.ds(o, tk), :].astype(BF16)
    i = pl.program_id(2)
    q_lo = q_off + i * tq
    nkb = (q_lo + tq - 2) // tk + 1
    nfull = q_lo // tk
    qidx = q_lo + lax.broadcasted_iota(jnp.int32, (tq, tk), 0)
    kloc = lax.broadcasted_iota(jnp.int32, (tq, tk), 1)
    lane1 = lax.broadcasted_iota(jnp.int32, (1, LANES), 1)
    hm = ((lane1 < HEAD).astype(F32) * SB_SCALE, (lane1 >= HEAD).astype(F32) * SB_SCALE)
    nh = 2 * hp
    q2 = q_ref[0]
    qh = [(q2[:, LANES * (h // 2):LANES * (h // 2 + 1)] * hm[h % 2]).astype(BF16) for h in range(nh)]
    us = us_ref[...]

    def block(kb, carry, masked):
        o = pl.multiple_of(kb * tk, tk)
        kblk = kload(o)
        vblk = vload(o)
        zs = [_dot_nt(qh[h], kblk[:, LANES * (h // 2):LANES * (h // 2 + 1)]) for h in range(nh)]
        zls = [jnp.minimum(z, 0.0) - jnp.log(1.0 + jnp.exp(-jnp.abs(z))) for z in zs]
        lgs = [zl - z for zl, z in zip(zls, zs)]
        if masked:
            vis = (kloc + kb * tk) < qidx
            lgs = [jnp.where(vis, lg, 0.0) for lg in lgs]
        his = [lg.astype(BF16) for lg in lgs]
        los = [(lg - hi.astype(F32)).astype(BF16) for lg, hi in zip(lgs, his)]
        later = _dot(jnp.concatenate(his + los, axis=0), us)
        new = []
        for h in range(nh):
            csum, acc = carry[h]
            tot = later[h * tq:(h + 1) * tq] + later[(nh + h) * tq:(nh + h + 1) * tq] + csum
            a = jnp.exp(zls[h] + tot)
            if masked:
                a = jnp.where(vis, a, 0.0)
            acc = acc + _dot(a.astype(BF16), vblk[:, LANES * (h // 2):LANES * (h // 2 + 1)])
            new.append((csum + jnp.sum(lgs[h], axis=-1, keepdims=True), acc))
        return tuple(new)

    init = tuple((jnp.zeros((tq, 1), F32), jnp.zeros((tq, LANES), F32)) for _ in range(nh))
    res = lax.fori_loop(0, nkb - nfull, lambda s, c: block(nkb - 1 - s, c, True), init)
    res = lax.fori_loop(0, nfull, lambda s, c: block(nfull - 1 - s, c, False), res)
    lane = lax.broadcasted_iota(jnp.int32, (tq, LANES), 1)
    outs = [jnp.where(lane < HEAD, res[2 * p][1], res[2 * p + 1][1]) for p in range(hp)]
    o_ref[0] = jnp.concatenate(outs, axis=1).astype(o_ref.dtype)


def _sb_attn(proj3, kp3, vp3, us, s_pad, tq, tk, hp):
    b, t, _ = proj3.shape
    past = 0 if kp3 is None else kp3.shape[1]
    wdt = LANES * hp
    in_specs = [pl.BlockSpec((1, tq, wdt), lambda bi, j, i: (bi, i, O_SBQ // wdt + j)),
                pl.BlockSpec((1, t, wdt), lambda bi, j, i: (bi, 0, O_SBK // wdt + j)),
                pl.BlockSpec((1, t, wdt), lambda bi, j, i: (bi, 0, O_SBV // wdt + j))]
    args = [proj3, proj3, proj3]
    scratch = []
    if past:
        in_specs += [pl.BlockSpec((1, past, wdt), lambda bi, j, i: (bi, 0, j)),
                     pl.BlockSpec((1, past, wdt), lambda bi, j, i: (bi, 0, j))]
        args += [kp3, vp3]
        scratch = [pltpu.VMEM((s_pad, wdt), BF16), pltpu.VMEM((s_pad, wdt), BF16)]
    in_specs.append(pl.BlockSpec(us.shape, lambda bi, j, i: (0, 0)))
    args.append(us)
    return pl.pallas_call(
        functools.partial(_sb_attn_kernel, tq=tq, tk=tk, q_off=past, hp=hp, past=past),
        out_shape=jax.ShapeDtypeStruct((b, t, BRANCH), BF16),
        grid=(b, 4 // hp, t // tq),
        in_specs=in_specs,
        out_specs=pl.BlockSpec((1, tq, wdt), lambda bi, j, i: (bi, i, j)),
        scratch_shapes=scratch,
        compiler_params=_cparams(("parallel", "parallel", "arbitrary"), 48),
        name="sb_attn",
    )(*args)


def _merge_kernel(yrw_ref, yssm_ref, ymla_ref, ysb_ref, gate_ref, pm_ref, x_ref, wb_ref, bm_ref, wo_ref,
                  fg_ref, o_ref, acc_ref, *, final):
    n = pl.program_id(1)

    @pl.when(n == 0)
    def _():
        acc_ref[...] = jnp.zeros_like(acc_ref)

    def branch(y_ref):
        g = gate_ref[...]
        gated = y_ref[...].astype(F32) * (g * jax.nn.sigmoid(g))
        up = _dot(gated.astype(BF16), wb_ref[0])
        acc_ref[...] += jax.nn.sigmoid(pm_ref[...] + bm_ref[0]) * up

    for idx, y_ref in enumerate((yrw_ref, yssm_ref, ymla_ref, ysb_ref)):
        pl.when(n == idx)(functools.partial(branch, y_ref))

    @pl.when(n == N_BRANCH - 1)
    def _():
        xn = x_ref[...] + _dot(acc_ref[...].astype(BF16), wo_ref[...])
        if final:
            xn = xn * lax.rsqrt(jnp.mean(xn * xn, axis=-1, keepdims=True) + EPS) * fg_ref[...]
        o_ref[...] = xn


def _merge(ys, proj, x2, wb, bm, wo, fg, final):
    n, d = x2.shape
    tm = min(n, 512)
    ysp = pl.BlockSpec((tm, BRANCH), lambda i, k: (i, 0))
    return pl.pallas_call(
        functools.partial(_merge_kernel, final=final),
        out_shape=jax.ShapeDtypeStruct((n, d), F32),
        grid=(n // tm, N_BRANCH),
        in_specs=[ysp, ysp, ysp, ysp,
                  pl.BlockSpec((tm, BRANCH), lambda i, k: (i, O_GATE // BRANCH + k)),
                  pl.BlockSpec((tm, d), lambda i, k: (i, O_MERGE // d + k)),
                  pl.BlockSpec((tm, d), lambda i, k: (i, 0)),
                  pl.BlockSpec((1, BRANCH, d), lambda i, k: (k, 0, 0)),
                  pl.BlockSpec((1, 1, d), lambda i, k: (k, 0, 0)),
                  pl.BlockSpec((d, d), lambda i, k: (0, 0)),
                  pl.BlockSpec((1, d), lambda i, k: (0, 0))],
        out_specs=pl.BlockSpec((tm, d), lambda i, k: (i, 0)),
        scratch_shapes=[pltpu.VMEM((tm, d), F32)],
        compiler_params=_cparams(("parallel", "arbitrary"), 56),
        name="merge_out",
    )(*ys, proj, proj, x2, wb, bm, wo, fg)


_SRC_SSM = RW_IN // LANES
_SRC_QLAT = _SRC_SSM + BRANCH // LANES
_SRC_KVLAT = _SRC_QLAT + MLA_Q_LORA // LANES
_SRC_KPE = _SRC_KVLAT + MLA_KV_LORA // LANES
_SRC_GATE = _SRC_KPE + 3 * BRANCH // LANES
_SRC_MERGE = _SRC_GATE + N_BRANCH * BRANCH // LANES
_SRC_LORA = 3 * BRANCH // LANES
_PW_BLK = 4 * LANES
_PW_SHIFTED = (O_RWR - O_MERGE) // _PW_BLK
_PW_LAST = PROJ_W // _PW_BLK - 1
_PW_TAIL = {_PW_LAST - 1: (_SRC_QLAT, _SRC_QLAT + 1, _SRC_QLAT + 2, _SRC_LORA, _SRC_LORA),
            _PW_LAST: (_SRC_KVLAT, _SRC_KVLAT + 1, _SRC_KPE, _SRC_KPE, _SRC_KPE)}


def _permute_src_block(j, w):
    nm, ng = N_BRANCH * D_MODEL // _PW_BLK, N_BRANCH * BRANCH // _PW_BLK
    shifted = jnp.where(j < nm, _SRC_MERGE + 4 * j,
                        jnp.where(j < nm + ng, _SRC_GATE + 4 * (j - nm), _SRC_KPE + 4 * (j - nm - ng))) + w
    aligned = jnp.where(j < _PW_LAST - 2, 4 * (j - _PW_SHIFTED), _SRC_SSM) + min(w, 3)
    blk = jnp.where(j < _PW_SHIFTED, shifted, aligned)
    for jj, src in _PW_TAIL.items():
        blk = jnp.where(j == jj, src[w], blk)
    return blk


def _permute_kernel(x0_ref, x1_ref, x2_ref, x3_ref, x4_ref, o_ref):
    j = pl.program_id(1)
    wins = (x0_ref, x1_ref, x2_ref, x3_ref, x4_ref)
    lane = lax.broadcasted_iota(jnp.int32, (D_MODEL, LANES), 1)
    half = MLA_ROPE // 2

    @pl.when(j < _PW_SHIFTED)
    def _():
        for k in range(4):
            lo = pltpu.roll(wins[k][0], LANES - MLA_ROPE, 1)
            hi = pltpu.roll(wins[k + 1][0], LANES - MLA_ROPE, 1)
            o_ref[0, :, LANES * k:LANES * (k + 1)] = jnp.where(lane < LANES - MLA_ROPE, lo, hi).astype(BF16)

    @pl.when(jnp.logical_and(j >= _PW_SHIFTED, j < _PW_LAST))
    def _():
        for k in range(4):
            o_ref[0, :, LANES * k:LANES * (k + 1)] = wins[k][0].astype(BF16)

    @pl.when(j == _PW_LAST)
    def _():
        for k in range(2):
            o_ref[0, :, LANES * k:LANES * (k + 1)] = wins[k][0].astype(BF16)
        kpe = wins[2][0]
        o_ref[0, :, 2 * LANES:3 * LANES] = jnp.where(lane < MLA_ROPE, kpe, 0.0).astype(BF16)
        sw = jnp.where(lane < half, -pltpu.roll(kpe, LANES - half, 1),
                       jnp.where(lane < MLA_ROPE, pltpu.roll(kpe, half, 1), 0.0))
        o_ref[0, :, 3 * LANES:4 * LANES] = sw.astype(BF16)


def _permute_w_in(w_in):
    depth, d, _ = w_in.shape
    win = lambda w: pl.BlockSpec((1, d, LANES), lambda l, j, w=w: (l, 0, _permute_src_block(j, w)))
    return pl.pallas_call(
        _permute_kernel,
        out_shape=jax.ShapeDtypeStruct((depth, d, PROJ_W), BF16),
        grid=(depth, PROJ_W // _PW_BLK),
        in_specs=[win(w) for w in range(5)],
        out_specs=pl.BlockSpec((1, d, _PW_BLK), lambda l, j: (l, 0, j)),
        compiler_params=_cparams(("parallel", "parallel"), 32),
        name="permute_w_in",
    )(w_in, w_in, w_in, w_in, w_in)


def _rope_tables(pos):
    half = MLA_ROPE // 2
    inv = ROPE_BASE ** (-jnp.arange(half, dtype=F32) / half)
    ang = pos.astype(F32)[:, None] * inv
    cos, sin = jnp.cos(ang), jnp.sin(ang)
    t = pos.shape[0]
    cc = jnp.concatenate([cos, cos], axis=1)
    ss = jnp.concatenate([sin, sin], axis=1)
    tq = jnp.concatenate([jnp.ones((t, MLA_NOPE), F32), cc, ss], axis=1) * MLA_SCALE
    pad = jnp.zeros((t, LANES - MLA_ROPE), F32)
    return tq, jnp.concatenate([cc, pad], axis=1), jnp.concatenate([ss, pad], axis=1)


def _mla_weights(w_q_up, w_kv_up):
    half = MLA_ROPE // 2
    wq = w_q_up.reshape(MLA_Q_LORA, HEADS, MLA_NOPE + MLA_ROPE)
    x1 = wq[:, :, MLA_NOPE:MLA_NOPE + half]
    x2 = wq[:, :, MLA_NOPE + half:]
    wq_p = jnp.concatenate([wq, -x2, x1], axis=2).reshape(MLA_Q_LORA, HEADS * LANES).astype(BF16)
    wkv = w_kv_up.reshape(MLA_KV_LORA, HEADS, 2 * HEAD)
    wk_p = jnp.concatenate([wkv[:, :, :HEAD], jnp.zeros((MLA_KV_LORA, HEADS, HEAD), F32)], axis=2)
    wk_p = wk_p.reshape(MLA_KV_LORA, HEADS * LANES).astype(BF16)
    wv_p = wkv[:, :, HEAD:].reshape(MLA_KV_LORA, BRANCH).astype(BF16)
    return wq_p, wk_p, wv_p


def _s5_tables(lam_re, lam_im, log_dt, b_re, b_im, c_re, c_im):
    dt = jnp.exp(log_dt)[:, None]
    mag = jnp.exp(lam_re * dt)
    ang = lam_im * dt
    lb_re, lb_im = mag * jnp.cos(ang), mag * jnp.sin(ang)
    nr, ni = lb_re - 1.0, lb_im
    den = lam_re * lam_re + lam_im * lam_im
    f_re = (nr * lam_re + ni * lam_im) / den
    f_im = (ni * lam_re - nr * lam_im) / den
    bb_re = f_re[..., None] * b_re - f_im[..., None] * b_im
    bb_im = f_re[..., None] * b_im + f_im[..., None] * b_re
    eye = jnp.eye(SSM_GROUPS, dtype=F32)
    blk_in = lambda m: jnp.einsum('gpc,gh->gchp', m, eye).reshape(BRANCH, SSM_W)
    blk_out = lambda m: jnp.einsum('gcp,gh->gphc', m, eye).reshape(SSM_W, BRANCH)
    bblk = jnp.concatenate([blk_in(bb_re), blk_in(bb_im)], axis=1).astype(BF16)
    cblk = jnp.concatenate([blk_out(c_re), blk_out(-c_im)], axis=0).astype(BF16)

    def power(j):
        m = jnp.exp(lam_re * dt * j)
        return (m * jnp.cos(ang * j)).reshape(1, SSM_W), (m * jnp.sin(ang * j)).reshape(1, SSM_W)

    lam = jnp.concatenate([p for j in (1, 2, 4) for p in power(j)] + [jnp.zeros((2, SSM_W), F32)], axis=0)
    pws = [power(j) for j in range(1, 9)]
    pw = jnp.concatenate([p[0] for p in pws] + [p[1] for p in pws], axis=0)
    return bblk, cblk, lam, pw


def _pair_states(s):
    b = s.shape[0]
    s = s.reshape(b, 4, 2, HEAD, HEAD)
    return jnp.einsum('bjivk,ih->bjivhk', s, jnp.eye(2, dtype=s.dtype)).reshape(b, 4, LANES, LANES)


def _unpair_states(s):
    b = s.shape[0]
    s = s.reshape(b, 4, 2, HEAD, 2, HEAD)
    return jnp.stack([s[:, :, 0, :, 0, :], s[:, :, 1, :, 1, :]], axis=2).reshape(b, HEADS, HEAD, HEAD)


def _layer(x, pos0, st, lw, final_g, final):
    b, t, d = x.shape
    n = b * t
    x2 = x.reshape(n, d)
    proj = _norm_proj(x2, lw["norm_g"], lw["w_in_all"], lw["layer"])
    proj3 = proj.reshape(b, t, PROJ_W)
    past = 0 if st["ckv"] is None else st["ckv"].shape[1]

    chunk = min(t, CHUNK)
    y_rw, s_new = _rwkv(proj3, st["sp_r"], st["sp_k"], st["sp_v"], st["sp_l"], _pair_states(st["wkv"]),
                        lw["rw_pvec"], lw["rw_mu_l"], lw["rw_w2p"], lw["rw_a2p"], lw["g2"], chunk)
    shift_new = jnp.concatenate([proj3[:, t - 1:, O_RWR:O_RWR + 3 * BRANCH],
                                 proj3[:, t - 1:, O_LORA:O_LORA + 2 * RW_LORA]], axis=-1)
    wkv_new = _unpair_states(s_new)

    y_ssm, hr, hi = _s5(proj3, st["ssm_re"], st["ssm_im"], lw["ssm_bblk"], lw["ssm_cblk"], lw["ssm_lam"],
                        lw["ssm_pw"], lw["ssm_d"], lw["ssm_wglu"], lw["ssm_bglu"])
    ssm_re_new = hr.reshape(b, SSM_GROUPS, SSM_STATE)
    ssm_im_new = hi.reshape(b, SSM_GROUPS, SSM_STATE)

    pos = pos0 + jnp.arange(t, dtype=jnp.int32)
    tq_tab, tc_tab, ts_tab = _rope_tables(pos)
    q, ckv, kr = _mla_prep(proj, t, tq_tab, tc_tab, ts_tab, lw["mla_gq"], lw["mla_gkv"], lw["mla_wq"])
    ckv3 = ckv.reshape(b, t, MLA_KV_LORA)
    kr3 = kr.reshape(b, t, MLA_ROPE)
    s_pad = -(-(past + t) // LANES) * LANES
    kc, vv = _mla_kv(ckv, kr, lw["mla_wk"], lw["mla_wv"], lw["dup"])
    kc3, vv3 = kc.reshape(b, t, HEADS * LANES), vv.reshape(b, t, BRANCH)
    kcp3 = vvp3 = None
    if past:
        kcp, vvp = _mla_kv(st["ckv"].reshape(b * past, MLA_KV_LORA), st["kpe"].reshape(b * past, MLA_ROPE),
                           lw["mla_wk"], lw["mla_wv"], lw["dup"])
        kcp3, vvp3 = kcp.reshape(b, past, HEADS * LANES), vvp.reshape(b, past, BRANCH)
    small = t <= 64
    assert small or not past
    tq = min(t, 256)
    hp = 4 if small else 1
    y_mla = _mla_attn(q.reshape(b, t, HEADS * LANES), kc3, vv3, kcp3, vvp3, s_pad, tq,
                      s_pad if small else min(512, s_pad), hp)

    sbk = proj3[:, :, O_SBK:O_SBK + BRANCH]
    sbv = proj3[:, :, O_SBV:O_SBV + BRANCH]
    tk_sb = s_pad if small else min(256, s_pad)
    idx = jnp.arange(tk_sb)
    us = (idx[:, None] > idx[None, :]).astype(BF16)
    y_sb = _sb_attn(proj3, st["sbk"], st["sbv"], us, s_pad, tq, tk_sb, hp)

    ys = [y.reshape(n, BRANCH) for y in (y_rw, y_ssm, y_mla, y_sb)]
    x_new = _merge(ys, proj, x2, lw["w_branch"], lw["b_merge"], lw["w_out"], final_g, final).reshape(b, t, d)
    new_state = (shift_new, wkv_new, ssm_re_new, ssm_im_new, ckv3, kr3,
                 sbk.reshape(b, t, HEADS, HEAD), sbv.reshape(b, t, HEADS, HEAD))
    return x_new, new_state


def kernel(x_prompt, x_sample, state_rwkv_shift, state_rwkv_wkv, state_ssm_re, state_ssm_im, cache_mla_ckv, cache_mla_kpe, cache_sb_k, cache_sb_v, norm_g, w_in, rw_mu, rw_w0, rw_w2, rw_a0, rw_a2, rw_k_k, rw_k_a, rw_r_k, rw_lnx_g, rw_lnx_b, ssm_lam_re, ssm_lam_im, ssm_log_dt, ssm_b_re, ssm_b_im, ssm_c_re, ssm_c_im, ssm_d, ssm_w_glu, ssm_b_glu, mla_q_norm, mla_w_q_up, mla_kv_norm, mla_w_kv_up, w_branch, b_merge, w_out, final_norm_g):
    depth = w_in.shape[0]
    bp, tp, _ = x_prompt.shape
    bs = x_sample.shape[0]
    past = cache_mla_ckv.shape[2]

    w_in_p = _permute_w_in(w_in)
    lane_i = jnp.arange(LANES)
    g2 = ((lane_i[:, None] // HEAD) == (lane_i[None, :] // HEAD)).astype(BF16)
    rope_i = jnp.arange(MLA_ROPE)
    dup = ((lane_i[None, :] == rope_i[:, None] + MLA_NOPE)
           | (lane_i[None, :] == rope_i[:, None] + MLA_NOPE + MLA_ROPE)).astype(BF16)
    final_g = final_norm_g.reshape(1, D_MODEL)
    zpad = jnp.zeros((RW_LORA, BRANCH), F32)

    layers = []
    for l in range(depth):
        wq_p, wk_p, wv_p = _mla_weights(mla_w_q_up[l], mla_w_kv_up[l])
        bblk, cblk, lam, pw = _s5_tables(ssm_lam_re[l], ssm_lam_im[l], ssm_log_dt[l], ssm_b_re[l], ssm_b_im[l],
                                         ssm_c_re[l], ssm_c_im[l])
        mu = rw_mu[l]
        rows = [mu[:BRANCH], mu[BRANCH:2 * BRANCH], mu[2 * BRANCH:3 * BRANCH], rw_w0[l], rw_a0[l], rw_k_k[l],
                rw_k_a[l], rw_r_k[l].reshape(BRANCH), rw_lnx_g[l], rw_lnx_b[l]]
        pvec = jnp.concatenate([jnp.stack(rows), jnp.zeros((16 - len(rows), BRANCH), F32)], axis=0)
        layers.append(dict(
            norm_g=norm_g[l].reshape(1, D_MODEL), w_in_all=w_in_p, layer=l,
            rw_pvec=pvec, rw_mu_l=mu[3 * BRANCH:].reshape(1, 2 * RW_LORA),
            rw_w2p=jnp.concatenate([rw_w2[l], zpad], axis=0).astype(BF16),
            rw_a2p=jnp.concatenate([zpad, rw_a2[l]], axis=0).astype(BF16),
            g2=g2, dup=dup,
            ssm_bblk=bblk, ssm_cblk=cblk, ssm_lam=lam, ssm_pw=pw,
            ssm_d=ssm_d[l].reshape(1, BRANCH), ssm_wglu=ssm_w_glu[l].astype(BF16),
            ssm_bglu=ssm_b_glu[l].reshape(1, BRANCH),
            mla_gq=mla_q_norm[l].reshape(1, MLA_Q_LORA), mla_gkv=mla_kv_norm[l].reshape(1, MLA_KV_LORA),
            mla_wq=wq_p, mla_wk=wk_p, mla_wv=wv_p,
            w_branch=w_branch[l].astype(BF16), b_merge=b_merge[l].reshape(N_BRANCH, 1, D_MODEL),
            w_out=w_out[l].astype(BF16)))

    def fresh(bn):
        return dict(sp_r=jnp.zeros((bn, 1, BRANCH), F32), sp_k=jnp.zeros((bn, 1, BRANCH), F32),
                    sp_v=jnp.zeros((bn, 1, BRANCH), F32), sp_l=jnp.zeros((bn, 1, 2 * RW_LORA), F32),
                    wkv=jnp.zeros((bn, HEADS, HEAD, HEAD), F32),
                    ssm_re=jnp.zeros((bn, 1, SSM_W), F32), ssm_im=jnp.zeros((bn, 1, SSM_W), F32),
                    ckv=None, kpe=None, sbk=None, sbv=None)

    def carried(l):
        sh = state_rwkv_shift[l]
        return dict(sp_r=sh[:, :, :BRANCH], sp_k=sh[:, :, BRANCH:2 * BRANCH], sp_v=sh[:, :, 2 * BRANCH:3 * BRANCH],
                    sp_l=sh[:, :, 3 * BRANCH:], wkv=state_rwkv_wkv[l],
                    ssm_re=state_ssm_re[l].reshape(bs, 1, SSM_W), ssm_im=state_ssm_im[l].reshape(bs, 1, SSM_W),
                    ckv=cache_mla_ckv[l], kpe=cache_mla_kpe[l],
                    sbk=cache_sb_k[l].reshape(bs, past, BRANCH), sbv=cache_sb_v[l].reshape(bs, past, BRANCH))

    xp, xs = x_prompt, x_sample
    new_p, new_s = [], []
    for l in range(depth):
        last = l == depth - 1
        xp, st_p = _layer(xp, 0, fresh(bp), layers[l], final_g, last)
        xs, st_s = _layer(xs, past, carried(l), layers[l], final_g, last)
        new_p.append(st_p)
        new_s.append(st_s)
    stk = lambda lst, i: jnp.stack([s[i] for s in lst], axis=0)
    return (xp, xs) + tuple(stk(new_p, i) for i in range(8)) + tuple(stk(new_s, i) for i in range(8))
```

```python
import functools
import math

import jax
import jax.numpy as jnp
from jax import lax
from jax.experimental import pallas as pl
from jax.experimental.pallas import tpu as pltpu

F32 = jnp.float32
BF16 = jnp.bfloat16

D_MODEL = 2048
BRANCH = 512
N_BRANCH = 4
EPS = 1e-6
CHUNK = 64
HEAD = 64
HEADS = 8
RW_LORA = 64
RW_IN = 3 * BRANCH + 2 * RW_LORA
RW_GN_EPS = 64e-5
SSM_GROUP = 16
SSM_GROUPS = 32
SSM_STATE = 64
SSM_W = SSM_GROUPS * SSM_STATE
MLA_NOPE = 64
MLA_ROPE = 32
MLA_Q_LORA = 384
MLA_KV_LORA = 256
MLA_SCALE = 1.0 / math.sqrt(MLA_NOPE + MLA_ROPE)
ROPE_BASE = 10000.0
SB_SCALE = 1.0 / math.sqrt(HEAD)
LANES = 128
MIB = 1024 * 1024

O_MERGE = 0
O_GATE = 8192
O_SBQ = 10240
O_SBK = 10752
O_SBV = 11264
O_RWR = 11776
O_RWK = 12288
O_RWV = 12800
O_SSM = 13312
O_QLAT = 13824
O_LORA = 14208
O_KVLAT = 14336
O_KPE = 14592
O_KPESW = 14720
PROJ_W = 14848


def _cparams(sem, vmem_mib):
    return pltpu.CompilerParams(dimension_semantics=sem, vmem_limit_bytes=vmem_mib * MIB)


def _dot(a, b):
    return jnp.dot(a, b, preferred_element_type=F32)


def _dot_nt(a, b):
    return lax.dot_general(a, b, (((1,), (1,)), ((), ())), preferred_element_type=F32)


def _split3(x):
    h1 = x.astype(BF16)
    r1 = x - h1.astype(F32)
    h2 = r1.astype(BF16)
    h3 = (r1 - h2.astype(F32)).astype(BF16)
    return h1, h2, h3


def _softplus(x):
    return jnp.maximum(x, 0.0) + jnp.log1p(jnp.exp(-jnp.abs(x)))


def _norm_proj_kernel(x_ref, g_ref, w_ref, o_ref, h_ref):
    @pl.when(pl.program_id(1) == 0)
    def _():
        x = x_ref[...]
        ms = jnp.mean(x * x, axis=-1, keepdims=True)
        h_ref[...] = (x * lax.rsqrt(ms + EPS) * g_ref[...]).astype(BF16)

    o_ref[...] = _dot(h_ref[...], w_ref[0])


def _norm_proj(x2, g, w_all, layer):
    n, d = x2.shape
    width = w_all.shape[2]
    tm = min(n, 1024)
    tn = 512
    return pl.pallas_call(
        _norm_proj_kernel,
        out_shape=jax.ShapeDtypeStruct((n, width), F32),
        grid=(n // tm, width // tn),
        in_specs=[
            pl.BlockSpec((tm, d), lambda i, j: (i, 0)),
            pl.BlockSpec((1, d), lambda i, j: (0, 0)),
            pl.BlockSpec((1, d, tn), lambda i, j: (layer, 0, j)),
        ],
        out_specs=pl.BlockSpec((tm, tn), lambda i, j: (i, j)),
        scratch_shapes=[pltpu.VMEM((tm, d), BF16)],
        compiler_params=_cparams(("parallel", "arbitrary"), 40),
        name="norm_proj",
    )(x2, g, w_all)


def _segsum(x, g2):
    outs = []
    for j in range(BRANCH // LANES):
        h1, h2, h3 = _split3(x[:, LANES * j:LANES * (j + 1)])
        outs.append(_dot(h1, g2) + _dot(h2, g2) + _dot(h3, g2))
    return jnp.concatenate(outs, axis=1)


def _rwkv_kernel(pr_ref, pk_ref, pv_ref, plo_ref, spr_ref, spk_ref, spv_ref, spl_ref, s0_ref,
                 pvec_ref, mul_ref, w2_ref, a2_ref, g2_ref,
                 y_ref, sout_ref,
                 st_ref, cr_ref, ck_ref, cv_ref, cl_ref, *, C):
    c = pl.program_id(1)

    @pl.when(c == 0)
    def _():
        st_ref[...] = s0_ref[0]
        cr_ref[...] = spr_ref[0]
        ck_ref[...] = spk_ref[0]
        cv_ref[...] = spv_ref[0]
        cl_ref[...] = spl_ref[0]

    pvec = pvec_ref[...]
    prow = lambda i: pvec[i:i + 1, :]
    g2 = g2_ref[...]

    def tshift(p, carry_ref, mu):
        rolled = pltpu.roll(p, 1, 0)
        rid = lax.broadcasted_iota(jnp.int32, p.shape, 0)
        prev = jnp.where(rid == 0, carry_ref[...], rolled)
        carry_ref[...] = p[C - 1:C, :]
        return p + (prev - p) * mu

    r = tshift(pr_ref[0], cr_ref, prow(0))
    k = tshift(pk_ref[0], ck_ref, prow(1))
    v = tshift(pv_ref[0], cv_ref, prow(2))
    lo = tshift(plo_ref[0], cl_ref, mul_ref[...])

    log_w = -_softplus(-(prow(3) + _dot(jnp.tanh(lo).astype(BF16), w2_ref[...]))) - 0.5
    ld = -jnp.exp(log_w)
    a_icl = jax.nn.sigmoid(prow(4) + _dot(lo.astype(BF16), a2_ref[...]))
    kkr = k * prow(5)
    kk = kkr / jnp.maximum(jnp.sqrt(_segsum(kkr * kkr, g2)), 1e-12)
    k2 = k * (1.0 + (a_icl - 1.0) * prow(6))
    av = -kk
    bv = kk * a_icl

    ri = lax.broadcasted_iota(jnp.int32, (C, C), 0)
    ci = lax.broadcasted_iota(jnp.int32, (C, C), 1)
    lincl = (ci <= ri).astype(BF16)
    h1, h2, h3 = _split3(ld)
    g = _dot(lincl, h1) + _dot(lincl, h2) + _dot(lincl, h3)
    g_last = g[C - 1:C, :]
    eg = jnp.exp(g)
    eng = jnp.exp(-g)
    at = av * jnp.exp(g - ld)
    rt = r * eg
    bt = bv * eng
    kt = k2 * eng
    e_c = jnp.exp(g_last - g)
    b_end = bv * e_c
    k_end = k2 * e_c
    eg_last = jnp.exp(g_last)

    c2 = 2 * C
    npair = BRANCH // LANES
    sh = C.bit_length() - 1
    r_i = lax.broadcasted_iota(jnp.int32, (c2, c2), 0)
    c_i = lax.broadcasted_iota(jnp.int32, (c2, c2), 1)
    c_m = jnp.where(c_i >= C, c_i - C, c_i)
    mhalf = c_m < jnp.where(r_i < C, r_i, r_i - C + 1)
    mfull = jnp.concatenate([mhalf, mhalf], axis=0)
    bd2 = (r_i >> sh) == (c_i >> sh)
    eye2 = (r_i == c_i).astype(F32)
    lane = lax.broadcasted_iota(jnp.int32, (1, LANES), 1)
    m0 = (lane < HEAD).astype(F32)
    m1 = (lane >= HEAD).astype(F32)
    v_i = lax.broadcasted_iota(jnp.int32, (LANES, LANES), 0)
    k_i = lax.broadcasted_iota(jnp.int32, (LANES, LANES), 1)
    bdmask = (v_i >> 6) == (k_i >> 6)
    zf = jnp.zeros((C, LANES), F32)
    zc = jnp.zeros((C, LANES), BF16)
    sls = [slice(LANES * j, LANES * (j + 1)) for j in range(npair)]

    ws, a_ss, s2s = [], [], []
    for sl, j in zip(sls, range(npair)):
        a0, r0, a1, r1 = at[:, sl] * m0, rt[:, sl] * m0, at[:, sl] * m1, rt[:, sl] * m1
        lhs_w = jnp.concatenate([jnp.concatenate([a0, zf], axis=1), jnp.concatenate([r0, zf], axis=1),
                                 jnp.concatenate([zf, a1], axis=1), jnp.concatenate([zf, r1], axis=1)],
                                axis=0).astype(BF16)
        btj, ktj = bt[:, sl], kt[:, sl]
        rhs_w = jnp.concatenate([jnp.concatenate([btj, ktj], axis=0),
                                 jnp.concatenate([ktj, btj], axis=0)], axis=1).astype(BF16)
        ws.append(jnp.where(mfull, _dot_nt(lhs_w, rhs_w), 0.0))
        lhs_s = jnp.concatenate([a0, r0, a1, r1], axis=0).astype(BF16)
        s2 = st_ref[j]
        s2h, s2l, _ = _split3(s2)
        a_ss.append(_dot_nt(lhs_s, s2h) + _dot_nt(lhs_s, s2l))
        s2s.append(s2)

    n_pows = [jnp.where(bd2, jnp.concatenate([w[:C], w[c2:c2 + C]], axis=0), 0.0) for w in ws]
    t_mats = [eye2 + n for n in n_pows]
    lv = 2
    while lv < C:
        n_pows = [_dot(n.astype(BF16), n.astype(BF16)) for n in n_pows]
        t_mats = [t + _dot(t.astype(BF16), n.astype(BF16)) for t, n in zip(t_mats, n_pows)]
        lv *= 2

    wbs = [w.astype(BF16) for w in ws]
    v0s = [(v[:, sl] * m0).astype(BF16) for sl in sls]
    v1s = [(v[:, sl] * m1).astype(BF16) for sl in sls]
    xs = [jnp.concatenate([a_s[:C] + _dot(wb[:C], jnp.concatenate([zc, v0], axis=0)),
                           a_s[c2:c2 + C] + _dot(wb[c2:c2 + C], jnp.concatenate([v1, zc], axis=0))], axis=0)
          for a_s, wb, v0, v1 in zip(a_ss, wbs, v0s, v1s)]
    us = [_dot(t.astype(BF16), x.astype(BF16)) for t, x in zip(t_mats, xs)]
    ys = []
    for j, sl in enumerate(sls):
        a_s, wb, u = a_ss[j], wbs[j], us[j]
        u0, u1 = u[:C], u[C:]
        y0 = a_s[C:c2] + _dot(wb[C:c2], jnp.concatenate([u0.astype(BF16), v0s[j]], axis=0))
        y1 = a_s[c2 + C:] + _dot(wb[c2 + C:], jnp.concatenate([v1s[j], u1.astype(BF16)], axis=0))
        ys.append(y0 + y1)
        uv = jnp.concatenate([u0 + u1, v[:, sl]], axis=0)
        bk = jnp.concatenate([b_end[:, sl], k_end[:, sl]], axis=0).astype(BF16)
        upd = _dot(uv.T.astype(BF16), bk)
        st_ref[j] = s2s[j] * eg_last[:, sl] + jnp.where(bdmask, upd, 0.0)
    y = jnp.concatenate(ys, axis=1)

    inv_n = 1.0 / HEAD
    yc = y - _segsum(y, g2) * inv_n
    yn = yc * lax.rsqrt(_segsum(yc * yc, g2) * inv_n + RW_GN_EPS) * prow(8) + prow(9)
    bonus = _segsum(r * k2 * prow(7), g2) * v
    y_ref[0] = (yn + bonus).astype(y_ref.dtype)

    @pl.when(c == pl.num_programs(1) - 1)
    def _():
        sout_ref[0] = st_ref[...]


def _rwkv(proj3, sp_r, sp_k, sp_v, sp_l, s0, pvec, mu_l, w2p, a2p, g2, chunk):
    b, t, _ = proj3.shape
    nc = t // chunk
    cb = lambda off, wdt: off // wdt
    bspec = lambda off, wdt: pl.BlockSpec((1, chunk, wdt), lambda i, c, o=cb(off, wdt): (i, c, o))
    row3 = lambda wdt: pl.BlockSpec((1, 1, wdt), lambda i, c: (i, 0, 0))
    full2 = lambda a: pl.BlockSpec(a.shape, lambda i, c: (0, 0))
    return pl.pallas_call(
        functools.partial(_rwkv_kernel, C=chunk),
        out_shape=(jax.ShapeDtypeStruct((b, t, BRANCH), BF16),
                   jax.ShapeDtypeStruct((b, 4, LANES, LANES), F32)),
        grid=(b, nc),
        in_specs=[
            bspec(O_RWR, BRANCH), bspec(O_RWK, BRANCH), bspec(O_RWV, BRANCH), bspec(O_LORA, LANES),
            row3(BRANCH), row3(BRANCH), row3(BRANCH), row3(LANES),
            pl.BlockSpec((1, 4, LANES, LANES), lambda i, c: (i, 0, 0, 0)),
            full2(pvec), full2(mu_l), full2(w2p), full2(a2p), full2(g2),
        ],
        out_specs=(pl.BlockSpec((1, chunk, BRANCH), lambda i, c: (i, c, 0)),
                   pl.BlockSpec((1, 4, LANES, LANES), lambda i, c: (i, 0, 0, 0))),
        scratch_shapes=[pltpu.VMEM((4, LANES, LANES), F32),
                        pltpu.VMEM((1, BRANCH), F32), pltpu.VMEM((1, BRANCH), F32),
                        pltpu.VMEM((1, BRANCH), F32), pltpu.VMEM((1, LANES), F32)],
        compiler_params=_cparams(("parallel", "arbitrary"), 32),
        name="rwkv7",
    )(proj3, proj3, proj3, proj3, sp_r, sp_k, sp_v, sp_l, s0, pvec, mu_l, w2p, a2p, g2)


def _s5_kernel(u_ref, h0r_ref, h0i_ref, bblk_ref, cblk_ref, lam_ref, pw_ref, dsk_ref, wglu_ref, bglu_ref,
               y_ref, hro_ref, hio_ref, x_ref, cr_ref, ci_ref, *, tb):
    @pl.when(pl.program_id(1) == 0)
    def _():
        cr_ref[...] = h0r_ref[0]
        ci_ref[...] = h0i_ref[0]

    u = u_ref[0]
    x = _dot(u.astype(BF16), bblk_ref[...])
    xr = x[:, :SSM_W]
    xi = x[:, SSM_W:]
    rid = lax.broadcasted_iota(jnp.int32, (tb, 1), 0) & 7
    for lvl, s in enumerate((1, 2, 4)):
        ar = lam_ref[2 * lvl:2 * lvl + 1, :]
        ai = lam_ref[2 * lvl + 1:2 * lvl + 2, :]
        sr = pltpu.roll(xr, s, 0)
        si = pltpu.roll(xi, s, 0)
        valid = rid >= s
        nr = xr + jnp.where(valid, ar * sr - ai * si, 0.0)
        ni = xi + jnp.where(valid, ar * si + ai * sr, 0.0)
        xr, xi = nr, ni
    x_ref[:, :SSM_W] = xr
    x_ref[:, SSM_W:] = xi
    pwr = pw_ref[0:8, :]
    pwi = pw_ref[8:16, :]

    def body(gi, carry):
        cr, ci = carry
        o = pl.multiple_of(gi * 8, 8)
        br = x_ref[pl.ds(o, 8), 0:SSM_W]
        bi = x_ref[pl.ds(o, 8), SSM_W:2 * SSM_W]
        br = br + pwr * cr - pwi * ci
        bi = bi + pwr * ci + pwi * cr
        x_ref[pl.ds(o, 8), 0:SSM_W] = br
        x_ref[pl.ds(o, 8), SSM_W:2 * SSM_W] = bi
        return br[7:8, :], bi[7:8, :]

    cr, ci = lax.fori_loop(0, tb // 8, body, (cr_ref[...], ci_ref[...]))
    cr_ref[...] = cr
    ci_ref[...] = ci
    hro_ref[0] = cr
    hio_ref[0] = ci
    y = _dot(x_ref[...].astype(BF16), cblk_ref[...]) + dsk_ref[...] * u
    g = jax.nn.gelu(y)
    out = g * jax.nn.sigmoid(_dot(g.astype(BF16), wglu_ref[...]) + bglu_ref[...])
    y_ref[0] = out.astype(y_ref.dtype)


def _s5(proj3, h0r, h0i, bblk, cblk, lam, pw, dsk, wglu, bglu):
    b, t, _ = proj3.shape
    tb = min(t, 128)
    full2 = lambda a: pl.BlockSpec(a.shape, lambda i, c: (0, 0))
    row3 = pl.BlockSpec((1, 1, SSM_W), lambda i, c: (i, 0, 0))
    return pl.pallas_call(
        functools.partial(_s5_kernel, tb=tb),
        out_shape=(jax.ShapeDtypeStruct((b, t, BRANCH), BF16),
                   jax.ShapeDtypeStruct((b, 1, SSM_W), F32),
                   jax.ShapeDtypeStruct((b, 1, SSM_W), F32)),
        grid=(b, t // tb),
        in_specs=[pl.BlockSpec((1, tb, BRANCH), lambda i, c: (i, c, O_SSM // BRANCH)),
                  row3, row3, full2(bblk), full2(cblk), full2(lam), full2(pw), full2(dsk),
                  full2(wglu), full2(bglu)],
        out_specs=(pl.BlockSpec((1, tb, BRANCH), lambda i, c: (i, c, 0)), row3, row3),
        scratch_shapes=[pltpu.VMEM((tb, 2 * SSM_W), F32),
                        pltpu.VMEM((1, SSM_W), F32), pltpu.VMEM((1, SSM_W), F32)],
        compiler_params=_cparams(("parallel", "arbitrary"), 48),
        name="s5",
    )(proj3, h0r, h0i, bblk, cblk, lam, pw, dsk, wglu, bglu)


def _mla_prep_kernel(ql_ref, kvl_ref, kpe_ref, kpesw_ref, tq_ref, tc_ref, ts_ref,
                     gq_ref, gkv_ref, wq_ref, q_ref, ckv_ref, kr_ref):
    ql = ql_ref[...]
    qn = ql * lax.rsqrt(jnp.mean(ql * ql, axis=-1, keepdims=True) + EPS) * gq_ref[...]
    q = _dot(qn.astype(BF16), wq_ref[...])
    tq = tq_ref[...]
    q_ref[...] = jnp.concatenate(
        [q[:, LANES * h:LANES * (h + 1)] * tq for h in range(HEADS)], axis=1).astype(BF16)
    kvl = kvl_ref[...]
    ckv_ref[...] = kvl * lax.rsqrt(jnp.mean(kvl * kvl, axis=-1, keepdims=True) + EPS) * gkv_ref[...]
    kr = kpe_ref[...] * tc_ref[...] + kpesw_ref[...] * ts_ref[...]
    kr_ref[...] = kr[:, :MLA_ROPE]


def _mla_prep(proj, t, tq_tab, tc_tab, ts_tab, gq, gkv, wq):
    n = proj.shape[0]
    tm = min(t, 512)
    nt = t // tm
    tab = pl.BlockSpec((tm, LANES), lambda i: (i % nt, 0))
    full2 = lambda a: pl.BlockSpec(a.shape, lambda i: (0, 0))
    return pl.pallas_call(
        _mla_prep_kernel,
        out_shape=(jax.ShapeDtypeStruct((n, HEADS * LANES), BF16),
                   jax.ShapeDtypeStruct((n, MLA_KV_LORA), F32),
                   jax.ShapeDtypeStruct((n, MLA_ROPE), F32)),
        grid=(n // tm,),
        in_specs=[pl.BlockSpec((tm, MLA_Q_LORA), lambda i: (i, O_QLAT // MLA_Q_LORA)),
                  pl.BlockSpec((tm, MLA_KV_LORA), lambda i: (i, O_KVLAT // MLA_KV_LORA)),
                  pl.BlockSpec((tm, LANES), lambda i: (i, O_KPE // LANES)),
                  pl.BlockSpec((tm, LANES), lambda i: (i, O_KPESW // LANES)),
                  tab, tab, tab, full2(gq), full2(gkv), full2(wq)],
        out_specs=(pl.BlockSpec((tm, HEADS * LANES), lambda i: (i, 0)),
                   pl.BlockSpec((tm, MLA_KV_LORA), lambda i: (i, 0)),
                   pl.BlockSpec((tm, MLA_ROPE), lambda i: (i, 0))),
        compiler_params=_cparams(("parallel",), 32),
        name="mla_prep",
    )(proj, proj, proj, proj, tq_tab, tc_tab, ts_tab, gq, gkv, wq)


def _mla_kv_kernel(ckv_ref, kr_ref, wk_ref, wv_ref, dup_ref, k_ref, v_ref):
    cb = ckv_ref[...].astype(BF16)
    kn = _dot(cb, wk_ref[...])
    krd = _dot(kr_ref[...].astype(BF16), dup_ref[...])
    k_ref[...] = jnp.concatenate(
        [kn[:, LANES * h:LANES * (h + 1)] + krd for h in range(HEADS)], axis=1).astype(BF16)
    v_ref[...] = _dot(cb, wv_ref[...]).astype(BF16)


def _mla_kv(ckv, kr, wk, wv, dup):
    n = ckv.shape[0]
    tm = next((c for c in (512, LANES) if n % c == 0), n)
    full2 = lambda a: pl.BlockSpec(a.shape, lambda i: (0, 0))
    return pl.pallas_call(
        _mla_kv_kernel,
        out_shape=(jax.ShapeDtypeStruct((n, HEADS * LANES), BF16),
                   jax.ShapeDtypeStruct((n, BRANCH), BF16)),
        grid=(n // tm,),
        in_specs=[pl.BlockSpec((tm, MLA_KV_LORA), lambda i: (i, 0)),
                  pl.BlockSpec((tm, MLA_ROPE), lambda i: (i, 0)),
                  full2(wk), full2(wv), full2(dup)],
        out_specs=(pl.BlockSpec((tm, HEADS * LANES), lambda i: (i, 0)),
                   pl.BlockSpec((tm, BRANCH), lambda i: (i, 0))),
        compiler_params=_cparams(("parallel",), 32),
        name="mla_kv",
    )(ckv, kr, wk, wv, dup)


def _assemble_keys(past_ref, new_ref, all_ref, past, t):
    all_ref[0:past, :] = past_ref[0].astype(BF16)
    all_ref[past:past + t, :] = new_ref[0].astype(BF16)
    pad = all_ref.shape[0] - past - t
    if pad:
        all_ref[past + t:, :] = jnp.zeros((pad, all_ref.shape[1]), BF16)


def _mla_attn_kernel(q_ref, k_ref, v_ref, *rest, tq, tk, q_off, s_valid, hp, past):
    if past:
        kp_ref, vp_ref, o_ref, kall_ref, vall_ref = rest
        _assemble_keys(kp_ref, k_ref, kall_ref, past, tq)
        _assemble_keys(vp_ref, v_ref, vall_ref, past, tq)
        kload = lambda o: kall_ref[pl.ds(o, tk), :]
        vload = lambda o: vall_ref[pl.ds(o, tk), :]
    else:
        (o_ref,) = rest
        kload = lambda o: k_ref[0, pl.ds(o, tk), :]
        vload = lambda o: v_ref[0, pl.ds(o, tk), :]
    i = pl.program_id(2)
    q_lo = q_off + i * tq
    limit = jnp.minimum(q_lo + tq, s_valid)
    nkb = (limit + tk - 1) // tk
    qidx = q_lo + lax.broadcasted_iota(jnp.int32, (tq, tk), 0)
    kloc = lax.broadcasted_iota(jnp.int32, (tq, tk), 1)
    nh = 2 * hp
    q2 = q_ref[0]
    qs = [q2[:, LANES * h:LANES * (h + 1)] for h in range(nh)]

    def body(kb, carry, masked):
        o = pl.multiple_of(kb * tk, tk)
        kblk = kload(o)
        vblk = vload(o)
        ss = [_dot_nt(qs[h], kblk[:, LANES * h:LANES * (h + 1)]) for h in range(nh)]
        if masked:
            kidx = kloc + kb * tk
            vis = jnp.logical_and((kidx >> 6) <= (qidx >> 6), kidx < s_valid)
            ss = [jnp.where(vis, s, -1e30) for s in ss]
        m_new = [jnp.maximum(carry[h][0], jnp.max(ss[h], axis=-1, keepdims=True)) for h in range(nh)]
        ps = [jnp.exp(ss[h] - m_new[h]) for h in range(nh)]
        pv = [_dot(ps[h].astype(BF16), vblk[:, LANES * (h // 2):LANES * (h // 2 + 1)]) for h in range(nh)]
        new = []
        for h in range(nh):
            m_prev, l_prev, acc = carry[h]
            alpha = jnp.exp(m_prev - m_new[h])
            new.append((m_new[h], alpha * l_prev + jnp.sum(ps[h], axis=-1, keepdims=True), alpha * acc + pv[h]))
        return tuple(new)

    init = tuple((jnp.full((tq, 1), -1e30, F32), jnp.zeros((tq, 1), F32), jnp.zeros((tq, LANES), F32))
                 for _ in range(nh))
    nfull = jnp.minimum(((q_lo >> 6) + 1) * CHUNK, s_valid) // tk
    res = lax.fori_loop(0, nfull, lambda kb, c: body(kb, c, False), init)
    res = lax.fori_loop(nfull, nkb, lambda kb, c: body(kb, c, True), res)
    lane = lax.broadcasted_iota(jnp.int32, (tq, LANES), 1)
    outs = [jnp.where(lane < HEAD, res[2 * p][2] / res[2 * p][1], res[2 * p + 1][2] / res[2 * p + 1][1])
            for p in range(hp)]
    o_ref[0] = jnp.concatenate(outs, axis=1).astype(o_ref.dtype)


def _mla_attn(q3, k3, v3, kp3, vp3, s_pad, tq, tk, hp):
    b, t, _ = q3.shape
    past = 0 if kp3 is None else kp3.shape[1]
    rows = k3.shape[1]
    in_specs = [pl.BlockSpec((1, tq, 2 * LANES * hp), lambda bi, j, i: (bi, i, j)),
                pl.BlockSpec((1, rows, 2 * LANES * hp), lambda bi, j, i: (bi, 0, j)),
                pl.BlockSpec((1, rows, LANES * hp), lambda bi, j, i: (bi, 0, j))]
    args = [q3, k3, v3]
    scratch = []
    if past:
        in_specs += [pl.BlockSpec((1, past, 2 * LANES * hp), lambda bi, j, i: (bi, 0, j)),
                     pl.BlockSpec((1, past, LANES * hp), lambda bi, j, i: (bi, 0, j))]
        args += [kp3, vp3]
        scratch = [pltpu.VMEM((s_pad, 2 * LANES * hp), BF16), pltpu.VMEM((s_pad, LANES * hp), BF16)]
    return pl.pallas_call(
        functools.partial(_mla_attn_kernel, tq=tq, tk=tk, q_off=past, s_valid=past + t, hp=hp, past=past),
        out_shape=jax.ShapeDtypeStruct((b, t, BRANCH), BF16),
        grid=(b, 4 // hp, t // tq),
        in_specs=in_specs,
        out_specs=pl.BlockSpec((1, tq, LANES * hp), lambda bi, j, i: (bi, i, j)),
        scratch_shapes=scratch,
        compiler_params=_cparams(("parallel", "parallel", "arbitrary"), 40),
        name="mla_attn",
    )(*args)


def _sb_block(zs, vis, us, carry, pv):
    nh = len(zs)
    tq = zs[0].shape[0]
    zls = [jnp.minimum(z, 0.0) - jnp.log(1.0 + jnp.exp(-jnp.abs(z))) for z in zs]
    lgs = [zl - z for zl, z in zip(zls, zs)]
    if vis is not None:
        lgs = [jnp.where(vis, lg, 0.0) for lg in lgs]
    his = [lg.astype(BF16) for lg in lgs]
    los = [(lg - hi.astype(F32)).astype(BF16) for lg, hi in zip(lgs, his)]
    later = _dot(jnp.concatenate(his + los, axis=0), us)
    new = []
    for h in range(nh):
        csum, acc = carry[h]
        tot = later[h * tq:(h + 1) * tq] + later[(nh + h) * tq:(nh + h + 1) * tq] + csum
        a = jnp.exp(zls[h] + tot)
        if vis is not None:
            a = jnp.where(vis, a, 0.0)
        new.append((csum + jnp.sum(lgs[h], axis=-1, keepdims=True), acc + pv(h, a.astype(BF16))))
    return tuple(new)


def _sb_heads(q2, nh):
    lane1 = lax.broadcasted_iota(jnp.int32, (1, LANES), 1)
    hm = ((lane1 < HEAD).astype(F32) * SB_SCALE, (lane1 >= HEAD).astype(F32) * SB_SCALE)
    return [(q2[:, LANES * (h // 2):LANES * (h // 2 + 1)] * hm[h % 2]).astype(BF16) for h in range(nh)]


def _sb_finish(res, o_ref, tq):
    lane = lax.broadcasted_iota(jnp.int32, (tq, LANES), 1)
    outs = [jnp.where(lane < HEAD, res[2 * p][1], res[2 * p + 1][1]) for p in range(len(res) // 2)]
    o_ref[0] = jnp.concatenate(outs, axis=1).astype(o_ref.dtype)


def _sb_attn_kernel(q_ref, k_ref, v_ref, us_ref, o_ref, *, tq, tk):
    i = pl.program_id(2)
    q_lo = i * tq
    nkb = (q_lo + tq - 2) // tk + 1
    nfull = q_lo // tk
    qidx = q_lo + lax.broadcasted_iota(jnp.int32, (tq, tk), 0)
    kloc = lax.broadcasted_iota(jnp.int32, (tq, tk), 1)
    qh = _sb_heads(q_ref[0], 2)
    us = us_ref[...]

    def block(kb, carry, masked):
        o = pl.multiple_of(kb * tk, tk)
        kblk = k_ref[0, pl.ds(o, tk), :].astype(BF16)
        vblk = v_ref[0, pl.ds(o, tk), :].astype(BF16)
        zs = [_dot_nt(q, kblk) for q in qh]
        vis = (kloc + kb * tk) < qidx if masked else None
        return _sb_block(zs, vis, us, carry, lambda h, a: _dot(a, vblk))

    init = tuple((jnp.zeros((tq, 1), F32), jnp.zeros((tq, LANES), F32)) for _ in range(2))
    res = lax.fori_loop(0, nkb - nfull, lambda s, c: block(nkb - 1 - s, c, True), init)
    res = lax.fori_loop(0, nfull, lambda s, c: block(nfull - 1 - s, c, False), res)
    _sb_finish(res, o_ref, tq)


def _sb_decode_kernel(q_ref, k_ref, v_ref, kpt_ref, vpt_ref, usn_ref, usp_ref, o_ref, *, t, past, pb):
    qh = _sb_heads(q_ref[0], HEADS)
    kn = k_ref[0].astype(BF16)
    vn = v_ref[0].astype(BF16)
    pair = lambda x, h: x[:, LANES * (h // 2):LANES * (h // 2 + 1)]
    ri = lax.broadcasted_iota(jnp.int32, (t, t), 0)
    ci = lax.broadcasted_iota(jnp.int32, (t, t), 1)
    carry = tuple((jnp.zeros((t, 1), F32), jnp.zeros((t, LANES), F32)) for _ in range(HEADS))
    zs = [_dot_nt(qh[h], pair(kn, h)) for h in range(HEADS)]
    carry = _sb_block(zs, ci < ri, usn_ref[...], carry, lambda h, a: _dot(a, pair(vn, h)))
    usp = usp_ref[...]
    for blk in reversed(range(past // pb)):
        kts = [kpt_ref[0, p, :, blk * pb:(blk + 1) * pb].astype(BF16) for p in range(HEADS // 2)]
        vts = [vpt_ref[0, p, :, blk * pb:(blk + 1) * pb].astype(BF16) for p in range(HEADS // 2)]
        zs = [_dot(qh[h], kts[h // 2]) for h in range(HEADS)]
        carry = _sb_block(zs, None, usp, carry, lambda h, a: _dot_nt(a, vts[h // 2]))
    _sb_finish(carry, o_ref, t)


def _strict_upper(n):
    idx = jnp.arange(n)
    return (idx[:, None] > idx[None, :]).astype(BF16)


def _sb_attn(proj3, tq, tk):
    b, t, _ = proj3.shape
    us = _strict_upper(tk)
    return pl.pallas_call(
        functools.partial(_sb_attn_kernel, tq=tq, tk=tk),
        out_shape=jax.ShapeDtypeStruct((b, t, BRANCH), BF16),
        grid=(b, HEADS // 2, t // tq),
        in_specs=[pl.BlockSpec((1, tq, LANES), lambda bi, j, i: (bi, i, O_SBQ // LANES + j)),
                  pl.BlockSpec((1, t, LANES), lambda bi, j, i: (bi, 0, O_SBK // LANES + j)),
                  pl.BlockSpec((1, t, LANES), lambda bi, j, i: (bi, 0, O_SBV // LANES + j)),
                  pl.BlockSpec(us.shape, lambda bi, j, i: (0, 0))],
        out_specs=pl.BlockSpec((1, tq, LANES), lambda bi, j, i: (bi, i, j)),
        compiler_params=_cparams(("parallel", "parallel", "arbitrary"), 48),
        name="sb_attn",
    )(proj3, proj3, proj3, us)


def _sb_decode(proj3, kpt, vpt, pb):
    b, t, _ = proj3.shape
    past = kpt.shape[3]
    usn, usp = _strict_upper(t), _strict_upper(pb)
    new = lambda off: pl.BlockSpec((1, t, BRANCH), lambda bi, o=off // BRANCH: (bi, 0, o))
    cache = pl.BlockSpec((1, HEADS // 2, LANES, past), lambda bi: (bi, 0, 0, 0))
    return pl.pallas_call(
        functools.partial(_sb_decode_kernel, t=t, past=past, pb=pb),
        out_shape=jax.ShapeDtypeStruct((b, t, BRANCH), BF16),
        grid=(b,),
        in_specs=[new(O_SBQ), new(O_SBK), new(O_SBV), cache, cache,
                  pl.BlockSpec(usn.shape, lambda bi: (0, 0)), pl.BlockSpec(usp.shape, lambda bi: (0, 0))],
        out_specs=pl.BlockSpec((1, t, BRANCH), lambda bi: (bi, 0, 0)),
        compiler_params=_cparams(("parallel",), 48),
        name="sb_decode",
    )(proj3, proj3, proj3, kpt, vpt, usn, usp)


def _merge_kernel(yrw_ref, yssm_ref, ymla_ref, ysb_ref, gate_ref, pm_ref, x_ref, wb_ref, bm_ref, wo_ref,
                  fg_ref, o_ref, acc_ref, *, final):
    n = pl.program_id(1)

    @pl.when(n == 0)
    def _():
        acc_ref[...] = jnp.zeros_like(acc_ref)

    def branch(y_ref):
        g = gate_ref[...]
        gated = y_ref[...].astype(F32) * (g * jax.nn.sigmoid(g))
        up = _dot(gated.astype(BF16), wb_ref[0])
        acc_ref[...] += jax.nn.sigmoid(pm_ref[...] + bm_ref[0]) * up

    for idx, y_ref in enumerate((yrw_ref, yssm_ref, ymla_ref, ysb_ref)):
        pl.when(n == idx)(functools.partial(branch, y_ref))

    @pl.when(n == N_BRANCH - 1)
    def _():
        xn = x_ref[...] + _dot(acc_ref[...].astype(BF16), wo_ref[...])
        if final:
            xn = xn * lax.rsqrt(jnp.mean(xn * xn, axis=-1, keepdims=True) + EPS) * fg_ref[...]
        o_ref[...] = xn


def _merge(ys, proj, x2, wb, bm, wo, fg, final):
    n, d = x2.shape
    tm = min(n, 512)
    ysp = pl.BlockSpec((tm, BRANCH), lambda i, k: (i, 0))
    return pl.pallas_call(
        functools.partial(_merge_kernel, final=final),
        out_shape=jax.ShapeDtypeStruct((n, d), F32),
        grid=(n // tm, N_BRANCH),
        in_specs=[ysp, ysp, ysp, ysp,
                  pl.BlockSpec((tm, BRANCH), lambda i, k: (i, O_GATE // BRANCH + k)),
                  pl.BlockSpec((tm, d), lambda i, k: (i, O_MERGE // d + k)),
                  pl.BlockSpec((tm, d), lambda i, k: (i, 0)),
                  pl.BlockSpec((1, BRANCH, d), lambda i, k: (k, 0, 0)),
                  pl.BlockSpec((1, 1, d), lambda i, k: (k, 0, 0)),
                  pl.BlockSpec((d, d), lambda i, k: (0, 0)),
                  pl.BlockSpec((1, d), lambda i, k: (0, 0))],
        out_specs=pl.BlockSpec((tm, d), lambda i, k: (i, 0)),
        scratch_shapes=[pltpu.VMEM((tm, d), F32)],
        compiler_params=_cparams(("parallel", "arbitrary"), 56),
        name="merge_out",
    )(*ys, proj, proj, x2, wb, bm, wo, fg)


_SRC_SSM = RW_IN
_SRC_QLAT = _SRC_SSM + BRANCH
_SRC_KVLAT = _SRC_QLAT + MLA_Q_LORA
_SRC_KPE = _SRC_KVLAT + MLA_KV_LORA
_SRC_SB = _SRC_KPE + MLA_ROPE
_SRC_GATE = _SRC_SB + 3 * BRANCH
_SRC_MERGE = _SRC_GATE + N_BRANCH * BRANCH
_SRC_LORA = 3 * BRANCH
_PW_BLK = 4 * LANES
_PW_LAST = PROJ_W // _PW_BLK - 1
_PW_TAIL = {_PW_LAST - 1: (_SRC_QLAT, _SRC_QLAT + LANES, _SRC_QLAT + 2 * LANES, _SRC_LORA),
            _PW_LAST: (_SRC_KVLAT, _SRC_KVLAT + LANES, _SRC_KPE, _SRC_KPE)}


def _permute_src_col(j, k):
    nm = N_BRANCH * D_MODEL // _PW_BLK
    ng = nm + N_BRANCH * BRANCH // _PW_BLK
    nsb = ng + 3 * BRANCH // _PW_BLK
    nrw = nsb + 3 * BRANCH // _PW_BLK
    u = MLA_ROPE
    col = jnp.where(j < nm, _SRC_MERGE // u + _PW_BLK // u * j,
                    jnp.where(j < ng, _SRC_GATE // u + _PW_BLK // u * (j - nm),
                              jnp.where(j < nsb, _SRC_SB // u + _PW_BLK // u * (j - ng),
                                        jnp.where(j < nrw, _PW_BLK // u * (j - nsb), _SRC_SSM // u))))
    col = col + LANES // u * k
    for jj, src in _PW_TAIL.items():
        col = jnp.where(j == jj, src[k] // u, col)
    return col * u


def _permute_kernel(x0_ref, x1_ref, x2_ref, x3_ref, o_ref):
    j = pl.program_id(1)
    wins = (x0_ref, x1_ref, x2_ref, x3_ref)
    half = MLA_ROPE // 2

    def put(k, rows):
        o_ref[0, :, LANES * k:LANES * (k + 1)] = rows.T.astype(BF16)

    @pl.when(j < _PW_LAST)
    def _():
        for k in range(4):
            put(k, wins[k][0])

    @pl.when(j == _PW_LAST)
    def _():
        for k in range(2):
            put(k, wins[k][0])
        kpe = wins[2][0]
        row = lax.broadcasted_iota(jnp.int32, kpe.shape, 0)
        put(2, jnp.where(row < MLA_ROPE, kpe, 0.0))
        zeros = jnp.zeros((LANES - MLA_ROPE, kpe.shape[1]), F32)
        put(3, jnp.concatenate([-kpe[half:MLA_ROPE], kpe[:half], zeros], axis=0))


def _permute_w_in(w_in):
    depth, d, _ = w_in.shape
    wt = jnp.swapaxes(w_in, 1, 2)
    win = lambda k: pl.BlockSpec((pl.Element(1), pl.Element(LANES), pl.Element(d)),
                                 lambda l, j, k=k: (l, _permute_src_col(j, k), 0))
    return pl.pallas_call(
        _permute_kernel,
        out_shape=jax.ShapeDtypeStruct((depth, d, PROJ_W), BF16),
        grid=(depth, PROJ_W // _PW_BLK),
        in_specs=[win(k) for k in range(4)],
        out_specs=pl.BlockSpec((1, d, _PW_BLK), lambda l, j: (l, 0, j)),
        compiler_params=_cparams(("parallel", "parallel"), 32),
        name="permute_w_in",
    )(wt, wt, wt, wt)


def _rope_tables(pos):
    half = MLA_ROPE // 2
    inv = ROPE_BASE ** (-jnp.arange(half, dtype=F32) / half)
    ang = pos.astype(F32)[:, None] * inv
    cos, sin = jnp.cos(ang), jnp.sin(ang)
    t = pos.shape[0]
    cc = jnp.concatenate([cos, cos], axis=1)
    ss = jnp.concatenate([sin, sin], axis=1)
    tq = jnp.concatenate([jnp.ones((t, MLA_NOPE), F32), cc, ss], axis=1) * MLA_SCALE
    pad = jnp.zeros((t, LANES - MLA_ROPE), F32)
    return tq, jnp.concatenate([cc, pad], axis=1), jnp.concatenate([ss, pad], axis=1)


def _mla_weights(w_q_up, w_kv_up):
    half = MLA_ROPE // 2
    wq = w_q_up.reshape(MLA_Q_LORA, HEADS, MLA_NOPE + MLA_ROPE)
    x1 = wq[:, :, MLA_NOPE:MLA_NOPE + half]
    x2 = wq[:, :, MLA_NOPE + half:]
    wq_p = jnp.concatenate([wq, -x2, x1], axis=2).reshape(MLA_Q_LORA, HEADS * LANES).astype(BF16)
    wkv = w_kv_up.reshape(MLA_KV_LORA, HEADS, 2 * HEAD)
    wk_p = jnp.concatenate([wkv[:, :, :HEAD], jnp.zeros((MLA_KV_LORA, HEADS, HEAD), F32)], axis=2)
    wk_p = wk_p.reshape(MLA_KV_LORA, HEADS * LANES).astype(BF16)
    wv_p = wkv[:, :, HEAD:].reshape(MLA_KV_LORA, BRANCH).astype(BF16)
    return wq_p, wk_p, wv_p


def _s5_tables(lam_re, lam_im, log_dt, b_re, b_im, c_re, c_im):
    dt = jnp.exp(log_dt)[:, None]
    mag = jnp.exp(lam_re * dt)
    ang = lam_im * dt
    lb_re, lb_im = mag * jnp.cos(ang), mag * jnp.sin(ang)
    nr, ni = lb_re - 1.0, lb_im
    den = lam_re * lam_re + lam_im * lam_im
    f_re = (nr * lam_re + ni * lam_im) / den
    f_im = (ni * lam_re - nr * lam_im) / den
    bb_re = f_re[..., None] * b_re - f_im[..., None] * b_im
    bb_im = f_re[..., None] * b_im + f_im[..., None] * b_re
    eye = jnp.eye(SSM_GROUPS, dtype=F32)
    blk_in = lambda m: jnp.einsum('gpc,gh->gchp', m, eye).reshape(BRANCH, SSM_W)
    blk_out = lambda m: jnp.einsum('gcp,gh->gphc', m, eye).reshape(SSM_W, BRANCH)
    bblk = jnp.concatenate([blk_in(bb_re), blk_in(bb_im)], axis=1).astype(BF16)
    cblk = jnp.concatenate([blk_out(c_re), blk_out(-c_im)], axis=0).astype(BF16)

    def power(j):
        m = jnp.exp(lam_re * dt * j)
        return (m * jnp.cos(ang * j)).reshape(1, SSM_W), (m * jnp.sin(ang * j)).reshape(1, SSM_W)

    lam = jnp.concatenate([p for j in (1, 2, 4) for p in power(j)] + [jnp.zeros((2, SSM_W), F32)], axis=0)
    pws = [power(j) for j in range(1, 9)]
    pw = jnp.concatenate([p[0] for p in pws] + [p[1] for p in pws], axis=0)
    return bblk, cblk, lam, pw


def _pair_states(s):
    b = s.shape[0]
    s = s.reshape(b, 4, 2, HEAD, HEAD)
    return jnp.einsum('bjivk,ih->bjivhk', s, jnp.eye(2, dtype=s.dtype)).reshape(b, 4, LANES, LANES)


def _unpair_states(s):
    b = s.shape[0]
    s = s.reshape(b, 4, 2, HEAD, 2, HEAD)
    return jnp.stack([s[:, :, 0, :, 0, :], s[:, :, 1, :, 1, :]], axis=2).reshape(b, HEADS, HEAD, HEAD)


def _layer(x, pos0, st, lw, final_g, final):
    b, t, d = x.shape
    n = b * t
    x2 = x.reshape(n, d)
    proj = _norm_proj(x2, lw["norm_g"], lw["w_in_all"], lw["layer"])
    proj3 = proj.reshape(b, t, PROJ_W)
    past = 0 if st["ckv"] is None else st["ckv"].shape[1]

    chunk = min(t, CHUNK)
    y_rw, s_new = _rwkv(proj3, st["sp_r"], st["sp_k"], st["sp_v"], st["sp_l"], _pair_states(st["wkv"]),
                        lw["rw_pvec"], lw["rw_mu_l"], lw["rw_w2p"], lw["rw_a2p"], lw["g2"], chunk)
    shift_new = jnp.concatenate([proj3[:, t - 1:, O_RWR:O_RWR + 3 * BRANCH],
                                 proj3[:, t - 1:, O_LORA:O_LORA + 2 * RW_LORA]], axis=-1)
    wkv_new = _unpair_states(s_new)

    y_ssm, hr, hi = _s5(proj3, st["ssm_re"], st["ssm_im"], lw["ssm_bblk"], lw["ssm_cblk"], lw["ssm_lam"],
                        lw["ssm_pw"], lw["ssm_d"], lw["ssm_wglu"], lw["ssm_bglu"])
    ssm_re_new = hr.reshape(b, SSM_GROUPS, SSM_STATE)
    ssm_im_new = hi.reshape(b, SSM_GROUPS, SSM_STATE)

    pos = pos0 + jnp.arange(t, dtype=jnp.int32)
    tq_tab, tc_tab, ts_tab = _rope_tables(pos)
    q, ckv, kr = _mla_prep(proj, t, tq_tab, tc_tab, ts_tab, lw["mla_gq"], lw["mla_gkv"], lw["mla_wq"])
    ckv3 = ckv.reshape(b, t, MLA_KV_LORA)
    kr3 = kr.reshape(b, t, MLA_ROPE)
    s_pad = -(-(past + t) // LANES) * LANES
    kc, vv = _mla_kv(ckv, kr, lw["mla_wk"], lw["mla_wv"], lw["dup"])
    kc3, vv3 = kc.reshape(b, t, HEADS * LANES), vv.reshape(b, t, BRANCH)
    kcp3 = vvp3 = None
    if past:
        kcp, vvp = _mla_kv(st["ckv"].reshape(b * past, MLA_KV_LORA), st["kpe"].reshape(b * past, MLA_ROPE),
                           lw["mla_wk"], lw["mla_wv"], lw["dup"])
        kcp3, vvp3 = kcp.reshape(b, past, HEADS * LANES), vvp.reshape(b, past, BRANCH)
    small = t <= 64
    assert small or not past
    tq = min(t, 256)
    hp = 4 if small else 1
    y_mla = _mla_attn(q.reshape(b, t, HEADS * LANES), kc3, vv3, kcp3, vvp3, s_pad, tq,
                      s_pad if small else min(512, s_pad), hp)

    sbk = proj3[:, :, O_SBK:O_SBK + BRANCH]
    sbv = proj3[:, :, O_SBV:O_SBV + BRANCH]
    if past:
        y_sb = _sb_decode(proj3, st["sbk_t"], st["sbv_t"], min(256, past))
    else:
        y_sb = _sb_attn(proj3, tq, min(256, t))

    ys = [y.reshape(n, BRANCH) for y in (y_rw, y_ssm, y_mla, y_sb)]
    x_new = _merge(ys, proj, x2, lw["w_branch"], lw["b_merge"], lw["w_out"], final_g, final).reshape(b, t, d)
    new_state = (shift_new, wkv_new, ssm_re_new, ssm_im_new, ckv3, kr3,
                 sbk.reshape(b, t, HEADS, HEAD), sbv.reshape(b, t, HEADS, HEAD))
    return x_new, new_state


def kernel(x_prompt, x_sample, state_rwkv_shift, state_rwkv_wkv, state_ssm_re, state_ssm_im, cache_mla_ckv, cache_mla_kpe, cache_sb_k, cache_sb_v, norm_g, w_in, rw_mu, rw_w0, rw_w2, rw_a0, rw_a2, rw_k_k, rw_k_a, rw_r_k, rw_lnx_g, rw_lnx_b, ssm_lam_re, ssm_lam_im, ssm_log_dt, ssm_b_re, ssm_b_im, ssm_c_re, ssm_c_im, ssm_d, ssm_w_glu, ssm_b_glu, mla_q_norm, mla_w_q_up, mla_kv_norm, mla_w_kv_up, w_branch, b_merge, w_out, final_norm_g):
    depth = w_in.shape[0]
    bp, tp, _ = x_prompt.shape
    bs = x_sample.shape[0]
    past = cache_mla_ckv.shape[2]

    w_in_p = _permute_w_in(w_in)
    lane_i = jnp.arange(LANES)
    g2 = ((lane_i[:, None] // HEAD) == (lane_i[None, :] // HEAD)).astype(BF16)
    rope_i = jnp.arange(MLA_ROPE)
    dup = ((lane_i[None, :] == rope_i[:, None] + MLA_NOPE)
           | (lane_i[None, :] == rope_i[:, None] + MLA_NOPE + MLA_ROPE)).astype(BF16)
    final_g = final_norm_g.reshape(1, D_MODEL)
    zpad = jnp.zeros((RW_LORA, BRANCH), F32)

    layers = []
    for l in range(depth):
        wq_p, wk_p, wv_p = _mla_weights(mla_w_q_up[l], mla_w_kv_up[l])
        bblk, cblk, lam, pw = _s5_tables(ssm_lam_re[l], ssm_lam_im[l], ssm_log_dt[l], ssm_b_re[l], ssm_b_im[l],
                                         ssm_c_re[l], ssm_c_im[l])
        mu = rw_mu[l]
        rows = [mu[:BRANCH], mu[BRANCH:2 * BRANCH], mu[2 * BRANCH:3 * BRANCH], rw_w0[l], rw_a0[l], rw_k_k[l],
                rw_k_a[l], rw_r_k[l].reshape(BRANCH), rw_lnx_g[l], rw_lnx_b[l]]
        pvec = jnp.concatenate([jnp.stack(rows), jnp.zeros((16 - len(rows), BRANCH), F32)], axis=0)
        layers.append(dict(
            norm_g=norm_g[l].reshape(1, D_MODEL), w_in_all=w_in_p, layer=l,
            rw_pvec=pvec, rw_mu_l=mu[3 * BRANCH:].reshape(1, 2 * RW_LORA),
            rw_w2p=jnp.concatenate([rw_w2[l], zpad], axis=0).astype(BF16),
            rw_a2p=jnp.concatenate([zpad, rw_a2[l]], axis=0).astype(BF16),
            g2=g2, dup=dup,
            ssm_bblk=bblk, ssm_cblk=cblk, ssm_lam=lam, ssm_pw=pw,
            ssm_d=ssm_d[l].reshape(1, BRANCH), ssm_wglu=ssm_w_glu[l].astype(BF16),
            ssm_bglu=ssm_b_glu[l].reshape(1, BRANCH),
            mla_gq=mla_q_norm[l].reshape(1, MLA_Q_LORA), mla_gkv=mla_kv_norm[l].reshape(1, MLA_KV_LORA),
            mla_wq=wq_p, mla_wk=wk_p, mla_wv=wv_p,
            w_branch=w_branch[l].astype(BF16), b_merge=b_merge[l].reshape(N_BRANCH, 1, D_MODEL),
            w_out=w_out[l].astype(BF16)))

    def fresh(bn):
        return dict(sp_r=jnp.zeros((bn, 1, BRANCH), F32), sp_k=jnp.zeros((bn, 1, BRANCH), F32),
                    sp_v=jnp.zeros((bn, 1, BRANCH), F32), sp_l=jnp.zeros((bn, 1, 2 * RW_LORA), F32),
                    wkv=jnp.zeros((bn, HEADS, HEAD, HEAD), F32),
                    ssm_re=jnp.zeros((bn, 1, SSM_W), F32), ssm_im=jnp.zeros((bn, 1, SSM_W), F32),
                    ckv=None, kpe=None, sbk=None, sbv=None)

    def carried(l):
        sh = state_rwkv_shift[l]
        return dict(sp_r=sh[:, :, :BRANCH], sp_k=sh[:, :, BRANCH:2 * BRANCH], sp_v=sh[:, :, 2 * BRANCH:3 * BRANCH],
                    sp_l=sh[:, :, 3 * BRANCH:], wkv=state_rwkv_wkv[l],
                    ssm_re=state_ssm_re[l].reshape(bs, 1, SSM_W), ssm_im=state_ssm_im[l].reshape(bs, 1, SSM_W),
                    ckv=cache_mla_ckv[l], kpe=cache_mla_kpe[l],
                    sbk_t=jnp.transpose(cache_sb_k[l], (0, 2, 3, 1)).reshape(bs, HEADS // 2, LANES, past),
                    sbv_t=jnp.transpose(cache_sb_v[l], (0, 2, 3, 1)).reshape(bs, HEADS // 2, LANES, past))

    xp, xs = x_prompt, x_sample
    new_p, new_s = [], []
    for l in range(depth):
        last = l == depth - 1
        xp, st_p = _layer(xp, 0, fresh(bp), layers[l], final_g, last)
        xs, st_s = _layer(xs, past, carried(l), layers[l], final_g, last)
        new_p.append(st_p)
        new_s.append(st_s)
    stk = lambda lst, i: jnp.stack([s[i] for s in lst], axis=0)
    return (xp, xs) + tuple(stk(new_p, i) for i in range(8)) + tuple(stk(new_s, i) for i in range(8))
```

```python
import functools
import math

import jax
import jax.numpy as jnp
from jax import lax
from jax.experimental import pallas as pl
from jax.experimental.pallas import tpu as pltpu

F32 = jnp.float32
BF16 = jnp.bfloat16

D_MODEL = 2048
BRANCH = 512
N_BRANCH = 4
EPS = 1e-6
CHUNK = 64
HEAD = 64
HEADS = 8
RW_LORA = 64
RW_IN = 3 * BRANCH + 2 * RW_LORA
RW_GN_EPS = 64e-5
SSM_GROUP = 16
SSM_GROUPS = 32
SSM_STATE = 64
SSM_W = SSM_GROUPS * SSM_STATE
MLA_NOPE = 64
MLA_ROPE = 32
MLA_Q_LORA = 384
MLA_KV_LORA = 256
MLA_SCALE = 1.0 / math.sqrt(MLA_NOPE + MLA_ROPE)
ROPE_BASE = 10000.0
SB_SCALE = 1.0 / math.sqrt(HEAD)
LANES = 128
MIB = 1024 * 1024

O_MERGE = 0
O_GATE = 8192
O_SBQ = 10240
O_SBK = 10752
O_SBV = 11264
O_RWR = 11776
O_RWK = 12288
O_RWV = 12800
O_SSM = 13312
O_QLAT = 13824
O_LORA = 14208
O_KVLAT = 14336
O_KPE = 14592
O_KPESW = 14720
PROJ_W = 14848


def _cparams(sem, vmem_mib):
    return pltpu.CompilerParams(dimension_semantics=sem, vmem_limit_bytes=vmem_mib * MIB)


def _dot(a, b):
    return jnp.dot(a, b, preferred_element_type=F32)


def _dot_nt(a, b):
    return lax.dot_general(a, b, (((1,), (1,)), ((), ())), preferred_element_type=F32)


def _split3(x):
    h1 = x.astype(BF16)
    r1 = x - h1.astype(F32)
    h2 = r1.astype(BF16)
    h3 = (r1 - h2.astype(F32)).astype(BF16)
    return h1, h2, h3


def _softplus(x):
    return jnp.maximum(x, 0.0) + jnp.log1p(jnp.exp(-jnp.abs(x)))


def _norm_proj_kernel(x_ref, g_ref, w_ref, o_ref, h_ref):
    @pl.when(pl.program_id(1) == 0)
    def _():
        x = x_ref[...]
        ms = jnp.mean(x * x, axis=-1, keepdims=True)
        h_ref[...] = (x * lax.rsqrt(ms + EPS) * g_ref[...]).astype(BF16)

    o_ref[...] = _dot(h_ref[...], w_ref[0])


def _norm_proj(x2, g, w_all, layer):
    n, d = x2.shape
    width = w_all.shape[2]
    tm = min(n, 1024)
    tn = 512
    return pl.pallas_call(
        _norm_proj_kernel,
        out_shape=jax.ShapeDtypeStruct((n, width), F32),
        grid=(n // tm, width // tn),
        in_specs=[
            pl.BlockSpec((tm, d), lambda i, j: (i, 0)),
            pl.BlockSpec((1, d), lambda i, j: (0, 0)),
            pl.BlockSpec((1, d, tn), lambda i, j: (layer, 0, j)),
        ],
        out_specs=pl.BlockSpec((tm, tn), lambda i, j: (i, j)),
        scratch_shapes=[pltpu.VMEM((tm, d), BF16)],
        compiler_params=_cparams(("parallel", "arbitrary"), 40),
        name="norm_proj",
    )(x2, g, w_all)


def _segsum(x, g2):
    outs = []
    for j in range(BRANCH // LANES):
        h1, h2, h3 = _split3(x[:, LANES * j:LANES * (j + 1)])
        outs.append(_dot(h1, g2) + _dot(h2, g2) + _dot(h3, g2))
    return jnp.concatenate(outs, axis=1)


def _rwkv_kernel(pr_ref, pk_ref, pv_ref, plo_ref, spr_ref, spk_ref, spv_ref, spl_ref, s0_ref,
                 pvec_ref, mul_ref, w2_ref, a2_ref, g2_ref,
                 y_ref, sout_ref,
                 st_ref, cr_ref, ck_ref, cv_ref, cl_ref, *, C, nb):
    c = pl.program_id(1)

    @pl.when(c == 0)
    def _():
        st_ref[...] = s0_ref[...]
        cr_ref[...] = spr_ref[...]
        ck_ref[...] = spk_ref[...]
        cv_ref[...] = spv_ref[...]
        cl_ref[...] = spl_ref[...]

    pvec = pvec_ref[...]
    prow = lambda i: pvec[i:i + 1, :]
    g2 = g2_ref[...]

    def tshift(p_ref, carry_ref, mu):
        outs = []
        for bi in range(nb):
            p = p_ref[bi]
            rolled = pltpu.roll(p, 1, 0)
            rid = lax.broadcasted_iota(jnp.int32, p.shape, 0)
            prev = jnp.where(rid == 0, carry_ref[bi], rolled)
            carry_ref[bi] = p[C - 1:C, :]
            outs.append(p + (prev - p) * mu)
        return jnp.concatenate(outs, axis=0)

    r = tshift(pr_ref, cr_ref, prow(0))
    k = tshift(pk_ref, ck_ref, prow(1))
    v = tshift(pv_ref, cv_ref, prow(2))
    lo = tshift(plo_ref, cl_ref, mul_ref[...])

    log_w = -_softplus(-(prow(3) + _dot(jnp.tanh(lo).astype(BF16), w2_ref[...]))) - 0.5
    ld = -jnp.exp(log_w)
    a_icl = jax.nn.sigmoid(prow(4) + _dot(lo.astype(BF16), a2_ref[...]))
    kkr = k * prow(5)
    kk = kkr / jnp.maximum(jnp.sqrt(_segsum(kkr * kkr, g2)), 1e-12)
    k2 = k * (1.0 + (a_icl - 1.0) * prow(6))
    av = -kk
    bv = kk * a_icl

    sh = C.bit_length() - 1
    ri = lax.broadcasted_iota(jnp.int32, (nb * C, nb * C), 0)
    ci = lax.broadcasted_iota(jnp.int32, (nb * C, nb * C), 1)
    lincl = jnp.logical_and(ci <= ri, (ci >> sh) == (ri >> sh)).astype(BF16)
    h1, h2, h3 = _split3(ld)
    g = _dot(lincl, h1) + _dot(lincl, h2) + _dot(lincl, h3)
    g_lasts = [g[(bi + 1) * C - 1:(bi + 1) * C, :] for bi in range(nb)]
    g_last = jnp.concatenate([jnp.broadcast_to(gl, (C, BRANCH)) for gl in g_lasts], axis=0)
    eg = jnp.exp(g)
    eng = jnp.exp(-g)
    at = av * jnp.exp(g - ld)
    rt = r * eg
    bt = bv * eng
    kt = k2 * eng
    e_c = jnp.exp(g_last - g)
    b_end = bv * e_c
    k_end = k2 * e_c
    eg_lasts = [jnp.exp(gl) for gl in g_lasts]

    c2 = 2 * C
    npair = BRANCH // LANES
    r_i = lax.broadcasted_iota(jnp.int32, (c2, c2), 0)
    c_i = lax.broadcasted_iota(jnp.int32, (c2, c2), 1)
    c_m = jnp.where(c_i >= C, c_i - C, c_i)
    mhalf = c_m < jnp.where(r_i < C, r_i, r_i - C + 1)
    mfull = jnp.concatenate([mhalf, mhalf], axis=0)
    bd2 = (r_i >> sh) == (c_i >> sh)
    eye2 = (r_i == c_i).astype(F32)
    lane = lax.broadcasted_iota(jnp.int32, (1, LANES), 1)
    m0 = (lane < HEAD).astype(F32)
    m1 = (lane >= HEAD).astype(F32)
    v_i = lax.broadcasted_iota(jnp.int32, (LANES, LANES), 0)
    k_i = lax.broadcasted_iota(jnp.int32, (LANES, LANES), 1)
    bdmask = (v_i >> 6) == (k_i >> 6)
    zf = jnp.zeros((C, LANES), F32)
    zc = jnp.zeros((C, LANES), BF16)
    sls = [slice(LANES * j, LANES * (j + 1)) for j in range(npair)]

    units = [(bi, j) for bi in range(nb) for j in range(npair)]
    tile = lambda x, bi, j: x[bi * C:(bi + 1) * C, LANES * j:LANES * (j + 1)]
    ws, a_ss, s2s = [], [], []
    for bi, j in units:
        atj, rtj = tile(at, bi, j), tile(rt, bi, j)
        a0, r0, a1, r1 = atj * m0, rtj * m0, atj * m1, rtj * m1
        lhs_w = jnp.concatenate([jnp.concatenate([a0, zf], axis=1), jnp.concatenate([r0, zf], axis=1),
                                 jnp.concatenate([zf, a1], axis=1), jnp.concatenate([zf, r1], axis=1)],
                                axis=0).astype(BF16)
        btj, ktj = tile(bt, bi, j), tile(kt, bi, j)
        rhs_w = jnp.concatenate([jnp.concatenate([btj, ktj], axis=0),
                                 jnp.concatenate([ktj, btj], axis=0)], axis=1).astype(BF16)
        ws.append(jnp.where(mfull, _dot_nt(lhs_w, rhs_w), 0.0))
        lhs_s = jnp.concatenate([a0, r0, a1, r1], axis=0).astype(BF16)
        s2 = st_ref[bi, j]
        s2h, s2l, _ = _split3(s2)
        a_ss.append(_dot_nt(lhs_s, s2h) + _dot_nt(lhs_s, s2l))
        s2s.append(s2)

    n_pows = [jnp.where(bd2, jnp.concatenate([w[:C], w[c2:c2 + C]], axis=0), 0.0) for w in ws]
    t_mats = [eye2 + n for n in n_pows]
    lv = 2
    while lv < C:
        n_pows = [_dot(n.astype(BF16), n.astype(BF16)) for n in n_pows]
        t_mats = [t + _dot(t.astype(BF16), n.astype(BF16)) for t, n in zip(t_mats, n_pows)]
        lv *= 2

    wbs = [w.astype(BF16) for w in ws]
    v0s = [(tile(v, bi, j) * m0).astype(BF16) for bi, j in units]
    v1s = [(tile(v, bi, j) * m1).astype(BF16) for bi, j in units]
    xs = [jnp.concatenate([a_s[:C] + _dot(wb[:C], jnp.concatenate([zc, v0], axis=0)),
                           a_s[c2:c2 + C] + _dot(wb[c2:c2 + C], jnp.concatenate([v1, zc], axis=0))], axis=0)
          for a_s, wb, v0, v1 in zip(a_ss, wbs, v0s, v1s)]
    us = [_dot(t.astype(BF16), x.astype(BF16)) for t, x in zip(t_mats, xs)]
    ys = []
    for n, (bi, j) in enumerate(units):
        a_s, wb, u = a_ss[n], wbs[n], us[n]
        u0, u1 = u[:C], u[C:]
        y0 = a_s[C:c2] + _dot(wb[C:c2], jnp.concatenate([u0.astype(BF16), v0s[n]], axis=0))
        y1 = a_s[c2 + C:] + _dot(wb[c2 + C:], jnp.concatenate([v1s[n], u1.astype(BF16)], axis=0))
        ys.append(y0 + y1)
        uv = jnp.concatenate([u0 + u1, tile(v, bi, j)], axis=0)
        bk = jnp.concatenate([tile(b_end, bi, j), tile(k_end, bi, j)], axis=0).astype(BF16)
        upd = _dot(uv.T.astype(BF16), bk)
        st_ref[bi, j] = s2s[n] * eg_lasts[bi][:, sls[j]] + jnp.where(bdmask, upd, 0.0)
    y = jnp.concatenate([jnp.concatenate(ys[bi * npair:(bi + 1) * npair], axis=1) for bi in range(nb)],
                        axis=0)

    inv_n = 1.0 / HEAD
    yc = y - _segsum(y, g2) * inv_n
    yn = yc * lax.rsqrt(_segsum(yc * yc, g2) * inv_n + RW_GN_EPS) * prow(8) + prow(9)
    out = (yn + _segsum(r * k2 * prow(7), g2) * v).astype(y_ref.dtype)
    for bi in range(nb):
        y_ref[bi] = out[bi * C:(bi + 1) * C]

    @pl.when(c == pl.num_programs(1) - 1)
    def _():
        sout_ref[...] = st_ref[...]


def _rwkv(proj3, sp_r, sp_k, sp_v, sp_l, s0, pvec, mu_l, w2p, a2p, g2, chunk):
    b, t, _ = proj3.shape
    nc = t // chunk
    nb = min(b, 4, 2 * CHUNK // chunk)
    cb = lambda off, wdt: off // wdt
    bspec = lambda off, wdt: pl.BlockSpec((nb, chunk, wdt), lambda i, c, o=cb(off, wdt): (i, c, o))
    row3 = lambda wdt: pl.BlockSpec((nb, 1, wdt), lambda i, c: (i, 0, 0))
    full2 = lambda a: pl.BlockSpec(a.shape, lambda i, c: (0, 0))
    return pl.pallas_call(
        functools.partial(_rwkv_kernel, C=chunk, nb=nb),
        out_shape=(jax.ShapeDtypeStruct((b, t, BRANCH), BF16),
                   jax.ShapeDtypeStruct((b, 4, LANES, LANES), F32)),
        grid=(b // nb, nc),
        in_specs=[
            bspec(O_RWR, BRANCH), bspec(O_RWK, BRANCH), bspec(O_RWV, BRANCH), bspec(O_LORA, LANES),
            row3(BRANCH), row3(BRANCH), row3(BRANCH), row3(LANES),
            pl.BlockSpec((nb, 4, LANES, LANES), lambda i, c: (i, 0, 0, 0)),
            full2(pvec), full2(mu_l), full2(w2p), full2(a2p), full2(g2),
        ],
        out_specs=(pl.BlockSpec((nb, chunk, BRANCH), lambda i, c: (i, c, 0)),
                   pl.BlockSpec((nb, 4, LANES, LANES), lambda i, c: (i, 0, 0, 0))),
        scratch_shapes=[pltpu.VMEM((nb, 4, LANES, LANES), F32),
                        pltpu.VMEM((nb, 1, BRANCH), F32), pltpu.VMEM((nb, 1, BRANCH), F32),
                        pltpu.VMEM((nb, 1, BRANCH), F32), pltpu.VMEM((nb, 1, LANES), F32)],
        compiler_params=_cparams(("parallel", "arbitrary"), 32),
        name="rwkv7",
    )(proj3, proj3, proj3, proj3, sp_r, sp_k, sp_v, sp_l, s0, pvec, mu_l, w2p, a2p, g2)


def _s5_kernel(u_ref, h0r_ref, h0i_ref, bblk_ref, cblk_ref, lam_ref, pw_ref, dsk_ref, wglu_ref, bglu_ref,
               y_ref, hro_ref, hio_ref, x_ref, cr_ref, ci_ref, *, tb):
    @pl.when(pl.program_id(1) == 0)
    def _():
        cr_ref[...] = h0r_ref[0]
        ci_ref[...] = h0i_ref[0]

    u = u_ref[0]
    x = _dot(u.astype(BF16), bblk_ref[...])
    xr = x[:, :SSM_W]
    xi = x[:, SSM_W:]
    rid = lax.broadcasted_iota(jnp.int32, (tb, 1), 0) & 7
    for lvl, s in enumerate((1, 2, 4)):
        ar = lam_ref[2 * lvl:2 * lvl + 1, :]
        ai = lam_ref[2 * lvl + 1:2 * lvl + 2, :]
        sr = pltpu.roll(xr, s, 0)
        si = pltpu.roll(xi, s, 0)
        valid = rid >= s
        nr = xr + jnp.where(valid, ar * sr - ai * si, 0.0)
        ni = xi + jnp.where(valid, ar * si + ai * sr, 0.0)
        xr, xi = nr, ni
    x_ref[:, :SSM_W] = xr
    x_ref[:, SSM_W:] = xi
    pwr = pw_ref[0:8, :]
    pwi = pw_ref[8:16, :]

    def body(gi, carry):
        cr, ci = carry
        o = pl.multiple_of(gi * 8, 8)
        br = x_ref[pl.ds(o, 8), 0:SSM_W]
        bi = x_ref[pl.ds(o, 8), SSM_W:2 * SSM_W]
        br = br + pwr * cr - pwi * ci
        bi = bi + pwr * ci + pwi * cr
        x_ref[pl.ds(o, 8), 0:SSM_W] = br
        x_ref[pl.ds(o, 8), SSM_W:2 * SSM_W] = bi
        return br[7:8, :], bi[7:8, :]

    cr, ci = lax.fori_loop(0, tb // 8, body, (cr_ref[...], ci_ref[...]))
    cr_ref[...] = cr
    ci_ref[...] = ci
    hro_ref[0] = cr
    hio_ref[0] = ci
    y = _dot(x_ref[...].astype(BF16), cblk_ref[...]) + dsk_ref[...] * u
    g = jax.nn.gelu(y)
    out = g * jax.nn.sigmoid(_dot(g.astype(BF16), wglu_ref[...]) + bglu_ref[...])
    y_ref[0] = out.astype(y_ref.dtype)


def _s5(proj3, h0r, h0i, bblk, cblk, lam, pw, dsk, wglu, bglu):
    b, t, _ = proj3.shape
    tb = min(t, 128)
    full2 = lambda a: pl.BlockSpec(a.shape, lambda i, c: (0, 0))
    row3 = pl.BlockSpec((1, 1, SSM_W), lambda i, c: (i, 0, 0))
    return pl.pallas_call(
        functools.partial(_s5_kernel, tb=tb),
        out_shape=(jax.ShapeDtypeStruct((b, t, BRANCH), BF16),
                   jax.ShapeDtypeStruct((b, 1, SSM_W), F32),
                   jax.ShapeDtypeStruct((b, 1, SSM_W), F32)),
        grid=(b, t // tb),
        in_specs=[pl.BlockSpec((1, tb, BRANCH), lambda i, c: (i, c, O_SSM // BRANCH)),
                  row3, row3, full2(bblk), full2(cblk), full2(lam), full2(pw), full2(dsk),
                  full2(wglu), full2(bglu)],
        out_specs=(pl.BlockSpec((1, tb, BRANCH), lambda i, c: (i, c, 0)), row3, row3),
        scratch_shapes=[pltpu.VMEM((tb, 2 * SSM_W), F32),
                        pltpu.VMEM((1, SSM_W), F32), pltpu.VMEM((1, SSM_W), F32)],
        compiler_params=_cparams(("parallel", "arbitrary"), 48),
        name="s5",
    )(proj3, h0r, h0i, bblk, cblk, lam, pw, dsk, wglu, bglu)


def _mla_prep_kernel(ql_ref, kvl_ref, kpe_ref, kpesw_ref, tq_ref, tc_ref, ts_ref,
                     gq_ref, gkv_ref, wq_ref, q_ref, ckv_ref, kr_ref):
    ql = ql_ref[...]
    qn = ql * lax.rsqrt(jnp.mean(ql * ql, axis=-1, keepdims=True) + EPS) * gq_ref[...]
    q = _dot(qn.astype(BF16), wq_ref[...])
    tq = tq_ref[...]
    q_ref[...] = jnp.concatenate(
        [q[:, LANES * h:LANES * (h + 1)] * tq for h in range(HEADS)], axis=1).astype(BF16)
    kvl = kvl_ref[...]
    ckv_ref[...] = kvl * lax.rsqrt(jnp.mean(kvl * kvl, axis=-1, keepdims=True) + EPS) * gkv_ref[...]
    kr = kpe_ref[...] * tc_ref[...] + kpesw_ref[...] * ts_ref[...]
    kr_ref[...] = kr[:, :MLA_ROPE]


def _mla_prep(proj, t, tq_tab, tc_tab, ts_tab, gq, gkv, wq):
    n = proj.shape[0]
    tm = min(t, 512)
    nt = t // tm
    tab = pl.BlockSpec((tm, LANES), lambda i: (i % nt, 0))
    full2 = lambda a: pl.BlockSpec(a.shape, lambda i: (0, 0))
    return pl.pallas_call(
        _mla_prep_kernel,
        out_shape=(jax.ShapeDtypeStruct((n, HEADS * LANES), BF16),
                   jax.ShapeDtypeStruct((n, MLA_KV_LORA), F32),
                   jax.ShapeDtypeStruct((n, MLA_ROPE), F32)),
        grid=(n // tm,),
        in_specs=[pl.BlockSpec((tm, MLA_Q_LORA), lambda i: (i, O_QLAT // MLA_Q_LORA)),
                  pl.BlockSpec((tm, MLA_KV_LORA), lambda i: (i, O_KVLAT // MLA_KV_LORA)),
                  pl.BlockSpec((tm, LANES), lambda i: (i, O_KPE // LANES)),
                  pl.BlockSpec((tm, LANES), lambda i: (i, O_KPESW // LANES)),
                  tab, tab, tab, full2(gq), full2(gkv), full2(wq)],
        out_specs=(pl.BlockSpec((tm, HEADS * LANES), lambda i: (i, 0)),
                   pl.BlockSpec((tm, MLA_KV_LORA), lambda i: (i, 0)),
                   pl.BlockSpec((tm, MLA_ROPE), lambda i: (i, 0))),
        compiler_params=_cparams(("parallel",), 32),
        name="mla_prep",
    )(proj, proj, proj, proj, tq_tab, tc_tab, ts_tab, gq, gkv, wq)


def _mla_kv_kernel(ckv_ref, kr_ref, wk_ref, wv_ref, dup_ref, k_ref, v_ref):
    cb = ckv_ref[0].astype(BF16)
    kn = _dot(cb, wk_ref[...])
    krd = _dot(kr_ref[0].astype(BF16), dup_ref[...])
    k_ref[...] = jnp.concatenate(
        [kn[:, LANES * h:LANES * (h + 1)] + krd for h in range(HEADS)], axis=1).astype(BF16)
    v_ref[...] = _dot(cb, wv_ref[...]).astype(BF16)


def _mla_kv(ckv, kr, wk, wv, dup, layer=0):
    n = ckv.shape[1]
    tm = next((c for c in (512, LANES) if n % c == 0), n)
    full2 = lambda a: pl.BlockSpec(a.shape, lambda i: (0, 0))
    return pl.pallas_call(
        _mla_kv_kernel,
        out_shape=(jax.ShapeDtypeStruct((n, HEADS * LANES), BF16),
                   jax.ShapeDtypeStruct((n, BRANCH), BF16)),
        grid=(n // tm,),
        in_specs=[pl.BlockSpec((1, tm, MLA_KV_LORA), lambda i: (layer, i, 0)),
                  pl.BlockSpec((1, tm, MLA_ROPE), lambda i: (layer, i, 0)),
                  full2(wk), full2(wv), full2(dup)],
        out_specs=(pl.BlockSpec((tm, HEADS * LANES), lambda i: (i, 0)),
                   pl.BlockSpec((tm, BRANCH), lambda i: (i, 0))),
        compiler_params=_cparams(("parallel",), 32),
        name="mla_kv",
    )(ckv, kr, wk, wv, dup)


def _assemble_keys(past_ref, new_ref, all_ref, past, t):
    all_ref[0:past, :] = past_ref[0].astype(BF16)
    all_ref[past:past + t, :] = new_ref[0].astype(BF16)
    pad = all_ref.shape[0] - past - t
    if pad:
        all_ref[past + t:, :] = jnp.zeros((pad, all_ref.shape[1]), BF16)


def _mla_attn_kernel(q_ref, k_ref, v_ref, *rest, tq, tk, q_off, s_valid, hp, past):
    if past:
        kp_ref, vp_ref, o_ref, kall_ref, vall_ref = rest
        _assemble_keys(kp_ref, k_ref, kall_ref, past, tq)
        _assemble_keys(vp_ref, v_ref, vall_ref, past, tq)
        kload = lambda o: kall_ref[pl.ds(o, tk), :]
        vload = lambda o: vall_ref[pl.ds(o, tk), :]
    else:
        (o_ref,) = rest
        kload = lambda o: k_ref[0, pl.ds(o, tk), :]
        vload = lambda o: v_ref[0, pl.ds(o, tk), :]
    i = pl.program_id(2)
    q_lo = q_off + i * tq
    limit = jnp.minimum(q_lo + tq, s_valid)
    nkb = (limit + tk - 1) // tk
    qidx = q_lo + lax.broadcasted_iota(jnp.int32, (tq, tk), 0)
    kloc = lax.broadcasted_iota(jnp.int32, (tq, tk), 1)
    nh = 2 * hp
    q2 = q_ref[0]
    qs = [q2[:, LANES * h:LANES * (h + 1)] for h in range(nh)]

    def body(kb, carry, masked):
        o = pl.multiple_of(kb * tk, tk)
        kblk = kload(o)
        vblk = vload(o)
        ss = [_dot_nt(qs[h], kblk[:, LANES * h:LANES * (h + 1)]) for h in range(nh)]
        if masked:
            kidx = kloc + kb * tk
            vis = jnp.logical_and((kidx >> 6) <= (qidx >> 6), kidx < s_valid)
            ss = [jnp.where(vis, s, -1e30) for s in ss]
        m_new = [jnp.maximum(carry[h][0], jnp.max(ss[h], axis=-1, keepdims=True)) for h in range(nh)]
        ps = [jnp.exp(ss[h] - m_new[h]) for h in range(nh)]
        pv = [_dot(ps[h].astype(BF16), vblk[:, LANES * (h // 2):LANES * (h // 2 + 1)]) for h in range(nh)]
        new = []
        for h in range(nh):
            m_prev, l_prev, acc = carry[h]
            alpha = jnp.exp(m_prev - m_new[h])
            new.append((m_new[h], alpha * l_prev + jnp.sum(ps[h], axis=-1, keepdims=True), alpha * acc + pv[h]))
        return tuple(new)

    init = tuple((jnp.full((tq, 1), -1e30, F32), jnp.zeros((tq, 1), F32), jnp.zeros((tq, LANES), F32))
                 for _ in range(nh))
    nfull = jnp.minimum(((q_lo >> 6) + 1) * CHUNK, s_valid) // tk
    res = lax.fori_loop(0, nfull, lambda kb, c: body(kb, c, False), init)
    res = lax.fori_loop(nfull, nkb, lambda kb, c: body(kb, c, True), res)
    lane = lax.broadcasted_iota(jnp.int32, (tq, LANES), 1)
    outs = [jnp.where(lane < HEAD, res[2 * p][2] / res[2 * p][1], res[2 * p + 1][2] / res[2 * p + 1][1])
            for p in range(hp)]
    o_ref[0] = jnp.concatenate(outs, axis=1).astype(o_ref.dtype)


def _mla_attn(q3, k3, v3, kp3, vp3, s_pad, tq, tk, hp):
    b, t, _ = q3.shape
    past = 0 if kp3 is None else kp3.shape[1]
    rows = k3.shape[1]
    in_specs = [pl.BlockSpec((1, tq, 2 * LANES * hp), lambda bi, j, i: (bi, i, j)),
                pl.BlockSpec((1, rows, 2 * LANES * hp), lambda bi, j, i: (bi, 0, j)),
                pl.BlockSpec((1, rows, LANES * hp), lambda bi, j, i: (bi, 0, j))]
    args = [q3, k3, v3]
    scratch = []
    if past:
        in_specs += [pl.BlockSpec((1, past, 2 * LANES * hp), lambda bi, j, i: (bi, 0, j)),
                     pl.BlockSpec((1, past, LANES * hp), lambda bi, j, i: (bi, 0, j))]
        args += [kp3, vp3]
        scratch = [pltpu.VMEM((s_pad, 2 * LANES * hp), BF16), pltpu.VMEM((s_pad, LANES * hp), BF16)]
    return pl.pallas_call(
        functools.partial(_mla_attn_kernel, tq=tq, tk=tk, q_off=past, s_valid=past + t, hp=hp, past=past),
        out_shape=jax.ShapeDtypeStruct((b, t, BRANCH), BF16),
        grid=(b, 4 // hp, t // tq),
        in_specs=in_specs,
        out_specs=pl.BlockSpec((1, tq, LANES * hp), lambda bi, j, i: (bi, i, j)),
        scratch_shapes=scratch,
        compiler_params=_cparams(("parallel", "parallel", "arbitrary"), 40),
        name="mla_attn",
    )(*args)


def _sb_block(zs, vis, us, carry, pv):
    nh = len(zs)
    tq = zs[0].shape[0]
    zls = [jnp.minimum(z, 0.0) - jnp.log(1.0 + jnp.exp(-jnp.abs(z))) for z in zs]
    lgs = [zl - z for zl, z in zip(zls, zs)]
    if vis is not None:
        lgs = [jnp.where(vis, lg, 0.0) for lg in lgs]
    his = [lg.astype(BF16) for lg in lgs]
    los = [(lg - hi.astype(F32)).astype(BF16) for lg, hi in zip(lgs, his)]
    later = _dot(jnp.concatenate(his + los, axis=0), us)
    new = []
    for h in range(nh):
        csum, acc = carry[h]
        tot = later[h * tq:(h + 1) * tq] + later[(nh + h) * tq:(nh + h + 1) * tq] + csum
        a = jnp.exp(zls[h] + tot)
        if vis is not None:
            a = jnp.where(vis, a, 0.0)
        new.append((csum + jnp.sum(lgs[h], axis=-1, keepdims=True), acc + pv(h, a.astype(BF16))))
    return tuple(new)


def _sb_heads(q2, nh):
    lane1 = lax.broadcasted_iota(jnp.int32, (1, LANES), 1)
    hm = ((lane1 < HEAD).astype(F32) * SB_SCALE, (lane1 >= HEAD).astype(F32) * SB_SCALE)
    return [(q2[:, LANES * (h // 2):LANES * (h // 2 + 1)] * hm[h % 2]).astype(BF16) for h in range(nh)]


def _sb_finish(res, o_ref, tq):
    lane = lax.broadcasted_iota(jnp.int32, (tq, LANES), 1)
    outs = [jnp.where(lane < HEAD, res[2 * p][1], res[2 * p + 1][1]) for p in range(len(res) // 2)]
    o_ref[0] = jnp.concatenate(outs, axis=1).astype(o_ref.dtype)


def _sb_attn_kernel(q_ref, k_ref, v_ref, us_ref, o_ref, *, tq, tk):
    i = pl.program_id(2)
    q_lo = i * tq
    nkb = (q_lo + tq - 2) // tk + 1
    nfull = q_lo // tk
    qidx = q_lo + lax.broadcasted_iota(jnp.int32, (tq, tk), 0)
    kloc = lax.broadcasted_iota(jnp.int32, (tq, tk), 1)
    qh = _sb_heads(q_ref[0], 2)
    us = us_ref[...]

    def logits(kb):
        o = pl.multiple_of(kb * tk, tk)
        kblk = k_ref[0, pl.ds(o, tk), :].astype(BF16)
        return tuple(_dot_nt(q, kblk) for q in qh)

    def block(kb, state, masked):
        zs, carry = state
        zs_next = logits(jnp.maximum(kb - 1, 0))
        o = pl.multiple_of(kb * tk, tk)
        vblk = v_ref[0, pl.ds(o, tk), :].astype(BF16)
        vis = (kloc + kb * tk) < qidx if masked else None
        return zs_next, _sb_block(zs, vis, us, carry, lambda h, a: _dot(a, vblk))

    init = tuple((jnp.zeros((tq, 1), F32), jnp.zeros((tq, LANES), F32)) for _ in range(2))
    state = (logits(nkb - 1), init)
    state = lax.fori_loop(0, nkb - nfull, lambda s, c: block(nkb - 1 - s, c, True), state)
    state = lax.fori_loop(0, nfull, lambda s, c: block(nfull - 1 - s, c, False), state)
    _sb_finish(state[1], o_ref, tq)


def _sb_decode_kernel(q_ref, k_ref, v_ref, kpt_ref, vpt_ref, usn_ref, usp_ref, o_ref, *, t, past, pb):
    qh = _sb_heads(q_ref[0], HEADS)
    kn = k_ref[0].astype(BF16)
    vn = v_ref[0].astype(BF16)
    pair = lambda x, h: x[:, LANES * (h // 2):LANES * (h // 2 + 1)]
    ri = lax.broadcasted_iota(jnp.int32, (t, t), 0)
    ci = lax.broadcasted_iota(jnp.int32, (t, t), 1)
    carry = tuple((jnp.zeros((t, 1), F32), jnp.zeros((t, LANES), F32)) for _ in range(HEADS))
    zs = [_dot_nt(qh[h], pair(kn, h)) for h in range(HEADS)]
    carry = _sb_block(zs, ci < ri, usn_ref[...], carry, lambda h, a: _dot(a, pair(vn, h)))
    usp = usp_ref[...]
    for blk in reversed(range(past // pb)):
        kts = [kpt_ref[0, 0, p, :, blk * pb:(blk + 1) * pb].astype(BF16) for p in range(HEADS // 2)]
        vts = [vpt_ref[0, 0, p, :, blk * pb:(blk + 1) * pb].astype(BF16) for p in range(HEADS // 2)]
        zs = [_dot(qh[h], kts[h // 2]) for h in range(HEADS)]
        carry = _sb_block(zs, None, usp, carry, lambda h, a: _dot_nt(a, vts[h // 2]))
    _sb_finish(carry, o_ref, t)


def _strict_upper(n):
    idx = jnp.arange(n)
    return (idx[:, None] > idx[None, :]).astype(BF16)


def _sb_attn(proj3, tq, tk):
    b, t, _ = proj3.shape
    us = _strict_upper(tk)
    return pl.pallas_call(
        functools.partial(_sb_attn_kernel, tq=tq, tk=tk),
        out_shape=jax.ShapeDtypeStruct((b, t, BRANCH), BF16),
        grid=(b, HEADS // 2, t // tq),
        in_specs=[pl.BlockSpec((1, tq, LANES), lambda bi, j, i: (bi, i, O_SBQ // LANES + j)),
                  pl.BlockSpec((1, t, LANES), lambda bi, j, i: (bi, 0, O_SBK // LANES + j)),
                  pl.BlockSpec((1, t, LANES), lambda bi, j, i: (bi, 0, O_SBV // LANES + j)),
                  pl.BlockSpec(us.shape, lambda bi, j, i: (0, 0))],
        out_specs=pl.BlockSpec((1, tq, LANES), lambda bi, j, i: (bi, i, j)),
        compiler_params=_cparams(("parallel", "parallel", "arbitrary"), 48),
        name="sb_attn",
    )(proj3, proj3, proj3, us)


def _sb_decode(proj3, kpt, vpt, layer, pb):
    b, t, _ = proj3.shape
    past = kpt.shape[4]
    usn, usp = _strict_upper(t), _strict_upper(pb)
    new = lambda off: pl.BlockSpec((1, t, BRANCH), lambda bi, o=off // BRANCH: (bi, 0, o))
    cache = pl.BlockSpec((1, 1, HEADS // 2, LANES, past), lambda bi: (layer, bi, 0, 0, 0))
    return pl.pallas_call(
        functools.partial(_sb_decode_kernel, t=t, past=past, pb=pb),
        out_shape=jax.ShapeDtypeStruct((b, t, BRANCH), BF16),
        grid=(b,),
        in_specs=[new(O_SBQ), new(O_SBK), new(O_SBV), cache, cache,
                  pl.BlockSpec(usn.shape, lambda bi: (0, 0)), pl.BlockSpec(usp.shape, lambda bi: (0, 0))],
        out_specs=pl.BlockSpec((1, t, BRANCH), lambda bi: (bi, 0, 0)),
        compiler_params=_cparams(("parallel",), 48),
        name="sb_decode",
    )(proj3, proj3, proj3, kpt, vpt, usn, usp)


def _mix_kernel(yrw_ref, yssm_ref, ymla_ref, ysb_ref, gate_ref, pm_ref, wb_ref, bm_ref, o_ref, acc_ref):
    n = pl.program_id(1)

    @pl.when(n == 0)
    def _():
        acc_ref[...] = jnp.zeros_like(acc_ref)

    def branch(y_ref):
        g = gate_ref[...]
        gated = y_ref[...].astype(F32) * (g * jax.nn.sigmoid(g))
        up = _dot(gated.astype(BF16), wb_ref[0])
        acc_ref[...] += jax.nn.sigmoid(pm_ref[...] + bm_ref[0]) * up

    for idx, y_ref in enumerate((yrw_ref, yssm_ref, ymla_ref, ysb_ref)):
        pl.when(n == idx)(functools.partial(branch, y_ref))

    @pl.when(n == N_BRANCH - 1)
    def _():
        o_ref[...] = acc_ref[...].astype(BF16)


def _out_kernel(m_ref, x_ref, wo_ref, fg_ref, o_ref, *, final):
    xn = x_ref[...] + _dot(m_ref[...], wo_ref[...])
    if final:
        xn = xn * lax.rsqrt(jnp.mean(xn * xn, axis=-1, keepdims=True) + EPS) * fg_ref[...]
    o_ref[...] = xn


def _merge(ys, proj, x2, wb, bm, wo, fg, final):
    n, d = x2.shape
    tm = min(n, 1024)
    ysp = pl.BlockSpec((tm, BRANCH), lambda i, k: (i, 0))
    mixed = pl.pallas_call(
        _mix_kernel,
        out_shape=jax.ShapeDtypeStruct((n, d), BF16),
        grid=(n // tm, N_BRANCH),
        in_specs=[ysp, ysp, ysp, ysp,
                  pl.BlockSpec((tm, BRANCH), lambda i, k: (i, O_GATE // BRANCH + k)),
                  pl.BlockSpec((tm, d), lambda i, k: (i, O_MERGE // d + k)),
                  pl.BlockSpec((1, BRANCH, d), lambda i, k: (k, 0, 0)),
                  pl.BlockSpec((1, 1, d), lambda i, k: (k, 0, 0))],
        out_specs=pl.BlockSpec((tm, d), lambda i, k: (i, 0)),
        scratch_shapes=[pltpu.VMEM((tm, d), F32)],
        compiler_params=_cparams(("parallel", "arbitrary"), 56),
        name="branch_mix",
    )(*ys, proj, proj, wb, bm)
    to = min(n, 512)
    return pl.pallas_call(
        functools.partial(_out_kernel, final=final),
        out_shape=jax.ShapeDtypeStruct((n, d), F32),
        grid=(n // to,),
        in_specs=[pl.BlockSpec((to, d), lambda i: (i, 0)),
                  pl.BlockSpec((to, d), lambda i: (i, 0)),
                  pl.BlockSpec((d, d), lambda i: (0, 0)),
                  pl.BlockSpec((1, d), lambda i: (0, 0))],
        out_specs=pl.BlockSpec((to, d), lambda i: (i, 0)),
        compiler_params=_cparams(("parallel",), 48),
        name="out_proj",
    )(mixed, x2, wo, fg)


_SRC_SSM = RW_IN
_SRC_QLAT = _SRC_SSM + BRANCH
_SRC_KVLAT = _SRC_QLAT + MLA_Q_LORA
_SRC_KPE = _SRC_KVLAT + MLA_KV_LORA
_SRC_SB = _SRC_KPE + MLA_ROPE
_SRC_GATE = _SRC_SB + 3 * BRANCH
_SRC_MERGE = _SRC_GATE + N_BRANCH * BRANCH
_SRC_LORA = 3 * BRANCH
_PW_BLK = 4 * LANES
_PW_LAST = PROJ_W // _PW_BLK - 1
_PW_TAIL = {_PW_LAST - 1: (_SRC_QLAT, _SRC_QLAT + LANES, _SRC_QLAT + 2 * LANES, _SRC_LORA),
            _PW_LAST: (_SRC_KVLAT, _SRC_KVLAT + LANES, _SRC_KPE, _SRC_KPE)}


def _permute_src_col(j, k):
    nm = N_BRANCH * D_MODEL // _PW_BLK
    ng = nm + N_BRANCH * BRANCH // _PW_BLK
    nsb = ng + 3 * BRANCH // _PW_BLK
    nrw = nsb + 3 * BRANCH // _PW_BLK
    u = MLA_ROPE
    col = jnp.where(j < nm, _SRC_MERGE // u + _PW_BLK // u * j,
                    jnp.where(j < ng, _SRC_GATE // u + _PW_BLK // u * (j - nm),
                              jnp.where(j < nsb, _SRC_SB // u + _PW_BLK // u * (j - ng),
                                        jnp.where(j < nrw, _PW_BLK // u * (j - nsb), _SRC_SSM // u))))
    col = col + LANES // u * k
    for jj, src in _PW_TAIL.items():
        col = jnp.where(j == jj, src[k] // u, col)
    return col * u


def _permute_kernel(x0_ref, x1_ref, x2_ref, x3_ref, o_ref):
    j = pl.program_id(1)
    wins = (x0_ref, x1_ref, x2_ref, x3_ref)
    half = MLA_ROPE // 2

    def put(k, rows):
        o_ref[0, :, LANES * k:LANES * (k + 1)] = rows.T.astype(BF16)

    @pl.when(j < _PW_LAST)
    def _():
        for k in range(4):
            put(k, wins[k][0])

    @pl.when(j == _PW_LAST)
    def _():
        for k in range(2):
            put(k, wins[k][0])
        kpe = wins[2][0]
        row = lax.broadcasted_iota(jnp.int32, kpe.shape, 0)
        put(2, jnp.where(row < MLA_ROPE, kpe, 0.0))
        zeros = jnp.zeros((LANES - MLA_ROPE, kpe.shape[1]), F32)
        put(3, jnp.concatenate([-kpe[half:MLA_ROPE], kpe[:half], zeros], axis=0))


def _permute_w_in(w_in):
    depth, d, _ = w_in.shape
    wt = jnp.swapaxes(w_in, 1, 2)
    win = lambda k: pl.BlockSpec((pl.Element(1), pl.Element(LANES), pl.Element(d)),
                                 lambda l, j, k=k: (l, _permute_src_col(j, k), 0))
    return pl.pallas_call(
        _permute_kernel,
        out_shape=jax.ShapeDtypeStruct((depth, d, PROJ_W), BF16),
        grid=(depth, PROJ_W // _PW_BLK),
        in_specs=[win(k) for k in range(4)],
        out_specs=pl.BlockSpec((1, d, _PW_BLK), lambda l, j: (l, 0, j)),
        compiler_params=_cparams(("parallel", "parallel"), 32),
        name="permute_w_in",
    )(wt, wt, wt, wt)


def _rope_tables(pos):
    half = MLA_ROPE // 2
    inv = ROPE_BASE ** (-jnp.arange(half, dtype=F32) / half)
    ang = pos.astype(F32)[:, None] * inv
    cos, sin = jnp.cos(ang), jnp.sin(ang)
    t = pos.shape[0]
    cc = jnp.concatenate([cos, cos], axis=1)
    ss = jnp.concatenate([sin, sin], axis=1)
    tq = jnp.concatenate([jnp.ones((t, MLA_NOPE), F32), cc, ss], axis=1) * MLA_SCALE
    pad = jnp.zeros((t, LANES - MLA_ROPE), F32)
    return tq, jnp.concatenate([cc, pad], axis=1), jnp.concatenate([ss, pad], axis=1)


def _mla_weights(w_q_up, w_kv_up):
    half = MLA_ROPE // 2
    wq = w_q_up.reshape(MLA_Q_LORA, HEADS, MLA_NOPE + MLA_ROPE)
    x1 = wq[:, :, MLA_NOPE:MLA_NOPE + half]
    x2 = wq[:, :, MLA_NOPE + half:]
    wq_p = jnp.concatenate([wq, -x2, x1], axis=2).reshape(MLA_Q_LORA, HEADS * LANES).astype(BF16)
    wkv = w_kv_up.reshape(MLA_KV_LORA, HEADS, 2 * HEAD)
    wk_p = jnp.concatenate([wkv[:, :, :HEAD], jnp.zeros((MLA_KV_LORA, HEADS, HEAD), F32)], axis=2)
    wk_p = wk_p.reshape(MLA_KV_LORA, HEADS * LANES).astype(BF16)
    wv_p = wkv[:, :, HEAD:].reshape(MLA_KV_LORA, BRANCH).astype(BF16)
    return wq_p, wk_p, wv_p


def _s5_tables(lam_re, lam_im, log_dt, b_re, b_im, c_re, c_im):
    dt = jnp.exp(log_dt)[:, None]
    mag = jnp.exp(lam_re * dt)
    ang = lam_im * dt
    lb_re, lb_im = mag * jnp.cos(ang), mag * jnp.sin(ang)
    nr, ni = lb_re - 1.0, lb_im
    den = lam_re * lam_re + lam_im * lam_im
    f_re = (nr * lam_re + ni * lam_im) / den
    f_im = (ni * lam_re - nr * lam_im) / den
    bb_re = f_re[..., None] * b_re - f_im[..., None] * b_im
    bb_im = f_re[..., None] * b_im + f_im[..., None] * b_re
    eye = jnp.eye(SSM_GROUPS, dtype=F32)
    blk_in = lambda m: jnp.einsum('gpc,gh->gchp', m, eye).reshape(BRANCH, SSM_W)
    blk_out = lambda m: jnp.einsum('gcp,gh->gphc', m, eye).reshape(SSM_W, BRANCH)
    bblk = jnp.concatenate([blk_in(bb_re), blk_in(bb_im)], axis=1).astype(BF16)
    cblk = jnp.concatenate([blk_out(c_re), blk_out(-c_im)], axis=0).astype(BF16)

    def power(j):
        m = jnp.exp(lam_re * dt * j)
        return (m * jnp.cos(ang * j)).reshape(1, SSM_W), (m * jnp.sin(ang * j)).reshape(1, SSM_W)

    lam = jnp.concatenate([p for j in (1, 2, 4) for p in power(j)] + [jnp.zeros((2, SSM_W), F32)], axis=0)
    pws = [power(j) for j in range(1, 9)]
    pw = jnp.concatenate([p[0] for p in pws] + [p[1] for p in pws], axis=0)
    return bblk, cblk, lam, pw


def _pair_states(s):
    b = s.shape[0]
    s = s.reshape(b, 4, 2, HEAD, HEAD)
    return jnp.einsum('bjivk,ih->bjivhk', s, jnp.eye(2, dtype=s.dtype)).reshape(b, 4, LANES, LANES)


def _unpair_states(s):
    b = s.shape[0]
    s = s.reshape(b, 4, 2, HEAD, 2, HEAD)
    return jnp.stack([s[:, :, 0, :, 0, :], s[:, :, 1, :, 1, :]], axis=2).reshape(b, HEADS, HEAD, HEAD)


def _layer(x, pos0, st, lw, final_g, final):
    b, t, d = x.shape
    n = b * t
    x2 = x.reshape(n, d)
    proj = _norm_proj(x2, lw["norm_g"], lw["w_in_all"], lw["layer"])
    proj3 = proj.reshape(b, t, PROJ_W)
    past = st["past"]

    chunk = min(t, CHUNK)
    y_rw, s_new = _rwkv(proj3, st["sp_r"], st["sp_k"], st["sp_v"], st["sp_l"], _pair_states(st["wkv"]),
                        lw["rw_pvec"], lw["rw_mu_l"], lw["rw_w2p"], lw["rw_a2p"], lw["g2"], chunk)
    shift_new = jnp.concatenate([proj3[:, t - 1:, O_RWR:O_RWR + 3 * BRANCH],
                                 proj3[:, t - 1:, O_LORA:O_LORA + 2 * RW_LORA]], axis=-1)
    wkv_new = _unpair_states(s_new)

    y_ssm, hr, hi = _s5(proj3, st["ssm_re"], st["ssm_im"], lw["ssm_bblk"], lw["ssm_cblk"], lw["ssm_lam"],
                        lw["ssm_pw"], lw["ssm_d"], lw["ssm_wglu"], lw["ssm_bglu"])
    ssm_re_new = hr.reshape(b, SSM_GROUPS, SSM_STATE)
    ssm_im_new = hi.reshape(b, SSM_GROUPS, SSM_STATE)

    pos = pos0 + jnp.arange(t, dtype=jnp.int32)
    tq_tab, tc_tab, ts_tab = _rope_tables(pos)
    q, ckv, kr = _mla_prep(proj, t, tq_tab, tc_tab, ts_tab, lw["mla_gq"], lw["mla_gkv"], lw["mla_wq"])
    ckv3 = ckv.reshape(b, t, MLA_KV_LORA)
    kr3 = kr.reshape(b, t, MLA_ROPE)
    s_pad = -(-(past + t) // LANES) * LANES
    kc, vv = _mla_kv(ckv[None], kr[None], lw["mla_wk"], lw["mla_wv"], lw["dup"])
    kc3, vv3 = kc.reshape(b, t, HEADS * LANES), vv.reshape(b, t, BRANCH)
    kcp3 = vvp3 = None
    if past:
        kcp, vvp = _mla_kv(st["ckv_all"], st["kpe_all"], lw["mla_wk"], lw["mla_wv"], lw["dup"], lw["layer"])
        kcp3, vvp3 = kcp.reshape(b, past, HEADS * LANES), vvp.reshape(b, past, BRANCH)
    small = t <= 64
    assert small or not past
    tq = min(t, 256)
    hp = 4 if small else 1
    y_mla = _mla_attn(q.reshape(b, t, HEADS * LANES), kc3, vv3, kcp3, vvp3, s_pad, tq,
                      s_pad if small else min(512, s_pad), hp)

    sbk = proj3[:, :, O_SBK:O_SBK + BRANCH]
    sbv = proj3[:, :, O_SBV:O_SBV + BRANCH]
    if past:
        y_sb = _sb_decode(proj3, st["sbk_t"], st["sbv_t"], lw["layer"], min(256, past))
    else:
        y_sb = _sb_attn(proj3, tq, min(256, t))

    ys = [y.reshape(n, BRANCH) for y in (y_rw, y_ssm, y_mla, y_sb)]
    x_new = _merge(ys, proj, x2, lw["w_branch"], lw["b_merge"], lw["w_out"], final_g, final).reshape(b, t, d)
    new_state = (shift_new, wkv_new, ssm_re_new, ssm_im_new, ckv3, kr3,
                 sbk.reshape(b, t, HEADS, HEAD), sbv.reshape(b, t, HEADS, HEAD))
    return x_new, new_state


def kernel(x_prompt, x_sample, state_rwkv_shift, state_rwkv_wkv, state_ssm_re, state_ssm_im, cache_mla_ckv, cache_mla_kpe, cache_sb_k, cache_sb_v, norm_g, w_in, rw_mu, rw_w0, rw_w2, rw_a0, rw_a2, rw_k_k, rw_k_a, rw_r_k, rw_lnx_g, rw_lnx_b, ssm_lam_re, ssm_lam_im, ssm_log_dt, ssm_b_re, ssm_b_im, ssm_c_re, ssm_c_im, ssm_d, ssm_w_glu, ssm_b_glu, mla_q_norm, mla_w_q_up, mla_kv_norm, mla_w_kv_up, w_branch, b_merge, w_out, final_norm_g):
    depth = w_in.shape[0]
    bp, tp, _ = x_prompt.shape
    bs = x_sample.shape[0]
    past = cache_mla_ckv.shape[2]

    w_in_p = _permute_w_in(w_in)
    lane_i = jnp.arange(LANES)
    g2 = ((lane_i[:, None] // HEAD) == (lane_i[None, :] // HEAD)).astype(BF16)
    rope_i = jnp.arange(MLA_ROPE)
    dup = ((lane_i[None, :] == rope_i[:, None] + MLA_NOPE)
           | (lane_i[None, :] == rope_i[:, None] + MLA_NOPE + MLA_ROPE)).astype(BF16)
    final_g = final_norm_g.reshape(1, D_MODEL)
    zpad = jnp.zeros((RW_LORA, BRANCH), F32)

    layers = []
    for l in range(depth):
        wq_p, wk_p, wv_p = _mla_weights(mla_w_q_up[l], mla_w_kv_up[l])
        bblk, cblk, lam, pw = _s5_tables(ssm_lam_re[l], ssm_lam_im[l], ssm_log_dt[l], ssm_b_re[l], ssm_b_im[l],
                                         ssm_c_re[l], ssm_c_im[l])
        mu = rw_mu[l]
        rows = [mu[:BRANCH], mu[BRANCH:2 * BRANCH], mu[2 * BRANCH:3 * BRANCH], rw_w0[l], rw_a0[l], rw_k_k[l],
                rw_k_a[l], rw_r_k[l].reshape(BRANCH), rw_lnx_g[l], rw_lnx_b[l]]
        pvec = jnp.concatenate([jnp.stack(rows), jnp.zeros((16 - len(rows), BRANCH), F32)], axis=0)
        layers.append(dict(
            norm_g=norm_g[l].reshape(1, D_MODEL), w_in_all=w_in_p, layer=l,
            rw_pvec=pvec, rw_mu_l=mu[3 * BRANCH:].reshape(1, 2 * RW_LORA),
            rw_w2p=jnp.concatenate([rw_w2[l], zpad], axis=0).astype(BF16),
            rw_a2p=jnp.concatenate([zpad, rw_a2[l]], axis=0).astype(BF16),
            g2=g2, dup=dup,
            ssm_bblk=bblk, ssm_cblk=cblk, ssm_lam=lam, ssm_pw=pw,
            ssm_d=ssm_d[l].reshape(1, BRANCH), ssm_wglu=ssm_w_glu[l].astype(BF16),
            ssm_bglu=ssm_b_glu[l].reshape(1, BRANCH),
            mla_gq=mla_q_norm[l].reshape(1, MLA_Q_LORA), mla_gkv=mla_kv_norm[l].reshape(1, MLA_KV_LORA),
            mla_wq=wq_p, mla_wk=wk_p, mla_wv=wv_p,
            w_branch=w_branch[l].astype(BF16), b_merge=b_merge[l].reshape(N_BRANCH, 1, D_MODEL),
            w_out=w_out[l].astype(BF16)))

    def fresh(bn):
        return dict(sp_r=jnp.zeros((bn, 1, BRANCH), F32), sp_k=jnp.zeros((bn, 1, BRANCH), F32),
                    sp_v=jnp.zeros((bn, 1, BRANCH), F32), sp_l=jnp.zeros((bn, 1, 2 * RW_LORA), F32),
                    wkv=jnp.zeros((bn, HEADS, HEAD, HEAD), F32),
                    ssm_re=jnp.zeros((bn, 1, SSM_W), F32), ssm_im=jnp.zeros((bn, 1, SSM_W), F32),
                    past=0)

    nl = cache_mla_ckv.shape[0]
    ckv_all = cache_mla_ckv.reshape(nl, bs * past, MLA_KV_LORA)
    kpe_all = cache_mla_kpe.reshape(nl, bs * past, MLA_ROPE)
    sbk_t = jnp.transpose(cache_sb_k, (0, 1, 3, 4, 2)).reshape(nl, bs, HEADS // 2, LANES, past)
    sbv_t = jnp.transpose(cache_sb_v, (0, 1, 3, 4, 2)).reshape(nl, bs, HEADS // 2, LANES, past)

    def carried(l):
        sh = state_rwkv_shift[l]
        return dict(sp_r=sh[:, :, :BRANCH], sp_k=sh[:, :, BRANCH:2 * BRANCH], sp_v=sh[:, :, 2 * BRANCH:3 * BRANCH],
                    sp_l=sh[:, :, 3 * BRANCH:], wkv=state_rwkv_wkv[l],
                    ssm_re=state_ssm_re[l].reshape(bs, 1, SSM_W), ssm_im=state_ssm_im[l].reshape(bs, 1, SSM_W),
                    past=past, ckv_all=ckv_all, kpe_all=kpe_all, sbk_t=sbk_t, sbv_t=sbv_t)

    xp, xs = x_prompt, x_sample
    new_p, new_s = [], []
    for l in range(depth):
        last = l == depth - 1
        xp, st_p = _layer(xp, 0, fresh(bp), layers[l], final_g, last)
        xs, st_s = _layer(xs, past, carried(l), layers[l], final_g, last)
        new_p.append(st_p)
        new_s.append(st_s)
    stk = lambda lst, i: jnp.stack([s[i] for s in lst], axis=0)
    return (xp, xs) + tuple(stk(new_p, i) for i in range(8)) + tuple(stk(new_s, i) for i in range(8))
```

```python
import functools
import math

import jax
import jax.numpy as jnp
from jax import lax
from jax.experimental import pallas as pl
from jax.experimental.pallas import tpu as pltpu

F32 = jnp.float32
BF16 = jnp.bfloat16

D_MODEL = 2048
BRANCH = 512
N_BRANCH = 4
EPS = 1e-6
CHUNK = 64
HEAD = 64
HEADS = 8
RW_LORA = 64
RW_IN = 3 * BRANCH + 2 * RW_LORA
RW_GN_EPS = 64e-5
SSM_GROUP = 16
SSM_GROUPS = 32
SSM_STATE = 64
SSM_W = SSM_GROUPS * SSM_STATE
MLA_NOPE = 64
MLA_ROPE = 32
MLA_Q_LORA = 384
MLA_KV_LORA = 256
MLA_SCALE = 1.0 / math.sqrt(MLA_NOPE + MLA_ROPE)
ROPE_BASE = 10000.0
SB_SCALE = 1.0 / math.sqrt(HEAD)
LANES = 128
MIB = 1024 * 1024

O_MERGE = 0
O_GATE = 8192
O_SBQ = 10240
O_SBK = 10752
O_SBV = 11264
O_RWR = 11776
O_RWK = 12288
O_RWV = 12800
O_SSM = 13312
O_QLAT = 13824
O_LORA = 14208
O_KVLAT = 14336
O_KPE = 14592
O_KPESW = 14720
PROJ_W = 14848


def _cparams(sem, vmem_mib):
    return pltpu.CompilerParams(dimension_semantics=sem, vmem_limit_bytes=vmem_mib * MIB)


def _dot(a, b):
    return jnp.dot(a, b, preferred_element_type=F32)


def _dot_nt(a, b):
    return lax.dot_general(a, b, (((1,), (1,)), ((), ())), preferred_element_type=F32)


def _split3(x):
    h1 = x.astype(BF16)
    r1 = x - h1.astype(F32)
    h2 = r1.astype(BF16)
    h3 = (r1 - h2.astype(F32)).astype(BF16)
    return h1, h2, h3


def _softplus(x):
    return jnp.maximum(x, 0.0) + jnp.log1p(jnp.exp(-jnp.abs(x)))


def _norm_proj_kernel(x_ref, g_ref, w_ref, o_ref, h_ref):
    @pl.when(pl.program_id(1) == 0)
    def _():
        x = x_ref[...]
        ms = jnp.mean(x * x, axis=-1, keepdims=True)
        h_ref[...] = (x * lax.rsqrt(ms + EPS) * g_ref[...]).astype(BF16)

    o_ref[...] = _dot(h_ref[...], w_ref[0])


def _norm_proj(x2, g, w_all, layer):
    n, d = x2.shape
    width = w_all.shape[2]
    tm = min(n, 1024)
    tn = 512
    return pl.pallas_call(
        _norm_proj_kernel,
        out_shape=jax.ShapeDtypeStruct((n, width), F32),
        grid=(n // tm, width // tn),
        in_specs=[
            pl.BlockSpec((tm, d), lambda i, j: (i, 0)),
            pl.BlockSpec((1, d), lambda i, j: (0, 0)),
            pl.BlockSpec((1, d, tn), lambda i, j: (layer, 0, j)),
        ],
        out_specs=pl.BlockSpec((tm, tn), lambda i, j: (i, j)),
        scratch_shapes=[pltpu.VMEM((tm, d), BF16)],
        compiler_params=_cparams(("parallel", "arbitrary"), 40),
        name="norm_proj",
    )(x2, g, w_all)


def _segsum(x, g2):
    outs = []
    for j in range(BRANCH // LANES):
        h1, h2, h3 = _split3(x[:, LANES * j:LANES * (j + 1)])
        outs.append(_dot(h1, g2) + _dot(h2, g2) + _dot(h3, g2))
    return jnp.concatenate(outs, axis=1)


def _rwkv_kernel(pr_ref, pk_ref, pv_ref, plo_ref, spr_ref, spk_ref, spv_ref, spl_ref, s0_ref,
                 pvec_ref, mul_ref, w2_ref, a2_ref, g2_ref,
                 y_ref, sout_ref,
                 st_ref, cr_ref, ck_ref, cv_ref, cl_ref, *, C, nb):
    c = pl.program_id(1)

    @pl.when(c == 0)
    def _():
        st_ref[...] = s0_ref[...]
        cr_ref[...] = spr_ref[...]
        ck_ref[...] = spk_ref[...]
        cv_ref[...] = spv_ref[...]
        cl_ref[...] = spl_ref[...]

    pvec = pvec_ref[...]
    prow = lambda i: pvec[i:i + 1, :]
    g2 = g2_ref[...]

    def tshift(p_ref, carry_ref, mu):
        outs = []
        for bi in range(nb):
            p = p_ref[bi]
            rolled = pltpu.roll(p, 1, 0)
            rid = lax.broadcasted_iota(jnp.int32, p.shape, 0)
            prev = jnp.where(rid == 0, carry_ref[bi], rolled)
            carry_ref[bi] = p[C - 1:C, :]
            outs.append(p + (prev - p) * mu)
        return jnp.concatenate(outs, axis=0)

    r = tshift(pr_ref, cr_ref, prow(0))
    k = tshift(pk_ref, ck_ref, prow(1))
    v = tshift(pv_ref, cv_ref, prow(2))
    lo = tshift(plo_ref, cl_ref, mul_ref[...])

    log_w = -_softplus(-(prow(3) + _dot(jnp.tanh(lo).astype(BF16), w2_ref[...]))) - 0.5
    ld = -jnp.exp(log_w)
    a_icl = jax.nn.sigmoid(prow(4) + _dot(lo.astype(BF16), a2_ref[...]))
    kkr = k * prow(5)
    kk = kkr / jnp.maximum(jnp.sqrt(_segsum(kkr * kkr, g2)), 1e-12)
    k2 = k * (1.0 + (a_icl - 1.0) * prow(6))
    av = -kk
    bv = kk * a_icl

    sh = C.bit_length() - 1
    ri = lax.broadcasted_iota(jnp.int32, (nb * C, nb * C), 0)
    ci = lax.broadcasted_iota(jnp.int32, (nb * C, nb * C), 1)
    lincl = jnp.logical_and(ci <= ri, (ci >> sh) == (ri >> sh)).astype(BF16)
    h1, h2, h3 = _split3(ld)
    g = _dot(lincl, h1) + _dot(lincl, h2) + _dot(lincl, h3)
    g_lasts = [g[(bi + 1) * C - 1:(bi + 1) * C, :] for bi in range(nb)]
    g_last = jnp.concatenate([jnp.broadcast_to(gl, (C, BRANCH)) for gl in g_lasts], axis=0)
    eg = jnp.exp(g)
    eng = jnp.exp(-g)
    at = av * jnp.exp(g - ld)
    rt = r * eg
    bt = bv * eng
    kt = k2 * eng
    e_c = jnp.exp(g_last - g)
    b_end = bv * e_c
    k_end = k2 * e_c
    eg_lasts = [jnp.exp(gl) for gl in g_lasts]

    c2 = 2 * C
    npair = BRANCH // LANES
    r_i = lax.broadcasted_iota(jnp.int32, (c2, c2), 0)
    c_i = lax.broadcasted_iota(jnp.int32, (c2, c2), 1)
    c_m = jnp.where(c_i >= C, c_i - C, c_i)
    mhalf = c_m < jnp.where(r_i < C, r_i, r_i - C + 1)
    mfull = jnp.concatenate([mhalf, mhalf], axis=0)
    bd2 = (r_i >> sh) == (c_i >> sh)
    eye2 = (r_i == c_i).astype(F32)
    lane = lax.broadcasted_iota(jnp.int32, (1, LANES), 1)
    m0 = (lane < HEAD).astype(F32)
    m1 = (lane >= HEAD).astype(F32)
    v_i = lax.broadcasted_iota(jnp.int32, (LANES, LANES), 0)
    k_i = lax.broadcasted_iota(jnp.int32, (LANES, LANES), 1)
    bdmask = (v_i >> 6) == (k_i >> 6)
    zf = jnp.zeros((C, LANES), F32)
    zc = jnp.zeros((C, LANES), BF16)
    sls = [slice(LANES * j, LANES * (j + 1)) for j in range(npair)]

    units = [(bi, j) for bi in range(nb) for j in range(npair)]
    tile = lambda x, bi, j: x[bi * C:(bi + 1) * C, LANES * j:LANES * (j + 1)]
    ws, a_ss, s2s = [], [], []
    for bi, j in units:
        atj, rtj = tile(at, bi, j), tile(rt, bi, j)
        a0, r0, a1, r1 = atj * m0, rtj * m0, atj * m1, rtj * m1
        lhs_w = jnp.concatenate([jnp.concatenate([a0, zf], axis=1), jnp.concatenate([r0, zf], axis=1),
                                 jnp.concatenate([zf, a1], axis=1), jnp.concatenate([zf, r1], axis=1)],
                                axis=0).astype(BF16)
        btj, ktj = tile(bt, bi, j), tile(kt, bi, j)
        rhs_w = jnp.concatenate([jnp.concatenate([btj, ktj], axis=0),
                                 jnp.concatenate([ktj, btj], axis=0)], axis=1).astype(BF16)
        ws.append(jnp.where(mfull, _dot_nt(lhs_w, rhs_w), 0.0))
        lhs_s = jnp.concatenate([a0, r0, a1, r1], axis=0).astype(BF16)
        s2 = st_ref[bi, j]
        s2h, s2l, _ = _split3(s2)
        a_ss.append(_dot_nt(lhs_s, s2h) + _dot_nt(lhs_s, s2l))
        s2s.append(s2)

    n_pows = [jnp.where(bd2, jnp.concatenate([w[:C], w[c2:c2 + C]], axis=0), 0.0) for w in ws]
    t_mats = [eye2 + n for n in n_pows]
    lv = 2
    while lv < C:
        n_pows = [_dot(n.astype(BF16), n.astype(BF16)) for n in n_pows]
        t_mats = [t + _dot(t.astype(BF16), n.astype(BF16)) for t, n in zip(t_mats, n_pows)]
        lv *= 2

    wbs = [w.astype(BF16) for w in ws]
    v0s = [(tile(v, bi, j) * m0).astype(BF16) for bi, j in units]
    v1s = [(tile(v, bi, j) * m1).astype(BF16) for bi, j in units]
    xs = [jnp.concatenate([a_s[:C] + _dot(wb[:C], jnp.concatenate([zc, v0], axis=0)),
                           a_s[c2:c2 + C] + _dot(wb[c2:c2 + C], jnp.concatenate([v1, zc], axis=0))], axis=0)
          for a_s, wb, v0, v1 in zip(a_ss, wbs, v0s, v1s)]
    us = [_dot(t.astype(BF16), x.astype(BF16)) for t, x in zip(t_mats, xs)]
    ys = []
    for n, (bi, j) in enumerate(units):
        a_s, wb, u = a_ss[n], wbs[n], us[n]
        u0, u1 = u[:C], u[C:]
        y0 = a_s[C:c2] + _dot(wb[C:c2], jnp.concatenate([u0.astype(BF16), v0s[n]], axis=0))
        y1 = a_s[c2 + C:] + _dot(wb[c2 + C:], jnp.concatenate([v1s[n], u1.astype(BF16)], axis=0))
        ys.append(y0 + y1)
        uv = jnp.concatenate([u0 + u1, tile(v, bi, j)], axis=0)
        bk = jnp.concatenate([tile(b_end, bi, j), tile(k_end, bi, j)], axis=0).astype(BF16)
        upd = _dot(uv.T.astype(BF16), bk)
        st_ref[bi, j] = s2s[n] * eg_lasts[bi][:, sls[j]] + jnp.where(bdmask, upd, 0.0)
    y = jnp.concatenate([jnp.concatenate(ys[bi * npair:(bi + 1) * npair], axis=1) for bi in range(nb)],
                        axis=0)

    inv_n = 1.0 / HEAD
    yc = y - _segsum(y, g2) * inv_n
    yn = yc * lax.rsqrt(_segsum(yc * yc, g2) * inv_n + RW_GN_EPS) * prow(8) + prow(9)
    out = (yn + _segsum(r * k2 * prow(7), g2) * v).astype(y_ref.dtype)
    for bi in range(nb):
        y_ref[bi] = out[bi * C:(bi + 1) * C]

    @pl.when(c == pl.num_programs(1) - 1)
    def _():
        sout_ref[...] = st_ref[...]


def _rwkv(proj3, sp_r, sp_k, sp_v, sp_l, s0, pvec, mu_l, w2p, a2p, g2, chunk):
    b, t, _ = proj3.shape
    nc = t // chunk
    nb = min(b, 4, 2 * CHUNK // chunk)
    cb = lambda off, wdt: off // wdt
    bspec = lambda off, wdt: pl.BlockSpec((nb, chunk, wdt), lambda i, c, o=cb(off, wdt): (i, c, o))
    row3 = lambda wdt: pl.BlockSpec((nb, 1, wdt), lambda i, c: (i, 0, 0))
    full2 = lambda a: pl.BlockSpec(a.shape, lambda i, c: (0, 0))
    return pl.pallas_call(
        functools.partial(_rwkv_kernel, C=chunk, nb=nb),
        out_shape=(jax.ShapeDtypeStruct((b, t, BRANCH), BF16),
                   jax.ShapeDtypeStruct((b, 4, LANES, LANES), F32)),
        grid=(b // nb, nc),
        in_specs=[
            bspec(O_RWR, BRANCH), bspec(O_RWK, BRANCH), bspec(O_RWV, BRANCH), bspec(O_LORA, LANES),
            row3(BRANCH), row3(BRANCH), row3(BRANCH), row3(LANES),
            pl.BlockSpec((nb, 4, LANES, LANES), lambda i, c: (i, 0, 0, 0)),
            full2(pvec), full2(mu_l), full2(w2p), full2(a2p), full2(g2),
        ],
        out_specs=(pl.BlockSpec((nb, chunk, BRANCH), lambda i, c: (i, c, 0)),
                   pl.BlockSpec((nb, 4, LANES, LANES), lambda i, c: (i, 0, 0, 0))),
        scratch_shapes=[pltpu.VMEM((nb, 4, LANES, LANES), F32),
                        pltpu.VMEM((nb, 1, BRANCH), F32), pltpu.VMEM((nb, 1, BRANCH), F32),
                        pltpu.VMEM((nb, 1, BRANCH), F32), pltpu.VMEM((nb, 1, LANES), F32)],
        compiler_params=_cparams(("parallel", "arbitrary"), 32),
        name="rwkv7",
    )(proj3, proj3, proj3, proj3, sp_r, sp_k, sp_v, sp_l, s0, pvec, mu_l, w2p, a2p, g2)


SSM_CHUNK = 16
SSM_IO = SSM_CHUNK * SSM_GROUP
SSM_RI = 2 * SSM_STATE


def _s5_group_kernel(u_ref, k_ref, w_ref, c_ref, lam_ref, pw_ref, h0_ref, y_ref, hf_ref, s_ref, *, nb, nseg):
    ub = u_ref[0].astype(BF16)
    y_loc = _dot(ub, k_ref[0])
    s = _dot(ub, w_ref[0])
    lam = lam_ref[0]
    h0 = [h0_ref[0, b] for b in range(nb)]

    def cmul(a1, a2, h):
        return a1 * h + a2 * pltpu.roll(h, SSM_STATE, 1)

    if nseg == 1:
        h_start = jnp.concatenate(h0, axis=0)
        h_in = s + cmul(lam[0:1], lam[3:4], h_start)
    else:
        rows = nb * nseg
        rid = lax.broadcasted_iota(jnp.int32, (rows, 1), 0)
        x = s
        for lvl, sft in enumerate((1, 2, 4)):
            sh = pltpu.roll(x, sft, 0)
            x = x + jnp.where((rid & 7) >= sft, cmul(lam[lvl:lvl + 1], lam[3 + lvl:4 + lvl], sh), 0.0)
        s_ref[...] = x
        pw1 = pw_ref[0, 0:8]
        pw2 = pw_ref[0, 8:16]
        for b in range(nb):

            def body(gi, carry, b=b):
                o = pl.multiple_of(b * nseg + gi * 8, 8)
                blk = s_ref[pl.ds(o, 8), :] + cmul(pw1, pw2, carry)
                s_ref[pl.ds(o, 8), :] = blk
                return blk[7:8]

            lax.fori_loop(0, nseg // 8, body, h0[b])
        h_in = s_ref[...]
        first = jnp.concatenate([jnp.broadcast_to(h, (nseg, SSM_RI)) for h in h0], axis=0)
        seg_start = (rid & (nseg - 1)) == 0
        h_start = jnp.where(seg_start, first, pltpu.roll(h_in, 1, 0))
    for b in range(nb):
        hf_ref[0, b] = h_in[(b + 1) * nseg - 1:(b + 1) * nseg]
    y_ref[0] = y_loc + _dot(h_start.astype(BF16), c_ref[0])


def _s5_groups(u_g, kmat, wst, cst, lam, pw, h0_g, nb):
    g, rows, _ = u_g.shape
    nseg = rows // nb
    gspec = lambda a: pl.BlockSpec((1,) + a.shape[1:], lambda i: (i,) + (0,) * (a.ndim - 1))
    return pl.pallas_call(
        functools.partial(_s5_group_kernel, nb=nb, nseg=nseg),
        out_shape=(jax.ShapeDtypeStruct((g, rows, SSM_IO), F32),
                   jax.ShapeDtypeStruct((g, nb, 1, SSM_RI), F32)),
        grid=(g,),
        in_specs=[gspec(u_g), gspec(kmat), gspec(wst), gspec(cst), gspec(lam), gspec(pw), gspec(h0_g)],
        out_specs=(pl.BlockSpec((1, rows, SSM_IO), lambda i: (i, 0, 0)),
                   pl.BlockSpec((1, nb, 1, SSM_RI), lambda i: (i, 0, 0, 0))),
        scratch_shapes=[pltpu.VMEM((rows, SSM_RI), F32)],
        compiler_params=_cparams(("parallel",), 32),
        name="s5_groups",
    )(u_g, kmat, wst, cst, lam, pw, h0_g)


def _s5_out_kernel(y_ref, u_ref, dsk_ref, wglu_ref, bglu_ref, o_ref):
    g = jax.nn.gelu(y_ref[0] + dsk_ref[...] * u_ref[0])
    o_ref[0] = (g * jax.nn.sigmoid(_dot(g.astype(BF16), wglu_ref[...]) + bglu_ref[...])).astype(o_ref.dtype)


def _s5_out(y3, proj3, dsk, wglu, bglu):
    b, t, _ = y3.shape
    tb = min(t, 512)
    full2 = lambda a: pl.BlockSpec(a.shape, lambda i, c: (0, 0))
    return pl.pallas_call(
        _s5_out_kernel,
        out_shape=jax.ShapeDtypeStruct((b, t, BRANCH), BF16),
        grid=(b, t // tb),
        in_specs=[pl.BlockSpec((1, tb, BRANCH), lambda i, c: (i, c, 0)),
                  pl.BlockSpec((1, tb, BRANCH), lambda i, c: (i, c, O_SSM // BRANCH)),
                  full2(dsk), full2(wglu), full2(bglu)],
        out_specs=pl.BlockSpec((1, tb, BRANCH), lambda i, c: (i, c, 0)),
        compiler_params=_cparams(("parallel", "parallel"), 32),
        name="s5_out",
    )(y3, proj3, dsk, wglu, bglu)


def _s5(proj3, h0_re, h0_im, tabs, dsk, wglu, bglu):
    b, t, _ = proj3.shape
    nch = t // SSM_CHUNK
    u = proj3[:, :, O_SSM:O_SSM + BRANCH].reshape(b, nch, SSM_CHUNK, SSM_GROUPS, SSM_GROUP)
    u_g = jnp.transpose(u, (3, 0, 1, 2, 4)).reshape(SSM_GROUPS, b * nch, SSM_IO)
    h0_g = jnp.transpose(jnp.concatenate([h0_re, h0_im], axis=-1), (1, 0, 2)).reshape(SSM_GROUPS, b, 1, SSM_RI)
    y_g, hf = _s5_groups(u_g, tabs["kmat"], tabs["wst"], tabs["cst"], tabs["lam"], tabs["pw"], h0_g, b)
    y = y_g.reshape(SSM_GROUPS, b, nch, SSM_CHUNK, SSM_GROUP)
    y3 = jnp.transpose(y, (1, 2, 3, 0, 4)).reshape(b, t, BRANCH)
    hf = jnp.transpose(hf.reshape(SSM_GROUPS, b, SSM_RI), (1, 0, 2))
    return _s5_out(y3, proj3, dsk, wglu, bglu), hf[:, :, :SSM_STATE], hf[:, :, SSM_STATE:]


def _mla_prep_kernel(ql_ref, kvl_ref, kpe_ref, kpesw_ref, tq_ref, tc_ref, ts_ref,
                     gq_ref, gkv_ref, wq_ref, q_ref, ckv_ref, kr_ref):
    ql = ql_ref[...]
    qn = ql * lax.rsqrt(jnp.mean(ql * ql, axis=-1, keepdims=True) + EPS) * gq_ref[...]
    q = _dot(qn.astype(BF16), wq_ref[...])
    tq = tq_ref[...]
    q_ref[...] = jnp.concatenate(
        [q[:, LANES * h:LANES * (h + 1)] * tq for h in range(HEADS)], axis=1).astype(BF16)
    kvl = kvl_ref[...]
    ckv_ref[...] = kvl * lax.rsqrt(jnp.mean(kvl * kvl, axis=-1, keepdims=True) + EPS) * gkv_ref[...]
    kr = kpe_ref[...] * tc_ref[...] + kpesw_ref[...] * ts_ref[...]
    kr_ref[...] = kr[:, :MLA_ROPE]


def _mla_prep(proj, t, tq_tab, tc_tab, ts_tab, gq, gkv, wq):
    n = proj.shape[0]
    tm = min(t, 512)
    nt = t // tm
    tab = pl.BlockSpec((tm, LANES), lambda i: (i % nt, 0))
    full2 = lambda a: pl.BlockSpec(a.shape, lambda i: (0, 0))
    return pl.pallas_call(
        _mla_prep_kernel,
        out_shape=(jax.ShapeDtypeStruct((n, HEADS * LANES), BF16),
                   jax.ShapeDtypeStruct((n, MLA_KV_LORA), F32),
                   jax.ShapeDtypeStruct((n, MLA_ROPE), F32)),
        grid=(n // tm,),
        in_specs=[pl.BlockSpec((tm, MLA_Q_LORA), lambda i: (i, O_QLAT // MLA_Q_LORA)),
                  pl.BlockSpec((tm, MLA_KV_LORA), lambda i: (i, O_KVLAT // MLA_KV_LORA)),
                  pl.BlockSpec((tm, LANES), lambda i: (i, O_KPE // LANES)),
                  pl.BlockSpec((tm, LANES), lambda i: (i, O_KPESW // LANES)),
                  tab, tab, tab, full2(gq), full2(gkv), full2(wq)],
        out_specs=(pl.BlockSpec((tm, HEADS * LANES), lambda i: (i, 0)),
                   pl.BlockSpec((tm, MLA_KV_LORA), lambda i: (i, 0)),
                   pl.BlockSpec((tm, MLA_ROPE), lambda i: (i, 0))),
        compiler_params=_cparams(("parallel",), 32),
        name="mla_prep",
    )(proj, proj, proj, proj, tq_tab, tc_tab, ts_tab, gq, gkv, wq)


def _mla_kv_kernel(ckv_ref, kr_ref, wk_ref, wv_ref, dup_ref, k_ref, v_ref):
    cb = ckv_ref[0].astype(BF16)
    kn = _dot(cb, wk_ref[...])
    krd = _dot(kr_ref[0].astype(BF16), dup_ref[...])
    k_ref[...] = jnp.concatenate(
        [kn[:, LANES * h:LANES * (h + 1)] + krd for h in range(HEADS)], axis=1).astype(BF16)
    v_ref[...] = _dot(cb, wv_ref[...]).astype(BF16)


def _mla_kv(ckv, kr, wk, wv, dup, layer=0):
    n = ckv.shape[1]
    tm = next((c for c in (512, LANES) if n % c == 0), n)
    full2 = lambda a: pl.BlockSpec(a.shape, lambda i: (0, 0))
    return pl.pallas_call(
        _mla_kv_kernel,
        out_shape=(jax.ShapeDtypeStruct((n, HEADS * LANES), BF16),
                   jax.ShapeDtypeStruct((n, BRANCH), BF16)),
        grid=(n // tm,),
        in_specs=[pl.BlockSpec((1, tm, MLA_KV_LORA), lambda i: (layer, i, 0)),
                  pl.BlockSpec((1, tm, MLA_ROPE), lambda i: (layer, i, 0)),
                  full2(wk), full2(wv), full2(dup)],
        out_specs=(pl.BlockSpec((tm, HEADS * LANES), lambda i: (i, 0)),
                   pl.BlockSpec((tm, BRANCH), lambda i: (i, 0))),
        compiler_params=_cparams(("parallel",), 32),
        name="mla_kv",
    )(ckv, kr, wk, wv, dup)


def _assemble_keys(past_ref, new_ref, all_ref, past, t):
    all_ref[0:past, :] = past_ref[0].astype(BF16)
    all_ref[past:past + t, :] = new_ref[0].astype(BF16)
    pad = all_ref.shape[0] - past - t
    if pad:
        all_ref[past + t:, :] = jnp.zeros((pad, all_ref.shape[1]), BF16)


def _mla_attn_kernel(q_ref, k_ref, v_ref, *rest, tq, tk, q_off, s_valid, hp, past):
    if past:
        kp_ref, vp_ref, o_ref, kall_ref, vall_ref = rest
        _assemble_keys(kp_ref, k_ref, kall_ref, past, tq)
        _assemble_keys(vp_ref, v_ref, vall_ref, past, tq)
        kload = lambda o: kall_ref[pl.ds(o, tk), :]
        vload = lambda o: vall_ref[pl.ds(o, tk), :]
    else:
        (o_ref,) = rest
        kload = lambda o: k_ref[0, pl.ds(o, tk), :]
        vload = lambda o: v_ref[0, pl.ds(o, tk), :]
    i = pl.program_id(2)
    q_lo = q_off + i * tq
    limit = jnp.minimum(q_lo + tq, s_valid)
    nkb = (limit + tk - 1) // tk
    qidx = q_lo + lax.broadcasted_iota(jnp.int32, (tq, tk), 0)
    kloc = lax.broadcasted_iota(jnp.int32, (tq, tk), 1)
    nh = 2 * hp
    q2 = q_ref[0]
    qs = [q2[:, LANES * h:LANES * (h + 1)] for h in range(nh)]

    def body(kb, carry, masked):
        o = pl.multiple_of(kb * tk, tk)
        kblk = kload(o)
        vblk = vload(o)
        ss = [_dot_nt(qs[h], kblk[:, LANES * h:LANES * (h + 1)]) for h in range(nh)]
        if masked:
            kidx = kloc + kb * tk
            vis = jnp.logical_and((kidx >> 6) <= (qidx >> 6), kidx < s_valid)
            ss = [jnp.where(vis, s, -1e30) for s in ss]
        m_new = [jnp.maximum(carry[h][0], jnp.max(ss[h], axis=-1, keepdims=True)) for h in range(nh)]
        ps = [jnp.exp(ss[h] - m_new[h]) for h in range(nh)]
        pv = [_dot(ps[h].astype(BF16), vblk[:, LANES * (h // 2):LANES * (h // 2 + 1)]) for h in range(nh)]
        new = []
        for h in range(nh):
            m_prev, l_prev, acc = carry[h]
            alpha = jnp.exp(m_prev - m_new[h])
            new.append((m_new[h], alpha * l_prev + jnp.sum(ps[h], axis=-1, keepdims=True), alpha * acc + pv[h]))
        return tuple(new)

    init = tuple((jnp.full((tq, 1), -1e30, F32), jnp.zeros((tq, 1), F32), jnp.zeros((tq, LANES), F32))
                 for _ in range(nh))
    nfull = jnp.minimum(((q_lo >> 6) + 1) * CHUNK, s_valid) // tk
    res = lax.fori_loop(0, nfull, lambda kb, c: body(kb, c, False), init)
    res = lax.fori_loop(nfull, nkb, lambda kb, c: body(kb, c, True), res)
    lane = lax.broadcasted_iota(jnp.int32, (tq, LANES), 1)
    outs = [jnp.where(lane < HEAD, res[2 * p][2] / res[2 * p][1], res[2 * p + 1][2] / res[2 * p + 1][1])
            for p in range(hp)]
    o_ref[0] = jnp.concatenate(outs, axis=1).astype(o_ref.dtype)


def _mla_attn(q3, k3, v3, kp3, vp3, s_pad, tq, tk, hp):
    b, t, _ = q3.shape
    past = 0 if kp3 is None else kp3.shape[1]
    rows = k3.shape[1]
    in_specs = [pl.BlockSpec((1, tq, 2 * LANES * hp), lambda bi, j, i: (bi, i, j)),
                pl.BlockSpec((1, rows, 2 * LANES * hp), lambda bi, j, i: (bi, 0, j)),
                pl.BlockSpec((1, rows, LANES * hp), lambda bi, j, i: (bi, 0, j))]
    args = [q3, k3, v3]
    scratch = []
    if past:
        in_specs += [pl.BlockSpec((1, past, 2 * LANES * hp), lambda bi, j, i: (bi, 0, j)),
                     pl.BlockSpec((1, past, LANES * hp), lambda bi, j, i: (bi, 0, j))]
        args += [kp3, vp3]
        scratch = [pltpu.VMEM((s_pad, 2 * LANES * hp), BF16), pltpu.VMEM((s_pad, LANES * hp), BF16)]
    return pl.pallas_call(
        functools.partial(_mla_attn_kernel, tq=tq, tk=tk, q_off=past, s_valid=past + t, hp=hp, past=past),
        out_shape=jax.ShapeDtypeStruct((b, t, BRANCH), BF16),
        grid=(b, 4 // hp, t // tq),
        in_specs=in_specs,
        out_specs=pl.BlockSpec((1, tq, LANES * hp), lambda bi, j, i: (bi, i, j)),
        scratch_shapes=scratch,
        compiler_params=_cparams(("parallel", "parallel", "arbitrary"), 40),
        name="mla_attn",
    )(*args)


def _sb_block(zs, vis, us, carry, pv):
    nh = len(zs)
    tq = zs[0].shape[0]
    zls = [jnp.minimum(z, 0.0) - jnp.log(1.0 + jnp.exp(-jnp.abs(z))) for z in zs]
    lgs = [zl - z for zl, z in zip(zls, zs)]
    if vis is not None:
        lgs = [jnp.where(vis, lg, 0.0) for lg in lgs]
    his = [lg.astype(BF16) for lg in lgs]
    los = [(lg - hi.astype(F32)).astype(BF16) for lg, hi in zip(lgs, his)]
    later = _dot(jnp.concatenate(his + los, axis=0), us)
    new = []
    for h in range(nh):
        csum, acc = carry[h]
        tot = later[h * tq:(h + 1) * tq] + later[(nh + h) * tq:(nh + h + 1) * tq] + csum
        a = jnp.exp(zls[h] + tot)
        if vis is not None:
            a = jnp.where(vis, a, 0.0)
        new.append((csum + jnp.sum(lgs[h], axis=-1, keepdims=True), acc + pv(h, a.astype(BF16))))
    return tuple(new)


def _sb_heads(q2, nh):
    lane1 = lax.broadcasted_iota(jnp.int32, (1, LANES), 1)
    hm = ((lane1 < HEAD).astype(F32) * SB_SCALE, (lane1 >= HEAD).astype(F32) * SB_SCALE)
    return [(q2[:, LANES * (h // 2):LANES * (h // 2 + 1)] * hm[h % 2]).astype(BF16) for h in range(nh)]


def _sb_finish(res, o_ref, tq):
    lane = lax.broadcasted_iota(jnp.int32, (tq, LANES), 1)
    outs = [jnp.where(lane < HEAD, res[2 * p][1], res[2 * p + 1][1]) for p in range(len(res) // 2)]
    o_ref[0] = jnp.concatenate(outs, axis=1).astype(o_ref.dtype)


def _sb_attn_kernel(q_ref, k_ref, v_ref, us_ref, o_ref, *, tq, tk):
    i = pl.program_id(2)
    q_lo = i * tq
    nkb = (q_lo + tq - 2) // tk + 1
    nfull = q_lo // tk
    ridx = lax.broadcasted_iota(jnp.int32, (2 * tq, tk), 0)
    qidx = q_lo + jnp.where(ridx >= tq, ridx - tq, ridx)
    kloc = lax.broadcasted_iota(jnp.int32, (2 * tq, tk), 1)
    qs = jnp.concatenate(_sb_heads(q_ref[0], 2), axis=0)
    us = us_ref[...]

    def block(kb, carry, masked):
        csum, acc = carry
        o = pl.multiple_of(kb * tk, tk)
        kblk = k_ref[0, pl.ds(o, tk), :].astype(BF16)
        vblk = v_ref[0, pl.ds(o, tk), :].astype(BF16)
        z = _dot_nt(qs, kblk)
        zl = jnp.minimum(z, 0.0) - jnp.log(1.0 + jnp.exp(-jnp.abs(z)))
        lg = zl - z
        if masked:
            vis = (kloc + kb * tk) < qidx
            lg = jnp.where(vis, lg, 0.0)
        hi = lg.astype(BF16)
        lo = (lg - hi.astype(F32)).astype(BF16)
        later = _dot(jnp.concatenate([hi, lo], axis=0), us)
        a = jnp.exp(zl + later[:2 * tq] + later[2 * tq:] + csum)
        if masked:
            a = jnp.where(vis, a, 0.0)
        return csum + jnp.sum(lg, axis=-1, keepdims=True), acc + _dot(a.astype(BF16), vblk)

    carry = (jnp.zeros((2 * tq, 1), F32), jnp.zeros((2 * tq, LANES), F32))
    carry = lax.fori_loop(0, nkb - nfull, lambda s, c: block(nkb - 1 - s, c, True), carry)
    carry = lax.fori_loop(0, nfull, lambda s, c: block(nfull - 1 - s, c, False), carry)
    acc = carry[1]
    lane = lax.broadcasted_iota(jnp.int32, (tq, LANES), 1)
    o_ref[0] = jnp.where(lane < HEAD, acc[:tq], acc[tq:]).astype(o_ref.dtype)


def _sb_decode_kernel(q_ref, k_ref, v_ref, kpt_ref, vpt_ref, usn_ref, usp_ref, o_ref, *, t, past, pb):
    qh = _sb_heads(q_ref[0], HEADS)
    kn = k_ref[0].astype(BF16)
    vn = v_ref[0].astype(BF16)
    pair = lambda x, h: x[:, LANES * (h // 2):LANES * (h // 2 + 1)]
    ri = lax.broadcasted_iota(jnp.int32, (t, t), 0)
    ci = lax.broadcasted_iota(jnp.int32, (t, t), 1)
    carry = tuple((jnp.zeros((t, 1), F32), jnp.zeros((t, LANES), F32)) for _ in range(HEADS))
    zs = [_dot_nt(qh[h], pair(kn, h)) for h in range(HEADS)]
    carry = _sb_block(zs, ci < ri, usn_ref[...], carry, lambda h, a: _dot(a, pair(vn, h)))
    usp = usp_ref[...]
    for blk in reversed(range(past // pb)):
        kts = [kpt_ref[0, 0, p, :, blk * pb:(blk + 1) * pb].astype(BF16) for p in range(HEADS // 2)]
        vts = [vpt_ref[0, 0, p, :, blk * pb:(blk + 1) * pb].astype(BF16) for p in range(HEADS // 2)]
        zs = [_dot(qh[h], kts[h // 2]) for h in range(HEADS)]
        carry = _sb_block(zs, None, usp, carry, lambda h, a: _dot_nt(a, vts[h // 2]))
    _sb_finish(carry, o_ref, t)


def _strict_upper(n):
    idx = jnp.arange(n)
    return (idx[:, None] > idx[None, :]).astype(BF16)


def _sb_attn(proj3, tq, tk):
    b, t, _ = proj3.shape
    us = _strict_upper(tk)
    return pl.pallas_call(
        functools.partial(_sb_attn_kernel, tq=tq, tk=tk),
        out_shape=jax.ShapeDtypeStruct((b, t, BRANCH), BF16),
        grid=(b, HEADS // 2, t // tq),
        in_specs=[pl.BlockSpec((1, tq, LANES), lambda bi, j, i: (bi, i, O_SBQ // LANES + j)),
                  pl.BlockSpec((1, t, LANES), lambda bi, j, i: (bi, 0, O_SBK // LANES + j)),
                  pl.BlockSpec((1, t, LANES), lambda bi, j, i: (bi, 0, O_SBV // LANES + j)),
                  pl.BlockSpec(us.shape, lambda bi, j, i: (0, 0))],
        out_specs=pl.BlockSpec((1, tq, LANES), lambda bi, j, i: (bi, i, j)),
        compiler_params=_cparams(("parallel", "parallel", "arbitrary"), 48),
        name="sb_attn",
    )(proj3, proj3, proj3, us)


def _sb_decode(proj3, kpt, vpt, layer, pb):
    b, t, _ = proj3.shape
    past = kpt.shape[4]
    usn, usp = _strict_upper(t), _strict_upper(pb)
    new = lambda off: pl.BlockSpec((1, t, BRANCH), lambda bi, o=off // BRANCH: (bi, 0, o))
    cache = pl.BlockSpec((1, 1, HEADS // 2, LANES, past), lambda bi: (layer, bi, 0, 0, 0))
    return pl.pallas_call(
        functools.partial(_sb_decode_kernel, t=t, past=past, pb=pb),
        out_shape=jax.ShapeDtypeStruct((b, t, BRANCH), BF16),
        grid=(b,),
        in_specs=[new(O_SBQ), new(O_SBK), new(O_SBV), cache, cache,
                  pl.BlockSpec(usn.shape, lambda bi: (0, 0)), pl.BlockSpec(usp.shape, lambda bi: (0, 0))],
        out_specs=pl.BlockSpec((1, t, BRANCH), lambda bi: (bi, 0, 0)),
        compiler_params=_cparams(("parallel",), 48),
        name="sb_decode",
    )(proj3, proj3, proj3, kpt, vpt, usn, usp)


def _mix_kernel(yrw_ref, yssm_ref, ymla_ref, ysb_ref, gate_ref, pm_ref, wb_ref, bm_ref, o_ref, acc_ref):
    n = pl.program_id(1)

    @pl.when(n == 0)
    def _():
        acc_ref[...] = jnp.zeros_like(acc_ref)

    def branch(y_ref):
        g = gate_ref[...]
        gated = y_ref[...].astype(F32) * (g * jax.nn.sigmoid(g))
        up = _dot(gated.astype(BF16), wb_ref[0])
        acc_ref[...] += jax.nn.sigmoid(pm_ref[...] + bm_ref[0]) * up

    for idx, y_ref in enumerate((yrw_ref, yssm_ref, ymla_ref, ysb_ref)):
        pl.when(n == idx)(functools.partial(branch, y_ref))

    @pl.when(n == N_BRANCH - 1)
    def _():
        o_ref[...] = acc_ref[...].astype(BF16)


def _out_kernel(m_ref, x_ref, wo_ref, fg_ref, o_ref, *, final):
    xn = x_ref[...] + _dot(m_ref[...], wo_ref[...])
    if final:
        xn = xn * lax.rsqrt(jnp.mean(xn * xn, axis=-1, keepdims=True) + EPS) * fg_ref[...]
    o_ref[...] = xn


def _merge(ys, proj, x2, wb, bm, wo, fg, final):
    n, d = x2.shape
    tm = min(n, 1024)
    ysp = pl.BlockSpec((tm, BRANCH), lambda i, k: (i, 0))
    mixed = pl.pallas_call(
        _mix_kernel,
        out_shape=jax.ShapeDtypeStruct((n, d), BF16),
        grid=(n // tm, N_BRANCH),
        in_specs=[ysp, ysp, ysp, ysp,
                  pl.BlockSpec((tm, BRANCH), lambda i, k: (i, O_GATE // BRANCH + k)),
                  pl.BlockSpec((tm, d), lambda i, k: (i, O_MERGE // d + k)),
                  pl.BlockSpec((1, BRANCH, d), lambda i, k: (k, 0, 0)),
                  pl.BlockSpec((1, 1, d), lambda i, k: (k, 0, 0))],
        out_specs=pl.BlockSpec((tm, d), lambda i, k: (i, 0)),
        scratch_shapes=[pltpu.VMEM((tm, d), F32)],
        compiler_params=_cparams(("parallel", "arbitrary"), 56),
        name="branch_mix",
    )(*ys, proj, proj, wb, bm)
    to = min(n, 512)
    return pl.pallas_call(
        functools.partial(_out_kernel, final=final),
        out_shape=jax.ShapeDtypeStruct((n, d), F32),
        grid=(n // to,),
        in_specs=[pl.BlockSpec((to, d), lambda i: (i, 0)),
                  pl.BlockSpec((to, d), lambda i: (i, 0)),
                  pl.BlockSpec((d, d), lambda i: (0, 0)),
                  pl.BlockSpec((1, d), lambda i: (0, 0))],
        out_specs=pl.BlockSpec((to, d), lambda i: (i, 0)),
        compiler_params=_cparams(("parallel",), 48),
        name="out_proj",
    )(mixed, x2, wo, fg)


_SRC_SSM = RW_IN
_SRC_QLAT = _SRC_SSM + BRANCH
_SRC_KVLAT = _SRC_QLAT + MLA_Q_LORA
_SRC_KPE = _SRC_KVLAT + MLA_KV_LORA
_SRC_SB = _SRC_KPE + MLA_ROPE
_SRC_GATE = _SRC_SB + 3 * BRANCH
_SRC_MERGE = _SRC_GATE + N_BRANCH * BRANCH
_SRC_LORA = 3 * BRANCH
_PW_BLK = 4 * LANES
_PW_LAST = PROJ_W // _PW_BLK - 1
_PW_TAIL = {_PW_LAST - 1: (_SRC_QLAT, _SRC_QLAT + LANES, _SRC_QLAT + 2 * LANES, _SRC_LORA),
            _PW_LAST: (_SRC_KVLAT, _SRC_KVLAT + LANES, _SRC_KPE, _SRC_KPE)}


def _permute_src_col(j, k):
    nm = N_BRANCH * D_MODEL // _PW_BLK
    ng = nm + N_BRANCH * BRANCH // _PW_BLK
    nsb = ng + 3 * BRANCH // _PW_BLK
    nrw = nsb + 3 * BRANCH // _PW_BLK
    u = MLA_ROPE
    col = jnp.where(j < nm, _SRC_MERGE // u + _PW_BLK // u * j,
                    jnp.where(j < ng, _SRC_GATE // u + _PW_BLK // u * (j - nm),
                              jnp.where(j < nsb, _SRC_SB // u + _PW_BLK // u * (j - ng),
                                        jnp.where(j < nrw, _PW_BLK // u * (j - nsb), _SRC_SSM // u))))
    col = col + LANES // u * k
    for jj, src in _PW_TAIL.items():
        col = jnp.where(j == jj, src[k] // u, col)
    return col * u


def _permute_kernel(x0_ref, x1_ref, x2_ref, x3_ref, o_ref):
    j = pl.program_id(1)
    wins = (x0_ref, x1_ref, x2_ref, x3_ref)
    half = MLA_ROPE // 2

    def put(k, rows):
        o_ref[0, :, LANES * k:LANES * (k + 1)] = rows.T.astype(BF16)

    @pl.when(j < _PW_LAST)
    def _():
        for k in range(4):
            put(k, wins[k][0])

    @pl.when(j == _PW_LAST)
    def _():
        for k in range(2):
            put(k, wins[k][0])
        kpe = wins[2][0]
        row = lax.broadcasted_iota(jnp.int32, kpe.shape, 0)
        put(2, jnp.where(row < MLA_ROPE, kpe, 0.0))
        zeros = jnp.zeros((LANES - MLA_ROPE, kpe.shape[1]), F32)
        put(3, jnp.concatenate([-kpe[half:MLA_ROPE], kpe[:half], zeros], axis=0))


def _permute_w_in(w_in):
    depth, d, _ = w_in.shape
    wt = jnp.swapaxes(w_in, 1, 2)
    win = lambda k: pl.BlockSpec((pl.Element(1), pl.Element(LANES), pl.Element(d)),
                                 lambda l, j, k=k: (l, _permute_src_col(j, k), 0))
    return pl.pallas_call(
        _permute_kernel,
        out_shape=jax.ShapeDtypeStruct((depth, d, PROJ_W), BF16),
        grid=(depth, PROJ_W // _PW_BLK),
        in_specs=[win(k) for k in range(4)],
        out_specs=pl.BlockSpec((1, d, _PW_BLK), lambda l, j: (l, 0, j)),
        compiler_params=_cparams(("parallel", "parallel"), 32),
        name="permute_w_in",
    )(wt, wt, wt, wt)


def _rope_tables(pos):
    half = MLA_ROPE // 2
    inv = ROPE_BASE ** (-jnp.arange(half, dtype=F32) / half)
    ang = pos.astype(F32)[:, None] * inv
    cos, sin = jnp.cos(ang), jnp.sin(ang)
    t = pos.shape[0]
    cc = jnp.concatenate([cos, cos], axis=1)
    ss = jnp.concatenate([sin, sin], axis=1)
    tq = jnp.concatenate([jnp.ones((t, MLA_NOPE), F32), cc, ss], axis=1) * MLA_SCALE
    pad = jnp.zeros((t, LANES - MLA_ROPE), F32)
    return tq, jnp.concatenate([cc, pad], axis=1), jnp.concatenate([ss, pad], axis=1)


def _mla_weights(w_q_up, w_kv_up):
    half = MLA_ROPE // 2
    wq = w_q_up.reshape(MLA_Q_LORA, HEADS, MLA_NOPE + MLA_ROPE)
    x1 = wq[:, :, MLA_NOPE:MLA_NOPE + half]
    x2 = wq[:, :, MLA_NOPE + half:]
    wq_p = jnp.concatenate([wq, -x2, x1], axis=2).reshape(MLA_Q_LORA, HEADS * LANES).astype(BF16)
    wkv = w_kv_up.reshape(MLA_KV_LORA, HEADS, 2 * HEAD)
    wk_p = jnp.concatenate([wkv[:, :, :HEAD], jnp.zeros((MLA_KV_LORA, HEADS, HEAD), F32)], axis=2)
    wk_p = wk_p.reshape(MLA_KV_LORA, HEADS * LANES).astype(BF16)
    wv_p = wkv[:, :, HEAD:].reshape(MLA_KV_LORA, BRANCH).astype(BF16)
    return wq_p, wk_p, wv_p


def _s5_tables(lam_re, lam_im, log_dt, b_re, b_im, c_re, c_im):
    dt = jnp.exp(log_dt)[:, None]
    mag = jnp.exp(lam_re * dt)
    ang = lam_im * dt
    lb_re, lb_im = mag * jnp.cos(ang), mag * jnp.sin(ang)
    nr, ni = lb_re - 1.0, lb_im
    den = lam_re * lam_re + lam_im * lam_im
    f_re = (nr * lam_re + ni * lam_im) / den
    f_im = (ni * lam_re - nr * lam_im) / den
    bb_re = f_re[..., None] * b_re - f_im[..., None] * b_im
    bb_im = f_re[..., None] * b_im + f_im[..., None] * b_re
    hp = lax.Precision.HIGHEST
    lc = SSM_CHUNK

    def power(j):
        j = jnp.asarray(j, F32).reshape(-1, 1, 1)
        m = jnp.exp(lam_re * dt * j)
        return m * jnp.cos(ang * j), m * jnp.sin(ang * j)

    er, ei = power(jnp.arange(lc + 1))
    m_re = er[..., None] * bb_re - ei[..., None] * bb_im
    m_im = er[..., None] * bb_im + ei[..., None] * bb_re
    kj = (jnp.einsum('gcp,jgpd->jgcd', c_re, m_re[:lc], precision=hp)
          - jnp.einsum('gcp,jgpd->jgcd', c_im, m_im[:lc], precision=hp))
    lag = jnp.arange(lc)
    shift = (lag[None, None, :] - lag[None, :, None] == lag[:, None, None]).astype(F32)
    kmat = jnp.einsum('jst,jgcd->gsdtc', shift, kj, precision=hp).reshape(SSM_GROUPS, SSM_IO, SSM_IO)
    to_state = lambda m: jnp.transpose(m[:lc][::-1], (1, 0, 3, 2)).reshape(SSM_GROUPS, SSM_IO, SSM_STATE)
    wst = jnp.concatenate([to_state(m_re), to_state(m_im)], axis=-1)
    cr_t, ci_t = jnp.transpose(c_re, (0, 2, 1))[:, :, None, :], jnp.transpose(c_im, (0, 2, 1))[:, :, None, :]
    er1 = jnp.transpose(er[1:], (1, 2, 0))[..., None]
    ei1 = jnp.transpose(ei[1:], (1, 2, 0))[..., None]
    cst = jnp.concatenate([(cr_t * er1 - ci_t * ei1).reshape(SSM_GROUPS, SSM_STATE, SSM_IO),
                           (-(cr_t * ei1 + ci_t * er1)).reshape(SSM_GROUPS, SSM_STATE, SSM_IO)], axis=1)

    def pair(j):
        r, i = power(lc * jnp.asarray(j))
        r, i = jnp.transpose(r, (1, 0, 2)), jnp.transpose(i, (1, 0, 2))
        return jnp.concatenate([r, r], axis=-1), jnp.concatenate([-i, i], axis=-1)

    l1, l2 = pair([1, 2, 4])
    lam = jnp.concatenate([l1, l2, jnp.zeros((SSM_GROUPS, 2, SSM_RI), F32)], axis=1)
    p1, p2 = pair(list(range(1, 9)))
    pw = jnp.concatenate([p1, p2], axis=1)
    return dict(kmat=kmat.astype(BF16), wst=wst.astype(BF16), cst=cst.astype(BF16), lam=lam, pw=pw)


def _pair_states(s):
    b = s.shape[0]
    s = s.reshape(b, 4, 2, HEAD, HEAD)
    return jnp.einsum('bjivk,ih->bjivhk', s, jnp.eye(2, dtype=s.dtype)).reshape(b, 4, LANES, LANES)


def _unpair_states(s):
    b = s.shape[0]
    s = s.reshape(b, 4, 2, HEAD, 2, HEAD)
    return jnp.stack([s[:, :, 0, :, 0, :], s[:, :, 1, :, 1, :]], axis=2).reshape(b, HEADS, HEAD, HEAD)


def _layer(x, pos0, st, lw, final_g, final):
    b, t, d = x.shape
    n = b * t
    x2 = x.reshape(n, d)
    proj = _norm_proj(x2, lw["norm_g"], lw["w_in_all"], lw["layer"])
    proj3 = proj.reshape(b, t, PROJ_W)
    past = st["past"]

    chunk = min(t, CHUNK)
    y_rw, s_new = _rwkv(proj3, st["sp_r"], st["sp_k"], st["sp_v"], st["sp_l"], _pair_states(st["wkv"]),
                        lw["rw_pvec"], lw["rw_mu_l"], lw["rw_w2p"], lw["rw_a2p"], lw["g2"], chunk)
    shift_new = jnp.concatenate([proj3[:, t - 1:, O_RWR:O_RWR + 3 * BRANCH],
                                 proj3[:, t - 1:, O_LORA:O_LORA + 2 * RW_LORA]], axis=-1)
    wkv_new = _unpair_states(s_new)

    y_ssm, ssm_re_new, ssm_im_new = _s5(proj3, st["ssm_re"], st["ssm_im"], lw["ssm_tabs"], lw["ssm_d"],
                                        lw["ssm_wglu"], lw["ssm_bglu"])

    pos = pos0 + jnp.arange(t, dtype=jnp.int32)
    tq_tab, tc_tab, ts_tab = _rope_tables(pos)
    q, ckv, kr = _mla_prep(proj, t, tq_tab, tc_tab, ts_tab, lw["mla_gq"], lw["mla_gkv"], lw["mla_wq"])
    ckv3 = ckv.reshape(b, t, MLA_KV_LORA)
    kr3 = kr.reshape(b, t, MLA_ROPE)
    s_pad = -(-(past + t) // LANES) * LANES
    kc, vv = _mla_kv(ckv[None], kr[None], lw["mla_wk"], lw["mla_wv"], lw["dup"])
    kc3, vv3 = kc.reshape(b, t, HEADS * LANES), vv.reshape(b, t, BRANCH)
    kcp3 = vvp3 = None
    if past:
        kcp, vvp = _mla_kv(st["ckv_all"], st["kpe_all"], lw["mla_wk"], lw["mla_wv"], lw["dup"], lw["layer"])
        kcp3, vvp3 = kcp.reshape(b, past, HEADS * LANES), vvp.reshape(b, past, BRANCH)
    small = t <= 64
    assert small or not past
    tq = min(t, 256)
    hp = 4 if small else 1
    y_mla = _mla_attn(q.reshape(b, t, HEADS * LANES), kc3, vv3, kcp3, vvp3, s_pad, tq,
                      s_pad if small else min(512, s_pad), hp)

    sbk = proj3[:, :, O_SBK:O_SBK + BRANCH]
    sbv = proj3[:, :, O_SBV:O_SBV + BRANCH]
    if past:
        y_sb = _sb_decode(proj3, st["sbk_t"], st["sbv_t"], lw["layer"], min(256, past))
    else:
        y_sb = _sb_attn(proj3, tq, min(256, t))

    ys = [y.reshape(n, BRANCH) for y in (y_rw, y_ssm, y_mla, y_sb)]
    x_new = _merge(ys, proj, x2, lw["w_branch"], lw["b_merge"], lw["w_out"], final_g, final).reshape(b, t, d)
    new_state = (shift_new, wkv_new, ssm_re_new, ssm_im_new, ckv3, kr3,
                 sbk.reshape(b, t, HEADS, HEAD), sbv.reshape(b, t, HEADS, HEAD))
    return x_new, new_state


def kernel(x_prompt, x_sample, state_rwkv_shift, state_rwkv_wkv, state_ssm_re, state_ssm_im, cache_mla_ckv, cache_mla_kpe, cache_sb_k, cache_sb_v, norm_g, w_in, rw_mu, rw_w0, rw_w2, rw_a0, rw_a2, rw_k_k, rw_k_a, rw_r_k, rw_lnx_g, rw_lnx_b, ssm_lam_re, ssm_lam_im, ssm_log_dt, ssm_b_re, ssm_b_im, ssm_c_re, ssm_c_im, ssm_d, ssm_w_glu, ssm_b_glu, mla_q_norm, mla_w_q_up, mla_kv_norm, mla_w_kv_up, w_branch, b_merge, w_out, final_norm_g):
    depth = w_in.shape[0]
    bp, tp, _ = x_prompt.shape
    bs = x_sample.shape[0]
    past = cache_mla_ckv.shape[2]

    w_in_p = _permute_w_in(w_in)
    lane_i = jnp.arange(LANES)
    g2 = ((lane_i[:, None] // HEAD) == (lane_i[None, :] // HEAD)).astype(BF16)
    rope_i = jnp.arange(MLA_ROPE)
    dup = ((lane_i[None, :] == rope_i[:, None] + MLA_NOPE)
           | (lane_i[None, :] == rope_i[:, None] + MLA_NOPE + MLA_ROPE)).astype(BF16)
    final_g = final_norm_g.reshape(1, D_MODEL)
    zpad = jnp.zeros((RW_LORA, BRANCH), F32)

    layers = []
    for l in range(depth):
        wq_p, wk_p, wv_p = _mla_weights(mla_w_q_up[l], mla_w_kv_up[l])
        ssm_tabs = _s5_tables(ssm_lam_re[l], ssm_lam_im[l], ssm_log_dt[l], ssm_b_re[l], ssm_b_im[l],
                              ssm_c_re[l], ssm_c_im[l])
        mu = rw_mu[l]
        rows = [mu[:BRANCH], mu[BRANCH:2 * BRANCH], mu[2 * BRANCH:3 * BRANCH], rw_w0[l], rw_a0[l], rw_k_k[l],
                rw_k_a[l], rw_r_k[l].reshape(BRANCH), rw_lnx_g[l], rw_lnx_b[l]]
        pvec = jnp.concatenate([jnp.stack(rows), jnp.zeros((16 - len(rows), BRANCH), F32)], axis=0)
        layers.append(dict(
            norm_g=norm_g[l].reshape(1, D_MODEL), w_in_all=w_in_p, layer=l,
            rw_pvec=pvec, rw_mu_l=mu[3 * BRANCH:].reshape(1, 2 * RW_LORA),
            rw_w2p=jnp.concatenate([rw_w2[l], zpad], axis=0).astype(BF16),
            rw_a2p=jnp.concatenate([zpad, rw_a2[l]], axis=0).astype(BF16),
            g2=g2, dup=dup,
            ssm_tabs=ssm_tabs,
            ssm_d=ssm_d[l].reshape(1, BRANCH), ssm_wglu=ssm_w_glu[l].astype(BF16),
            ssm_bglu=ssm_b_glu[l].reshape(1, BRANCH),
            mla_gq=mla_q_norm[l].reshape(1, MLA_Q_LORA), mla_gkv=mla_kv_norm[l].reshape(1, MLA_KV_LORA),
            mla_wq=wq_p, mla_wk=wk_p, mla_wv=wv_p,
            w_branch=w_branch[l].astype(BF16), b_merge=b_merge[l].reshape(N_BRANCH, 1, D_MODEL),
            w_out=w_out[l].astype(BF16)))

    def fresh(bn):
        return dict(sp_r=jnp.zeros((bn, 1, BRANCH), F32), sp_k=jnp.zeros((bn, 1, BRANCH), F32),
                    sp_v=jnp.zeros((bn, 1, BRANCH), F32), sp_l=jnp.zeros((bn, 1, 2 * RW_LORA), F32),
                    wkv=jnp.zeros((bn, HEADS, HEAD, HEAD), F32),
                    ssm_re=jnp.zeros((bn, SSM_GROUPS, SSM_STATE), F32),
                    ssm_im=jnp.zeros((bn, SSM_GROUPS, SSM_STATE), F32),
                    past=0)

    nl = cache_mla_ckv.shape[0]
    ckv_all = cache_mla_ckv.reshape(nl, bs * past, MLA_KV_LORA)
    kpe_all = cache_mla_kpe.reshape(nl, bs * past, MLA_ROPE)
    sbk_t = jnp.transpose(cache_sb_k, (0, 1, 3, 4, 2)).reshape(nl, bs, HEADS // 2, LANES, past)
    sbv_t = jnp.transpose(cache_sb_v, (0, 1, 3, 4, 2)).reshape(nl, bs, HEADS // 2, LANES, past)

    def carried(l):
        sh = state_rwkv_shift[l]
        return dict(sp_r=sh[:, :, :BRANCH], sp_k=sh[:, :, BRANCH:2 * BRANCH], sp_v=sh[:, :, 2 * BRANCH:3 * BRANCH],
                    sp_l=sh[:, :, 3 * BRANCH:], wkv=state_rwkv_wkv[l],
                    ssm_re=state_ssm_re[l], ssm_im=state_ssm_im[l],
                    past=past, ckv_all=ckv_all, kpe_all=kpe_all, sbk_t=sbk_t, sbv_t=sbv_t)

    xp, xs = x_prompt, x_sample
    new_p, new_s = [], []
    for l in range(depth):
        last = l == depth - 1
        xp, st_p = _layer(xp, 0, fresh(bp), layers[l], final_g, last)
        xs, st_s = _layer(xs, past, carried(l), layers[l], final_g, last)
        new_p.append(st_p)
        new_s.append(st_s)
    stk = lambda lst, i: jnp.stack([s[i] for s in lst], axis=0)
    return (xp, xs) + tuple(stk(new_p, i) for i in range(8)) + tuple(stk(new_s, i) for i in range(8))
```

```python
import functools
import math

import jax
import jax.numpy as jnp
from jax import lax
from jax.experimental import pallas as pl
from jax.experimental.pallas import tpu as pltpu

F32 = jnp.float32
BF16 = jnp.bfloat16

D_MODEL = 2048
BRANCH = 512
N_BRANCH = 4
EPS = 1e-6
CHUNK = 64
HEAD = 64
HEADS = 8
RW_LORA = 64
RW_IN = 3 * BRANCH + 2 * RW_LORA
RW_GN_EPS = 64e-5
SSM_GROUP = 16
SSM_GROUPS = 32
SSM_STATE = 64
SSM_W = SSM_GROUPS * SSM_STATE
MLA_NOPE = 64
MLA_ROPE = 32
MLA_Q_LORA = 384
MLA_KV_LORA = 256
MLA_SCALE = 1.0 / math.sqrt(MLA_NOPE + MLA_ROPE)
ROPE_BASE = 10000.0
SB_SCALE = 1.0 / math.sqrt(HEAD)
LANES = 128
MIB = 1024 * 1024

O_MERGE = 0
O_GATE = 8192
O_SBQ = 10240
O_SBK = 10752
O_SBV = 11264
O_RWR = 11776
O_RWK = 12288
O_RWV = 12800
O_SSM = 13312
O_QLAT = 13824
O_LORA = 14208
O_KVLAT = 14336
O_KPE = 14592
O_KPESW = 14720
PROJ_W = 14848


def _cparams(sem, vmem_mib):
    return pltpu.CompilerParams(dimension_semantics=sem, vmem_limit_bytes=vmem_mib * MIB)


def _dot(a, b):
    return jnp.dot(a, b, preferred_element_type=F32)


def _dot_nt(a, b):
    return lax.dot_general(a, b, (((1,), (1,)), ((), ())), preferred_element_type=F32)


def _split3(x):
    h1 = x.astype(BF16)
    r1 = x - h1.astype(F32)
    h2 = r1.astype(BF16)
    h3 = (r1 - h2.astype(F32)).astype(BF16)
    return h1, h2, h3


def _softplus(x):
    return jnp.maximum(x, 0.0) + jnp.log1p(jnp.exp(-jnp.abs(x)))


def _norm_proj_kernel(x_ref, g_ref, w_ref, o_ref, h_ref):
    @pl.when(pl.program_id(1) == 0)
    def _():
        x = x_ref[...]
        ms = jnp.mean(x * x, axis=-1, keepdims=True)
        h_ref[...] = (x * lax.rsqrt(ms + EPS) * g_ref[...]).astype(BF16)

    o_ref[...] = _dot(h_ref[...], w_ref[0])


def _norm_proj(x2, g, w_all, layer):
    n, d = x2.shape
    width = w_all.shape[2]
    tm = min(n, 1024)
    tn = 512
    return pl.pallas_call(
        _norm_proj_kernel,
        out_shape=jax.ShapeDtypeStruct((n, width), F32),
        grid=(n // tm, width // tn),
        in_specs=[
            pl.BlockSpec((tm, d), lambda i, j: (i, 0)),
            pl.BlockSpec((1, d), lambda i, j: (0, 0)),
            pl.BlockSpec((1, d, tn), lambda i, j: (layer, 0, j)),
        ],
        out_specs=pl.BlockSpec((tm, tn), lambda i, j: (i, j)),
        scratch_shapes=[pltpu.VMEM((tm, d), BF16)],
        compiler_params=_cparams(("parallel", "arbitrary"), 40),
        name="norm_proj",
    )(x2, g, w_all)


def _segsum(x, g2):
    outs = []
    for j in range(BRANCH // LANES):
        h1, h2, h3 = _split3(x[:, LANES * j:LANES * (j + 1)])
        outs.append(_dot(h1, g2) + _dot(h2, g2) + _dot(h3, g2))
    return jnp.concatenate(outs, axis=1)


def _rwkv_kernel(pr_ref, pk_ref, pv_ref, plo_ref, spr_ref, spk_ref, spv_ref, spl_ref, s0_ref,
                 pvec_ref, mul_ref, w2_ref, a2_ref, g2_ref,
                 y_ref, sout_ref,
                 st_ref, cr_ref, ck_ref, cv_ref, cl_ref, *, C, nb):
    c = pl.program_id(1)

    @pl.when(c == 0)
    def _():
        zero = jnp.zeros((HEAD, HEAD), F32)
        for bi in range(nb):
            for j in range(HEADS // 2):
                st_ref[bi, j] = jnp.concatenate(
                    [jnp.concatenate([s0_ref[bi, 2 * j], zero], axis=1),
                     jnp.concatenate([zero, s0_ref[bi, 2 * j + 1]], axis=1)], axis=0)
        cr_ref[...] = spr_ref[...]
        ck_ref[...] = spk_ref[...]
        cv_ref[...] = spv_ref[...]
        cl_ref[...] = spl_ref[...]

    pvec = pvec_ref[...]
    prow = lambda i: pvec[i:i + 1, :]
    g2 = g2_ref[...]

    def tshift(p_ref, carry_ref, mu):
        outs = []
        for bi in range(nb):
            p = p_ref[bi]
            rolled = pltpu.roll(p, 1, 0)
            rid = lax.broadcasted_iota(jnp.int32, p.shape, 0)
            prev = jnp.where(rid == 0, carry_ref[bi], rolled)
            carry_ref[bi] = p[C - 1:C, :]
            outs.append(p + (prev - p) * mu)
        return jnp.concatenate(outs, axis=0)

    r = tshift(pr_ref, cr_ref, prow(0))
    k = tshift(pk_ref, ck_ref, prow(1))
    v = tshift(pv_ref, cv_ref, prow(2))
    lo = tshift(plo_ref, cl_ref, mul_ref[...])

    log_w = -_softplus(-(prow(3) + _dot(jnp.tanh(lo).astype(BF16), w2_ref[...]))) - 0.5
    ld = -jnp.exp(log_w)
    a_icl = jax.nn.sigmoid(prow(4) + _dot(lo.astype(BF16), a2_ref[...]))
    kkr = k * prow(5)
    kk = kkr / jnp.maximum(jnp.sqrt(_segsum(kkr * kkr, g2)), 1e-12)
    k2 = k * (1.0 + (a_icl - 1.0) * prow(6))
    av = -kk
    bv = kk * a_icl

    sh = C.bit_length() - 1
    ri = lax.broadcasted_iota(jnp.int32, (nb * C, nb * C), 0)
    ci = lax.broadcasted_iota(jnp.int32, (nb * C, nb * C), 1)
    lincl = jnp.logical_and(ci <= ri, (ci >> sh) == (ri >> sh)).astype(BF16)
    h1, h2, h3 = _split3(ld)
    g = _dot(lincl, h1) + _dot(lincl, h2) + _dot(lincl, h3)
    g_lasts = [g[(bi + 1) * C - 1:(bi + 1) * C, :] for bi in range(nb)]
    g_last = jnp.concatenate([jnp.broadcast_to(gl, (C, BRANCH)) for gl in g_lasts], axis=0)
    eg = jnp.exp(g)
    eng = jnp.exp(-g)
    at = av * jnp.exp(g - ld)
    rt = r * eg
    bt = bv * eng
    kt = k2 * eng
    e_c = jnp.exp(g_last - g)
    b_end = bv * e_c
    k_end = k2 * e_c
    eg_lasts = [jnp.exp(gl) for gl in g_lasts]

    c2 = 2 * C
    npair = BRANCH // LANES
    r_i = lax.broadcasted_iota(jnp.int32, (c2, c2), 0)
    c_i = lax.broadcasted_iota(jnp.int32, (c2, c2), 1)
    c_m = jnp.where(c_i >= C, c_i - C, c_i)
    mhalf = c_m < jnp.where(r_i < C, r_i, r_i - C + 1)
    mfull = jnp.concatenate([mhalf, mhalf], axis=0)
    bd2 = (r_i >> sh) == (c_i >> sh)
    eye2 = (r_i == c_i).astype(F32)
    lane = lax.broadcasted_iota(jnp.int32, (1, LANES), 1)
    m0 = (lane < HEAD).astype(F32)
    m1 = (lane >= HEAD).astype(F32)
    v_i = lax.broadcasted_iota(jnp.int32, (LANES, LANES), 0)
    k_i = lax.broadcasted_iota(jnp.int32, (LANES, LANES), 1)
    bdmask = (v_i >> 6) == (k_i >> 6)
    zf = jnp.zeros((C, LANES), F32)
    zc = jnp.zeros((C, LANES), BF16)
    sls = [slice(LANES * j, LANES * (j + 1)) for j in range(npair)]

    units = [(bi, j) for bi in range(nb) for j in range(npair)]
    tile = lambda x, bi, j: x[bi * C:(bi + 1) * C, LANES * j:LANES * (j + 1)]
    ws, a_ss, s2s = [], [], []
    for bi, j in units:
        atj, rtj = tile(at, bi, j), tile(rt, bi, j)
        a0, r0, a1, r1 = atj * m0, rtj * m0, atj * m1, rtj * m1
        lhs_w = jnp.concatenate([jnp.concatenate([a0, zf], axis=1), jnp.concatenate([r0, zf], axis=1),
                                 jnp.concatenate([zf, a1], axis=1), jnp.concatenate([zf, r1], axis=1)],
                                axis=0).astype(BF16)
        btj, ktj = tile(bt, bi, j), tile(kt, bi, j)
        rhs_w = jnp.concatenate([jnp.concatenate([btj, ktj], axis=0),
                                 jnp.concatenate([ktj, btj], axis=0)], axis=1).astype(BF16)
        ws.append(jnp.where(mfull, _dot_nt(lhs_w, rhs_w), 0.0))
        lhs_s = jnp.concatenate([a0, r0, a1, r1], axis=0).astype(BF16)
        s2 = st_ref[bi, j]
        s2h, s2l, _ = _split3(s2)
        a_ss.append(_dot_nt(lhs_s, s2h) + _dot_nt(lhs_s, s2l))
        s2s.append(s2)

    n_pows = [jnp.where(bd2, jnp.concatenate([w[:C], w[c2:c2 + C]], axis=0), 0.0) for w in ws]
    t_mats = [eye2 + n for n in n_pows]
    lv = 2
    while lv < C:
        n_pows = [_dot(n.astype(BF16), n.astype(BF16)) for n in n_pows]
        t_mats = [t + _dot(t.astype(BF16), n.astype(BF16)) for t, n in zip(t_mats, n_pows)]
        lv *= 2

    wbs = [w.astype(BF16) for w in ws]
    v0s = [(tile(v, bi, j) * m0).astype(BF16) for bi, j in units]
    v1s = [(tile(v, bi, j) * m1).astype(BF16) for bi, j in units]
    xs = [jnp.concatenate([a_s[:C] + _dot(wb[:C], jnp.concatenate([zc, v0], axis=0)),
                           a_s[c2:c2 + C] + _dot(wb[c2:c2 + C], jnp.concatenate([v1, zc], axis=0))], axis=0)
          for a_s, wb, v0, v1 in zip(a_ss, wbs, v0s, v1s)]
    us = [_dot(t.astype(BF16), x.astype(BF16)) for t, x in zip(t_mats, xs)]
    ys = []
    for n, (bi, j) in enumerate(units):
        a_s, wb, u = a_ss[n], wbs[n], us[n]
        u0, u1 = u[:C], u[C:]
        y0 = a_s[C:c2] + _dot(wb[C:c2], jnp.concatenate([u0.astype(BF16), v0s[n]], axis=0))
        y1 = a_s[c2 + C:] + _dot(wb[c2 + C:], jnp.concatenate([v1s[n], u1.astype(BF16)], axis=0))
        ys.append(y0 + y1)
        uv = jnp.concatenate([u0 + u1, tile(v, bi, j)], axis=0)
        bk = jnp.concatenate([tile(b_end, bi, j), tile(k_end, bi, j)], axis=0).astype(BF16)
        upd = _dot(uv.T.astype(BF16), bk)
        st_ref[bi, j] = s2s[n] * eg_lasts[bi][:, sls[j]] + jnp.where(bdmask, upd, 0.0)
    y = jnp.concatenate([jnp.concatenate(ys[bi * npair:(bi + 1) * npair], axis=1) for bi in range(nb)],
                        axis=0)

    inv_n = 1.0 / HEAD
    yc = y - _segsum(y, g2) * inv_n
    yn = yc * lax.rsqrt(_segsum(yc * yc, g2) * inv_n + RW_GN_EPS) * prow(8) + prow(9)
    out = (yn + _segsum(r * k2 * prow(7), g2) * v).astype(y_ref.dtype)
    for bi in range(nb):
        y_ref[bi] = out[bi * C:(bi + 1) * C]

    @pl.when(c == pl.num_programs(1) - 1)
    def _():
        for bi in range(nb):
            for j in range(HEADS // 2):
                s2 = st_ref[bi, j]
                sout_ref[bi, 2 * j] = s2[:HEAD, :HEAD]
                sout_ref[bi, 2 * j + 1] = s2[HEAD:, HEAD:]


def _rwkv(proj3, sp_r, sp_k, sp_v, sp_l, s0, pvec, mu_l, w2p, a2p, g2, chunk):
    b, t, _ = proj3.shape
    nc = t // chunk
    nb = min(b, 4)
    cb = lambda off, wdt: off // wdt
    bspec = lambda off, wdt: pl.BlockSpec((nb, chunk, wdt), lambda i, c, o=cb(off, wdt): (i, c, o))
    row3 = lambda wdt: pl.BlockSpec((nb, 1, wdt), lambda i, c: (i, 0, 0))
    full2 = lambda a: pl.BlockSpec(a.shape, lambda i, c: (0, 0))
    return pl.pallas_call(
        functools.partial(_rwkv_kernel, C=chunk, nb=nb),
        out_shape=(jax.ShapeDtypeStruct((b, t, BRANCH), BF16),
                   jax.ShapeDtypeStruct((b, HEADS, HEAD, HEAD), F32)),
        grid=(b // nb, nc),
        in_specs=[
            bspec(O_RWR, BRANCH), bspec(O_RWK, BRANCH), bspec(O_RWV, BRANCH), bspec(O_LORA, LANES),
            row3(BRANCH), row3(BRANCH), row3(BRANCH), row3(LANES),
            pl.BlockSpec((nb, HEADS, HEAD, HEAD), lambda i, c: (i, 0, 0, 0)),
            full2(pvec), full2(mu_l), full2(w2p), full2(a2p), full2(g2),
        ],
        out_specs=(pl.BlockSpec((nb, chunk, BRANCH), lambda i, c: (i, c, 0)),
                   pl.BlockSpec((nb, HEADS, HEAD, HEAD), lambda i, c: (i, 0, 0, 0))),
        scratch_shapes=[pltpu.VMEM((nb, 4, LANES, LANES), F32),
                        pltpu.VMEM((nb, 1, BRANCH), F32), pltpu.VMEM((nb, 1, BRANCH), F32),
                        pltpu.VMEM((nb, 1, BRANCH), F32), pltpu.VMEM((nb, 1, LANES), F32)],
        compiler_params=_cparams(("parallel", "arbitrary"), 32),
        name="rwkv7",
    )(proj3, proj3, proj3, proj3, sp_r, sp_k, sp_v, sp_l, s0, pvec, mu_l, w2p, a2p, g2)


def _s5_kernel(u_ref, h0r_ref, h0i_ref, bblk_ref, cblk_ref, lam_ref, pw_ref, dsk_ref, wglu_ref, bglu_ref,
               y_ref, hro_ref, hio_ref, x_ref, cr_ref, ci_ref, *, tb):
    @pl.when(pl.program_id(1) == 0)
    def _():
        cr_ref[...] = h0r_ref[0]
        ci_ref[...] = h0i_ref[0]

    u = u_ref[0]
    x = _dot(u.astype(BF16), bblk_ref[...])
    xr = x[:, :SSM_W]
    xi = x[:, SSM_W:]
    rid = lax.broadcasted_iota(jnp.int32, (tb, 1), 0) & 7
    for lvl, s in enumerate((1, 2, 4)):
        ar = lam_ref[2 * lvl:2 * lvl + 1, :]
        ai = lam_ref[2 * lvl + 1:2 * lvl + 2, :]
        sr = pltpu.roll(xr, s, 0)
        si = pltpu.roll(xi, s, 0)
        valid = rid >= s
        nr = xr + jnp.where(valid, ar * sr - ai * si, 0.0)
        ni = xi + jnp.where(valid, ar * si + ai * sr, 0.0)
        xr, xi = nr, ni
    x_ref[:, :SSM_W] = xr
    x_ref[:, SSM_W:] = xi
    pwr = pw_ref[0:8, :]
    pwi = pw_ref[8:16, :]

    def body(gi, carry):
        cr, ci = carry
        o = pl.multiple_of(gi * 8, 8)
        br = x_ref[pl.ds(o, 8), 0:SSM_W]
        bi = x_ref[pl.ds(o, 8), SSM_W:2 * SSM_W]
        br = br + pwr * cr - pwi * ci
        bi = bi + pwr * ci + pwi * cr
        x_ref[pl.ds(o, 8), 0:SSM_W] = br
        x_ref[pl.ds(o, 8), SSM_W:2 * SSM_W] = bi
        return br[7:8, :], bi[7:8, :]

    cr, ci = lax.fori_loop(0, tb // 8, body, (cr_ref[...], ci_ref[...]))
    cr_ref[...] = cr
    ci_ref[...] = ci
    hro_ref[0] = cr
    hio_ref[0] = ci
    y = _dot(x_ref[...].astype(BF16), cblk_ref[...]) + dsk_ref[...] * u
    g = jax.nn.gelu(y)
    out = g * jax.nn.sigmoid(_dot(g.astype(BF16), wglu_ref[...]) + bglu_ref[...])
    y_ref[0] = out.astype(y_ref.dtype)


def _s5(proj3, h0r, h0i, bblk, cblk, lam, pw, dsk, wglu, bglu):
    b, t, _ = proj3.shape
    tb = min(t, 128)
    full2 = lambda a: pl.BlockSpec(a.shape, lambda i, c: (0, 0))
    row3 = pl.BlockSpec((1, 1, SSM_W), lambda i, c: (i, 0, 0))
    return pl.pallas_call(
        functools.partial(_s5_kernel, tb=tb),
        out_shape=(jax.ShapeDtypeStruct((b, t, BRANCH), BF16),
                   jax.ShapeDtypeStruct((b, 1, SSM_W), F32),
                   jax.ShapeDtypeStruct((b, 1, SSM_W), F32)),
        grid=(b, t // tb),
        in_specs=[pl.BlockSpec((1, tb, BRANCH), lambda i, c: (i, c, O_SSM // BRANCH)),
                  row3, row3, full2(bblk), full2(cblk), full2(lam), full2(pw), full2(dsk),
                  full2(wglu), full2(bglu)],
        out_specs=(pl.BlockSpec((1, tb, BRANCH), lambda i, c: (i, c, 0)), row3, row3),
        scratch_shapes=[pltpu.VMEM((tb, 2 * SSM_W), F32),
                        pltpu.VMEM((1, SSM_W), F32), pltpu.VMEM((1, SSM_W), F32)],
        compiler_params=_cparams(("parallel", "arbitrary"), 48),
        name="s5",
    )(proj3, h0r, h0i, bblk, cblk, lam, pw, dsk, wglu, bglu)


def _mla_prep_kernel(ql_ref, kvl_ref, kpe_ref, kpesw_ref, tq_ref, tc_ref, ts_ref,
                     gq_ref, gkv_ref, wq_ref, q_ref, ckv_ref, kr_ref):
    ql = ql_ref[...]
    qn = ql * lax.rsqrt(jnp.mean(ql * ql, axis=-1, keepdims=True) + EPS) * gq_ref[...]
    q = _dot(qn.astype(BF16), wq_ref[...])
    tq = tq_ref[...]
    q_ref[...] = jnp.concatenate(
        [q[:, LANES * h:LANES * (h + 1)] * tq for h in range(HEADS)], axis=1).astype(BF16)
    kvl = kvl_ref[...]
    ckv_ref[...] = kvl * lax.rsqrt(jnp.mean(kvl * kvl, axis=-1, keepdims=True) + EPS) * gkv_ref[...]
    kr = kpe_ref[...] * tc_ref[...] + kpesw_ref[...] * ts_ref[...]
    kr_ref[...] = kr[:, :MLA_ROPE]


def _mla_prep(proj, t, tq_tab, tc_tab, ts_tab, gq, gkv, wq):
    n = proj.shape[0]
    tm = min(t, 512)
    nt = t // tm
    tab = pl.BlockSpec((tm, LANES), lambda i: (i % nt, 0))
    full2 = lambda a: pl.BlockSpec(a.shape, lambda i: (0, 0))
    return pl.pallas_call(
        _mla_prep_kernel,
        out_shape=(jax.ShapeDtypeStruct((n, HEADS * LANES), BF16),
                   jax.ShapeDtypeStruct((n, MLA_KV_LORA), F32),
                   jax.ShapeDtypeStruct((n, MLA_ROPE), F32)),
        grid=(n // tm,),
        in_specs=[pl.BlockSpec((tm, MLA_Q_LORA), lambda i: (i, O_QLAT // MLA_Q_LORA)),
                  pl.BlockSpec((tm, MLA_KV_LORA), lambda i: (i, O_KVLAT // MLA_KV_LORA)),
                  pl.BlockSpec((tm, LANES), lambda i: (i, O_KPE // LANES)),
                  pl.BlockSpec((tm, LANES), lambda i: (i, O_KPESW // LANES)),
                  tab, tab, tab, full2(gq), full2(gkv), full2(wq)],
        out_specs=(pl.BlockSpec((tm, HEADS * LANES), lambda i: (i, 0)),
                   pl.BlockSpec((tm, MLA_KV_LORA), lambda i: (i, 0)),
                   pl.BlockSpec((tm, MLA_ROPE), lambda i: (i, 0))),
        compiler_params=_cparams(("parallel",), 32),
        name="mla_prep",
    )(proj, proj, proj, proj, tq_tab, tc_tab, ts_tab, gq, gkv, wq)


def _mla_kv_kernel(ckv_ref, kr_ref, wk_ref, wv_ref, dup_ref, k_ref, v_ref):
    cb = ckv_ref[0].astype(BF16)
    kn = _dot(cb, wk_ref[...])
    krd = _dot(kr_ref[0].astype(BF16), dup_ref[...])
    k_ref[...] = jnp.concatenate(
        [kn[:, LANES * h:LANES * (h + 1)] + krd for h in range(HEADS)], axis=1).astype(BF16)
    v_ref[...] = _dot(cb, wv_ref[...]).astype(BF16)


def _mla_kv(ckv, kr, wk, wv, dup, layer=0):
    n = ckv.shape[1]
    tm = next((c for c in (512, LANES) if n % c == 0), n)
    full2 = lambda a: pl.BlockSpec(a.shape, lambda i: (0, 0))
    return pl.pallas_call(
        _mla_kv_kernel,
        out_shape=(jax.ShapeDtypeStruct((n, HEADS * LANES), BF16),
                   jax.ShapeDtypeStruct((n, BRANCH), BF16)),
        grid=(n // tm,),
        in_specs=[pl.BlockSpec((1, tm, MLA_KV_LORA), lambda i: (layer, i, 0)),
                  pl.BlockSpec((1, tm, MLA_ROPE), lambda i: (layer, i, 0)),
                  full2(wk), full2(wv), full2(dup)],
        out_specs=(pl.BlockSpec((tm, HEADS * LANES), lambda i: (i, 0)),
                   pl.BlockSpec((tm, BRANCH), lambda i: (i, 0))),
        compiler_params=_cparams(("parallel",), 32),
        name="mla_kv",
    )(ckv, kr, wk, wv, dup)


def _assemble_keys(past_ref, new_ref, all_ref, past, t):
    all_ref[0:past, :] = past_ref[0].astype(BF16)
    all_ref[past:past + t, :] = new_ref[0].astype(BF16)
    pad = all_ref.shape[0] - past - t
    if pad:
        all_ref[past + t:, :] = jnp.zeros((pad, all_ref.shape[1]), BF16)


def _mla_attn_kernel(q_ref, k_ref, v_ref, *rest, tq, tk, q_off, s_valid, hp, past):
    if past:
        kp_ref, vp_ref, o_ref, kall_ref, vall_ref = rest
        _assemble_keys(kp_ref, k_ref, kall_ref, past, tq)
        _assemble_keys(vp_ref, v_ref, vall_ref, past, tq)
        kload = lambda o: kall_ref[pl.ds(o, tk), :]
        vload = lambda o: vall_ref[pl.ds(o, tk), :]
    else:
        (o_ref,) = rest
        kload = lambda o: k_ref[0, pl.ds(o, tk), :]
        vload = lambda o: v_ref[0, pl.ds(o, tk), :]
    i = pl.program_id(2)
    q_lo = q_off + i * tq
    limit = jnp.minimum(q_lo + tq, s_valid)
    nkb = (limit + tk - 1) // tk
    qidx = q_lo + lax.broadcasted_iota(jnp.int32, (tq, tk), 0)
    kloc = lax.broadcasted_iota(jnp.int32, (tq, tk), 1)
    nh = 2 * hp
    q2 = q_ref[0]
    qs = [q2[:, LANES * h:LANES * (h + 1)] for h in range(nh)]

    def body(kb, carry, masked):
        o = pl.multiple_of(kb * tk, tk)
        kblk = kload(o)
        vblk = vload(o)
        ss = [_dot_nt(qs[h], kblk[:, LANES * h:LANES * (h + 1)]) for h in range(nh)]
        if masked:
            kidx = kloc + kb * tk
            vis = jnp.logical_and((kidx >> 6) <= (qidx >> 6), kidx < s_valid)
            ss = [jnp.where(vis, s, -1e30) for s in ss]
        m_new = [jnp.maximum(carry[h][0], jnp.max(ss[h], axis=-1, keepdims=True)) for h in range(nh)]
        ps = [jnp.exp(ss[h] - m_new[h]) for h in range(nh)]
        pv = [_dot(ps[h].astype(BF16), vblk[:, LANES * (h // 2):LANES * (h // 2 + 1)]) for h in range(nh)]
        new = []
        for h in range(nh):
            m_prev, l_prev, acc = carry[h]
            alpha = jnp.exp(m_prev - m_new[h])
            new.append((m_new[h], alpha * l_prev + jnp.sum(ps[h], axis=-1, keepdims=True), alpha * acc + pv[h]))
        return tuple(new)

    init = tuple((jnp.full((tq, 1), -1e30, F32), jnp.zeros((tq, 1), F32), jnp.zeros((tq, LANES), F32))
                 for _ in range(nh))
    nfull = jnp.minimum(((q_lo >> 6) + 1) * CHUNK, s_valid) // tk
    res = lax.fori_loop(0, nfull, lambda kb, c: body(kb, c, False), init)
    res = lax.fori_loop(nfull, nkb, lambda kb, c: body(kb, c, True), res)
    lane = lax.broadcasted_iota(jnp.int32, (tq, LANES), 1)
    outs = [jnp.where(lane < HEAD, res[2 * p][2] / res[2 * p][1], res[2 * p + 1][2] / res[2 * p + 1][1])
            for p in range(hp)]
    o_ref[0] = jnp.concatenate(outs, axis=1).astype(o_ref.dtype)


def _mla_attn(q3, k3, v3, kp3, vp3, s_pad, tq, tk, hp):
    b, t, _ = q3.shape
    past = 0 if kp3 is None else kp3.shape[1]
    rows = k3.shape[1]
    in_specs = [pl.BlockSpec((1, tq, 2 * LANES * hp), lambda bi, j, i: (bi, i, j)),
                pl.BlockSpec((1, rows, 2 * LANES * hp), lambda bi, j, i: (bi, 0, j)),
                pl.BlockSpec((1, rows, LANES * hp), lambda bi, j, i: (bi, 0, j))]
    args = [q3, k3, v3]
    scratch = []
    if past:
        in_specs += [pl.BlockSpec((1, past, 2 * LANES * hp), lambda bi, j, i: (bi, 0, j)),
                     pl.BlockSpec((1, past, LANES * hp), lambda bi, j, i: (bi, 0, j))]
        args += [kp3, vp3]
        scratch = [pltpu.VMEM((s_pad, 2 * LANES * hp), BF16), pltpu.VMEM((s_pad, LANES * hp), BF16)]
    return pl.pallas_call(
        functools.partial(_mla_attn_kernel, tq=tq, tk=tk, q_off=past, s_valid=past + t, hp=hp, past=past),
        out_shape=jax.ShapeDtypeStruct((b, t, BRANCH), BF16),
        grid=(b, 4 // hp, t // tq),
        in_specs=in_specs,
        out_specs=pl.BlockSpec((1, tq, LANES * hp), lambda bi, j, i: (bi, i, j)),
        scratch_shapes=scratch,
        compiler_params=_cparams(("parallel", "parallel", "arbitrary"), 40),
        name="mla_attn",
    )(*args)


def _sb_block(zs, vis, us, carry, pv):
    nh = len(zs)
    tq = zs[0].shape[0]
    zls = [jnp.minimum(z, 0.0) - jnp.log(1.0 + jnp.exp(-jnp.abs(z))) for z in zs]
    lgs = [zl - z for zl, z in zip(zls, zs)]
    if vis is not None:
        lgs = [jnp.where(vis, lg, 0.0) for lg in lgs]
    his = [lg.astype(BF16) for lg in lgs]
    los = [(lg - hi.astype(F32)).astype(BF16) for lg, hi in zip(lgs, his)]
    later = _dot(jnp.concatenate(his + los, axis=0), us)
    new = []
    for h in range(nh):
        csum, acc = carry[h]
        tot = later[h * tq:(h + 1) * tq] + later[(nh + h) * tq:(nh + h + 1) * tq] + csum
        a = jnp.exp(zls[h] + tot)
        if vis is not None:
            a = jnp.where(vis, a, 0.0)
        new.append((csum + jnp.sum(lgs[h], axis=-1, keepdims=True), acc + pv(h, a.astype(BF16))))
    return tuple(new)


def _sb_heads(q2, nh):
    lane1 = lax.broadcasted_iota(jnp.int32, (1, LANES), 1)
    hm = ((lane1 < HEAD).astype(F32) * SB_SCALE, (lane1 >= HEAD).astype(F32) * SB_SCALE)
    return [(q2[:, LANES * (h // 2):LANES * (h // 2 + 1)] * hm[h % 2]).astype(BF16) for h in range(nh)]


def _sb_finish(res, o_ref, tq):
    lane = lax.broadcasted_iota(jnp.int32, (tq, LANES), 1)
    outs = [jnp.where(lane < HEAD, res[2 * p][1], res[2 * p + 1][1]) for p in range(len(res) // 2)]
    o_ref[0] = jnp.concatenate(outs, axis=1).astype(o_ref.dtype)


def _sb_attn_kernel(q_ref, k_ref, v_ref, us_ref, o_ref, *, tq, tk):
    i = pl.program_id(2)
    q_lo = i * tq
    nkb = (q_lo + tq - 2) // tk + 1
    nfull = q_lo // tk
    ridx = lax.broadcasted_iota(jnp.int32, (2 * tq, tk), 0)
    qidx = q_lo + jnp.where(ridx >= tq, ridx - tq, ridx)
    kloc = lax.broadcasted_iota(jnp.int32, (2 * tq, tk), 1)
    qs = jnp.concatenate(_sb_heads(q_ref[0], 2), axis=0)
    us = us_ref[...]

    def block(kb, carry, masked):
        csum, acc = carry
        o = pl.multiple_of(kb * tk, tk)
        kblk = k_ref[0, pl.ds(o, tk), :].astype(BF16)
        vblk = v_ref[0, pl.ds(o, tk), :].astype(BF16)
        z = _dot_nt(qs, kblk)
        zl = jnp.minimum(z, 0.0) - jnp.log(1.0 + jnp.exp(-jnp.abs(z)))
        lg = zl - z
        if masked:
            vis = (kloc + kb * tk) < qidx
            lg = jnp.where(vis, lg, 0.0)
        hi = lg.astype(BF16)
        lo = (lg - hi.astype(F32)).astype(BF16)
        later = _dot(jnp.concatenate([hi, lo], axis=0), us)
        a = jnp.exp(zl + later[:2 * tq] + later[2 * tq:] + csum)
        if masked:
            a = jnp.where(vis, a, 0.0)
        return csum + jnp.sum(lg, axis=-1, keepdims=True), acc + _dot(a.astype(BF16), vblk)

    carry = (jnp.zeros((2 * tq, 1), F32), jnp.zeros((2 * tq, LANES), F32))
    carry = lax.fori_loop(0, nkb - nfull, lambda s, c: block(nkb - 1 - s, c, True), carry)
    carry = lax.fori_loop(0, nfull, lambda s, c: block(nfull - 1 - s, c, False), carry)
    acc = carry[1]
    lane = lax.broadcasted_iota(jnp.int32, (tq, LANES), 1)
    o_ref[0] = jnp.where(lane < HEAD, acc[:tq], acc[tq:]).astype(o_ref.dtype)


def _sb_decode_kernel(q_ref, k_ref, v_ref, kpt_ref, vpt_ref, usn_ref, usp_ref, o_ref, *, t, past, pb):
    qh = _sb_heads(q_ref[0], HEADS)
    kn = k_ref[0].astype(BF16)
    vn = v_ref[0].astype(BF16)
    pair = lambda x, h: x[:, LANES * (h // 2):LANES * (h // 2 + 1)]
    ri = lax.broadcasted_iota(jnp.int32, (t, t), 0)
    ci = lax.broadcasted_iota(jnp.int32, (t, t), 1)
    carry = tuple((jnp.zeros((t, 1), F32), jnp.zeros((t, LANES), F32)) for _ in range(HEADS))
    zs = [_dot_nt(qh[h], pair(kn, h)) for h in range(HEADS)]
    carry = _sb_block(zs, ci < ri, usn_ref[...], carry, lambda h, a: _dot(a, pair(vn, h)))
    usp = usp_ref[...]
    for blk in reversed(range(past // pb)):
        kts = [kpt_ref[0, 0, p, :, blk * pb:(blk + 1) * pb].astype(BF16) for p in range(HEADS // 2)]
        vts = [vpt_ref[0, 0, p, :, blk * pb:(blk + 1) * pb].astype(BF16) for p in range(HEADS // 2)]
        zs = [_dot(qh[h], kts[h // 2]) for h in range(HEADS)]
        carry = _sb_block(zs, None, usp, carry, lambda h, a: _dot_nt(a, vts[h // 2]))
    _sb_finish(carry, o_ref, t)


def _strict_upper(n):
    idx = jnp.arange(n)
    return (idx[:, None] > idx[None, :]).astype(BF16)


def _sb_attn(proj3, tq, tk):
    b, t, _ = proj3.shape
    us = _strict_upper(tk)
    return pl.pallas_call(
        functools.partial(_sb_attn_kernel, tq=tq, tk=tk),
        out_shape=jax.ShapeDtypeStruct((b, t, BRANCH), BF16),
        grid=(b, HEADS // 2, t // tq),
        in_specs=[pl.BlockSpec((1, tq, LANES), lambda bi, j, i: (bi, i, O_SBQ // LANES + j)),
                  pl.BlockSpec((1, t, LANES), lambda bi, j, i: (bi, 0, O_SBK // LANES + j)),
                  pl.BlockSpec((1, t, LANES), lambda bi, j, i: (bi, 0, O_SBV // LANES + j)),
                  pl.BlockSpec(us.shape, lambda bi, j, i: (0, 0))],
        out_specs=pl.BlockSpec((1, tq, LANES), lambda bi, j, i: (bi, i, j)),
        compiler_params=_cparams(("parallel", "parallel", "arbitrary"), 48),
        name="sb_attn",
    )(proj3, proj3, proj3, us)


def _sb_decode(proj3, kpt, vpt, layer, pb):
    b, t, _ = proj3.shape
    past = kpt.shape[4]
    usn, usp = _strict_upper(t), _strict_upper(pb)
    new = lambda off: pl.BlockSpec((1, t, BRANCH), lambda bi, o=off // BRANCH: (bi, 0, o))
    cache = pl.BlockSpec((1, 1, HEADS // 2, LANES, past), lambda bi: (layer, bi, 0, 0, 0))
    return pl.pallas_call(
        functools.partial(_sb_decode_kernel, t=t, past=past, pb=pb),
        out_shape=jax.ShapeDtypeStruct((b, t, BRANCH), BF16),
        grid=(b,),
        in_specs=[new(O_SBQ), new(O_SBK), new(O_SBV), cache, cache,
                  pl.BlockSpec(usn.shape, lambda bi: (0, 0)), pl.BlockSpec(usp.shape, lambda bi: (0, 0))],
        out_specs=pl.BlockSpec((1, t, BRANCH), lambda bi: (bi, 0, 0)),
        compiler_params=_cparams(("parallel",), 48),
        name="sb_decode",
    )(proj3, proj3, proj3, kpt, vpt, usn, usp)


def _mix_kernel(yrw_ref, yssm_ref, ymla_ref, ysb_ref, gate_ref, pm_ref, wb_ref, bm_ref, o_ref, acc_ref):
    n = pl.program_id(1)

    @pl.when(n == 0)
    def _():
        acc_ref[...] = jnp.zeros_like(acc_ref)

    def branch(y_ref):
        g = gate_ref[...]
        gated = y_ref[...].astype(F32) * (g * jax.nn.sigmoid(g))
        up = _dot(gated.astype(BF16), wb_ref[0, 0].astype(BF16))
        acc_ref[...] += jax.nn.sigmoid(pm_ref[...] + bm_ref[0]) * up

    for idx, y_ref in enumerate((yrw_ref, yssm_ref, ymla_ref, ysb_ref)):
        pl.when(n == idx)(functools.partial(branch, y_ref))

    @pl.when(n == N_BRANCH - 1)
    def _():
        o_ref[...] = acc_ref[...].astype(BF16)


def _out_kernel(m_ref, x_ref, wo_ref, fg_ref, o_ref, *, final):
    xn = x_ref[...] + _dot(m_ref[...], wo_ref[0].astype(BF16))
    if final:
        xn = xn * lax.rsqrt(jnp.mean(xn * xn, axis=-1, keepdims=True) + EPS) * fg_ref[...]
    o_ref[...] = xn


def _merge(ys, proj, x2, wb, bm, wo, fg, final, layer):
    n, d = x2.shape
    tm = min(n, 1024)
    ysp = pl.BlockSpec((tm, BRANCH), lambda i, k: (i, 0))
    mixed = pl.pallas_call(
        _mix_kernel,
        out_shape=jax.ShapeDtypeStruct((n, d), BF16),
        grid=(n // tm, N_BRANCH),
        in_specs=[ysp, ysp, ysp, ysp,
                  pl.BlockSpec((tm, BRANCH), lambda i, k: (i, O_GATE // BRANCH + k)),
                  pl.BlockSpec((tm, d), lambda i, k: (i, O_MERGE // d + k)),
                  pl.BlockSpec((1, 1, BRANCH, d), lambda i, k: (layer, k, 0, 0)),
                  pl.BlockSpec((1, 1, d), lambda i, k: (k, 0, 0))],
        out_specs=pl.BlockSpec((tm, d), lambda i, k: (i, 0)),
        scratch_shapes=[pltpu.VMEM((tm, d), F32)],
        compiler_params=_cparams(("parallel", "arbitrary"), 56),
        name="branch_mix",
    )(*ys, proj, proj, wb, bm)
    to = min(n, 512)
    return pl.pallas_call(
        functools.partial(_out_kernel, final=final),
        out_shape=jax.ShapeDtypeStruct((n, d), F32),
        grid=(n // to,),
        in_specs=[pl.BlockSpec((to, d), lambda i: (i, 0)),
                  pl.BlockSpec((to, d), lambda i: (i, 0)),
                  pl.BlockSpec((1, d, d), lambda i: (layer, 0, 0), pipeline_mode=pl.Buffered(1)),
                  pl.BlockSpec((1, d), lambda i: (0, 0))],
        out_specs=pl.BlockSpec((to, d), lambda i: (i, 0)),
        compiler_params=_cparams(("parallel",), 48),
        name="out_proj",
    )(mixed, x2, wo, fg)


_SRC_SSM = RW_IN
_SRC_QLAT = _SRC_SSM + BRANCH
_SRC_KVLAT = _SRC_QLAT + MLA_Q_LORA
_SRC_KPE = _SRC_KVLAT + MLA_KV_LORA
_SRC_SB = _SRC_KPE + MLA_ROPE
_SRC_GATE = _SRC_SB + 3 * BRANCH
_SRC_MERGE = _SRC_GATE + N_BRANCH * BRANCH
_SRC_LORA = 3 * BRANCH
_PW_BLK = 4 * LANES
_PW_LAST = PROJ_W // _PW_BLK - 1
_PW_TAIL = {_PW_LAST - 1: (_SRC_QLAT, _SRC_QLAT + LANES, _SRC_QLAT + 2 * LANES, _SRC_LORA),
            _PW_LAST: (_SRC_KVLAT, _SRC_KVLAT + LANES, _SRC_KPE, _SRC_KPE)}


def _permute_src_col(j, k):
    nm = N_BRANCH * D_MODEL // _PW_BLK
    ng = nm + N_BRANCH * BRANCH // _PW_BLK
    nsb = ng + 3 * BRANCH // _PW_BLK
    nrw = nsb + 3 * BRANCH // _PW_BLK
    u = MLA_ROPE
    col = jnp.where(j < nm, _SRC_MERGE // u + _PW_BLK // u * j,
                    jnp.where(j < ng, _SRC_GATE // u + _PW_BLK // u * (j - nm),
                              jnp.where(j < nsb, _SRC_SB // u + _PW_BLK // u * (j - ng),
                                        jnp.where(j < nrw, _PW_BLK // u * (j - nsb), _SRC_SSM // u))))
    col = col + LANES // u * k
    for jj, src in _PW_TAIL.items():
        col = jnp.where(j == jj, src[k] // u, col)
    return col * u


def _permute_kernel(x0_ref, x1_ref, x2_ref, x3_ref, o_ref):
    j = pl.program_id(1)
    wins = (x0_ref, x1_ref, x2_ref, x3_ref)
    half = MLA_ROPE // 2

    def put(k, rows):
        o_ref[0, :, LANES * k:LANES * (k + 1)] = rows.T.astype(BF16)

    @pl.when(j < _PW_LAST)
    def _():
        for k in range(4):
            put(k, wins[k][0])

    @pl.when(j == _PW_LAST)
    def _():
        for k in range(2):
            put(k, wins[k][0])
        kpe = wins[2][0]
        row = lax.broadcasted_iota(jnp.int32, kpe.shape, 0)
        put(2, jnp.where(row < MLA_ROPE, kpe, 0.0))
        zeros = jnp.zeros((LANES - MLA_ROPE, kpe.shape[1]), F32)
        put(3, jnp.concatenate([-kpe[half:MLA_ROPE], kpe[:half], zeros], axis=0))


def _permute_w_in(w_in):
    depth, d, _ = w_in.shape
    wt = jnp.swapaxes(w_in, 1, 2)
    win = lambda k: pl.BlockSpec((pl.Element(1), pl.Element(LANES), pl.Element(d)),
                                 lambda l, j, k=k: (l, _permute_src_col(j, k), 0))
    return pl.pallas_call(
        _permute_kernel,
        out_shape=jax.ShapeDtypeStruct((depth, d, PROJ_W), BF16),
        grid=(depth, PROJ_W // _PW_BLK),
        in_specs=[win(k) for k in range(4)],
        out_specs=pl.BlockSpec((1, d, _PW_BLK), lambda l, j: (l, 0, j)),
        compiler_params=_cparams(("parallel", "parallel"), 32),
        name="permute_w_in",
    )(wt, wt, wt, wt)


def _rope_tables(pos):
    half = MLA_ROPE // 2
    inv = ROPE_BASE ** (-jnp.arange(half, dtype=F32) / half)
    ang = pos.astype(F32)[:, None] * inv
    cos, sin = jnp.cos(ang), jnp.sin(ang)
    t = pos.shape[0]
    cc = jnp.concatenate([cos, cos], axis=1)
    ss = jnp.concatenate([sin, sin], axis=1)
    tq = jnp.concatenate([jnp.ones((t, MLA_NOPE), F32), cc, ss], axis=1) * MLA_SCALE
    pad = jnp.zeros((t, LANES - MLA_ROPE), F32)
    return tq, jnp.concatenate([cc, pad], axis=1), jnp.concatenate([ss, pad], axis=1)


def _mla_weights(w_q_up, w_kv_up):
    half = MLA_ROPE // 2
    wq = w_q_up.reshape(MLA_Q_LORA, HEADS, MLA_NOPE + MLA_ROPE)
    x1 = wq[:, :, MLA_NOPE:MLA_NOPE + half]
    x2 = wq[:, :, MLA_NOPE + half:]
    wq_p = jnp.concatenate([wq, -x2, x1], axis=2).reshape(MLA_Q_LORA, HEADS * LANES).astype(BF16)
    wkv = w_kv_up.reshape(MLA_KV_LORA, HEADS, 2 * HEAD)
    wk_p = jnp.concatenate([wkv[:, :, :HEAD], jnp.zeros((MLA_KV_LORA, HEADS, HEAD), F32)], axis=2)
    wk_p = wk_p.reshape(MLA_KV_LORA, HEADS * LANES).astype(BF16)
    wv_p = wkv[:, :, HEAD:].reshape(MLA_KV_LORA, BRANCH).astype(BF16)
    return wq_p, wk_p, wv_p


def _s5_tables(lam_re, lam_im, log_dt, b_re, b_im, c_re, c_im):
    dt = jnp.exp(log_dt)[:, None]
    mag = jnp.exp(lam_re * dt)
    ang = lam_im * dt
    lb_re, lb_im = mag * jnp.cos(ang), mag * jnp.sin(ang)
    nr, ni = lb_re - 1.0, lb_im
    den = lam_re * lam_re + lam_im * lam_im
    f_re = (nr * lam_re + ni * lam_im) / den
    f_im = (ni * lam_re - nr * lam_im) / den
    bb_re = f_re[..., None] * b_re - f_im[..., None] * b_im
    bb_im = f_re[..., None] * b_im + f_im[..., None] * b_re
    eye = jnp.eye(SSM_GROUPS, dtype=F32)
    blk_in = lambda m: jnp.einsum('gpc,gh->gchp', m, eye).reshape(BRANCH, SSM_W)
    blk_out = lambda m: jnp.einsum('gcp,gh->gphc', m, eye).reshape(SSM_W, BRANCH)
    bblk = jnp.concatenate([blk_in(bb_re), blk_in(bb_im)], axis=1).astype(BF16)
    cblk = jnp.concatenate([blk_out(c_re), blk_out(-c_im)], axis=0).astype(BF16)

    def power(j):
        m = jnp.exp(lam_re * dt * j)
        return (m * jnp.cos(ang * j)).reshape(1, SSM_W), (m * jnp.sin(ang * j)).reshape(1, SSM_W)

    lam = jnp.concatenate([p for j in (1, 2, 4) for p in power(j)] + [jnp.zeros((2, SSM_W), F32)], axis=0)
    pws = [power(j) for j in range(1, 9)]
    pw = jnp.concatenate([p[0] for p in pws] + [p[1] for p in pws], axis=0)
    return bblk, cblk, lam, pw


def _layer(x, pos0, st, lw, final_g, final):
    b, t, d = x.shape
    n = b * t
    x2 = x.reshape(n, d)
    proj = _norm_proj(x2, lw["norm_g"], lw["w_in_all"], lw["layer"])
    proj3 = proj.reshape(b, t, PROJ_W)
    past = st["past"]

    chunk = min(t, CHUNK)
    y_rw, wkv_new = _rwkv(proj3, st["sp_r"], st["sp_k"], st["sp_v"], st["sp_l"], st["wkv"],
                        lw["rw_pvec"], lw["rw_mu_l"], lw["rw_w2p"], lw["rw_a2p"], lw["g2"], chunk)
    shift_new = jnp.concatenate([proj3[:, t - 1:, O_RWR:O_RWR + 3 * BRANCH],
                                 proj3[:, t - 1:, O_LORA:O_LORA + 2 * RW_LORA]], axis=-1)

    y_ssm, hr, hi = _s5(proj3, st["ssm_re"], st["ssm_im"], lw["ssm_bblk"], lw["ssm_cblk"], lw["ssm_lam"],
                        lw["ssm_pw"], lw["ssm_d"], lw["ssm_wglu"], lw["ssm_bglu"])
    ssm_re_new = hr.reshape(b, SSM_GROUPS, SSM_STATE)
    ssm_im_new = hi.reshape(b, SSM_GROUPS, SSM_STATE)

    pos = pos0 + jnp.arange(t, dtype=jnp.int32)
    tq_tab, tc_tab, ts_tab = _rope_tables(pos)
    q, ckv, kr = _mla_prep(proj, t, tq_tab, tc_tab, ts_tab, lw["mla_gq"], lw["mla_gkv"], lw["mla_wq"])
    ckv3 = ckv.reshape(b, t, MLA_KV_LORA)
    kr3 = kr.reshape(b, t, MLA_ROPE)
    s_pad = -(-(past + t) // LANES) * LANES
    kc, vv = _mla_kv(ckv[None], kr[None], lw["mla_wk"], lw["mla_wv"], lw["dup"])
    kc3, vv3 = kc.reshape(b, t, HEADS * LANES), vv.reshape(b, t, BRANCH)
    kcp3 = vvp3 = None
    if past:
        kcp, vvp = _mla_kv(st["ckv_all"], st["kpe_all"], lw["mla_wk"], lw["mla_wv"], lw["dup"], lw["layer"])
        kcp3, vvp3 = kcp.reshape(b, past, HEADS * LANES), vvp.reshape(b, past, BRANCH)
    small = t <= 64
    assert small or not past
    tq = min(t, 256)
    hp = 4 if small else 1
    y_mla = _mla_attn(q.reshape(b, t, HEADS * LANES), kc3, vv3, kcp3, vvp3, s_pad, tq,
                      s_pad if small else min(512, s_pad), hp)

    sbk = proj3[:, :, O_SBK:O_SBK + BRANCH]
    sbv = proj3[:, :, O_SBV:O_SBV + BRANCH]
    if past:
        y_sb = _sb_decode(proj3, st["sbk_t"], st["sbv_t"], lw["layer"], min(256, past))
    else:
        y_sb = _sb_attn(proj3, tq, min(256, t))

    ys = [y.reshape(n, BRANCH) for y in (y_rw, y_ssm, y_mla, y_sb)]
    x_new = _merge(ys, proj, x2, lw["w_branch"], lw["b_merge"], lw["w_out"], final_g, final,
                   lw["layer"]).reshape(b, t, d)
    new_state = (shift_new, wkv_new, ssm_re_new, ssm_im_new, ckv3, kr3,
                 sbk.reshape(b, t, HEADS, HEAD), sbv.reshape(b, t, HEADS, HEAD))
    return x_new, new_state


def kernel(x_prompt, x_sample, state_rwkv_shift, state_rwkv_wkv, state_ssm_re, state_ssm_im, cache_mla_ckv, cache_mla_kpe, cache_sb_k, cache_sb_v, norm_g, w_in, rw_mu, rw_w0, rw_w2, rw_a0, rw_a2, rw_k_k, rw_k_a, rw_r_k, rw_lnx_g, rw_lnx_b, ssm_lam_re, ssm_lam_im, ssm_log_dt, ssm_b_re, ssm_b_im, ssm_c_re, ssm_c_im, ssm_d, ssm_w_glu, ssm_b_glu, mla_q_norm, mla_w_q_up, mla_kv_norm, mla_w_kv_up, w_branch, b_merge, w_out, final_norm_g):
    depth = w_in.shape[0]
    bp, tp, _ = x_prompt.shape
    bs = x_sample.shape[0]
    past = cache_mla_ckv.shape[2]

    w_in_p = _permute_w_in(w_in)
    lane_i = jnp.arange(LANES)
    g2 = ((lane_i[:, None] // HEAD) == (lane_i[None, :] // HEAD)).astype(BF16)
    rope_i = jnp.arange(MLA_ROPE)
    dup = ((lane_i[None, :] == rope_i[:, None] + MLA_NOPE)
           | (lane_i[None, :] == rope_i[:, None] + MLA_NOPE + MLA_ROPE)).astype(BF16)
    final_g = final_norm_g.reshape(1, D_MODEL)
    zpad = jnp.zeros((RW_LORA, BRANCH), F32)

    layers = []
    for l in range(depth):
        wq_p, wk_p, wv_p = _mla_weights(mla_w_q_up[l], mla_w_kv_up[l])
        bblk, cblk, lam, pw = _s5_tables(ssm_lam_re[l], ssm_lam_im[l], ssm_log_dt[l], ssm_b_re[l], ssm_b_im[l],
                                         ssm_c_re[l], ssm_c_im[l])
        mu = rw_mu[l]
        rows = [mu[:BRANCH], mu[BRANCH:2 * BRANCH], mu[2 * BRANCH:3 * BRANCH], rw_w0[l], rw_a0[l], rw_k_k[l],
                rw_k_a[l], rw_r_k[l].reshape(BRANCH), rw_lnx_g[l], rw_lnx_b[l]]
        pvec = jnp.concatenate([jnp.stack(rows), jnp.zeros((16 - len(rows), BRANCH), F32)], axis=0)
        layers.append(dict(
            norm_g=norm_g[l].reshape(1, D_MODEL), w_in_all=w_in_p, layer=l,
            rw_pvec=pvec, rw_mu_l=mu[3 * BRANCH:].reshape(1, 2 * RW_LORA),
            rw_w2p=jnp.concatenate([rw_w2[l], zpad], axis=0).astype(BF16),
            rw_a2p=jnp.concatenate([zpad, rw_a2[l]], axis=0).astype(BF16),
            g2=g2, dup=dup,
            ssm_bblk=bblk, ssm_cblk=cblk, ssm_lam=lam, ssm_pw=pw,
            ssm_d=ssm_d[l].reshape(1, BRANCH), ssm_wglu=ssm_w_glu[l].astype(BF16),
            ssm_bglu=ssm_b_glu[l].reshape(1, BRANCH),
            mla_gq=mla_q_norm[l].reshape(1, MLA_Q_LORA), mla_gkv=mla_kv_norm[l].reshape(1, MLA_KV_LORA),
            mla_wq=wq_p, mla_wk=wk_p, mla_wv=wv_p,
            w_branch=w_branch, b_merge=b_merge[l].reshape(N_BRANCH, 1, D_MODEL), w_out=w_out))

    def fresh(bn):
        return dict(sp_r=jnp.zeros((bn, 1, BRANCH), F32), sp_k=jnp.zeros((bn, 1, BRANCH), F32),
                    sp_v=jnp.zeros((bn, 1, BRANCH), F32), sp_l=jnp.zeros((bn, 1, 2 * RW_LORA), F32),
                    wkv=jnp.zeros((bn, HEADS, HEAD, HEAD), F32),
                    ssm_re=jnp.zeros((bn, 1, SSM_W), F32), ssm_im=jnp.zeros((bn, 1, SSM_W), F32),
                    past=0)

    nl = cache_mla_ckv.shape[0]
    ckv_all = cache_mla_ckv.reshape(nl, bs * past, MLA_KV_LORA)
    kpe_all = cache_mla_kpe.reshape(nl, bs * past, MLA_ROPE)
    sbk_t = jnp.transpose(cache_sb_k, (0, 1, 3, 4, 2)).reshape(nl, bs, HEADS // 2, LANES, past)
    sbv_t = jnp.transpose(cache_sb_v, (0, 1, 3, 4, 2)).reshape(nl, bs, HEADS // 2, LANES, past)

    def carried(l):
        sh = state_rwkv_shift[l]
        return dict(sp_r=sh[:, :, :BRANCH], sp_k=sh[:, :, BRANCH:2 * BRANCH], sp_v=sh[:, :, 2 * BRANCH:3 * BRANCH],
                    sp_l=sh[:, :, 3 * BRANCH:], wkv=state_rwkv_wkv[l],
                    ssm_re=state_ssm_re[l].reshape(bs, 1, SSM_W), ssm_im=state_ssm_im[l].reshape(bs, 1, SSM_W),
                    past=past, ckv_all=ckv_all, kpe_all=kpe_all, sbk_t=sbk_t, sbv_t=sbv_t)

    xp, xs = x_prompt, x_sample
    new_p, new_s = [], []
    for l in range(depth):
        last = l == depth - 1
        xp, st_p = _layer(xp, 0, fresh(bp), layers[l], final_g, last)
        xs, st_s = _layer(xs, past, carried(l), layers[l], final_g, last)
        new_p.append(st_p)
        new_s.append(st_s)
    stk = lambda lst, i: jnp.stack([s[i] for s in lst], axis=0)
    return (xp, xs) + tuple(stk(new_p, i) for i in range(8)) + tuple(stk(new_s, i) for i in range(8))
```

```python
import functools
import math

import jax
import jax.numpy as jnp
from jax import lax
from jax.experimental import pallas as pl
from jax.experimental.pallas import tpu as pltpu

F32 = jnp.float32
BF16 = jnp.bfloat16

D_MODEL = 2048
BRANCH = 512
N_BRANCH = 4
EPS = 1e-6
CHUNK = 64
HEAD = 64
HEADS = 8
RW_LORA = 64
RW_IN = 3 * BRANCH + 2 * RW_LORA
RW_GN_EPS = 64e-5
SSM_GROUP = 16
SSM_GROUPS = 32
SSM_STATE = 64
SSM_W = SSM_GROUPS * SSM_STATE
MLA_NOPE = 64
MLA_ROPE = 32
MLA_Q_LORA = 384
MLA_KV_LORA = 256
MLA_SCALE = 1.0 / math.sqrt(MLA_NOPE + MLA_ROPE)
ROPE_BASE = 10000.0
SB_SCALE = 1.0 / math.sqrt(HEAD)
LANES = 128
MIB = 1024 * 1024

O_MERGE = 0
O_GATE = 8192
O_SBQ = 10240
O_SBK = 10752
O_SBV = 11264
O_RWR = 11776
O_RWK = 12288
O_RWV = 12800
O_SSM = 13312
O_QLAT = 13824
O_LORA = 14208
O_KVLAT = 14336
O_KPE = 14592
O_KPESW = 14720
PROJ_W = 14848


def _cparams(sem, vmem_mib):
    return pltpu.CompilerParams(dimension_semantics=sem, vmem_limit_bytes=vmem_mib * MIB)


def _dot(a, b):
    return jnp.dot(a, b, preferred_element_type=F32)


def _dot_nt(a, b):
    return lax.dot_general(a, b, (((1,), (1,)), ((), ())), preferred_element_type=F32)


def _split3(x):
    h1 = x.astype(BF16)
    r1 = x - h1.astype(F32)
    h2 = r1.astype(BF16)
    h3 = (r1 - h2.astype(F32)).astype(BF16)
    return h1, h2, h3


def _softplus(x):
    return jnp.maximum(x, 0.0) + jnp.log1p(jnp.exp(-jnp.abs(x)))


def _sigmoid(x):
    return 0.5 * jnp.tanh(0.5 * x) + 0.5


def _norm_proj_kernel(x_ref, g_ref, w_ref, o_ref, h_ref):
    @pl.when(pl.program_id(1) == 0)
    def _():
        x = x_ref[...]
        ms = jnp.mean(x * x, axis=-1, keepdims=True)
        h_ref[...] = (x * lax.rsqrt(ms + EPS) * g_ref[...]).astype(BF16)

    o_ref[...] = _dot(h_ref[...], w_ref[0])


def _norm_proj(x2, g, w_all, layer):
    n, d = x2.shape
    width = w_all.shape[2]
    tm = min(n, 1024)
    tn = 512
    return pl.pallas_call(
        _norm_proj_kernel,
        out_shape=jax.ShapeDtypeStruct((n, width), F32),
        grid=(n // tm, width // tn),
        in_specs=[
            pl.BlockSpec((tm, d), lambda i, j: (i, 0)),
            pl.BlockSpec((1, d), lambda i, j: (0, 0)),
            pl.BlockSpec((1, d, tn), lambda i, j: (layer, 0, j)),
        ],
        out_specs=pl.BlockSpec((tm, tn), lambda i, j: (i, j)),
        scratch_shapes=[pltpu.VMEM((tm, d), BF16)],
        compiler_params=_cparams(("parallel", "arbitrary"), 40),
        name="norm_proj",
    )(x2, g, w_all)


def _segsum(x, g2):
    outs = []
    for j in range(BRANCH // LANES):
        h1, h2, h3 = _split3(x[:, LANES * j:LANES * (j + 1)])
        outs.append(_dot(h1, g2) + _dot(h2, g2) + _dot(h3, g2))
    return jnp.concatenate(outs, axis=1)


def _rwkv_kernel(pr_ref, pk_ref, pv_ref, plo_ref, spr_ref, spk_ref, spv_ref, spl_ref, s0_ref,
                 pvec_ref, mul_ref, w2_ref, a2_ref, g2_ref,
                 y_ref, sout_ref,
                 st_ref, cr_ref, ck_ref, cv_ref, cl_ref, *, C, nb):
    c = pl.program_id(1)

    @pl.when(c == 0)
    def _():
        zero = jnp.zeros((HEAD, HEAD), F32)
        for bi in range(nb):
            for j in range(HEADS // 2):
                st_ref[bi, j] = jnp.concatenate(
                    [jnp.concatenate([s0_ref[bi, 2 * j], zero], axis=1),
                     jnp.concatenate([zero, s0_ref[bi, 2 * j + 1]], axis=1)], axis=0)
        cr_ref[...] = spr_ref[...]
        ck_ref[...] = spk_ref[...]
        cv_ref[...] = spv_ref[...]
        cl_ref[...] = spl_ref[...]

    pvec = pvec_ref[...]
    prow = lambda i: pvec[i:i + 1, :]
    g2 = g2_ref[...]

    def tshift(p_ref, carry_ref, mu):
        outs = []
        for bi in range(nb):
            p = p_ref[bi]
            rolled = pltpu.roll(p, 1, 0)
            rid = lax.broadcasted_iota(jnp.int32, p.shape, 0)
            prev = jnp.where(rid == 0, carry_ref[bi], rolled)
            carry_ref[bi] = p[C - 1:C, :]
            outs.append(p + (prev - p) * mu)
        return jnp.concatenate(outs, axis=0)

    r = tshift(pr_ref, cr_ref, prow(0))
    k = tshift(pk_ref, ck_ref, prow(1))
    v = tshift(pv_ref, cv_ref, prow(2))
    lo = tshift(plo_ref, cl_ref, mul_ref[...])

    log_w = -_softplus(-(prow(3) + _dot(jnp.tanh(lo).astype(BF16), w2_ref[...]))) - 0.5
    ld = -jnp.exp(log_w)
    a_icl = jax.nn.sigmoid(prow(4) + _dot(lo.astype(BF16), a2_ref[...]))
    kkr = k * prow(5)
    kk = kkr / jnp.maximum(jnp.sqrt(_segsum(kkr * kkr, g2)), 1e-12)
    k2 = k * (1.0 + (a_icl - 1.0) * prow(6))
    av = -kk
    bv = kk * a_icl

    sh = C.bit_length() - 1
    ri = lax.broadcasted_iota(jnp.int32, (nb * C, nb * C), 0)
    ci = lax.broadcasted_iota(jnp.int32, (nb * C, nb * C), 1)
    lincl = jnp.logical_and(ci <= ri, (ci >> sh) == (ri >> sh)).astype(BF16)
    h1, h2, h3 = _split3(ld)
    g = _dot(lincl, h1) + _dot(lincl, h2) + _dot(lincl, h3)
    g_lasts = [g[(bi + 1) * C - 1:(bi + 1) * C, :] for bi in range(nb)]
    g_last = jnp.concatenate([jnp.broadcast_to(gl, (C, BRANCH)) for gl in g_lasts], axis=0)
    eg = jnp.exp(g)
    eng = jnp.exp(-g)
    at = av * jnp.exp(g - ld)
    rt = r * eg
    bt = bv * eng
    kt = k2 * eng
    e_c = jnp.exp(g_last - g)
    b_end = bv * e_c
    k_end = k2 * e_c
    eg_lasts = [jnp.exp(gl) for gl in g_lasts]

    c2 = 2 * C
    npair = BRANCH // LANES
    r_i = lax.broadcasted_iota(jnp.int32, (c2, c2), 0)
    c_i = lax.broadcasted_iota(jnp.int32, (c2, c2), 1)
    c_m = jnp.where(c_i >= C, c_i - C, c_i)
    mhalf = c_m < jnp.where(r_i < C, r_i, r_i - C + 1)
    mfull = jnp.concatenate([mhalf, mhalf], axis=0)
    bd2 = (r_i >> sh) == (c_i >> sh)
    eye2 = (r_i == c_i).astype(F32)
    lane = lax.broadcasted_iota(jnp.int32, (1, LANES), 1)
    m0 = (lane < HEAD).astype(F32)
    m1 = (lane >= HEAD).astype(F32)
    v_i = lax.broadcasted_iota(jnp.int32, (LANES, LANES), 0)
    k_i = lax.broadcasted_iota(jnp.int32, (LANES, LANES), 1)
    bdmask = (v_i >> 6) == (k_i >> 6)
    zf = jnp.zeros((C, LANES), F32)
    zc = jnp.zeros((C, LANES), BF16)
    sls = [slice(LANES * j, LANES * (j + 1)) for j in range(npair)]

    units = [(bi, j) for bi in range(nb) for j in range(npair)]
    tile = lambda x, bi, j: x[bi * C:(bi + 1) * C, LANES * j:LANES * (j + 1)]
    ws, a_ss, s2s = [], [], []
    for bi, j in units:
        atj, rtj = tile(at, bi, j), tile(rt, bi, j)
        a0, r0, a1, r1 = atj * m0, rtj * m0, atj * m1, rtj * m1
        lhs_w = jnp.concatenate([jnp.concatenate([a0, zf], axis=1), jnp.concatenate([r0, zf], axis=1),
                                 jnp.concatenate([zf, a1], axis=1), jnp.concatenate([zf, r1], axis=1)],
                                axis=0).astype(BF16)
        btj, ktj = tile(bt, bi, j), tile(kt, bi, j)
        rhs_w = jnp.concatenate([jnp.concatenate([btj, ktj], axis=0),
                                 jnp.concatenate([ktj, btj], axis=0)], axis=1).astype(BF16)
        ws.append(jnp.where(mfull, _dot_nt(lhs_w, rhs_w), 0.0))
        lhs_s = jnp.concatenate([a0, r0, a1, r1], axis=0).astype(BF16)
        s2 = st_ref[bi, j]
        s2h, s2l, _ = _split3(s2)
        a_ss.append(_dot_nt(lhs_s, s2h) + _dot_nt(lhs_s, s2l))
        s2s.append(s2)

    n_pows = [jnp.where(bd2, jnp.concatenate([w[:C], w[c2:c2 + C]], axis=0), 0.0) for w in ws]
    t_mats = [eye2 + n for n in n_pows]
    lv = 2
    while lv < C:
        n_pows = [_dot(n.astype(BF16), n.astype(BF16)) for n in n_pows]
        t_mats = [t + _dot(t.astype(BF16), n.astype(BF16)) for t, n in zip(t_mats, n_pows)]
        lv *= 2

    wbs = [w.astype(BF16) for w in ws]
    v0s = [(tile(v, bi, j) * m0).astype(BF16) for bi, j in units]
    v1s = [(tile(v, bi, j) * m1).astype(BF16) for bi, j in units]
    xs = [jnp.concatenate([a_s[:C] + _dot(wb[:C], jnp.concatenate([zc, v0], axis=0)),
                           a_s[c2:c2 + C] + _dot(wb[c2:c2 + C], jnp.concatenate([v1, zc], axis=0))], axis=0)
          for a_s, wb, v0, v1 in zip(a_ss, wbs, v0s, v1s)]
    us = [_dot(t.astype(BF16), x.astype(BF16)) for t, x in zip(t_mats, xs)]
    ys = []
    for n, (bi, j) in enumerate(units):
        a_s, wb, u = a_ss[n], wbs[n], us[n]
        u0, u1 = u[:C], u[C:]
        y0 = a_s[C:c2] + _dot(wb[C:c2], jnp.concatenate([u0.astype(BF16), v0s[n]], axis=0))
        y1 = a_s[c2 + C:] + _dot(wb[c2 + C:], jnp.concatenate([v1s[n], u1.astype(BF16)], axis=0))
        ys.append(y0 + y1)
        uv = jnp.concatenate([u0 + u1, tile(v, bi, j)], axis=0)
        bk = jnp.concatenate([tile(b_end, bi, j), tile(k_end, bi, j)], axis=0).astype(BF16)
        upd = _dot(uv.T.astype(BF16), bk)
        st_ref[bi, j] = s2s[n] * eg_lasts[bi][:, sls[j]] + jnp.where(bdmask, upd, 0.0)
    y = jnp.concatenate([jnp.concatenate(ys[bi * npair:(bi + 1) * npair], axis=1) for bi in range(nb)],
                        axis=0)

    inv_n = 1.0 / HEAD
    yc = y - _segsum(y, g2) * inv_n
    yn = yc * lax.rsqrt(_segsum(yc * yc, g2) * inv_n + RW_GN_EPS) * prow(8) + prow(9)
    out = (yn + _segsum(r * k2 * prow(7), g2) * v).astype(y_ref.dtype)
    for bi in range(nb):
        y_ref[bi] = out[bi * C:(bi + 1) * C]

    @pl.when(c == pl.num_programs(1) - 1)
    def _():
        for bi in range(nb):
            for j in range(HEADS // 2):
                s2 = st_ref[bi, j]
                sout_ref[bi, 2 * j] = s2[:HEAD, :HEAD]
                sout_ref[bi, 2 * j + 1] = s2[HEAD:, HEAD:]


def _rwkv(proj3, sp_r, sp_k, sp_v, sp_l, s0, pvec, mu_l, w2p, a2p, g2, chunk):
    b, t, _ = proj3.shape
    nc = t // chunk
    nb = min(b, 4)
    cb = lambda off, wdt: off // wdt
    bspec = lambda off, wdt: pl.BlockSpec((nb, chunk, wdt), lambda i, c, o=cb(off, wdt): (i, c, o))
    row3 = lambda wdt: pl.BlockSpec((nb, 1, wdt), lambda i, c: (i, 0, 0))
    full2 = lambda a: pl.BlockSpec(a.shape, lambda i, c: (0, 0))
    return pl.pallas_call(
        functools.partial(_rwkv_kernel, C=chunk, nb=nb),
        out_shape=(jax.ShapeDtypeStruct((b, t, BRANCH), BF16),
                   jax.ShapeDtypeStruct((b, HEADS, HEAD, HEAD), F32)),
        grid=(b // nb, nc),
        in_specs=[
            bspec(O_RWR, BRANCH), bspec(O_RWK, BRANCH), bspec(O_RWV, BRANCH), bspec(O_LORA, LANES),
            row3(BRANCH), row3(BRANCH), row3(BRANCH), row3(LANES),
            pl.BlockSpec((nb, HEADS, HEAD, HEAD), lambda i, c: (i, 0, 0, 0)),
            full2(pvec), full2(mu_l), full2(w2p), full2(a2p), full2(g2),
        ],
        out_specs=(pl.BlockSpec((nb, chunk, BRANCH), lambda i, c: (i, c, 0)),
                   pl.BlockSpec((nb, HEADS, HEAD, HEAD), lambda i, c: (i, 0, 0, 0))),
        scratch_shapes=[pltpu.VMEM((nb, 4, LANES, LANES), F32),
                        pltpu.VMEM((nb, 1, BRANCH), F32), pltpu.VMEM((nb, 1, BRANCH), F32),
                        pltpu.VMEM((nb, 1, BRANCH), F32), pltpu.VMEM((nb, 1, LANES), F32)],
        compiler_params=_cparams(("parallel", "arbitrary"), 32),
        name="rwkv7",
    )(proj3, proj3, proj3, proj3, sp_r, sp_k, sp_v, sp_l, s0, pvec, mu_l, w2p, a2p, g2)


def _s5_kernel(u_ref, h0r_ref, h0i_ref, bblk_ref, cblk_ref, lam_ref, pw_ref, dsk_ref, wglu_ref, bglu_ref,
               y_ref, hro_ref, hio_ref, x_ref, cr_ref, ci_ref, *, tb):
    @pl.when(pl.program_id(1) == 0)
    def _():
        cr_ref[...] = h0r_ref[0]
        ci_ref[...] = h0i_ref[0]

    u = u_ref[0]
    x = _dot(u.astype(BF16), bblk_ref[...])
    xr = x[:, :SSM_W]
    xi = x[:, SSM_W:]
    rid = lax.broadcasted_iota(jnp.int32, (tb, 1), 0) & 7
    for lvl, s in enumerate((1, 2, 4)):
        ar = lam_ref[2 * lvl:2 * lvl + 1, :]
        ai = lam_ref[2 * lvl + 1:2 * lvl + 2, :]
        sr = pltpu.roll(xr, s, 0)
        si = pltpu.roll(xi, s, 0)
        valid = rid >= s
        nr = xr + jnp.where(valid, ar * sr - ai * si, 0.0)
        ni = xi + jnp.where(valid, ar * si + ai * sr, 0.0)
        xr, xi = nr, ni
    x_ref[:, :SSM_W] = xr
    x_ref[:, SSM_W:] = xi
    pwr = pw_ref[0:8, :]
    pwi = pw_ref[8:16, :]

    def body(gi, carry):
        cr, ci = carry
        o = pl.multiple_of(gi * 8, 8)
        br = x_ref[pl.ds(o, 8), 0:SSM_W]
        bi = x_ref[pl.ds(o, 8), SSM_W:2 * SSM_W]
        br = br + pwr * cr - pwi * ci
        bi = bi + pwr * ci + pwi * cr
        x_ref[pl.ds(o, 8), 0:SSM_W] = br
        x_ref[pl.ds(o, 8), SSM_W:2 * SSM_W] = bi
        return br[7:8, :], bi[7:8, :]

    cr, ci = lax.fori_loop(0, tb // 8, body, (cr_ref[...], ci_ref[...]))
    cr_ref[...] = cr
    ci_ref[...] = ci
    hro_ref[0] = cr
    hio_ref[0] = ci
    y = _dot(x_ref[...].astype(BF16), cblk_ref[...]) + dsk_ref[...] * u
    g = jax.nn.gelu(y)
    out = g * jax.nn.sigmoid(_dot(g.astype(BF16), wglu_ref[...]) + bglu_ref[...])
    y_ref[0] = out.astype(y_ref.dtype)


def _s5(proj3, h0r, h0i, bblk, cblk, lam, pw, dsk, wglu, bglu):
    b, t, _ = proj3.shape
    tb = min(t, 128)
    full2 = lambda a: pl.BlockSpec(a.shape, lambda i, c: (0, 0))
    row3 = pl.BlockSpec((1, 1, SSM_W), lambda i, c: (i, 0, 0))
    return pl.pallas_call(
        functools.partial(_s5_kernel, tb=tb),
        out_shape=(jax.ShapeDtypeStruct((b, t, BRANCH), BF16),
                   jax.ShapeDtypeStruct((b, 1, SSM_W), F32),
                   jax.ShapeDtypeStruct((b, 1, SSM_W), F32)),
        grid=(b, t // tb),
        in_specs=[pl.BlockSpec((1, tb, BRANCH), lambda i, c: (i, c, O_SSM // BRANCH)),
                  row3, row3, full2(bblk), full2(cblk), full2(lam), full2(pw), full2(dsk),
                  full2(wglu), full2(bglu)],
        out_specs=(pl.BlockSpec((1, tb, BRANCH), lambda i, c: (i, c, 0)), row3, row3),
        scratch_shapes=[pltpu.VMEM((tb, 2 * SSM_W), F32),
                        pltpu.VMEM((1, SSM_W), F32), pltpu.VMEM((1, SSM_W), F32)],
        compiler_params=_cparams(("parallel", "arbitrary"), 48),
        name="s5",
    )(proj3, h0r, h0i, bblk, cblk, lam, pw, dsk, wglu, bglu)


def _mla_prep_kernel(ql_ref, kvl_ref, kpe_ref, kpesw_ref, tq_ref, tc_ref, ts_ref,
                     gq_ref, gkv_ref, wq_ref, q_ref, ckv_ref, kr_ref):
    ql = ql_ref[...]
    qn = ql * lax.rsqrt(jnp.mean(ql * ql, axis=-1, keepdims=True) + EPS) * gq_ref[...]
    q = _dot(qn.astype(BF16), wq_ref[...])
    tq = tq_ref[...]
    q_ref[...] = jnp.concatenate(
        [q[:, LANES * h:LANES * (h + 1)] * tq for h in range(HEADS)], axis=1).astype(BF16)
    kvl = kvl_ref[...]
    ckv_ref[...] = kvl * lax.rsqrt(jnp.mean(kvl * kvl, axis=-1, keepdims=True) + EPS) * gkv_ref[...]
    kr = kpe_ref[...] * tc_ref[...] + kpesw_ref[...] * ts_ref[...]
    kr_ref[...] = kr[:, :MLA_ROPE]


def _mla_prep(proj, t, tq_tab, tc_tab, ts_tab, gq, gkv, wq):
    n = proj.shape[0]
    tm = min(t, 512)
    nt = t // tm
    tab = pl.BlockSpec((tm, LANES), lambda i: (i % nt, 0))
    full2 = lambda a: pl.BlockSpec(a.shape, lambda i: (0, 0))
    return pl.pallas_call(
        _mla_prep_kernel,
        out_shape=(jax.ShapeDtypeStruct((n, HEADS * LANES), BF16),
                   jax.ShapeDtypeStruct((n, MLA_KV_LORA), F32),
                   jax.ShapeDtypeStruct((n, MLA_ROPE), F32)),
        grid=(n // tm,),
        in_specs=[pl.BlockSpec((tm, MLA_Q_LORA), lambda i: (i, O_QLAT // MLA_Q_LORA)),
                  pl.BlockSpec((tm, MLA_KV_LORA), lambda i: (i, O_KVLAT // MLA_KV_LORA)),
                  pl.BlockSpec((tm, LANES), lambda i: (i, O_KPE // LANES)),
                  pl.BlockSpec((tm, LANES), lambda i: (i, O_KPESW // LANES)),
                  tab, tab, tab, full2(gq), full2(gkv), full2(wq)],
        out_specs=(pl.BlockSpec((tm, HEADS * LANES), lambda i: (i, 0)),
                   pl.BlockSpec((tm, MLA_KV_LORA), lambda i: (i, 0)),
                   pl.BlockSpec((tm, MLA_ROPE), lambda i: (i, 0))),
        compiler_params=_cparams(("parallel",), 32),
        name="mla_prep",
    )(proj, proj, proj, proj, tq_tab, tc_tab, ts_tab, gq, gkv, wq)


def _mla_kv_kernel(ckv_ref, kr_ref, wk_ref, wv_ref, dup_ref, k_ref, v_ref):
    cb = ckv_ref[0].astype(BF16)
    kn = _dot(cb, wk_ref[...])
    krd = _dot(kr_ref[0].astype(BF16), dup_ref[...])
    k_ref[...] = jnp.concatenate(
        [kn[:, LANES * h:LANES * (h + 1)] + krd for h in range(HEADS)], axis=1).astype(BF16)
    v_ref[...] = _dot(cb, wv_ref[...]).astype(BF16)


def _mla_kv(ckv, kr, wk, wv, dup, layer=0):
    n = ckv.shape[1]
    tm = next((c for c in (512, LANES) if n % c == 0), n)
    full2 = lambda a: pl.BlockSpec(a.shape, lambda i: (0, 0))
    return pl.pallas_call(
        _mla_kv_kernel,
        out_shape=(jax.ShapeDtypeStruct((n, HEADS * LANES), BF16),
                   jax.ShapeDtypeStruct((n, BRANCH), BF16)),
        grid=(n // tm,),
        in_specs=[pl.BlockSpec((1, tm, MLA_KV_LORA), lambda i: (layer, i, 0)),
                  pl.BlockSpec((1, tm, MLA_ROPE), lambda i: (layer, i, 0)),
                  full2(wk), full2(wv), full2(dup)],
        out_specs=(pl.BlockSpec((tm, HEADS * LANES), lambda i: (i, 0)),
                   pl.BlockSpec((tm, BRANCH), lambda i: (i, 0))),
        compiler_params=_cparams(("parallel",), 32),
        name="mla_kv",
    )(ckv, kr, wk, wv, dup)


def _assemble_keys(past_ref, new_ref, all_ref, past, t):
    all_ref[0:past, :] = past_ref[0].astype(BF16)
    all_ref[past:past + t, :] = new_ref[0].astype(BF16)
    pad = all_ref.shape[0] - past - t
    if pad:
        all_ref[past + t:, :] = jnp.zeros((pad, all_ref.shape[1]), BF16)


def _mla_attn_kernel(q_ref, k_ref, v_ref, *rest, tq, tk, q_off, s_valid, hp, past):
    if past:
        kp_ref, vp_ref, o_ref, kall_ref, vall_ref = rest
        _assemble_keys(kp_ref, k_ref, kall_ref, past, tq)
        _assemble_keys(vp_ref, v_ref, vall_ref, past, tq)
        kload = lambda o: kall_ref[pl.ds(o, tk), :]
        vload = lambda o: vall_ref[pl.ds(o, tk), :]
    else:
        (o_ref,) = rest
        kload = lambda o: k_ref[0, pl.ds(o, tk), :]
        vload = lambda o: v_ref[0, pl.ds(o, tk), :]
    i = pl.program_id(2)
    q_lo = q_off + i * tq
    limit = jnp.minimum(q_lo + tq, s_valid)
    nkb = (limit + tk - 1) // tk
    qidx = q_lo + lax.broadcasted_iota(jnp.int32, (tq, tk), 0)
    kloc = lax.broadcasted_iota(jnp.int32, (tq, tk), 1)
    nh = 2 * hp
    q2 = q_ref[0]
    qs = [q2[:, LANES * h:LANES * (h + 1)] for h in range(nh)]

    def body(kb, carry, masked):
        o = pl.multiple_of(kb * tk, tk)
        kblk = kload(o)
        vblk = vload(o)
        ss = [_dot_nt(qs[h], kblk[:, LANES * h:LANES * (h + 1)]) for h in range(nh)]
        if masked:
            kidx = kloc + kb * tk
            vis = jnp.logical_and((kidx >> 6) <= (qidx >> 6), kidx < s_valid)
            ss = [jnp.where(vis, s, -1e30) for s in ss]
        m_new = [jnp.maximum(carry[h][0], jnp.max(ss[h], axis=-1, keepdims=True)) for h in range(nh)]
        ps = [jnp.exp(ss[h] - m_new[h]) for h in range(nh)]
        pv = [_dot(ps[h].astype(BF16), vblk[:, LANES * (h // 2):LANES * (h // 2 + 1)]) for h in range(nh)]
        new = []
        for h in range(nh):
            m_prev, l_prev, acc = carry[h]
            alpha = jnp.exp(m_prev - m_new[h])
            new.append((m_new[h], alpha * l_prev + jnp.sum(ps[h], axis=-1, keepdims=True), alpha * acc + pv[h]))
        return tuple(new)

    init = tuple((jnp.full((tq, 1), -1e30, F32), jnp.zeros((tq, 1), F32), jnp.zeros((tq, LANES), F32))
                 for _ in range(nh))
    nfull = jnp.minimum(((q_lo >> 6) + 1) * CHUNK, s_valid) // tk
    res = lax.fori_loop(0, nfull, lambda kb, c: body(kb, c, False), init)
    res = lax.fori_loop(nfull, nkb, lambda kb, c: body(kb, c, True), res)
    lane = lax.broadcasted_iota(jnp.int32, (tq, LANES), 1)
    outs = [jnp.where(lane < HEAD, res[2 * p][2] / res[2 * p][1], res[2 * p + 1][2] / res[2 * p + 1][1])
            for p in range(hp)]
    o_ref[0] = jnp.concatenate(outs, axis=1).astype(o_ref.dtype)


def _mla_attn(q3, k3, v3, kp3, vp3, s_pad, tq, tk, hp):
    b, t, _ = q3.shape
    past = 0 if kp3 is None else kp3.shape[1]
    rows = k3.shape[1]
    in_specs = [pl.BlockSpec((1, tq, 2 * LANES * hp), lambda bi, j, i: (bi, i, j)),
                pl.BlockSpec((1, rows, 2 * LANES * hp), lambda bi, j, i: (bi, 0, j)),
                pl.BlockSpec((1, rows, LANES * hp), lambda bi, j, i: (bi, 0, j))]
    args = [q3, k3, v3]
    scratch = []
    if past:
        in_specs += [pl.BlockSpec((1, past, 2 * LANES * hp), lambda bi, j, i: (bi, 0, j)),
                     pl.BlockSpec((1, past, LANES * hp), lambda bi, j, i: (bi, 0, j))]
        args += [kp3, vp3]
        scratch = [pltpu.VMEM((s_pad, 2 * LANES * hp), BF16), pltpu.VMEM((s_pad, LANES * hp), BF16)]
    return pl.pallas_call(
        functools.partial(_mla_attn_kernel, tq=tq, tk=tk, q_off=past, s_valid=past + t, hp=hp, past=past),
        out_shape=jax.ShapeDtypeStruct((b, t, BRANCH), BF16),
        grid=(b, 4 // hp, t // tq),
        in_specs=in_specs,
        out_specs=pl.BlockSpec((1, tq, LANES * hp), lambda bi, j, i: (bi, i, j)),
        scratch_shapes=scratch,
        compiler_params=_cparams(("parallel", "parallel", "arbitrary"), 40),
        name="mla_attn",
    )(*args)


def _sb_block(zs, vis, us, carry, pv):
    nh = len(zs)
    tq = zs[0].shape[0]
    zls = [jnp.minimum(z, 0.0) - jnp.log(1.0 + jnp.exp(-jnp.abs(z))) for z in zs]
    lgs = [zl - z for zl, z in zip(zls, zs)]
    if vis is not None:
        lgs = [jnp.where(vis, lg, 0.0) for lg in lgs]
    his = [lg.astype(BF16) for lg in lgs]
    los = [(lg - hi.astype(F32)).astype(BF16) for lg, hi in zip(lgs, his)]
    later = _dot(jnp.concatenate(his + los, axis=0), us)
    new = []
    for h in range(nh):
        csum, acc = carry[h]
        tot = later[h * tq:(h + 1) * tq] + later[(nh + h) * tq:(nh + h + 1) * tq] + csum
        a = jnp.exp(zls[h] + tot)
        if vis is not None:
            a = jnp.where(vis, a, 0.0)
        new.append((csum + jnp.sum(lgs[h], axis=-1, keepdims=True), acc + pv(h, a.astype(BF16))))
    return tuple(new)


def _sb_heads(q2, nh):
    lane1 = lax.broadcasted_iota(jnp.int32, (1, LANES), 1)
    hm = ((lane1 < HEAD).astype(F32) * SB_SCALE, (lane1 >= HEAD).astype(F32) * SB_SCALE)
    return [(q2[:, LANES * (h // 2):LANES * (h // 2 + 1)] * hm[h % 2]).astype(BF16) for h in range(nh)]


def _sb_finish(res, o_ref, tq):
    lane = lax.broadcasted_iota(jnp.int32, (tq, LANES), 1)
    outs = [jnp.where(lane < HEAD, res[2 * p][1], res[2 * p + 1][1]) for p in range(len(res) // 2)]
    o_ref[0] = jnp.concatenate(outs, axis=1).astype(o_ref.dtype)


def _sb_attn_kernel(q_ref, k_ref, v_ref, us_ref, o_ref, *, tq, tk):
    i = pl.program_id(2)
    q_lo = i * tq
    nkb = (q_lo + tq - 2) // tk + 1
    cidx = lax.broadcasted_iota(jnp.int32, (tk, 2 * tq), 1)
    qidx = q_lo + jnp.where(cidx >= tq, cidx - tq, cidx)
    kloc = lax.broadcasted_iota(jnp.int32, (tk, 2 * tq), 0)
    qs = jnp.concatenate(_sb_heads(q_ref[0], 2), axis=0)
    us = us_ref[...]

    w = 2 * tq

    def group(kb, carry, n):
        csum, acc_t = carry
        o = pl.multiple_of((kb - (n - 1)) * tk, tk)
        kn = k_ref[0, pl.ds(o, n * tk), :].astype(BF16)
        vn_t = v_ref[0, pl.ds(o, n * tk), :].T.astype(BF16)
        z = _dot_nt(kn, qs)
        zl = jnp.minimum(z, 0.0) - jnp.log(1.0 + jnp.exp(-jnp.abs(z)))
        lg = zl - z
        vis = (kloc + kb * tk) < qidx
        lgs = [lg[j * tk:(j + 1) * tk] for j in range(n - 1)] + [jnp.where(vis, lg[(n - 1) * tk:], 0.0)]
        his = [x.astype(BF16) for x in lgs]
        los = [(x - h.astype(F32)).astype(BF16) for x, h in zip(lgs, his)]
        later = _dot(us, jnp.concatenate([jnp.concatenate([his[j], los[j]], axis=0) for j in range(n)], axis=1))
        tots = [None] * n
        for j in reversed(range(n)):
            tots[j] = later[:, j * w:(j + 1) * w] + csum
            csum = csum + jnp.sum(lgs[j], axis=0, keepdims=True)
        a = jnp.exp(zl + jnp.concatenate(tots, axis=0))
        a = jnp.concatenate([a[:(n - 1) * tk], jnp.where(vis, a[(n - 1) * tk:], 0.0)], axis=0) if n > 1 \
            else jnp.where(vis, a, 0.0)
        return csum, acc_t + _dot(vn_t, a.astype(BF16))

    carry = (jnp.zeros((1, 2 * tq), F32), jnp.zeros((LANES, 2 * tq), F32))
    n4, n2, n1 = nkb // 4, (nkb % 4) // 2, nkb % 2
    carry = lax.fori_loop(0, n4, lambda s, c: group(nkb - 1 - 4 * s, c, 4), carry)
    carry = lax.fori_loop(0, n2, lambda s, c: group(nkb - 1 - 4 * n4, c, 2), carry)
    carry = lax.fori_loop(0, n1, lambda s, c: group(0, c, 1), carry)
    acc_t = carry[1]
    row = lax.broadcasted_iota(jnp.int32, (LANES, tq), 0)
    o_ref[0] = jnp.where(row < HEAD, acc_t[:, :tq], acc_t[:, tq:]).T.astype(o_ref.dtype)


def _sb_decode_kernel(q_ref, k_ref, v_ref, kpt_ref, vpt_ref, usn_ref, usp_ref, o_ref, *, t, past, pb):
    qh = _sb_heads(q_ref[0], HEADS)
    kn = k_ref[0].astype(BF16)
    vn = v_ref[0].astype(BF16)
    pair = lambda x, h: x[:, LANES * (h // 2):LANES * (h // 2 + 1)]
    ri = lax.broadcasted_iota(jnp.int32, (t, t), 0)
    ci = lax.broadcasted_iota(jnp.int32, (t, t), 1)
    carry = tuple((jnp.zeros((t, 1), F32), jnp.zeros((t, LANES), F32)) for _ in range(HEADS))
    zs = [_dot_nt(qh[h], pair(kn, h)) for h in range(HEADS)]
    carry = _sb_block(zs, ci < ri, usn_ref[...], carry, lambda h, a: _dot(a, pair(vn, h)))
    usp = usp_ref[...]
    for blk in reversed(range(past // pb)):
        kts = [kpt_ref[0, 0, p, :, blk * pb:(blk + 1) * pb].astype(BF16) for p in range(HEADS // 2)]
        vts = [vpt_ref[0, 0, p, :, blk * pb:(blk + 1) * pb].astype(BF16) for p in range(HEADS // 2)]
        zs = [_dot(qh[h], kts[h // 2]) for h in range(HEADS)]
        carry = _sb_block(zs, None, usp, carry, lambda h, a: _dot_nt(a, vts[h // 2]))
    _sb_finish(carry, o_ref, t)


def _strict_upper(n):
    idx = jnp.arange(n)
    return (idx[:, None] > idx[None, :]).astype(BF16)


def _sb_attn(proj3, tq, tk):
    b, t, _ = proj3.shape
    assert tq == tk
    u = _strict_upper(tk).T
    us = jnp.concatenate([u, u], axis=1)
    return pl.pallas_call(
        functools.partial(_sb_attn_kernel, tq=tq, tk=tk),
        out_shape=jax.ShapeDtypeStruct((b, t, BRANCH), BF16),
        grid=(b, HEADS // 2, t // tq),
        in_specs=[pl.BlockSpec((1, tq, LANES), lambda bi, j, i: (bi, i, O_SBQ // LANES + j)),
                  pl.BlockSpec((1, t, LANES), lambda bi, j, i: (bi, 0, O_SBK // LANES + j)),
                  pl.BlockSpec((1, t, LANES), lambda bi, j, i: (bi, 0, O_SBV // LANES + j)),
                  pl.BlockSpec(us.shape, lambda bi, j, i: (0, 0))],
        out_specs=pl.BlockSpec((1, tq, LANES), lambda bi, j, i: (bi, i, j)),
        compiler_params=_cparams(("parallel", "parallel", "arbitrary"), 48),
        name="sb_attn",
    )(proj3, proj3, proj3, us)


def _sb_decode(proj3, kpt, vpt, layer, pb):
    b, t, _ = proj3.shape
    past = kpt.shape[4]
    usn, usp = _strict_upper(t), _strict_upper(pb)
    new = lambda off: pl.BlockSpec((1, t, BRANCH), lambda bi, o=off // BRANCH: (bi, 0, o))
    cache = pl.BlockSpec((1, 1, HEADS // 2, LANES, past), lambda bi: (layer, bi, 0, 0, 0))
    return pl.pallas_call(
        functools.partial(_sb_decode_kernel, t=t, past=past, pb=pb),
        out_shape=jax.ShapeDtypeStruct((b, t, BRANCH), BF16),
        grid=(b,),
        in_specs=[new(O_SBQ), new(O_SBK), new(O_SBV), cache, cache,
                  pl.BlockSpec(usn.shape, lambda bi: (0, 0)), pl.BlockSpec(usp.shape, lambda bi: (0, 0))],
        out_specs=pl.BlockSpec((1, t, BRANCH), lambda bi: (bi, 0, 0)),
        compiler_params=_cparams(("parallel",), 48),
        name="sb_decode",
    )(proj3, proj3, proj3, kpt, vpt, usn, usp)


def _mix_kernel(yrw_ref, yssm_ref, ymla_ref, ysb_ref, gate_ref, pm_ref, wb_ref, bm_ref, o_ref, acc_ref):
    n = pl.program_id(1)

    @pl.when(n == 0)
    def _():
        acc_ref[...] = jnp.zeros_like(acc_ref)

    def branch(y_ref):
        g = gate_ref[...]
        gated = y_ref[...].astype(F32) * (g * _sigmoid(g))
        up = _dot(gated.astype(BF16), wb_ref[0, 0].astype(BF16))
        acc_ref[...] += _sigmoid(pm_ref[...] + bm_ref[0]) * up

    for idx, y_ref in enumerate((yrw_ref, yssm_ref, ymla_ref, ysb_ref)):
        pl.when(n == idx)(functools.partial(branch, y_ref))

    @pl.when(n == N_BRANCH - 1)
    def _():
        o_ref[...] = acc_ref[...].astype(BF16)


def _out_kernel(m_ref, x_ref, wo_ref, fg_ref, o_ref, *, final):
    xn = x_ref[...] + _dot(m_ref[...], wo_ref[0].astype(BF16))
    if final:
        xn = xn * lax.rsqrt(jnp.mean(xn * xn, axis=-1, keepdims=True) + EPS) * fg_ref[...]
    o_ref[...] = xn


def _merge(ys, proj, x2, wb, bm, wo, fg, final, layer):
    n, d = x2.shape
    tm = min(n, 1024)
    ysp = pl.BlockSpec((tm, BRANCH), lambda i, k: (i, 0))
    mixed = pl.pallas_call(
        _mix_kernel,
        out_shape=jax.ShapeDtypeStruct((n, d), BF16),
        grid=(n // tm, N_BRANCH),
        in_specs=[ysp, ysp, ysp, ysp,
                  pl.BlockSpec((tm, BRANCH), lambda i, k: (i, O_GATE // BRANCH + k)),
                  pl.BlockSpec((tm, d), lambda i, k: (i, O_MERGE // d + k)),
                  pl.BlockSpec((1, 1, BRANCH, d), lambda i, k: (layer, k, 0, 0)),
                  pl.BlockSpec((1, 1, d), lambda i, k: (k, 0, 0))],
        out_specs=pl.BlockSpec((tm, d), lambda i, k: (i, 0)),
        scratch_shapes=[pltpu.VMEM((tm, d), F32)],
        compiler_params=_cparams(("parallel", "arbitrary"), 56),
        name="branch_mix",
    )(*ys, proj, proj, wb, bm)
    to = min(n, 512)
    return pl.pallas_call(
        functools.partial(_out_kernel, final=final),
        out_shape=jax.ShapeDtypeStruct((n, d), F32),
        grid=(n // to,),
        in_specs=[pl.BlockSpec((to, d), lambda i: (i, 0)),
                  pl.BlockSpec((to, d), lambda i: (i, 0)),
                  pl.BlockSpec((1, d, d), lambda i: (layer, 0, 0), pipeline_mode=pl.Buffered(1)),
                  pl.BlockSpec((1, d), lambda i: (0, 0))],
        out_specs=pl.BlockSpec((to, d), lambda i: (i, 0)),
        compiler_params=_cparams(("parallel",), 48),
        name="out_proj",
    )(mixed, x2, wo, fg)


_SRC_SSM = RW_IN
_SRC_QLAT = _SRC_SSM + BRANCH
_SRC_KVLAT = _SRC_QLAT + MLA_Q_LORA
_SRC_KPE = _SRC_KVLAT + MLA_KV_LORA
_SRC_SB = _SRC_KPE + MLA_ROPE
_SRC_GATE = _SRC_SB + 3 * BRANCH
_SRC_MERGE = _SRC_GATE + N_BRANCH * BRANCH
_SRC_LORA = 3 * BRANCH
_PW_BLK = 4 * LANES
_PW_LAST = PROJ_W // _PW_BLK - 1
_PW_TAIL = {_PW_LAST - 1: (_SRC_QLAT, _SRC_QLAT + LANES, _SRC_QLAT + 2 * LANES, _SRC_LORA),
            _PW_LAST: (_SRC_KVLAT, _SRC_KVLAT + LANES, _SRC_KPE, _SRC_KPE)}


def _permute_src_col(j, k):
    nm = N_BRANCH * D_MODEL // _PW_BLK
    ng = nm + N_BRANCH * BRANCH // _PW_BLK
    nsb = ng + 3 * BRANCH // _PW_BLK
    nrw = nsb + 3 * BRANCH // _PW_BLK
    u = MLA_ROPE
    col = jnp.where(j < nm, _SRC_MERGE // u + _PW_BLK // u * j,
                    jnp.where(j < ng, _SRC_GATE // u + _PW_BLK // u * (j - nm),
                              jnp.where(j < nsb, _SRC_SB // u + _PW_BLK // u * (j - ng),
                                        jnp.where(j < nrw, _PW_BLK // u * (j - nsb), _SRC_SSM // u))))
    col = col + LANES // u * k
    for jj, src in _PW_TAIL.items():
        col = jnp.where(j == jj, src[k] // u, col)
    return col * u


def _permute_kernel(x0_ref, x1_ref, x2_ref, x3_ref, o_ref):
    j = pl.program_id(1)
    wins = (x0_ref, x1_ref, x2_ref, x3_ref)
    half = MLA_ROPE // 2

    def put(k, rows):
        o_ref[0, :, LANES * k:LANES * (k + 1)] = rows.T.astype(BF16)

    @pl.when(j < _PW_LAST)
    def _():
        for k in range(4):
            put(k, wins[k][0])

    @pl.when(j == _PW_LAST)
    def _():
        for k in range(2):
            put(k, wins[k][0])
        kpe = wins[2][0]
        row = lax.broadcasted_iota(jnp.int32, kpe.shape, 0)
        put(2, jnp.where(row < MLA_ROPE, kpe, 0.0))
        zeros = jnp.zeros((LANES - MLA_ROPE, kpe.shape[1]), F32)
        put(3, jnp.concatenate([-kpe[half:MLA_ROPE], kpe[:half], zeros], axis=0))


def _permute_w_in(w_in):
    depth, d, _ = w_in.shape
    wt = jnp.swapaxes(w_in, 1, 2)
    win = lambda k: pl.BlockSpec((pl.Element(1), pl.Element(LANES), pl.Element(d)),
                                 lambda l, j, k=k: (l, _permute_src_col(j, k), 0))
    return pl.pallas_call(
        _permute_kernel,
        out_shape=jax.ShapeDtypeStruct((depth, d, PROJ_W), BF16),
        grid=(depth, PROJ_W // _PW_BLK),
        in_specs=[win(k) for k in range(4)],
        out_specs=pl.BlockSpec((1, d, _PW_BLK), lambda l, j: (l, 0, j)),
        compiler_params=_cparams(("parallel", "parallel"), 32),
        name="permute_w_in",
    )(wt, wt, wt, wt)


def _rope_tables(pos):
    half = MLA_ROPE // 2
    inv = ROPE_BASE ** (-jnp.arange(half, dtype=F32) / half)
    ang = pos.astype(F32)[:, None] * inv
    cos, sin = jnp.cos(ang), jnp.sin(ang)
    t = pos.shape[0]
    cc = jnp.concatenate([cos, cos], axis=1)
    ss = jnp.concatenate([sin, sin], axis=1)
    tq = jnp.concatenate([jnp.ones((t, MLA_NOPE), F32), cc, ss], axis=1) * MLA_SCALE
    pad = jnp.zeros((t, LANES - MLA_ROPE), F32)
    return tq, jnp.concatenate([cc, pad], axis=1), jnp.concatenate([ss, pad], axis=1)


def _mla_weights(w_q_up, w_kv_up):
    half = MLA_ROPE // 2
    wq = w_q_up.reshape(MLA_Q_LORA, HEADS, MLA_NOPE + MLA_ROPE)
    x1 = wq[:, :, MLA_NOPE:MLA_NOPE + half]
    x2 = wq[:, :, MLA_NOPE + half:]
    wq_p = jnp.concatenate([wq, -x2, x1], axis=2).reshape(MLA_Q_LORA, HEADS * LANES).astype(BF16)
    wkv = w_kv_up.reshape(MLA_KV_LORA, HEADS, 2 * HEAD)
    wk_p = jnp.concatenate([wkv[:, :, :HEAD], jnp.zeros((MLA_KV_LORA, HEADS, HEAD), F32)], axis=2)
    wk_p = wk_p.reshape(MLA_KV_LORA, HEADS * LANES).astype(BF16)
    wv_p = wkv[:, :, HEAD:].reshape(MLA_KV_LORA, BRANCH).astype(BF16)
    return wq_p, wk_p, wv_p


def _s5_tables(lam_re, lam_im, log_dt, b_re, b_im, c_re, c_im):
    dt = jnp.exp(log_dt)[:, None]
    mag = jnp.exp(lam_re * dt)
    ang = lam_im * dt
    lb_re, lb_im = mag * jnp.cos(ang), mag * jnp.sin(ang)
    nr, ni = lb_re - 1.0, lb_im
    den = lam_re * lam_re + lam_im * lam_im
    f_re = (nr * lam_re + ni * lam_im) / den
    f_im = (ni * lam_re - nr * lam_im) / den
    bb_re = f_re[..., None] * b_re - f_im[..., None] * b_im
    bb_im = f_re[..., None] * b_im + f_im[..., None] * b_re
    eye = jnp.eye(SSM_GROUPS, dtype=F32)
    blk_in = lambda m: jnp.einsum('gpc,gh->gchp', m, eye).reshape(BRANCH, SSM_W)
    blk_out = lambda m: jnp.einsum('gcp,gh->gphc', m, eye).reshape(SSM_W, BRANCH)
    bblk = jnp.concatenate([blk_in(bb_re), blk_in(bb_im)], axis=1).astype(BF16)
    cblk = jnp.concatenate([blk_out(c_re), blk_out(-c_im)], axis=0).astype(BF16)

    def power(j):
        m = jnp.exp(lam_re * dt * j)
        return (m * jnp.cos(ang * j)).reshape(1, SSM_W), (m * jnp.sin(ang * j)).reshape(1, SSM_W)

    lam = jnp.concatenate([p for j in (1, 2, 4) for p in power(j)] + [jnp.zeros((2, SSM_W), F32)], axis=0)
    pws = [power(j) for j in range(1, 9)]
    pw = jnp.concatenate([p[0] for p in pws] + [p[1] for p in pws], axis=0)
    return bblk, cblk, lam, pw


def _layer(x, pos0, st, lw, final_g, final):
    b, t, d = x.shape
    n = b * t
    x2 = x.reshape(n, d)
    proj = _norm_proj(x2, lw["norm_g"], lw["w_in_all"], lw["layer"])
    proj3 = proj.reshape(b, t, PROJ_W)
    past = st["past"]

    chunk = min(t, CHUNK)
    y_rw, wkv_new = _rwkv(proj3, st["sp_r"], st["sp_k"], st["sp_v"], st["sp_l"], st["wkv"],
                        lw["rw_pvec"], lw["rw_mu_l"], lw["rw_w2p"], lw["rw_a2p"], lw["g2"], chunk)
    shift_new = jnp.concatenate([proj3[:, t - 1:, O_RWR:O_RWR + 3 * BRANCH],
                                 proj3[:, t - 1:, O_LORA:O_LORA + 2 * RW_LORA]], axis=-1)

    y_ssm, hr, hi = _s5(proj3, st["ssm_re"], st["ssm_im"], lw["ssm_bblk"], lw["ssm_cblk"], lw["ssm_lam"],
                        lw["ssm_pw"], lw["ssm_d"], lw["ssm_wglu"], lw["ssm_bglu"])
    ssm_re_new = hr.reshape(b, SSM_GROUPS, SSM_STATE)
    ssm_im_new = hi.reshape(b, SSM_GROUPS, SSM_STATE)

    pos = pos0 + jnp.arange(t, dtype=jnp.int32)
    tq_tab, tc_tab, ts_tab = _rope_tables(pos)
    q, ckv, kr = _mla_prep(proj, t, tq_tab, tc_tab, ts_tab, lw["mla_gq"], lw["mla_gkv"], lw["mla_wq"])
    ckv3 = ckv.reshape(b, t, MLA_KV_LORA)
    kr3 = kr.reshape(b, t, MLA_ROPE)
    s_pad = -(-(past + t) // LANES) * LANES
    kc, vv = _mla_kv(ckv[None], kr[None], lw["mla_wk"], lw["mla_wv"], lw["dup"])
    kc3, vv3 = kc.reshape(b, t, HEADS * LANES), vv.reshape(b, t, BRANCH)
    kcp3 = vvp3 = None
    if past:
        kcp, vvp = _mla_kv(st["ckv_all"], st["kpe_all"], lw["mla_wk"], lw["mla_wv"], lw["dup"], lw["layer"])
        kcp3, vvp3 = kcp.reshape(b, past, HEADS * LANES), vvp.reshape(b, past, BRANCH)
    small = t <= 64
    assert small or not past
    tq = min(t, 256)
    hp = 4 if small else 1
    y_mla = _mla_attn(q.reshape(b, t, HEADS * LANES), kc3, vv3, kcp3, vvp3, s_pad, tq,
                      s_pad if small else min(512, s_pad), hp)

    sbk = proj3[:, :, O_SBK:O_SBK + BRANCH]
    sbv = proj3[:, :, O_SBV:O_SBV + BRANCH]
    if past:
        y_sb = _sb_decode(proj3, st["sbk_t"], st["sbv_t"], lw["layer"], min(256, past))
    else:
        y_sb = _sb_attn(proj3, tq, min(256, t))

    ys = [y.reshape(n, BRANCH) for y in (y_rw, y_ssm, y_mla, y_sb)]
    x_new = _merge(ys, proj, x2, lw["w_branch"], lw["b_merge"], lw["w_out"], final_g, final,
                   lw["layer"]).reshape(b, t, d)
    new_state = (shift_new, wkv_new, ssm_re_new, ssm_im_new, ckv3, kr3,
                 sbk.reshape(b, t, HEADS, HEAD), sbv.reshape(b, t, HEADS, HEAD))
    return x_new, new_state


def kernel(x_prompt, x_sample, state_rwkv_shift, state_rwkv_wkv, state_ssm_re, state_ssm_im, cache_mla_ckv, cache_mla_kpe, cache_sb_k, cache_sb_v, norm_g, w_in, rw_mu, rw_w0, rw_w2, rw_a0, rw_a2, rw_k_k, rw_k_a, rw_r_k, rw_lnx_g, rw_lnx_b, ssm_lam_re, ssm_lam_im, ssm_log_dt, ssm_b_re, ssm_b_im, ssm_c_re, ssm_c_im, ssm_d, ssm_w_glu, ssm_b_glu, mla_q_norm, mla_w_q_up, mla_kv_norm, mla_w_kv_up, w_branch, b_merge, w_out, final_norm_g):
    depth = w_in.shape[0]
    bp, tp, _ = x_prompt.shape
    bs = x_sample.shape[0]
    past = cache_mla_ckv.shape[2]

    w_in_p = _permute_w_in(w_in)
    lane_i = jnp.arange(LANES)
    g2 = ((lane_i[:, None] // HEAD) == (lane_i[None, :] // HEAD)).astype(BF16)
    rope_i = jnp.arange(MLA_ROPE)
    dup = ((lane_i[None, :] == rope_i[:, None] + MLA_NOPE)
           | (lane_i[None, :] == rope_i[:, None] + MLA_NOPE + MLA_ROPE)).astype(BF16)
    final_g = final_norm_g.reshape(1, D_MODEL)
    zpad = jnp.zeros((RW_LORA, BRANCH), F32)

    layers = []
    for l in range(depth):
        wq_p, wk_p, wv_p = _mla_weights(mla_w_q_up[l], mla_w_kv_up[l])
        bblk, cblk, lam, pw = _s5_tables(ssm_lam_re[l], ssm_lam_im[l], ssm_log_dt[l], ssm_b_re[l], ssm_b_im[l],
                                         ssm_c_re[l], ssm_c_im[l])
        mu = rw_mu[l]
        rows = [mu[:BRANCH], mu[BRANCH:2 * BRANCH], mu[2 * BRANCH:3 * BRANCH], rw_w0[l], rw_a0[l], rw_k_k[l],
                rw_k_a[l], rw_r_k[l].reshape(BRANCH), rw_lnx_g[l], rw_lnx_b[l]]
        pvec = jnp.concatenate([jnp.stack(rows), jnp.zeros((16 - len(rows), BRANCH), F32)], axis=0)
        layers.append(dict(
            norm_g=norm_g[l].reshape(1, D_MODEL), w_in_all=w_in_p, layer=l,
            rw_pvec=pvec, rw_mu_l=mu[3 * BRANCH:].reshape(1, 2 * RW_LORA),
            rw_w2p=jnp.concatenate([rw_w2[l], zpad], axis=0).astype(BF16),
            rw_a2p=jnp.concatenate([zpad, rw_a2[l]], axis=0).astype(BF16),
            g2=g2, dup=dup,
            ssm_bblk=bblk, ssm_cblk=cblk, ssm_lam=lam, ssm_pw=pw,
            ssm_d=ssm_d[l].reshape(1, BRANCH), ssm_wglu=ssm_w_glu[l].astype(BF16),
            ssm_bglu=ssm_b_glu[l].reshape(1, BRANCH),
            mla_gq=mla_q_norm[l].reshape(1, MLA_Q_LORA), mla_gkv=mla_kv_norm[l].reshape(1, MLA_KV_LORA),
            mla_wq=wq_p, mla_wk=wk_p, mla_wv=wv_p,
            w_branch=w_branch, b_merge=b_merge[l].reshape(N_BRANCH, 1, D_MODEL), w_out=w_out))

    def fresh(bn):
        return dict(sp_r=jnp.zeros((bn, 1, BRANCH), F32), sp_k=jnp.zeros((bn, 1, BRANCH), F32),
                    sp_v=jnp.zeros((bn, 1, BRANCH), F32), sp_l=jnp.zeros((bn, 1, 2 * RW_LORA), F32),
                    wkv=jnp.zeros((bn, HEADS, HEAD, HEAD), F32),
                    ssm_re=jnp.zeros((bn, 1, SSM_W), F32), ssm_im=jnp.zeros((bn, 1, SSM_W), F32),
                    past=0)

    nl = cache_mla_ckv.shape[0]
    ckv_all = cache_mla_ckv.reshape(nl, bs * past, MLA_KV_LORA)
    kpe_all = cache_mla_kpe.reshape(nl, bs * past, MLA_ROPE)
    sbk_t = jnp.transpose(cache_sb_k, (0, 1, 3, 4, 2)).reshape(nl, bs, HEADS // 2, LANES, past)
    sbv_t = jnp.transpose(cache_sb_v, (0, 1, 3, 4, 2)).reshape(nl, bs, HEADS // 2, LANES, past)

    def carried(l):
        sh = state_rwkv_shift[l]
        return dict(sp_r=sh[:, :, :BRANCH], sp_k=sh[:, :, BRANCH:2 * BRANCH], sp_v=sh[:, :, 2 * BRANCH:3 * BRANCH],
                    sp_l=sh[:, :, 3 * BRANCH:], wkv=state_rwkv_wkv[l],
                    ssm_re=state_ssm_re[l].reshape(bs, 1, SSM_W), ssm_im=state_ssm_im[l].reshape(bs, 1, SSM_W),
                    past=past, ckv_all=ckv_all, kpe_all=kpe_all, sbk_t=sbk_t, sbv_t=sbv_t)

    xp, xs = x_prompt, x_sample
    new_p, new_s = [], []
    for l in range(depth):
        last = l == depth - 1
        xp, st_p = _layer(xp, 0, fresh(bp), layers[l], final_g, last)
        xs, st_s = _layer(xs, past, carried(l), layers[l], final_g, last)
        new_p.append(st_p)
        new_s.append(st_s)
    stk = lambda lst, i: jnp.stack([s[i] for s in lst], axis=0)
    return (xp, xs) + tuple(stk(new_p, i) for i in range(8)) + tuple(stk(new_s, i) for i in range(8))
```

```python
import functools
import math

import jax
import jax.numpy as jnp
from jax import lax
from jax.experimental import pallas as pl
from jax.experimental.pallas import tpu as pltpu

F32 = jnp.float32
BF16 = jnp.bfloat16

D_MODEL = 2048
BRANCH = 512
N_BRANCH = 4
EPS = 1e-6
CHUNK = 64
HEAD = 64
HEADS = 8
RW_LORA = 64
RW_IN = 3 * BRANCH + 2 * RW_LORA
RW_GN_EPS = 64e-5
SSM_GROUP = 16
SSM_GROUPS = 32
SSM_STATE = 64
SSM_W = SSM_GROUPS * SSM_STATE
MLA_NOPE = 64
MLA_ROPE = 32
MLA_Q_LORA = 384
MLA_KV_LORA = 256
MLA_SCALE = 1.0 / math.sqrt(MLA_NOPE + MLA_ROPE)
ROPE_BASE = 10000.0
SB_SCALE = 1.0 / math.sqrt(HEAD)
LANES = 128
MIB = 1024 * 1024

O_MERGE = 0
O_GATE = 8192
O_SBQ = 10240
O_SBK = 10752
O_SBV = 11264
O_RWR = 11776
O_RWK = 12288
O_RWV = 12800
O_SSM = 13312
O_QLAT = 13824
O_LORA = 14208
O_KVLAT = 14336
O_KPE = 14592
O_KPESW = 14720
PROJ_W = 14848


def _cparams(sem, vmem_mib):
    return pltpu.CompilerParams(dimension_semantics=sem, vmem_limit_bytes=vmem_mib * MIB)


def _dot(a, b):
    return jnp.dot(a, b, preferred_element_type=F32)


def _dot_nt(a, b):
    return lax.dot_general(a, b, (((1,), (1,)), ((), ())), preferred_element_type=F32)


def _split3(x):
    h1 = x.astype(BF16)
    r1 = x - h1.astype(F32)
    h2 = r1.astype(BF16)
    h3 = (r1 - h2.astype(F32)).astype(BF16)
    return h1, h2, h3


def _softplus(x):
    return jnp.maximum(x, 0.0) + jnp.log1p(jnp.exp(-jnp.abs(x)))


def _sigmoid(x):
    return 0.5 * jnp.tanh(0.5 * x) + 0.5


def _norm_proj_kernel(x_ref, g_ref, w_ref, o_ref, h_ref):
    @pl.when(pl.program_id(1) == 0)
    def _():
        x = x_ref[...]
        ms = jnp.mean(x * x, axis=-1, keepdims=True)
        h_ref[...] = (x * lax.rsqrt(ms + EPS) * g_ref[...]).astype(BF16)

    o_ref[...] = _dot(h_ref[...], w_ref[0])


def _norm_proj(x2, g, w_all, layer):
    n, d = x2.shape
    width = w_all.shape[2]
    tm = min(n, 1024)
    tn = 512
    return pl.pallas_call(
        _norm_proj_kernel,
        out_shape=jax.ShapeDtypeStruct((n, width), F32),
        grid=(n // tm, width // tn),
        in_specs=[
            pl.BlockSpec((tm, d), lambda i, j: (i, 0)),
            pl.BlockSpec((1, d), lambda i, j: (0, 0)),
            pl.BlockSpec((1, d, tn), lambda i, j: (layer, 0, j)),
        ],
        out_specs=pl.BlockSpec((tm, tn), lambda i, j: (i, j)),
        scratch_shapes=[pltpu.VMEM((tm, d), BF16)],
        compiler_params=_cparams(("parallel", "arbitrary"), 40),
        name="norm_proj",
    )(x2, g, w_all)


def _segsum(x, g2):
    outs = []
    for j in range(BRANCH // LANES):
        h1, h2, h3 = _split3(x[:, LANES * j:LANES * (j + 1)])
        outs.append(_dot(h1, g2) + _dot(h2, g2) + _dot(h3, g2))
    return jnp.concatenate(outs, axis=1)


def _rwkv_kernel(pr_ref, pk_ref, pv_ref, plo_ref, spr_ref, spk_ref, spv_ref, spl_ref, s0_ref,
                 pvec_ref, mul_ref, w2_ref, a2_ref, g2_ref,
                 y_ref, sout_ref,
                 st_ref, cr_ref, ck_ref, cv_ref, cl_ref, *, C, nb):
    c = pl.program_id(1)

    @pl.when(c == 0)
    def _():
        zero = jnp.zeros((HEAD, HEAD), F32)
        for bi in range(nb):
            for j in range(HEADS // 2):
                st_ref[bi, j] = jnp.concatenate(
                    [jnp.concatenate([s0_ref[bi, 2 * j], zero], axis=1),
                     jnp.concatenate([zero, s0_ref[bi, 2 * j + 1]], axis=1)], axis=0)
        cr_ref[...] = spr_ref[...]
        ck_ref[...] = spk_ref[...]
        cv_ref[...] = spv_ref[...]
        cl_ref[...] = spl_ref[...]

    pvec = pvec_ref[...]
    prow = lambda i: pvec[i:i + 1, :]
    g2 = g2_ref[...]

    def tshift(p_ref, carry_ref, mu):
        outs = []
        for bi in range(nb):
            p = p_ref[bi]
            rolled = pltpu.roll(p, 1, 0)
            rid = lax.broadcasted_iota(jnp.int32, p.shape, 0)
            prev = jnp.where(rid == 0, carry_ref[bi], rolled)
            carry_ref[bi] = p[C - 1:C, :]
            outs.append(p + (prev - p) * mu)
        return jnp.concatenate(outs, axis=0)

    r = tshift(pr_ref, cr_ref, prow(0))
    k = tshift(pk_ref, ck_ref, prow(1))
    v = tshift(pv_ref, cv_ref, prow(2))
    lo = tshift(plo_ref, cl_ref, mul_ref[...])

    log_w = -_softplus(-(prow(3) + _dot(jnp.tanh(lo).astype(BF16), w2_ref[...]))) - 0.5
    ld = -jnp.exp(log_w)
    a_icl = jax.nn.sigmoid(prow(4) + _dot(lo.astype(BF16), a2_ref[...]))
    kkr = k * prow(5)
    kk = kkr / jnp.maximum(jnp.sqrt(_segsum(kkr * kkr, g2)), 1e-12)
    k2 = k * (1.0 + (a_icl - 1.0) * prow(6))
    av = -kk
    bv = kk * a_icl

    sh = C.bit_length() - 1
    ri = lax.broadcasted_iota(jnp.int32, (nb * C, nb * C), 0)
    ci = lax.broadcasted_iota(jnp.int32, (nb * C, nb * C), 1)
    lincl = jnp.logical_and(ci <= ri, (ci >> sh) == (ri >> sh)).astype(BF16)
    h1, h2, h3 = _split3(ld)
    g = _dot(lincl, h1) + _dot(lincl, h2) + _dot(lincl, h3)
    g_lasts = [g[(bi + 1) * C - 1:(bi + 1) * C, :] for bi in range(nb)]
    g_last = jnp.concatenate([jnp.broadcast_to(gl, (C, BRANCH)) for gl in g_lasts], axis=0)
    eg = jnp.exp(g)
    eng = jnp.exp(-g)
    at = av * jnp.exp(g - ld)
    rt = r * eg
    bt = bv * eng
    kt = k2 * eng
    e_c = jnp.exp(g_last - g)
    b_end = bv * e_c
    k_end = k2 * e_c
    eg_lasts = [jnp.exp(gl) for gl in g_lasts]

    c2 = 2 * C
    npair = BRANCH // LANES
    r_i = lax.broadcasted_iota(jnp.int32, (c2, c2), 0)
    c_i = lax.broadcasted_iota(jnp.int32, (c2, c2), 1)
    c_m = jnp.where(c_i >= C, c_i - C, c_i)
    mhalf = c_m < jnp.where(r_i < C, r_i, r_i - C + 1)
    mfull = jnp.concatenate([mhalf, mhalf], axis=0)
    bd2 = (r_i >> sh) == (c_i >> sh)
    eye2 = (r_i == c_i).astype(F32)
    lane = lax.broadcasted_iota(jnp.int32, (1, LANES), 1)
    m0 = (lane < HEAD).astype(F32)
    m1 = (lane >= HEAD).astype(F32)
    v_i = lax.broadcasted_iota(jnp.int32, (LANES, LANES), 0)
    k_i = lax.broadcasted_iota(jnp.int32, (LANES, LANES), 1)
    bdmask = (v_i >> 6) == (k_i >> 6)
    zf = jnp.zeros((C, LANES), F32)
    zc = jnp.zeros((C, LANES), BF16)
    sls = [slice(LANES * j, LANES * (j + 1)) for j in range(npair)]

    units = [(bi, j) for bi in range(nb) for j in range(npair)]
    tile = lambda x, bi, j: x[bi * C:(bi + 1) * C, LANES * j:LANES * (j + 1)]
    ws, a_ss, s2s = [], [], []
    for bi, j in units:
        atj, rtj = tile(at, bi, j), tile(rt, bi, j)
        a0, r0, a1, r1 = atj * m0, rtj * m0, atj * m1, rtj * m1
        lhs_w = jnp.concatenate([jnp.concatenate([a0, zf], axis=1), jnp.concatenate([r0, zf], axis=1),
                                 jnp.concatenate([zf, a1], axis=1), jnp.concatenate([zf, r1], axis=1)],
                                axis=0).astype(BF16)
        btj, ktj = tile(bt, bi, j), tile(kt, bi, j)
        rhs_w = jnp.concatenate([jnp.concatenate([btj, ktj], axis=0),
                                 jnp.concatenate([ktj, btj], axis=0)], axis=1).astype(BF16)
        ws.append(jnp.where(mfull, _dot_nt(lhs_w, rhs_w), 0.0))
        lhs_s = jnp.concatenate([a0, r0, a1, r1], axis=0).astype(BF16)
        s2 = st_ref[bi, j]
        s2h, s2l, _ = _split3(s2)
        a_ss.append(_dot_nt(lhs_s, s2h) + _dot_nt(lhs_s, s2l))
        s2s.append(s2)

    n_pows = [jnp.where(bd2, jnp.concatenate([w[:C], w[c2:c2 + C]], axis=0), 0.0) for w in ws]
    t_mats = [eye2 + n for n in n_pows]
    lv = 2
    while lv < C:
        n_pows = [_dot(n.astype(BF16), n.astype(BF16)) for n in n_pows]
        t_mats = [t + _dot(t.astype(BF16), n.astype(BF16)) for t, n in zip(t_mats, n_pows)]
        lv *= 2

    wbs = [w.astype(BF16) for w in ws]
    v0s = [(tile(v, bi, j) * m0).astype(BF16) for bi, j in units]
    v1s = [(tile(v, bi, j) * m1).astype(BF16) for bi, j in units]
    xs = [jnp.concatenate([a_s[:C] + _dot(wb[:C], jnp.concatenate([zc, v0], axis=0)),
                           a_s[c2:c2 + C] + _dot(wb[c2:c2 + C], jnp.concatenate([v1, zc], axis=0))], axis=0)
          for a_s, wb, v0, v1 in zip(a_ss, wbs, v0s, v1s)]
    us = [_dot(t.astype(BF16), x.astype(BF16)) for t, x in zip(t_mats, xs)]
    ys = []
    for n, (bi, j) in enumerate(units):
        a_s, wb, u = a_ss[n], wbs[n], us[n]
        u0, u1 = u[:C], u[C:]
        y0 = a_s[C:c2] + _dot(wb[C:c2], jnp.concatenate([u0.astype(BF16), v0s[n]], axis=0))
        y1 = a_s[c2 + C:] + _dot(wb[c2 + C:], jnp.concatenate([v1s[n], u1.astype(BF16)], axis=0))
        ys.append(y0 + y1)
        uv = jnp.concatenate([u0 + u1, tile(v, bi, j)], axis=0)
        bk = jnp.concatenate([tile(b_end, bi, j), tile(k_end, bi, j)], axis=0).astype(BF16)
        upd = _dot(uv.T.astype(BF16), bk)
        st_ref[bi, j] = s2s[n] * eg_lasts[bi][:, sls[j]] + jnp.where(bdmask, upd, 0.0)
    y = jnp.concatenate([jnp.concatenate(ys[bi * npair:(bi + 1) * npair], axis=1) for bi in range(nb)],
                        axis=0)

    inv_n = 1.0 / HEAD
    yc = y - _segsum(y, g2) * inv_n
    yn = yc * lax.rsqrt(_segsum(yc * yc, g2) * inv_n + RW_GN_EPS) * prow(8) + prow(9)
    out = (yn + _segsum(r * k2 * prow(7), g2) * v).astype(y_ref.dtype)
    for bi in range(nb):
        y_ref[bi] = out[bi * C:(bi + 1) * C]

    @pl.when(c == pl.num_programs(1) - 1)
    def _():
        for bi in range(nb):
            for j in range(HEADS // 2):
                s2 = st_ref[bi, j]
                sout_ref[bi, 2 * j] = s2[:HEAD, :HEAD]
                sout_ref[bi, 2 * j + 1] = s2[HEAD:, HEAD:]


def _rwkv(proj3, sp_r, sp_k, sp_v, sp_l, s0, pvec, mu_l, w2p, a2p, g2, chunk):
    b, t, _ = proj3.shape
    nc = t // chunk
    nb = min(b, 4)
    cb = lambda off, wdt: off // wdt
    bspec = lambda off, wdt: pl.BlockSpec((nb, chunk, wdt), lambda i, c, o=cb(off, wdt): (i, c, o))
    row3 = lambda wdt: pl.BlockSpec((nb, 1, wdt), lambda i, c: (i, 0, 0))
    full2 = lambda a: pl.BlockSpec(a.shape, lambda i, c: (0, 0))
    return pl.pallas_call(
        functools.partial(_rwkv_kernel, C=chunk, nb=nb),
        out_shape=(jax.ShapeDtypeStruct((b, t, BRANCH), BF16),
                   jax.ShapeDtypeStruct((b, HEADS, HEAD, HEAD), F32)),
        grid=(b // nb, nc),
        in_specs=[
            bspec(O_RWR, BRANCH), bspec(O_RWK, BRANCH), bspec(O_RWV, BRANCH), bspec(O_LORA, LANES),
            row3(BRANCH), row3(BRANCH), row3(BRANCH), row3(LANES),
            pl.BlockSpec((nb, HEADS, HEAD, HEAD), lambda i, c: (i, 0, 0, 0)),
            full2(pvec), full2(mu_l), full2(w2p), full2(a2p), full2(g2),
        ],
        out_specs=(pl.BlockSpec((nb, chunk, BRANCH), lambda i, c: (i, c, 0)),
                   pl.BlockSpec((nb, HEADS, HEAD, HEAD), lambda i, c: (i, 0, 0, 0))),
        scratch_shapes=[pltpu.VMEM((nb, 4, LANES, LANES), F32),
                        pltpu.VMEM((nb, 1, BRANCH), F32), pltpu.VMEM((nb, 1, BRANCH), F32),
                        pltpu.VMEM((nb, 1, BRANCH), F32), pltpu.VMEM((nb, 1, LANES), F32)],
        compiler_params=_cparams(("parallel", "arbitrary"), 32),
        name="rwkv7",
    )(proj3, proj3, proj3, proj3, sp_r, sp_k, sp_v, sp_l, s0, pvec, mu_l, w2p, a2p, g2)


def _s5_kernel(u_ref, h0r_ref, h0i_ref, bblk_ref, cblk_ref, lam_ref, pw_ref, dsk_ref, wglu_ref, bglu_ref,
               y_ref, hro_ref, hio_ref, x_ref, cr_ref, ci_ref, *, tb):
    @pl.when(pl.program_id(1) == 0)
    def _():
        cr_ref[...] = h0r_ref[0]
        ci_ref[...] = h0i_ref[0]

    u = u_ref[0]
    x = _dot(u.astype(BF16), bblk_ref[...])
    xr = x[:, :SSM_W]
    xi = x[:, SSM_W:]
    for lvl, s in enumerate((1, 2, 4)):
        ar = lam_ref[2 * lvl]
        ai = lam_ref[2 * lvl + 1]
        sr = pltpu.roll(xr, s, 0)
        si = pltpu.roll(xi, s, 0)
        xr, xi = xr + (ar * sr - ai * si), xi + (ar * si + ai * sr)
    x_ref[:, :SSM_W] = xr
    x_ref[:, SSM_W:] = xi
    pwr = pw_ref[0:8, :]
    pwi = pw_ref[8:16, :]

    def body(gi, carry):
        cr, ci = carry
        o = pl.multiple_of(gi * 8, 8)
        br = x_ref[pl.ds(o, 8), 0:SSM_W]
        bi = x_ref[pl.ds(o, 8), SSM_W:2 * SSM_W]
        br = br + pwr * cr - pwi * ci
        bi = bi + pwr * ci + pwi * cr
        x_ref[pl.ds(o, 8), 0:SSM_W] = br
        x_ref[pl.ds(o, 8), SSM_W:2 * SSM_W] = bi
        return br[7:8, :], bi[7:8, :]

    cr, ci = lax.fori_loop(0, tb // 8, body, (cr_ref[...], ci_ref[...]))
    cr_ref[...] = cr
    ci_ref[...] = ci
    hro_ref[0] = cr
    hio_ref[0] = ci
    y = _dot(x_ref[...].astype(BF16), cblk_ref[...]) + dsk_ref[...] * u
    g = jax.nn.gelu(y)
    out = g * jax.nn.sigmoid(_dot(g.astype(BF16), wglu_ref[...]) + bglu_ref[...])
    y_ref[0] = out.astype(y_ref.dtype)


def _s5(proj3, h0r, h0i, bblk, cblk, lam, pw, dsk, wglu, bglu):
    b, t, _ = proj3.shape
    tb = min(t, 128)
    sub = jnp.arange(tb)[:, None] & 7
    lam = jnp.stack([jnp.where(sub >= s, lam[2 * lv + c:2 * lv + c + 1], 0.0)
                     for lv, s in enumerate((1, 2, 4)) for c in (0, 1)])
    full2 = lambda a: pl.BlockSpec(a.shape, lambda i, c: (0,) * a.ndim)
    row3 = pl.BlockSpec((1, 1, SSM_W), lambda i, c: (i, 0, 0))
    return pl.pallas_call(
        functools.partial(_s5_kernel, tb=tb),
        out_shape=(jax.ShapeDtypeStruct((b, t, BRANCH), BF16),
                   jax.ShapeDtypeStruct((b, 1, SSM_W), F32),
                   jax.ShapeDtypeStruct((b, 1, SSM_W), F32)),
        grid=(b, t // tb),
        in_specs=[pl.BlockSpec((1, tb, BRANCH), lambda i, c: (i, c, O_SSM // BRANCH)),
                  row3, row3, full2(bblk), full2(cblk), full2(lam), full2(pw), full2(dsk),
                  full2(wglu), full2(bglu)],
        out_specs=(pl.BlockSpec((1, tb, BRANCH), lambda i, c: (i, c, 0)), row3, row3),
        scratch_shapes=[pltpu.VMEM((tb, 2 * SSM_W), F32),
                        pltpu.VMEM((1, SSM_W), F32), pltpu.VMEM((1, SSM_W), F32)],
        compiler_params=_cparams(("parallel", "arbitrary"), 48),
        name="s5",
    )(proj3, h0r, h0i, bblk, cblk, lam, pw, dsk, wglu, bglu)


def _mla_prep_kernel(ql_ref, kvl_ref, kpe_ref, kpesw_ref, tq_ref, tc_ref, ts_ref,
                     gq_ref, gkv_ref, wq_ref, q_ref, ckv_ref, kr_ref):
    ql = ql_ref[...]
    qn = ql * lax.rsqrt(jnp.mean(ql * ql, axis=-1, keepdims=True) + EPS) * gq_ref[...]
    q = _dot(qn.astype(BF16), wq_ref[...])
    tq = tq_ref[...]
    q_ref[...] = jnp.concatenate(
        [q[:, LANES * h:LANES * (h + 1)] * tq for h in range(HEADS)], axis=1).astype(BF16)
    kvl = kvl_ref[...]
    ckv_ref[...] = kvl * lax.rsqrt(jnp.mean(kvl * kvl, axis=-1, keepdims=True) + EPS) * gkv_ref[...]
    kr = kpe_ref[...] * tc_ref[...] + kpesw_ref[...] * ts_ref[...]
    kr_ref[...] = kr[:, :MLA_ROPE]


def _mla_prep(proj, t, tq_tab, tc_tab, ts_tab, gq, gkv, wq):
    n = proj.shape[0]
    tm = min(t, 512)
    nt = t // tm
    tab = pl.BlockSpec((tm, LANES), lambda i: (i % nt, 0))
    full2 = lambda a: pl.BlockSpec(a.shape, lambda i: (0, 0))
    return pl.pallas_call(
        _mla_prep_kernel,
        out_shape=(jax.ShapeDtypeStruct((n, HEADS * LANES), BF16),
                   jax.ShapeDtypeStruct((n, MLA_KV_LORA), F32),
                   jax.ShapeDtypeStruct((n, MLA_ROPE), F32)),
        grid=(n // tm,),
        in_specs=[pl.BlockSpec((tm, MLA_Q_LORA), lambda i: (i, O_QLAT // MLA_Q_LORA)),
                  pl.BlockSpec((tm, MLA_KV_LORA), lambda i: (i, O_KVLAT // MLA_KV_LORA)),
                  pl.BlockSpec((tm, LANES), lambda i: (i, O_KPE // LANES)),
                  pl.BlockSpec((tm, LANES), lambda i: (i, O_KPESW // LANES)),
                  tab, tab, tab, full2(gq), full2(gkv), full2(wq)],
        out_specs=(pl.BlockSpec((tm, HEADS * LANES), lambda i: (i, 0)),
                   pl.BlockSpec((tm, MLA_KV_LORA), lambda i: (i, 0)),
                   pl.BlockSpec((tm, MLA_ROPE), lambda i: (i, 0))),
        compiler_params=_cparams(("parallel",), 32),
        name="mla_prep",
    )(proj, proj, proj, proj, tq_tab, tc_tab, ts_tab, gq, gkv, wq)


def _mla_kv_kernel(ckv_ref, kr_ref, wk_ref, wv_ref, dup_ref, k_ref, v_ref):
    cb = ckv_ref[0].astype(BF16)
    kn = _dot(cb, wk_ref[...])
    krd = _dot(kr_ref[0].astype(BF16), dup_ref[...])
    k_ref[...] = jnp.concatenate(
        [kn[:, LANES * h:LANES * (h + 1)] + krd for h in range(HEADS)], axis=1).astype(BF16)
    v_ref[...] = _dot(cb, wv_ref[...]).astype(BF16)


def _mla_kv(ckv, kr, wk, wv, dup, layer=0):
    n = ckv.shape[1]
    tm = next((c for c in (512, LANES) if n % c == 0), n)
    full2 = lambda a: pl.BlockSpec(a.shape, lambda i: (0, 0))
    return pl.pallas_call(
        _mla_kv_kernel,
        out_shape=(jax.ShapeDtypeStruct((n, HEADS * LANES), BF16),
                   jax.ShapeDtypeStruct((n, BRANCH), BF16)),
        grid=(n // tm,),
        in_specs=[pl.BlockSpec((1, tm, MLA_KV_LORA), lambda i: (layer, i, 0)),
                  pl.BlockSpec((1, tm, MLA_ROPE), lambda i: (layer, i, 0)),
                  full2(wk), full2(wv), full2(dup)],
        out_specs=(pl.BlockSpec((tm, HEADS * LANES), lambda i: (i, 0)),
                   pl.BlockSpec((tm, BRANCH), lambda i: (i, 0))),
        compiler_params=_cparams(("parallel",), 32),
        name="mla_kv",
    )(ckv, kr, wk, wv, dup)


def _assemble_keys(past_ref, new_ref, all_ref, past, t):
    all_ref[0:past, :] = past_ref[0].astype(BF16)
    all_ref[past:past + t, :] = new_ref[0].astype(BF16)
    pad = all_ref.shape[0] - past - t
    if pad:
        all_ref[past + t:, :] = jnp.zeros((pad, all_ref.shape[1]), BF16)


def _mla_attn_kernel(q_ref, k_ref, v_ref, *rest, tq, tk, q_off, s_valid, hp, past):
    if past:
        kp_ref, vp_ref, o_ref, kall_ref, vall_ref = rest
        _assemble_keys(kp_ref, k_ref, kall_ref, past, tq)
        _assemble_keys(vp_ref, v_ref, vall_ref, past, tq)
        kload = lambda o: kall_ref[pl.ds(o, tk), :]
        vload = lambda o: vall_ref[pl.ds(o, tk), :]
    else:
        (o_ref,) = rest
        kload = lambda o: k_ref[0, pl.ds(o, tk), :]
        vload = lambda o: v_ref[0, pl.ds(o, tk), :]
    i = pl.program_id(2)
    q_lo = q_off + i * tq
    limit = jnp.minimum(q_lo + tq, s_valid)
    nkb = (limit + tk - 1) // tk
    qidx = q_lo + lax.broadcasted_iota(jnp.int32, (tq, tk), 0)
    kloc = lax.broadcasted_iota(jnp.int32, (tq, tk), 1)
    nh = 2 * hp
    q2 = q_ref[0]
    qs = [q2[:, LANES * h:LANES * (h + 1)] for h in range(nh)]

    def body(kb, carry, masked):
        o = pl.multiple_of(kb * tk, tk)
        kblk = kload(o)
        vblk = vload(o)
        ss = [_dot_nt(qs[h], kblk[:, LANES * h:LANES * (h + 1)]) for h in range(nh)]
        if masked:
            kidx = kloc + kb * tk
            vis = jnp.logical_and((kidx >> 6) <= (qidx >> 6), kidx < s_valid)
            ss = [jnp.where(vis, s, -1e30) for s in ss]
        m_new = [jnp.maximum(carry[h][0], jnp.max(ss[h], axis=-1, keepdims=True)) for h in range(nh)]
        ps = [jnp.exp(ss[h] - m_new[h]) for h in range(nh)]
        pv = [_dot(ps[h].astype(BF16), vblk[:, LANES * (h // 2):LANES * (h // 2 + 1)]) for h in range(nh)]
        new = []
        for h in range(nh):
            m_prev, l_prev, acc = carry[h]
            alpha = jnp.exp(m_prev - m_new[h])
            new.append((m_new[h], alpha * l_prev + jnp.sum(ps[h], axis=-1, keepdims=True), alpha * acc + pv[h]))
        return tuple(new)

    init = tuple((jnp.full((tq, 1), -1e30, F32), jnp.zeros((tq, 1), F32), jnp.zeros((tq, LANES), F32))
                 for _ in range(nh))
    nfull = jnp.minimum(((q_lo >> 6) + 1) * CHUNK, s_valid) // tk
    res = lax.fori_loop(0, nfull, lambda kb, c: body(kb, c, False), init)
    res = lax.fori_loop(nfull, nkb, lambda kb, c: body(kb, c, True), res)
    lane = lax.broadcasted_iota(jnp.int32, (tq, LANES), 1)
    outs = [jnp.where(lane < HEAD, res[2 * p][2] / res[2 * p][1], res[2 * p + 1][2] / res[2 * p + 1][1])
            for p in range(hp)]
    o_ref[0] = jnp.concatenate(outs, axis=1).astype(o_ref.dtype)


def _mla_prompt_kernel(q_ref, k_ref, v_ref, o_ref, *, tq):
    i = pl.program_id(2)
    q_lo = i * tq
    nkb = i + 1
    w = 2 * tq
    cidx = lax.broadcasted_iota(jnp.int32, (tq, w), 1)
    qidx = q_lo + jnp.where(cidx >= tq, cidx - tq, cidx)
    kloc = lax.broadcasted_iota(jnp.int32, (tq, w), 0)
    q2 = q_ref[0]
    zq = jnp.zeros((tq, LANES), BF16)
    qbd = jnp.concatenate([jnp.concatenate([q2[:, :LANES], zq], axis=1),
                           jnp.concatenate([zq, q2[:, LANES:]], axis=1)], axis=0)

    def group(kb, carry, n):
        m_prev, l_prev, acc_t = carry
        o = pl.multiple_of((kb - (n - 1)) * tq, tq)
        kn = k_ref[0, pl.ds(o, n * tq), :]
        vn_t = v_ref[0, pl.ds(o, n * tq), :].astype(F32).T.astype(BF16)
        s = _dot_nt(kn, qbd)
        vis = ((kloc + kb * tq) >> 6) <= (qidx >> 6)
        s_r = jnp.where(vis, s[(n - 1) * tq:], -1e30)
        s = jnp.concatenate([s[:(n - 1) * tq], s_r], axis=0) if n > 1 else s_r
        m_new = jnp.maximum(m_prev, jnp.max(s, axis=0, keepdims=True))
        alpha = jnp.exp(m_prev - m_new)
        p = jnp.exp(s - m_new)
        l_new = alpha * l_prev + jnp.sum(p, axis=0, keepdims=True)
        return m_new, l_new, alpha * acc_t + _dot(vn_t, p.astype(BF16))

    carry = (jnp.full((1, w), -1e30, F32), jnp.zeros((1, w), F32), jnp.zeros((LANES, w), F32))
    n4, n2, n1 = nkb // 4, (nkb % 4) // 2, nkb % 2
    carry = lax.fori_loop(0, n4, lambda s, c: group(nkb - 1 - 4 * s, c, 4), carry)
    carry = lax.fori_loop(0, n2, lambda s, c: group(nkb - 1 - 4 * n4, c, 2), carry)
    carry = lax.fori_loop(0, n1, lambda s, c: group(0, c, 1), carry)
    out_t = carry[2] / carry[1]
    row = lax.broadcasted_iota(jnp.int32, (LANES, tq), 0)
    o_ref[0] = jnp.where(row < HEAD, out_t[:, :tq], out_t[:, tq:]).T.astype(o_ref.dtype)


def _mla_prompt(q3, k3, v3, tq):
    b, t, _ = q3.shape
    assert tq % CHUNK == 0 and t % tq == 0
    return pl.pallas_call(
        functools.partial(_mla_prompt_kernel, tq=tq),
        out_shape=jax.ShapeDtypeStruct((b, t, BRANCH), BF16),
        grid=(b, HEADS // 2, t // tq),
        in_specs=[pl.BlockSpec((1, tq, 2 * LANES), lambda bi, j, i: (bi, i, j)),
                  pl.BlockSpec((1, t, 2 * LANES), lambda bi, j, i: (bi, 0, j)),
                  pl.BlockSpec((1, t, LANES), lambda bi, j, i: (bi, 0, j))],
        out_specs=pl.BlockSpec((1, tq, LANES), lambda bi, j, i: (bi, i, j)),
        compiler_params=_cparams(("parallel", "parallel", "arbitrary"), 40),
        name="mla_prompt",
    )(q3, k3, v3)


def _mla_attn(q3, k3, v3, kp3, vp3, s_pad, tq, tk, hp):
    b, t, _ = q3.shape
    past = 0 if kp3 is None else kp3.shape[1]
    rows = k3.shape[1]
    in_specs = [pl.BlockSpec((1, tq, 2 * LANES * hp), lambda bi, j, i: (bi, i, j)),
                pl.BlockSpec((1, rows, 2 * LANES * hp), lambda bi, j, i: (bi, 0, j)),
                pl.BlockSpec((1, rows, LANES * hp), lambda bi, j, i: (bi, 0, j))]
    args = [q3, k3, v3]
    scratch = []
    if past:
        in_specs += [pl.BlockSpec((1, past, 2 * LANES * hp), lambda bi, j, i: (bi, 0, j)),
                     pl.BlockSpec((1, past, LANES * hp), lambda bi, j, i: (bi, 0, j))]
        args += [kp3, vp3]
        scratch = [pltpu.VMEM((s_pad, 2 * LANES * hp), BF16), pltpu.VMEM((s_pad, LANES * hp), BF16)]
    return pl.pallas_call(
        functools.partial(_mla_attn_kernel, tq=tq, tk=tk, q_off=past, s_valid=past + t, hp=hp, past=past),
        out_shape=jax.ShapeDtypeStruct((b, t, BRANCH), BF16),
        grid=(b, 4 // hp, t // tq),
        in_specs=in_specs,
        out_specs=pl.BlockSpec((1, tq, LANES * hp), lambda bi, j, i: (bi, i, j)),
        scratch_shapes=scratch,
        compiler_params=_cparams(("parallel", "parallel", "arbitrary"), 40),
        name="mla_attn",
    )(*args)


def _sb_block(zs, vis, us, carry, pv):
    nh = len(zs)
    tq = zs[0].shape[0]
    zls = [jnp.minimum(z, 0.0) - jnp.log(1.0 + jnp.exp(-jnp.abs(z))) for z in zs]
    lgs = [zl - z for zl, z in zip(zls, zs)]
    if vis is not None:
        lgs = [jnp.where(vis, lg, 0.0) for lg in lgs]
    his = [lg.astype(BF16) for lg in lgs]
    los = [(lg - hi.astype(F32)).astype(BF16) for lg, hi in zip(lgs, his)]
    later = _dot(jnp.concatenate(his + los, axis=0), us)
    new = []
    for h in range(nh):
        csum, acc = carry[h]
        tot = later[h * tq:(h + 1) * tq] + later[(nh + h) * tq:(nh + h + 1) * tq] + csum
        a = jnp.exp(zls[h] + tot)
        if vis is not None:
            a = jnp.where(vis, a, 0.0)
        new.append((csum + jnp.sum(lgs[h], axis=-1, keepdims=True), acc + pv(h, a.astype(BF16))))
    return tuple(new)


def _sb_heads(q2, nh):
    lane1 = lax.broadcasted_iota(jnp.int32, (1, LANES), 1)
    hm = ((lane1 < HEAD).astype(F32) * SB_SCALE, (lane1 >= HEAD).astype(F32) * SB_SCALE)
    return [(q2[:, LANES * (h // 2):LANES * (h // 2 + 1)] * hm[h % 2]).astype(BF16) for h in range(nh)]


def _sb_finish(res, o_ref, tq):
    lane = lax.broadcasted_iota(jnp.int32, (tq, LANES), 1)
    outs = [jnp.where(lane < HEAD, res[2 * p][1], res[2 * p + 1][1]) for p in range(len(res) // 2)]
    o_ref[0] = jnp.concatenate(outs, axis=1).astype(o_ref.dtype)


def _sb_attn_kernel(q_ref, k_ref, v_ref, us_ref, o_ref, *, tq, tk):
    i = pl.program_id(2)
    q_lo = i * tq
    nkb = (q_lo + tq - 2) // tk + 1
    cidx = lax.broadcasted_iota(jnp.int32, (tk, 2 * tq), 1)
    qidx = q_lo + jnp.where(cidx >= tq, cidx - tq, cidx)
    kloc = lax.broadcasted_iota(jnp.int32, (tk, 2 * tq), 0)
    qs = jnp.concatenate(_sb_heads(q_ref[0], 2), axis=0)
    us = us_ref[...]

    w = 2 * tq

    def group(kb, carry, n):
        csum, acc_t = carry
        o = pl.multiple_of((kb - (n - 1)) * tk, tk)
        kn = k_ref[0, pl.ds(o, n * tk), :].astype(BF16)
        vn_t = v_ref[0, pl.ds(o, n * tk), :].T.astype(BF16)
        z = _dot_nt(kn, qs)
        zl = jnp.minimum(z, 0.0) - jnp.log(1.0 + jnp.exp(-jnp.abs(z)))
        lg = zl - z
        vis = (kloc + kb * tk) < qidx
        lgs = [lg[j * tk:(j + 1) * tk] for j in range(n - 1)] + [jnp.where(vis, lg[(n - 1) * tk:], 0.0)]
        his = [x.astype(BF16) for x in lgs]
        los = [(x - h.astype(F32)).astype(BF16) for x, h in zip(lgs, his)]
        later = _dot(us, jnp.concatenate([jnp.concatenate([his[j], los[j]], axis=0) for j in range(n)], axis=1))
        tots = [None] * n
        for j in reversed(range(n)):
            tots[j] = later[:, j * w:(j + 1) * w] + csum
            csum = csum + jnp.sum(lgs[j], axis=0, keepdims=True)
        a = jnp.exp(zl + jnp.concatenate(tots, axis=0))
        a = jnp.concatenate([a[:(n - 1) * tk], jnp.where(vis, a[(n - 1) * tk:], 0.0)], axis=0) if n > 1 \
            else jnp.where(vis, a, 0.0)
        return csum, acc_t + _dot(vn_t, a.astype(BF16))

    carry = (jnp.zeros((1, 2 * tq), F32), jnp.zeros((LANES, 2 * tq), F32))
    n4, n2, n1 = nkb // 4, (nkb % 4) // 2, nkb % 2
    carry = lax.fori_loop(0, n4, lambda s, c: group(nkb - 1 - 4 * s, c, 4), carry)
    carry = lax.fori_loop(0, n2, lambda s, c: group(nkb - 1 - 4 * n4, c, 2), carry)
    carry = lax.fori_loop(0, n1, lambda s, c: group(0, c, 1), carry)
    acc_t = carry[1]
    row = lax.broadcasted_iota(jnp.int32, (LANES, tq), 0)
    o_ref[0] = jnp.where(row < HEAD, acc_t[:, :tq], acc_t[:, tq:]).T.astype(o_ref.dtype)


def _sb_decode_kernel(q_ref, k_ref, v_ref, kpt_ref, vpt_ref, usn_ref, usp_ref, o_ref, *, t, past, pb):
    qh = _sb_heads(q_ref[0], HEADS)
    kn = k_ref[0].astype(BF16)
    vn = v_ref[0].astype(BF16)
    pair = lambda x, h: x[:, LANES * (h // 2):LANES * (h // 2 + 1)]
    ri = lax.broadcasted_iota(jnp.int32, (t, t), 0)
    ci = lax.broadcasted_iota(jnp.int32, (t, t), 1)
    carry = tuple((jnp.zeros((t, 1), F32), jnp.zeros((t, LANES), F32)) for _ in range(HEADS))
    zs = [_dot_nt(qh[h], pair(kn, h)) for h in range(HEADS)]
    carry = _sb_block(zs, ci < ri, usn_ref[...], carry, lambda h, a: _dot(a, pair(vn, h)))
    usp = usp_ref[...]
    for blk in reversed(range(past // pb)):
        kts = [kpt_ref[0, 0, p, :, blk * pb:(blk + 1) * pb].astype(BF16) for p in range(HEADS // 2)]
        vts = [vpt_ref[0, 0, p, :, blk * pb:(blk + 1) * pb].astype(BF16) for p in range(HEADS // 2)]
        zs = [_dot(qh[h], kts[h // 2]) for h in range(HEADS)]
        carry = _sb_block(zs, None, usp, carry, lambda h, a: _dot_nt(a, vts[h // 2]))
    _sb_finish(carry, o_ref, t)


def _strict_upper(n):
    idx = jnp.arange(n)
    return (idx[:, None] > idx[None, :]).astype(BF16)


def _sb_attn(proj3, tq, tk):
    b, t, _ = proj3.shape
    assert tq == tk
    u = _strict_upper(tk).T
    us = jnp.concatenate([u, u], axis=1)
    return pl.pallas_call(
        functools.partial(_sb_attn_kernel, tq=tq, tk=tk),
        out_shape=jax.ShapeDtypeStruct((b, t, BRANCH), BF16),
        grid=(b, HEADS // 2, t // tq),
        in_specs=[pl.BlockSpec((1, tq, LANES), lambda bi, j, i: (bi, i, O_SBQ // LANES + j)),
                  pl.BlockSpec((1, t, LANES), lambda bi, j, i: (bi, 0, O_SBK // LANES + j)),
                  pl.BlockSpec((1, t, LANES), lambda bi, j, i: (bi, 0, O_SBV // LANES + j)),
                  pl.BlockSpec(us.shape, lambda bi, j, i: (0, 0))],
        out_specs=pl.BlockSpec((1, tq, LANES), lambda bi, j, i: (bi, i, j)),
        compiler_params=_cparams(("parallel", "parallel", "arbitrary"), 48),
        name="sb_attn",
    )(proj3, proj3, proj3, us)


def _sb_decode(proj3, kpt, vpt, layer, pb):
    b, t, _ = proj3.shape
    past = kpt.shape[4]
    usn, usp = _strict_upper(t), _strict_upper(pb)
    new = lambda off: pl.BlockSpec((1, t, BRANCH), lambda bi, o=off // BRANCH: (bi, 0, o))
    cache = pl.BlockSpec((1, 1, HEADS // 2, LANES, past), lambda bi: (layer, bi, 0, 0, 0))
    return pl.pallas_call(
        functools.partial(_sb_decode_kernel, t=t, past=past, pb=pb),
        out_shape=jax.ShapeDtypeStruct((b, t, BRANCH), BF16),
        grid=(b,),
        in_specs=[new(O_SBQ), new(O_SBK), new(O_SBV), cache, cache,
                  pl.BlockSpec(usn.shape, lambda bi: (0, 0)), pl.BlockSpec(usp.shape, lambda bi: (0, 0))],
        out_specs=pl.BlockSpec((1, t, BRANCH), lambda bi: (bi, 0, 0)),
        compiler_params=_cparams(("parallel",), 48),
        name="sb_decode",
    )(proj3, proj3, proj3, kpt, vpt, usn, usp)


def _mix_kernel(yrw_ref, yssm_ref, ymla_ref, ysb_ref, gate_ref, pm_ref, wb_ref, bm_ref, o_ref, acc_ref):
    n = pl.program_id(1)

    @pl.when(n == 0)
    def _():
        acc_ref[...] = jnp.zeros_like(acc_ref)

    def branch(y_ref):
        g = gate_ref[...]
        gated_half = y_ref[...].astype(F32) * (g * (0.5 * _sigmoid(g)))
        up_half = _dot(gated_half.astype(BF16), wb_ref[0, 0].astype(BF16))
        acc_ref[...] += (jnp.tanh(0.5 * (pm_ref[...] + bm_ref[0])) + 1.0) * up_half

    for idx, y_ref in enumerate((yrw_ref, yssm_ref, ymla_ref, ysb_ref)):
        pl.when(n == idx)(functools.partial(branch, y_ref))

    @pl.when(n == N_BRANCH - 1)
    def _():
        o_ref[...] = acc_ref[...].astype(BF16)


def _out_kernel(m_ref, x_ref, wo_ref, fg_ref, o_ref, *, final):
    xn = x_ref[...] + _dot(m_ref[...], wo_ref[0].astype(BF16))
    if final:
        xn = xn * lax.rsqrt(jnp.mean(xn * xn, axis=-1, keepdims=True) + EPS) * fg_ref[...]
    o_ref[...] = xn


def _merge(ys, proj, x2, wb, bm, wo, fg, final, layer):
    n, d = x2.shape
    tm = min(n, 1024)
    ysp = pl.BlockSpec((tm, BRANCH), lambda i, k: (i, 0))
    mixed = pl.pallas_call(
        _mix_kernel,
        out_shape=jax.ShapeDtypeStruct((n, d), BF16),
        grid=(n // tm, N_BRANCH),
        in_specs=[ysp, ysp, ysp, ysp,
                  pl.BlockSpec((tm, BRANCH), lambda i, k: (i, O_GATE // BRANCH + k)),
                  pl.BlockSpec((tm, d), lambda i, k: (i, O_MERGE // d + k)),
                  pl.BlockSpec((1, 1, BRANCH, d), lambda i, k: (layer, k, 0, 0)),
                  pl.BlockSpec((1, 1, d), lambda i, k: (k, 0, 0))],
        out_specs=pl.BlockSpec((tm, d), lambda i, k: (i, 0)),
        scratch_shapes=[pltpu.VMEM((tm, d), F32)],
        compiler_params=_cparams(("parallel", "arbitrary"), 56),
        name="branch_mix",
    )(*ys, proj, proj, wb, bm)
    to = min(n, 512)
    return pl.pallas_call(
        functools.partial(_out_kernel, final=final),
        out_shape=jax.ShapeDtypeStruct((n, d), F32),
        grid=(n // to,),
        in_specs=[pl.BlockSpec((to, d), lambda i: (i, 0)),
                  pl.BlockSpec((to, d), lambda i: (i, 0)),
                  pl.BlockSpec((1, d, d), lambda i: (layer, 0, 0), pipeline_mode=pl.Buffered(1)),
                  pl.BlockSpec((1, d), lambda i: (0, 0))],
        out_specs=pl.BlockSpec((to, d), lambda i: (i, 0)),
        compiler_params=_cparams(("parallel",), 48),
        name="out_proj",
    )(mixed, x2, wo, fg)


_SRC_SSM = RW_IN
_SRC_QLAT = _SRC_SSM + BRANCH
_SRC_KVLAT = _SRC_QLAT + MLA_Q_LORA
_SRC_KPE = _SRC_KVLAT + MLA_KV_LORA
_SRC_SB = _SRC_KPE + MLA_ROPE
_SRC_GATE = _SRC_SB + 3 * BRANCH
_SRC_MERGE = _SRC_GATE + N_BRANCH * BRANCH
_SRC_LORA = 3 * BRANCH
_PW_BLK = 4 * LANES
_PW_LAST = PROJ_W // _PW_BLK - 1
_PW_TAIL = {_PW_LAST - 1: (_SRC_QLAT, _SRC_QLAT + LANES, _SRC_QLAT + 2 * LANES, _SRC_LORA),
            _PW_LAST: (_SRC_KVLAT, _SRC_KVLAT + LANES, _SRC_KPE, _SRC_KPE)}


def _permute_src_col(j, k):
    nm = N_BRANCH * D_MODEL // _PW_BLK
    ng = nm + N_BRANCH * BRANCH // _PW_BLK
    nsb = ng + 3 * BRANCH // _PW_BLK
    nrw = nsb + 3 * BRANCH // _PW_BLK
    u = MLA_ROPE
    col = jnp.where(j < nm, _SRC_MERGE // u + _PW_BLK // u * j,
                    jnp.where(j < ng, _SRC_GATE // u + _PW_BLK // u * (j - nm),
                              jnp.where(j < nsb, _SRC_SB // u + _PW_BLK // u * (j - ng),
                                        jnp.where(j < nrw, _PW_BLK // u * (j - nsb), _SRC_SSM // u))))
    col = col + LANES // u * k
    for jj, src in _PW_TAIL.items():
        col = jnp.where(j == jj, src[k] // u, col)
    return col * u


def _permute_kernel(x0_ref, x1_ref, x2_ref, x3_ref, o_ref):
    j = pl.program_id(1)
    wins = (x0_ref, x1_ref, x2_ref, x3_ref)
    half = MLA_ROPE // 2

    def put(k, rows):
        o_ref[0, :, LANES * k:LANES * (k + 1)] = rows.T.astype(BF16)

    @pl.when(j < _PW_LAST)
    def _():
        for k in range(4):
            put(k, wins[k][0])

    @pl.when(j == _PW_LAST)
    def _():
        for k in range(2):
            put(k, wins[k][0])
        kpe = wins[2][0]
        row = lax.broadcasted_iota(jnp.int32, kpe.shape, 0)
        put(2, jnp.where(row < MLA_ROPE, kpe, 0.0))
        zeros = jnp.zeros((LANES - MLA_ROPE, kpe.shape[1]), F32)
        put(3, jnp.concatenate([-kpe[half:MLA_ROPE], kpe[:half], zeros], axis=0))


def _permute_w_in(w_in):
    depth, d, _ = w_in.shape
    wt = jnp.swapaxes(w_in, 1, 2)
    win = lambda k: pl.BlockSpec((pl.Element(1), pl.Element(LANES), pl.Element(d)),
                                 lambda l, j, k=k: (l, _permute_src_col(j, k), 0))
    return pl.pallas_call(
        _permute_kernel,
        out_shape=jax.ShapeDtypeStruct((depth, d, PROJ_W), BF16),
        grid=(depth, PROJ_W // _PW_BLK),
        in_specs=[win(k) for k in range(4)],
        out_specs=pl.BlockSpec((1, d, _PW_BLK), lambda l, j: (l, 0, j)),
        compiler_params=_cparams(("parallel", "parallel"), 32),
        name="permute_w_in",
    )(wt, wt, wt, wt)


def _rope_tables(pos):
    half = MLA_ROPE // 2
    inv = ROPE_BASE ** (-jnp.arange(half, dtype=F32) / half)
    ang = pos.astype(F32)[:, None] * inv
    cos, sin = jnp.cos(ang), jnp.sin(ang)
    t = pos.shape[0]
    cc = jnp.concatenate([cos, cos], axis=1)
    ss = jnp.concatenate([sin, sin], axis=1)
    tq = jnp.concatenate([jnp.ones((t, MLA_NOPE), F32), cc, ss], axis=1) * MLA_SCALE
    pad = jnp.zeros((t, LANES - MLA_ROPE), F32)
    return tq, jnp.concatenate([cc, pad], axis=1), jnp.concatenate([ss, pad], axis=1)


def _mla_weights(w_q_up, w_kv_up):
    half = MLA_ROPE // 2
    wq = w_q_up.reshape(MLA_Q_LORA, HEADS, MLA_NOPE + MLA_ROPE)
    x1 = wq[:, :, MLA_NOPE:MLA_NOPE + half]
    x2 = wq[:, :, MLA_NOPE + half:]
    wq_p = jnp.concatenate([wq, -x2, x1], axis=2).reshape(MLA_Q_LORA, HEADS * LANES).astype(BF16)
    wkv = w_kv_up.reshape(MLA_KV_LORA, HEADS, 2 * HEAD)
    wk_p = jnp.concatenate([wkv[:, :, :HEAD], jnp.zeros((MLA_KV_LORA, HEADS, HEAD), F32)], axis=2)
    wk_p = wk_p.reshape(MLA_KV_LORA, HEADS * LANES).astype(BF16)
    wv_p = wkv[:, :, HEAD:].reshape(MLA_KV_LORA, BRANCH).astype(BF16)
    return wq_p, wk_p, wv_p


def _s5_tables(lam_re, lam_im, log_dt, b_re, b_im, c_re, c_im):
    dt = jnp.exp(log_dt)[:, None]
    mag = jnp.exp(lam_re * dt)
    ang = lam_im * dt
    lb_re, lb_im = mag * jnp.cos(ang), mag * jnp.sin(ang)
    nr, ni = lb_re - 1.0, lb_im
    den = lam_re * lam_re + lam_im * lam_im
    f_re = (nr * lam_re + ni * lam_im) / den
    f_im = (ni * lam_re - nr * lam_im) / den
    bb_re = f_re[..., None] * b_re - f_im[..., None] * b_im
    bb_im = f_re[..., None] * b_im + f_im[..., None] * b_re
    eye = jnp.eye(SSM_GROUPS, dtype=F32)
    blk_in = lambda m: jnp.einsum('gpc,gh->gchp', m, eye).reshape(BRANCH, SSM_W)
    blk_out = lambda m: jnp.einsum('gcp,gh->gphc', m, eye).reshape(SSM_W, BRANCH)
    bblk = jnp.concatenate([blk_in(bb_re), blk_in(bb_im)], axis=1).astype(BF16)
    cblk = jnp.concatenate([blk_out(c_re), blk_out(-c_im)], axis=0).astype(BF16)

    def power(j):
        m = jnp.exp(lam_re * dt * j)
        return (m * jnp.cos(ang * j)).reshape(1, SSM_W), (m * jnp.sin(ang * j)).reshape(1, SSM_W)

    lam = jnp.concatenate([p for j in (1, 2, 4) for p in power(j)] + [jnp.zeros((2, SSM_W), F32)], axis=0)
    pws = [power(j) for j in range(1, 9)]
    pw = jnp.concatenate([p[0] for p in pws] + [p[1] for p in pws], axis=0)
    return bblk, cblk, lam, pw


def _layer(x, pos0, st, lw, final_g, final):
    b, t, d = x.shape
    n = b * t
    x2 = x.reshape(n, d)
    proj = _norm_proj(x2, lw["norm_g"], lw["w_in_all"], lw["layer"])
    proj3 = proj.reshape(b, t, PROJ_W)
    past = st["past"]

    chunk = min(t, CHUNK)
    y_rw, wkv_new = _rwkv(proj3, st["sp_r"], st["sp_k"], st["sp_v"], st["sp_l"], st["wkv"],
                        lw["rw_pvec"], lw["rw_mu_l"], lw["rw_w2p"], lw["rw_a2p"], lw["g2"], chunk)
    shift_new = jnp.concatenate([proj3[:, t - 1:, O_RWR:O_RWR + 3 * BRANCH],
                                 proj3[:, t - 1:, O_LORA:O_LORA + 2 * RW_LORA]], axis=-1)

    y_ssm, hr, hi = _s5(proj3, st["ssm_re"], st["ssm_im"], lw["ssm_bblk"], lw["ssm_cblk"], lw["ssm_lam"],
                        lw["ssm_pw"], lw["ssm_d"], lw["ssm_wglu"], lw["ssm_bglu"])
    ssm_re_new = hr.reshape(b, SSM_GROUPS, SSM_STATE)
    ssm_im_new = hi.reshape(b, SSM_GROUPS, SSM_STATE)

    pos = pos0 + jnp.arange(t, dtype=jnp.int32)
    tq_tab, tc_tab, ts_tab = _rope_tables(pos)
    q, ckv, kr = _mla_prep(proj, t, tq_tab, tc_tab, ts_tab, lw["mla_gq"], lw["mla_gkv"], lw["mla_wq"])
    ckv3 = ckv.reshape(b, t, MLA_KV_LORA)
    kr3 = kr.reshape(b, t, MLA_ROPE)
    s_pad = -(-(past + t) // LANES) * LANES
    kc, vv = _mla_kv(ckv[None], kr[None], lw["mla_wk"], lw["mla_wv"], lw["dup"])
    kc3, vv3 = kc.reshape(b, t, HEADS * LANES), vv.reshape(b, t, BRANCH)
    kcp3 = vvp3 = None
    if past:
        kcp, vvp = _mla_kv(st["ckv_all"], st["kpe_all"], lw["mla_wk"], lw["mla_wv"], lw["dup"], lw["layer"])
        kcp3, vvp3 = kcp.reshape(b, past, HEADS * LANES), vvp.reshape(b, past, BRANCH)
    small = t <= 64
    assert small or not past
    tq = min(t, 256)
    hp = 4 if small else 1
    q3 = q.reshape(b, t, HEADS * LANES)
    if small:
        y_mla = _mla_attn(q3, kc3, vv3, kcp3, vvp3, s_pad, tq, s_pad, hp)
    else:
        y_mla = _mla_prompt(q3, kc3, vv3, tq)

    sbk = proj3[:, :, O_SBK:O_SBK + BRANCH]
    sbv = proj3[:, :, O_SBV:O_SBV + BRANCH]
    if past:
        y_sb = _sb_decode(proj3, st["sbk_t"], st["sbv_t"], lw["layer"], min(256, past))
    else:
        y_sb = _sb_attn(proj3, tq, min(256, t))

    ys = [y.reshape(n, BRANCH) for y in (y_rw, y_ssm, y_mla, y_sb)]
    x_new = _merge(ys, proj, x2, lw["w_branch"], lw["b_merge"], lw["w_out"], final_g, final,
                   lw["layer"]).reshape(b, t, d)
    new_state = (shift_new, wkv_new, ssm_re_new, ssm_im_new, ckv3, kr3,
                 sbk.reshape(b, t, HEADS, HEAD), sbv.reshape(b, t, HEADS, HEAD))
    return x_new, new_state


def kernel(x_prompt, x_sample, state_rwkv_shift, state_rwkv_wkv, state_ssm_re, state_ssm_im, cache_mla_ckv, cache_mla_kpe, cache_sb_k, cache_sb_v, norm_g, w_in, rw_mu, rw_w0, rw_w2, rw_a0, rw_a2, rw_k_k, rw_k_a, rw_r_k, rw_lnx_g, rw_lnx_b, ssm_lam_re, ssm_lam_im, ssm_log_dt, ssm_b_re, ssm_b_im, ssm_c_re, ssm_c_im, ssm_d, ssm_w_glu, ssm_b_glu, mla_q_norm, mla_w_q_up, mla_kv_norm, mla_w_kv_up, w_branch, b_merge, w_out, final_norm_g):
    depth = w_in.shape[0]
    bp, tp, _ = x_prompt.shape
    bs = x_sample.shape[0]
    past = cache_mla_ckv.shape[2]

    w_in_p = _permute_w_in(w_in)
    lane_i = jnp.arange(LANES)
    g2 = ((lane_i[:, None] // HEAD) == (lane_i[None, :] // HEAD)).astype(BF16)
    rope_i = jnp.arange(MLA_ROPE)
    dup = ((lane_i[None, :] == rope_i[:, None] + MLA_NOPE)
           | (lane_i[None, :] == rope_i[:, None] + MLA_NOPE + MLA_ROPE)).astype(BF16)
    final_g = final_norm_g.reshape(1, D_MODEL)
    zpad = jnp.zeros((RW_LORA, BRANCH), F32)

    layers = []
    for l in range(depth):
        wq_p, wk_p, wv_p = _mla_weights(mla_w_q_up[l], mla_w_kv_up[l])
        bblk, cblk, lam, pw = _s5_tables(ssm_lam_re[l], ssm_lam_im[l], ssm_log_dt[l], ssm_b_re[l], ssm_b_im[l],
                                         ssm_c_re[l], ssm_c_im[l])
        mu = rw_mu[l]
        rows = [mu[:BRANCH], mu[BRANCH:2 * BRANCH], mu[2 * BRANCH:3 * BRANCH], rw_w0[l], rw_a0[l], rw_k_k[l],
                rw_k_a[l], rw_r_k[l].reshape(BRANCH), rw_lnx_g[l], rw_lnx_b[l]]
        pvec = jnp.concatenate([jnp.stack(rows), jnp.zeros((16 - len(rows), BRANCH), F32)], axis=0)
        layers.append(dict(
            norm_g=norm_g[l].reshape(1, D_MODEL), w_in_all=w_in_p, layer=l,
            rw_pvec=pvec, rw_mu_l=mu[3 * BRANCH:].reshape(1, 2 * RW_LORA),
            rw_w2p=jnp.concatenate([rw_w2[l], zpad], axis=0).astype(BF16),
            rw_a2p=jnp.concatenate([zpad, rw_a2[l]], axis=0).astype(BF16),
            g2=g2, dup=dup,
            ssm_bblk=bblk, ssm_cblk=cblk, ssm_lam=lam, ssm_pw=pw,
            ssm_d=ssm_d[l].reshape(1, BRANCH), ssm_wglu=ssm_w_glu[l].astype(BF16),
            ssm_bglu=ssm_b_glu[l].reshape(1, BRANCH),
            mla_gq=mla_q_norm[l].reshape(1, MLA_Q_LORA), mla_gkv=mla_kv_norm[l].reshape(1, MLA_KV_LORA),
            mla_wq=wq_p, mla_wk=wk_p, mla_wv=wv_p,
            w_branch=w_branch, b_merge=b_merge[l].reshape(N_BRANCH, 1, D_MODEL), w_out=w_out))

    def fresh(bn):
        return dict(sp_r=jnp.zeros((bn, 1, BRANCH), F32), sp_k=jnp.zeros((bn, 1, BRANCH), F32),
                    sp_v=jnp.zeros((bn, 1, BRANCH), F32), sp_l=jnp.zeros((bn, 1, 2 * RW_LORA), F32),
                    wkv=jnp.zeros((bn, HEADS, HEAD, HEAD), F32),
                    ssm_re=jnp.zeros((bn, 1, SSM_W), F32), ssm_im=jnp.zeros((bn, 1, SSM_W), F32),
                    past=0)

    nl = cache_mla_ckv.shape[0]
    ckv_all = cache_mla_ckv.reshape(nl, bs * past, MLA_KV_LORA)
    kpe_all = cache_mla_kpe.reshape(nl, bs * past, MLA_ROPE)
    sbk_t = jnp.transpose(cache_sb_k, (0, 1, 3, 4, 2)).reshape(nl, bs, HEADS // 2, LANES, past)
    sbv_t = jnp.transpose(cache_sb_v, (0, 1, 3, 4, 2)).reshape(nl, bs, HEADS // 2, LANES, past)

    def carried(l):
        sh = state_rwkv_shift[l]
        return dict(sp_r=sh[:, :, :BRANCH], sp_k=sh[:, :, BRANCH:2 * BRANCH], sp_v=sh[:, :, 2 * BRANCH:3 * BRANCH],
                    sp_l=sh[:, :, 3 * BRANCH:], wkv=state_rwkv_wkv[l],
                    ssm_re=state_ssm_re[l].reshape(bs, 1, SSM_W), ssm_im=state_ssm_im[l].reshape(bs, 1, SSM_W),
                    past=past, ckv_all=ckv_all, kpe_all=kpe_all, sbk_t=sbk_t, sbv_t=sbv_t)

    xp, xs = x_prompt, x_sample
    new_p, new_s = [], []
    for l in range(depth):
        last = l == depth - 1
        xp, st_p = _layer(xp, 0, fresh(bp), layers[l], final_g, last)
        xs, st_s = _layer(xs, past, carried(l), layers[l], final_g, last)
        new_p.append(st_p)
        new_s.append(st_s)
    stk = lambda lst, i: jnp.stack([s[i] for s in lst], axis=0)
    return (xp, xs) + tuple(stk(new_p, i) for i in range(8)) + tuple(stk(new_s, i) for i in range(8))
```

```python
import functools
import math

import jax
import jax.numpy as jnp
from jax import lax
from jax.experimental import pallas as pl
from jax.experimental.pallas import tpu as pltpu

F32 = jnp.float32
BF16 = jnp.bfloat16

D_MODEL = 2048
BRANCH = 512
N_BRANCH = 4
EPS = 1e-6
CHUNK = 64
HEAD = 64
HEADS = 8
RW_LORA = 64
RW_IN = 3 * BRANCH + 2 * RW_LORA
RW_GN_EPS = 64e-5
SSM_GROUP = 16
SSM_GROUPS = 32
SSM_STATE = 64
SSM_W = SSM_GROUPS * SSM_STATE
MLA_NOPE = 64
MLA_ROPE = 32
MLA_Q_LORA = 384
MLA_KV_LORA = 256
MLA_SCALE = 1.0 / math.sqrt(MLA_NOPE + MLA_ROPE)
ROPE_BASE = 10000.0
SB_SCALE = 1.0 / math.sqrt(HEAD)
LANES = 128
MIB = 1024 * 1024

O_MERGE = 0
O_GATE = 8192
PB_W = 10240
O_QLAT = 0
O_LORA = 384
O_KVLAT = 512
O_KPE = 768
O_KPESW = 896
O_SBQ = 1024
O_SBK = 1536
O_SBV = 2048
O_RWR = 2560
O_RWK = 3072
O_RWV = 3584
O_SSM = 4096
PF_W = 4608
PROJ_W = PB_W + PF_W


def _cparams(sem, vmem_mib):
    return pltpu.CompilerParams(dimension_semantics=sem, vmem_limit_bytes=vmem_mib * MIB)


def _dot(a, b):
    return jnp.dot(a, b, preferred_element_type=F32)


def _dot_nt(a, b):
    return lax.dot_general(a, b, (((1,), (1,)), ((), ())), preferred_element_type=F32)


def _split3(x):
    h1 = x.astype(BF16)
    r1 = x - h1.astype(F32)
    h2 = r1.astype(BF16)
    h3 = (r1 - h2.astype(F32)).astype(BF16)
    return h1, h2, h3


def _softplus(x):
    return jnp.maximum(x, 0.0) + jnp.log1p(jnp.exp(-jnp.abs(x)))


def _sigmoid(x):
    return 0.5 * jnp.tanh(0.5 * x) + 0.5


def _norm_proj_kernel(x_ref, g_ref, w_ref, ob_ref, of_ref, h_ref, *, nb16):
    j = pl.program_id(1)

    @pl.when(j == 0)
    def _():
        x = x_ref[...]
        ms = jnp.mean(x * x, axis=-1, keepdims=True)
        h_ref[...] = (x * lax.rsqrt(ms + EPS) * g_ref[...]).astype(BF16)

    @pl.when(j < nb16)
    def _():
        ob_ref[...] = _dot(h_ref[...], w_ref[0]).astype(BF16)

    @pl.when(j >= nb16)
    def _():
        of_ref[...] = _dot(h_ref[...], w_ref[0])


def _norm_proj(x2, g, w_all, layer):
    n, d = x2.shape
    tm = min(n, 1024)
    tn = 512
    nb16 = PB_W // tn
    return pl.pallas_call(
        functools.partial(_norm_proj_kernel, nb16=nb16),
        out_shape=(jax.ShapeDtypeStruct((n, PB_W), BF16), jax.ShapeDtypeStruct((n, PF_W), F32)),
        grid=(n // tm, PROJ_W // tn),
        in_specs=[
            pl.BlockSpec((tm, d), lambda i, j: (i, 0)),
            pl.BlockSpec((1, d), lambda i, j: (0, 0)),
            pl.BlockSpec((1, d, tn), lambda i, j: (layer, 0, j)),
        ],
        out_specs=(pl.BlockSpec((tm, tn), lambda i, j: (i, jnp.minimum(j, nb16 - 1))),
                   pl.BlockSpec((tm, tn), lambda i, j: (i, jnp.maximum(j - nb16, 0)))),
        scratch_shapes=[pltpu.VMEM((tm, d), BF16)],
        compiler_params=_cparams(("parallel", "arbitrary"), 40),
        name="norm_proj",
    )(x2, g, w_all)


def _segsum(x, g2):
    outs = []
    for j in range(BRANCH // LANES):
        h1, h2, h3 = _split3(x[:, LANES * j:LANES * (j + 1)])
        outs.append(_dot(h1, g2) + _dot(h2, g2) + _dot(h3, g2))
    return jnp.concatenate(outs, axis=1)


def _rwkv_kernel(pr_ref, pk_ref, pv_ref, plo_ref, spr_ref, spk_ref, spv_ref, spl_ref, s0_ref,
                 pvec_ref, mul_ref, w2_ref, a2_ref, g2_ref,
                 y_ref, sout_ref,
                 st_ref, cr_ref, ck_ref, cv_ref, cl_ref, *, C, nb):
    c = pl.program_id(1)

    @pl.when(c == 0)
    def _():
        zero = jnp.zeros((HEAD, HEAD), F32)
        for bi in range(nb):
            for j in range(HEADS // 2):
                st_ref[bi, j] = jnp.concatenate(
                    [jnp.concatenate([s0_ref[bi, 2 * j], zero], axis=1),
                     jnp.concatenate([zero, s0_ref[bi, 2 * j + 1]], axis=1)], axis=0)
        cr_ref[...] = spr_ref[...]
        ck_ref[...] = spk_ref[...]
        cv_ref[...] = spv_ref[...]
        cl_ref[...] = spl_ref[...]

    pvec = pvec_ref[...]
    prow = lambda i: pvec[i:i + 1, :]
    g2 = g2_ref[...]

    def tshift(p_ref, carry_ref, mu):
        outs = []
        for bi in range(nb):
            p = p_ref[bi]
            rolled = pltpu.roll(p, 1, 0)
            rid = lax.broadcasted_iota(jnp.int32, p.shape, 0)
            prev = jnp.where(rid == 0, carry_ref[bi], rolled)
            carry_ref[bi] = p[C - 1:C, :]
            outs.append(p + (prev - p) * mu)
        return jnp.concatenate(outs, axis=0)

    r = tshift(pr_ref, cr_ref, prow(0))
    k = tshift(pk_ref, ck_ref, prow(1))
    v = tshift(pv_ref, cv_ref, prow(2))
    lo = tshift(plo_ref, cl_ref, mul_ref[...])

    log_w = -_softplus(-(prow(3) + _dot(jnp.tanh(lo).astype(BF16), w2_ref[...]))) - 0.5
    ld = -jnp.exp(log_w)
    a_icl = jax.nn.sigmoid(prow(4) + _dot(lo.astype(BF16), a2_ref[...]))
    kkr = k * prow(5)
    kk = kkr / jnp.maximum(jnp.sqrt(_segsum(kkr * kkr, g2)), 1e-12)
    k2 = k * (1.0 + (a_icl - 1.0) * prow(6))
    av = -kk
    bv = kk * a_icl

    sh = C.bit_length() - 1
    ri = lax.broadcasted_iota(jnp.int32, (nb * C, nb * C), 0)
    ci = lax.broadcasted_iota(jnp.int32, (nb * C, nb * C), 1)
    lincl = jnp.logical_and(ci <= ri, (ci >> sh) == (ri >> sh)).astype(BF16)
    h1, h2, h3 = _split3(ld)
    g = _dot(lincl, h1) + _dot(lincl, h2) + _dot(lincl, h3)
    g_lasts = [g[(bi + 1) * C - 1:(bi + 1) * C, :] for bi in range(nb)]
    g_last = jnp.concatenate([jnp.broadcast_to(gl, (C, BRANCH)) for gl in g_lasts], axis=0)
    eg = jnp.exp(g)
    eng = jnp.exp(-g)
    at = av * jnp.exp(g - ld)
    rt = r * eg
    bt = bv * eng
    kt = k2 * eng
    e_c = jnp.exp(g_last - g)
    b_end = bv * e_c
    k_end = k2 * e_c
    eg_lasts = [jnp.exp(gl) for gl in g_lasts]

    c2 = 2 * C
    npair = BRANCH // LANES
    r_i = lax.broadcasted_iota(jnp.int32, (c2, c2), 0)
    c_i = lax.broadcasted_iota(jnp.int32, (c2, c2), 1)
    c_m = jnp.where(c_i >= C, c_i - C, c_i)
    mhalf = c_m < jnp.where(r_i < C, r_i, r_i - C + 1)
    mfull = jnp.concatenate([mhalf, mhalf], axis=0)
    bd2 = (r_i >> sh) == (c_i >> sh)
    eye2 = (r_i == c_i).astype(F32)
    lane = lax.broadcasted_iota(jnp.int32, (1, LANES), 1)
    m0 = (lane < HEAD).astype(F32)
    m1 = (lane >= HEAD).astype(F32)
    v_i = lax.broadcasted_iota(jnp.int32, (LANES, LANES), 0)
    k_i = lax.broadcasted_iota(jnp.int32, (LANES, LANES), 1)
    bdmask = (v_i >> 6) == (k_i >> 6)
    zf = jnp.zeros((C, LANES), F32)
    zc = jnp.zeros((C, LANES), BF16)
    sls = [slice(LANES * j, LANES * (j + 1)) for j in range(npair)]

    units = [(bi, j) for bi in range(nb) for j in range(npair)]
    tile = lambda x, bi, j: x[bi * C:(bi + 1) * C, LANES * j:LANES * (j + 1)]
    ws, a_ss, s2s = [], [], []
    for bi, j in units:
        atj, rtj = tile(at, bi, j), tile(rt, bi, j)
        a0, r0, a1, r1 = atj * m0, rtj * m0, atj * m1, rtj * m1
        lhs_w = jnp.concatenate([jnp.concatenate([a0, zf], axis=1), jnp.concatenate([r0, zf], axis=1),
                                 jnp.concatenate([zf, a1], axis=1), jnp.concatenate([zf, r1], axis=1)],
                                axis=0).astype(BF16)
        btj, ktj = tile(bt, bi, j), tile(kt, bi, j)
        rhs_w = jnp.concatenate([jnp.concatenate([btj, ktj], axis=0),
                                 jnp.concatenate([ktj, btj], axis=0)], axis=1).astype(BF16)
        ws.append(jnp.where(mfull, _dot_nt(lhs_w, rhs_w), 0.0))
        lhs_s = jnp.concatenate([a0, r0, a1, r1], axis=0).astype(BF16)
        s2 = st_ref[bi, j]
        s2h, s2l, _ = _split3(s2)
        a_ss.append(_dot_nt(lhs_s, s2h) + _dot_nt(lhs_s, s2l))
        s2s.append(s2)

    n_pows = [jnp.where(bd2, jnp.concatenate([w[:C], w[c2:c2 + C]], axis=0), 0.0) for w in ws]
    t_mats = [eye2 + n for n in n_pows]
    lv = 2
    while lv < C:
        n_pows = [_dot(n.astype(BF16), n.astype(BF16)) for n in n_pows]
        t_mats = [t + _dot(t.astype(BF16), n.astype(BF16)) for t, n in zip(t_mats, n_pows)]
        lv *= 2

    wbs = [w.astype(BF16) for w in ws]
    v0s = [(tile(v, bi, j) * m0).astype(BF16) for bi, j in units]
    v1s = [(tile(v, bi, j) * m1).astype(BF16) for bi, j in units]
    xs = [jnp.concatenate([a_s[:C] + _dot(wb[:C], jnp.concatenate([zc, v0], axis=0)),
                           a_s[c2:c2 + C] + _dot(wb[c2:c2 + C], jnp.concatenate([v1, zc], axis=0))], axis=0)
          for a_s, wb, v0, v1 in zip(a_ss, wbs, v0s, v1s)]
    us = [_dot(t.astype(BF16), x.astype(BF16)) for t, x in zip(t_mats, xs)]
    ys = []
    for n, (bi, j) in enumerate(units):
        a_s, wb, u = a_ss[n], wbs[n], us[n]
        u0, u1 = u[:C], u[C:]
        y0 = a_s[C:c2] + _dot(wb[C:c2], jnp.concatenate([u0.astype(BF16), v0s[n]], axis=0))
        y1 = a_s[c2 + C:] + _dot(wb[c2 + C:], jnp.concatenate([v1s[n], u1.astype(BF16)], axis=0))
        ys.append(y0 + y1)
        uv = jnp.concatenate([u0 + u1, tile(v, bi, j)], axis=0)
        bk = jnp.concatenate([tile(b_end, bi, j), tile(k_end, bi, j)], axis=0).astype(BF16)
        upd = _dot(uv.T.astype(BF16), bk)
        st_ref[bi, j] = s2s[n] * eg_lasts[bi][:, sls[j]] + jnp.where(bdmask, upd, 0.0)
    y = jnp.concatenate([jnp.concatenate(ys[bi * npair:(bi + 1) * npair], axis=1) for bi in range(nb)],
                        axis=0)

    inv_n = 1.0 / HEAD
    yc = y - _segsum(y, g2) * inv_n
    yn = yc * lax.rsqrt(_segsum(yc * yc, g2) * inv_n + RW_GN_EPS) * prow(8) + prow(9)
    out = (yn + _segsum(r * k2 * prow(7), g2) * v).astype(y_ref.dtype)
    for bi in range(nb):
        y_ref[bi] = out[bi * C:(bi + 1) * C]

    @pl.when(c == pl.num_programs(1) - 1)
    def _():
        for bi in range(nb):
            for j in range(HEADS // 2):
                s2 = st_ref[bi, j]
                sout_ref[bi, 2 * j] = s2[:HEAD, :HEAD]
                sout_ref[bi, 2 * j + 1] = s2[HEAD:, HEAD:]


def _rwkv(proj3, sp_r, sp_k, sp_v, sp_l, s0, pvec, mu_l, w2p, a2p, g2, chunk):
    b, t, _ = proj3.shape
    nc = t // chunk
    nb = min(b, 4)
    cb = lambda off, wdt: off // wdt
    bspec = lambda off, wdt: pl.BlockSpec((nb, chunk, wdt), lambda i, c, o=cb(off, wdt): (i, c, o))
    row3 = lambda wdt: pl.BlockSpec((nb, 1, wdt), lambda i, c: (i, 0, 0))
    full2 = lambda a: pl.BlockSpec(a.shape, lambda i, c: (0, 0))
    return pl.pallas_call(
        functools.partial(_rwkv_kernel, C=chunk, nb=nb),
        out_shape=(jax.ShapeDtypeStruct((b, t, BRANCH), BF16),
                   jax.ShapeDtypeStruct((b, HEADS, HEAD, HEAD), F32)),
        grid=(b // nb, nc),
        in_specs=[
            bspec(O_RWR, BRANCH), bspec(O_RWK, BRANCH), bspec(O_RWV, BRANCH), bspec(O_LORA, LANES),
            row3(BRANCH), row3(BRANCH), row3(BRANCH), row3(LANES),
            pl.BlockSpec((nb, HEADS, HEAD, HEAD), lambda i, c: (i, 0, 0, 0)),
            full2(pvec), full2(mu_l), full2(w2p), full2(a2p), full2(g2),
        ],
        out_specs=(pl.BlockSpec((nb, chunk, BRANCH), lambda i, c: (i, c, 0)),
                   pl.BlockSpec((nb, HEADS, HEAD, HEAD), lambda i, c: (i, 0, 0, 0))),
        scratch_shapes=[pltpu.VMEM((nb, 4, LANES, LANES), F32),
                        pltpu.VMEM((nb, 1, BRANCH), F32), pltpu.VMEM((nb, 1, BRANCH), F32),
                        pltpu.VMEM((nb, 1, BRANCH), F32), pltpu.VMEM((nb, 1, LANES), F32)],
        compiler_params=_cparams(("parallel", "arbitrary"), 32),
        name="rwkv7",
    )(proj3, proj3, proj3, proj3, sp_r, sp_k, sp_v, sp_l, s0, pvec, mu_l, w2p, a2p, g2)


def _s5_kernel(u_ref, h0r_ref, h0i_ref, bblk_ref, cblk_ref, lam_ref, pw_ref, dsk_ref, wglu_ref, bglu_ref,
               y_ref, hro_ref, hio_ref, x_ref, cr_ref, ci_ref, *, tb):
    @pl.when(pl.program_id(1) == 0)
    def _():
        cr_ref[...] = h0r_ref[0]
        ci_ref[...] = h0i_ref[0]

    u = u_ref[0]
    x = _dot(u.astype(BF16), bblk_ref[...])
    xr = x[:, :SSM_W]
    xi = x[:, SSM_W:]
    for lvl, s in enumerate((1, 2, 4)):
        ar = lam_ref[2 * lvl]
        ai = lam_ref[2 * lvl + 1]
        sr = pltpu.roll(xr, s, 0)
        si = pltpu.roll(xi, s, 0)
        xr, xi = xr + (ar * sr - ai * si), xi + (ar * si + ai * sr)
    x_ref[:, :SSM_W] = xr
    x_ref[:, SSM_W:] = xi
    pwr = pw_ref[0:8, :]
    pwi = pw_ref[8:16, :]

    def body(gi, carry):
        cr, ci = carry
        o = pl.multiple_of(gi * 8, 8)
        br = x_ref[pl.ds(o, 8), 0:SSM_W]
        bi = x_ref[pl.ds(o, 8), SSM_W:2 * SSM_W]
        br = br + pwr * cr - pwi * ci
        bi = bi + pwr * ci + pwi * cr
        x_ref[pl.ds(o, 8), 0:SSM_W] = br
        x_ref[pl.ds(o, 8), SSM_W:2 * SSM_W] = bi
        return br[7:8, :], bi[7:8, :]

    cr, ci = lax.fori_loop(0, tb // 8, body, (cr_ref[...], ci_ref[...]))
    cr_ref[...] = cr
    ci_ref[...] = ci
    hro_ref[0] = cr
    hio_ref[0] = ci
    y = _dot(x_ref[...].astype(BF16), cblk_ref[...]) + dsk_ref[...] * u
    g = jax.nn.gelu(y)
    out = g * jax.nn.sigmoid(_dot(g.astype(BF16), wglu_ref[...]) + bglu_ref[...])
    y_ref[0] = out.astype(y_ref.dtype)


def _s5(proj3, h0r, h0i, bblk, cblk, lam, pw, dsk, wglu, bglu):
    b, t, _ = proj3.shape
    tb = min(t, 128)
    sub = jnp.arange(tb)[:, None] & 7
    lam = jnp.stack([jnp.where(sub >= s, lam[2 * lv + c:2 * lv + c + 1], 0.0)
                     for lv, s in enumerate((1, 2, 4)) for c in (0, 1)])
    full2 = lambda a: pl.BlockSpec(a.shape, lambda i, c: (0,) * a.ndim)
    row3 = pl.BlockSpec((1, 1, SSM_W), lambda i, c: (i, 0, 0))
    return pl.pallas_call(
        functools.partial(_s5_kernel, tb=tb),
        out_shape=(jax.ShapeDtypeStruct((b, t, BRANCH), BF16),
                   jax.ShapeDtypeStruct((b, 1, SSM_W), F32),
                   jax.ShapeDtypeStruct((b, 1, SSM_W), F32)),
        grid=(b, t // tb),
        in_specs=[pl.BlockSpec((1, tb, BRANCH), lambda i, c: (i, c, O_SSM // BRANCH)),
                  row3, row3, full2(bblk), full2(cblk), full2(lam), full2(pw), full2(dsk),
                  full2(wglu), full2(bglu)],
        out_specs=(pl.BlockSpec((1, tb, BRANCH), lambda i, c: (i, c, 0)), row3, row3),
        scratch_shapes=[pltpu.VMEM((tb, 2 * SSM_W), F32),
                        pltpu.VMEM((1, SSM_W), F32), pltpu.VMEM((1, SSM_W), F32)],
        compiler_params=_cparams(("parallel", "arbitrary"), 48),
        name="s5",
    )(proj3, h0r, h0i, bblk, cblk, lam, pw, dsk, wglu, bglu)


def _mla_prep_kernel(ql_ref, kvl_ref, kpe_ref, kpesw_ref, tq_ref, tc_ref, ts_ref,
                     gq_ref, gkv_ref, wq_ref, q_ref, ckv_ref, kr_ref):
    ql = ql_ref[...]
    qn = ql * lax.rsqrt(jnp.mean(ql * ql, axis=-1, keepdims=True) + EPS) * gq_ref[...]
    q = _dot(qn.astype(BF16), wq_ref[...])
    tq = tq_ref[...]
    q_ref[...] = jnp.concatenate(
        [q[:, LANES * h:LANES * (h + 1)] * tq for h in range(HEADS)], axis=1).astype(BF16)
    kvl = kvl_ref[...]
    ckv_ref[...] = kvl * lax.rsqrt(jnp.mean(kvl * kvl, axis=-1, keepdims=True) + EPS) * gkv_ref[...]
    kr = kpe_ref[...] * tc_ref[...] + kpesw_ref[...] * ts_ref[...]
    kr_ref[...] = kr[:, :MLA_ROPE]


def _mla_prep(proj, t, tq_tab, tc_tab, ts_tab, gq, gkv, wq):
    n = proj.shape[0]
    tm = min(t, 512)
    nt = t // tm
    tab = pl.BlockSpec((tm, LANES), lambda i: (i % nt, 0))
    full2 = lambda a: pl.BlockSpec(a.shape, lambda i: (0, 0))
    return pl.pallas_call(
        _mla_prep_kernel,
        out_shape=(jax.ShapeDtypeStruct((n, HEADS * LANES), BF16),
                   jax.ShapeDtypeStruct((n, MLA_KV_LORA), F32),
                   jax.ShapeDtypeStruct((n, MLA_ROPE), F32)),
        grid=(n // tm,),
        in_specs=[pl.BlockSpec((tm, MLA_Q_LORA), lambda i: (i, O_QLAT // MLA_Q_LORA)),
                  pl.BlockSpec((tm, MLA_KV_LORA), lambda i: (i, O_KVLAT // MLA_KV_LORA)),
                  pl.BlockSpec((tm, LANES), lambda i: (i, O_KPE // LANES)),
                  pl.BlockSpec((tm, LANES), lambda i: (i, O_KPESW // LANES)),
                  tab, tab, tab, full2(gq), full2(gkv), full2(wq)],
        out_specs=(pl.BlockSpec((tm, HEADS * LANES), lambda i: (i, 0)),
                   pl.BlockSpec((tm, MLA_KV_LORA), lambda i: (i, 0)),
                   pl.BlockSpec((tm, MLA_ROPE), lambda i: (i, 0))),
        compiler_params=_cparams(("parallel",), 32),
        name="mla_prep",
    )(proj, proj, proj, proj, tq_tab, tc_tab, ts_tab, gq, gkv, wq)


def _mla_kv_kernel(ckv_ref, kr_ref, wk_ref, wv_ref, dup_ref, k_ref, v_ref):
    cb = ckv_ref[0].astype(BF16)
    kn = _dot(cb, wk_ref[...])
    krd = _dot(kr_ref[0].astype(BF16), dup_ref[...])
    k_ref[...] = jnp.concatenate(
        [kn[:, LANES * h:LANES * (h + 1)] + krd for h in range(HEADS)], axis=1).astype(BF16)
    v_ref[...] = _dot(cb, wv_ref[...]).astype(BF16)


def _mla_kv(ckv, kr, wk, wv, dup, layer=0):
    n = ckv.shape[1]
    tm = next((c for c in (512, LANES) if n % c == 0), n)
    full2 = lambda a: pl.BlockSpec(a.shape, lambda i: (0, 0))
    return pl.pallas_call(
        _mla_kv_kernel,
        out_shape=(jax.ShapeDtypeStruct((n, HEADS * LANES), BF16),
                   jax.ShapeDtypeStruct((n, BRANCH), BF16)),
        grid=(n // tm,),
        in_specs=[pl.BlockSpec((1, tm, MLA_KV_LORA), lambda i: (layer, i, 0)),
                  pl.BlockSpec((1, tm, MLA_ROPE), lambda i: (layer, i, 0)),
                  full2(wk), full2(wv), full2(dup)],
        out_specs=(pl.BlockSpec((tm, HEADS * LANES), lambda i: (i, 0)),
                   pl.BlockSpec((tm, BRANCH), lambda i: (i, 0))),
        compiler_params=_cparams(("parallel",), 32),
        name="mla_kv",
    )(ckv, kr, wk, wv, dup)


def _assemble_keys(past_ref, new_ref, all_ref, past, t):
    all_ref[0:past, :] = past_ref[0].astype(BF16)
    all_ref[past:past + t, :] = new_ref[0].astype(BF16)
    pad = all_ref.shape[0] - past - t
    if pad:
        all_ref[past + t:, :] = jnp.zeros((pad, all_ref.shape[1]), BF16)


def _mla_attn_kernel(q_ref, k_ref, v_ref, *rest, tq, tk, q_off, s_valid, hp, past):
    if past:
        kp_ref, vp_ref, o_ref, kall_ref, vall_ref = rest
        _assemble_keys(kp_ref, k_ref, kall_ref, past, tq)
        _assemble_keys(vp_ref, v_ref, vall_ref, past, tq)
        kload = lambda o: kall_ref[pl.ds(o, tk), :]
        vload = lambda o: vall_ref[pl.ds(o, tk), :]
    else:
        (o_ref,) = rest
        kload = lambda o: k_ref[0, pl.ds(o, tk), :]
        vload = lambda o: v_ref[0, pl.ds(o, tk), :]
    i = pl.program_id(2)
    q_lo = q_off + i * tq
    limit = jnp.minimum(q_lo + tq, s_valid)
    nkb = (limit + tk - 1) // tk
    qidx = q_lo + lax.broadcasted_iota(jnp.int32, (tq, tk), 0)
    kloc = lax.broadcasted_iota(jnp.int32, (tq, tk), 1)
    nh = 2 * hp
    q2 = q_ref[0]
    qs = [q2[:, LANES * h:LANES * (h + 1)] for h in range(nh)]

    def body(kb, carry, masked):
        o = pl.multiple_of(kb * tk, tk)
        kblk = kload(o)
        vblk = vload(o)
        ss = [_dot_nt(qs[h], kblk[:, LANES * h:LANES * (h + 1)]) for h in range(nh)]
        if masked:
            kidx = kloc + kb * tk
            vis = jnp.logical_and((kidx >> 6) <= (qidx >> 6), kidx < s_valid)
            ss = [jnp.where(vis, s, -1e30) for s in ss]
        m_new = [jnp.maximum(carry[h][0], jnp.max(ss[h], axis=-1, keepdims=True)) for h in range(nh)]
        ps = [jnp.exp(ss[h] - m_new[h]) for h in range(nh)]
        pv = [_dot(ps[h].astype(BF16), vblk[:, LANES * (h // 2):LANES * (h // 2 + 1)]) for h in range(nh)]
        new = []
        for h in range(nh):
            m_prev, l_prev, acc = carry[h]
            alpha = jnp.exp(m_prev - m_new[h])
            new.append((m_new[h], alpha * l_prev + jnp.sum(ps[h], axis=-1, keepdims=True), alpha * acc + pv[h]))
        return tuple(new)

    init = tuple((jnp.full((tq, 1), -1e30, F32), jnp.zeros((tq, 1), F32), jnp.zeros((tq, LANES), F32))
                 for _ in range(nh))
    nfull = jnp.minimum(((q_lo >> 6) + 1) * CHUNK, s_valid) // tk
    res = lax.fori_loop(0, nfull, lambda kb, c: body(kb, c, False), init)
    res = lax.fori_loop(nfull, nkb, lambda kb, c: body(kb, c, True), res)
    lane = lax.broadcasted_iota(jnp.int32, (tq, LANES), 1)
    outs = [jnp.where(lane < HEAD, res[2 * p][2] / res[2 * p][1], res[2 * p + 1][2] / res[2 * p + 1][1])
            for p in range(hp)]
    o_ref[0] = jnp.concatenate(outs, axis=1).astype(o_ref.dtype)


def _mla_prompt_kernel(q_ref, k_ref, v_ref, o_ref, *, tq):
    i = pl.program_id(2)
    q_lo = i * tq
    nkb = i + 1
    w = 2 * tq
    cidx = lax.broadcasted_iota(jnp.int32, (tq, w), 1)
    qidx = q_lo + jnp.where(cidx >= tq, cidx - tq, cidx)
    kloc = lax.broadcasted_iota(jnp.int32, (tq, w), 0)
    q2 = q_ref[0]
    zq = jnp.zeros((tq, LANES), BF16)
    qbd = jnp.concatenate([jnp.concatenate([q2[:, :LANES], zq], axis=1),
                           jnp.concatenate([zq, q2[:, LANES:]], axis=1)], axis=0)

    def group(kb, carry, n):
        m_prev, l_prev, acc_t = carry
        o = pl.multiple_of((kb - (n - 1)) * tq, tq)
        kn = k_ref[0, pl.ds(o, n * tq), :]
        vn_t = v_ref[0, pl.ds(o, n * tq), :].astype(F32).T.astype(BF16)
        s = _dot_nt(kn, qbd)
        vis = ((kloc + kb * tq) >> 6) <= (qidx >> 6)
        s_r = jnp.where(vis, s[(n - 1) * tq:], -1e30)
        s = jnp.concatenate([s[:(n - 1) * tq], s_r], axis=0) if n > 1 else s_r
        m_new = jnp.maximum(m_prev, jnp.max(s, axis=0, keepdims=True))
        alpha = jnp.exp(m_prev - m_new)
        p = jnp.exp(s - m_new)
        l_new = alpha * l_prev + jnp.sum(p, axis=0, keepdims=True)
        return m_new, l_new, alpha * acc_t + _dot(vn_t, p.astype(BF16))

    carry = (jnp.full((1, w), -1e30, F32), jnp.zeros((1, w), F32), jnp.zeros((LANES, w), F32))
    n4, n2, n1 = nkb // 4, (nkb % 4) // 2, nkb % 2
    carry = lax.fori_loop(0, n4, lambda s, c: group(nkb - 1 - 4 * s, c, 4), carry)
    carry = lax.fori_loop(0, n2, lambda s, c: group(nkb - 1 - 4 * n4, c, 2), carry)
    carry = lax.fori_loop(0, n1, lambda s, c: group(0, c, 1), carry)
    out_t = carry[2] / carry[1]
    row = lax.broadcasted_iota(jnp.int32, (LANES, tq), 0)
    o_ref[0] = jnp.where(row < HEAD, out_t[:, :tq], out_t[:, tq:]).T.astype(o_ref.dtype)


def _mla_prompt(q3, k3, v3, tq):
    b, t, _ = q3.shape
    assert tq % CHUNK == 0 and t % tq == 0
    return pl.pallas_call(
        functools.partial(_mla_prompt_kernel, tq=tq),
        out_shape=jax.ShapeDtypeStruct((b, t, BRANCH), BF16),
        grid=(b, HEADS // 2, t // tq),
        in_specs=[pl.BlockSpec((1, tq, 2 * LANES), lambda bi, j, i: (bi, i, j)),
                  pl.BlockSpec((1, t, 2 * LANES), lambda bi, j, i: (bi, 0, j)),
                  pl.BlockSpec((1, t, LANES), lambda bi, j, i: (bi, 0, j))],
        out_specs=pl.BlockSpec((1, tq, LANES), lambda bi, j, i: (bi, i, j)),
        compiler_params=_cparams(("parallel", "parallel", "arbitrary"), 40),
        name="mla_prompt",
    )(q3, k3, v3)


def _mla_attn(q3, k3, v3, kp3, vp3, s_pad, tq, tk, hp):
    b, t, _ = q3.shape
    past = 0 if kp3 is None else kp3.shape[1]
    rows = k3.shape[1]
    in_specs = [pl.BlockSpec((1, tq, 2 * LANES * hp), lambda bi, j, i: (bi, i, j)),
                pl.BlockSpec((1, rows, 2 * LANES * hp), lambda bi, j, i: (bi, 0, j)),
                pl.BlockSpec((1, rows, LANES * hp), lambda bi, j, i: (bi, 0, j))]
    args = [q3, k3, v3]
    scratch = []
    if past:
        in_specs += [pl.BlockSpec((1, past, 2 * LANES * hp), lambda bi, j, i: (bi, 0, j)),
                     pl.BlockSpec((1, past, LANES * hp), lambda bi, j, i: (bi, 0, j))]
        args += [kp3, vp3]
        scratch = [pltpu.VMEM((s_pad, 2 * LANES * hp), BF16), pltpu.VMEM((s_pad, LANES * hp), BF16)]
    return pl.pallas_call(
        functools.partial(_mla_attn_kernel, tq=tq, tk=tk, q_off=past, s_valid=past + t, hp=hp, past=past),
        out_shape=jax.ShapeDtypeStruct((b, t, BRANCH), BF16),
        grid=(b, 4 // hp, t // tq),
        in_specs=in_specs,
        out_specs=pl.BlockSpec((1, tq, LANES * hp), lambda bi, j, i: (bi, i, j)),
        scratch_shapes=scratch,
        compiler_params=_cparams(("parallel", "parallel", "arbitrary"), 40),
        name="mla_attn",
    )(*args)


def _sb_block(zs, vis, us, carry, pv):
    nh = len(zs)
    tq = zs[0].shape[0]
    zls = [jnp.minimum(z, 0.0) - jnp.log(1.0 + jnp.exp(-jnp.abs(z))) for z in zs]
    lgs = [zl - z for zl, z in zip(zls, zs)]
    if vis is not None:
        lgs = [jnp.where(vis, lg, 0.0) for lg in lgs]
    his = [lg.astype(BF16) for lg in lgs]
    los = [(lg - hi.astype(F32)).astype(BF16) for lg, hi in zip(lgs, his)]
    later = _dot(jnp.concatenate(his + los, axis=0), us)
    new = []
    for h in range(nh):
        csum, acc = carry[h]
        tot = later[h * tq:(h + 1) * tq] + later[(nh + h) * tq:(nh + h + 1) * tq] + csum
        a = jnp.exp(zls[h] + tot)
        if vis is not None:
            a = jnp.where(vis, a, 0.0)
        new.append((csum + jnp.sum(lgs[h], axis=-1, keepdims=True), acc + pv(h, a.astype(BF16))))
    return tuple(new)


def _sb_heads(q2, nh):
    lane1 = lax.broadcasted_iota(jnp.int32, (1, LANES), 1)
    hm = ((lane1 < HEAD).astype(F32) * SB_SCALE, (lane1 >= HEAD).astype(F32) * SB_SCALE)
    return [(q2[:, LANES * (h // 2):LANES * (h // 2 + 1)] * hm[h % 2]).astype(BF16) for h in range(nh)]


def _sb_finish(res, o_ref, tq):
    lane = lax.broadcasted_iota(jnp.int32, (tq, LANES), 1)
    outs = [jnp.where(lane < HEAD, res[2 * p][1], res[2 * p + 1][1]) for p in range(len(res) // 2)]
    o_ref[0] = jnp.concatenate(outs, axis=1).astype(o_ref.dtype)


def _sb_attn_kernel(q_ref, k_ref, v_ref, us_ref, o_ref, *, tq, tk):
    i = pl.program_id(2)
    q_lo = i * tq
    nkb = (q_lo + tq - 2) // tk + 1
    cidx = lax.broadcasted_iota(jnp.int32, (tk, 2 * tq), 1)
    qidx = q_lo + jnp.where(cidx >= tq, cidx - tq, cidx)
    kloc = lax.broadcasted_iota(jnp.int32, (tk, 2 * tq), 0)
    qs = jnp.concatenate(_sb_heads(q_ref[0], 2), axis=0)
    us = us_ref[...]

    w = 2 * tq

    def group(kb, carry, n):
        csum, acc_t = carry
        o = pl.multiple_of((kb - (n - 1)) * tk, tk)
        kn = k_ref[0, pl.ds(o, n * tk), :].astype(BF16)
        vn_t = v_ref[0, pl.ds(o, n * tk), :].T.astype(BF16)
        z = _dot_nt(kn, qs)
        zl = jnp.minimum(z, 0.0) - jnp.log(1.0 + jnp.exp(-jnp.abs(z)))
        lg = zl - z
        vis = (kloc + kb * tk) < qidx
        lgs = [lg[j * tk:(j + 1) * tk] for j in range(n - 1)] + [jnp.where(vis, lg[(n - 1) * tk:], 0.0)]
        his = [x.astype(BF16) for x in lgs]
        los = [(x - h.astype(F32)).astype(BF16) for x, h in zip(lgs, his)]
        later = _dot(us, jnp.concatenate([jnp.concatenate([his[j], los[j]], axis=0) for j in range(n)], axis=1))
        tots = [None] * n
        for j in reversed(range(n)):
            tots[j] = later[:, j * w:(j + 1) * w] + csum
            csum = csum + jnp.sum(lgs[j], axis=0, keepdims=True)
        a = jnp.exp(zl + jnp.concatenate(tots, axis=0))
        a = jnp.concatenate([a[:(n - 1) * tk], jnp.where(vis, a[(n - 1) * tk:], 0.0)], axis=0) if n > 1 \
            else jnp.where(vis, a, 0.0)
        return csum, acc_t + _dot(vn_t, a.astype(BF16))

    carry = (jnp.zeros((1, 2 * tq), F32), jnp.zeros((LANES, 2 * tq), F32))
    n4, n2, n1 = nkb // 4, (nkb % 4) // 2, nkb % 2
    carry = lax.fori_loop(0, n4, lambda s, c: group(nkb - 1 - 4 * s, c, 4), carry)
    carry = lax.fori_loop(0, n2, lambda s, c: group(nkb - 1 - 4 * n4, c, 2), carry)
    carry = lax.fori_loop(0, n1, lambda s, c: group(0, c, 1), carry)
    acc_t = carry[1]
    row = lax.broadcasted_iota(jnp.int32, (LANES, tq), 0)
    o_ref[0] = jnp.where(row < HEAD, acc_t[:, :tq], acc_t[:, tq:]).T.astype(o_ref.dtype)


def _sb_decode_kernel(q_ref, k_ref, v_ref, kpt_ref, vpt_ref, usn_ref, usp_ref, o_ref, *, t, past, pb):
    qh = _sb_heads(q_ref[0], HEADS)
    kn = k_ref[0].astype(BF16)
    vn = v_ref[0].astype(BF16)
    pair = lambda x, h: x[:, LANES * (h // 2):LANES * (h // 2 + 1)]
    ri = lax.broadcasted_iota(jnp.int32, (t, t), 0)
    ci = lax.broadcasted_iota(jnp.int32, (t, t), 1)
    carry = tuple((jnp.zeros((t, 1), F32), jnp.zeros((t, LANES), F32)) for _ in range(HEADS))
    zs = [_dot_nt(qh[h], pair(kn, h)) for h in range(HEADS)]
    carry = _sb_block(zs, ci < ri, usn_ref[...], carry, lambda h, a: _dot(a, pair(vn, h)))
    usp = usp_ref[...]
    for blk in reversed(range(past // pb)):
        kts = [kpt_ref[0, 0, p, :, blk * pb:(blk + 1) * pb].astype(BF16) for p in range(HEADS // 2)]
        vts = [vpt_ref[0, 0, p, :, blk * pb:(blk + 1) * pb].astype(BF16) for p in range(HEADS // 2)]
        zs = [_dot(qh[h], kts[h // 2]) for h in range(HEADS)]
        carry = _sb_block(zs, None, usp, carry, lambda h, a: _dot_nt(a, vts[h // 2]))
    _sb_finish(carry, o_ref, t)


def _strict_upper(n):
    idx = jnp.arange(n)
    return (idx[:, None] > idx[None, :]).astype(BF16)


def _sb_attn(proj3, tq, tk):
    b, t, _ = proj3.shape
    assert tq == tk
    u = _strict_upper(tk).T
    us = jnp.concatenate([u, u], axis=1)
    return pl.pallas_call(
        functools.partial(_sb_attn_kernel, tq=tq, tk=tk),
        out_shape=jax.ShapeDtypeStruct((b, t, BRANCH), BF16),
        grid=(b, HEADS // 2, t // tq),
        in_specs=[pl.BlockSpec((1, tq, LANES), lambda bi, j, i: (bi, i, O_SBQ // LANES + j)),
                  pl.BlockSpec((1, t, LANES), lambda bi, j, i: (bi, 0, O_SBK // LANES + j)),
                  pl.BlockSpec((1, t, LANES), lambda bi, j, i: (bi, 0, O_SBV // LANES + j)),
                  pl.BlockSpec(us.shape, lambda bi, j, i: (0, 0))],
        out_specs=pl.BlockSpec((1, tq, LANES), lambda bi, j, i: (bi, i, j)),
        compiler_params=_cparams(("parallel", "parallel", "arbitrary"), 48),
        name="sb_attn",
    )(proj3, proj3, proj3, us)


def _sb_decode(proj3, kpt, vpt, layer, pb):
    b, t, _ = proj3.shape
    past = kpt.shape[4]
    usn, usp = _strict_upper(t), _strict_upper(pb)
    new = lambda off: pl.BlockSpec((1, t, BRANCH), lambda bi, o=off // BRANCH: (bi, 0, o))
    cache = pl.BlockSpec((1, 1, HEADS // 2, LANES, past), lambda bi: (layer, bi, 0, 0, 0))
    return pl.pallas_call(
        functools.partial(_sb_decode_kernel, t=t, past=past, pb=pb),
        out_shape=jax.ShapeDtypeStruct((b, t, BRANCH), BF16),
        grid=(b,),
        in_specs=[new(O_SBQ), new(O_SBK), new(O_SBV), cache, cache,
                  pl.BlockSpec(usn.shape, lambda bi: (0, 0)), pl.BlockSpec(usp.shape, lambda bi: (0, 0))],
        out_specs=pl.BlockSpec((1, t, BRANCH), lambda bi: (bi, 0, 0)),
        compiler_params=_cparams(("parallel",), 48),
        name="sb_decode",
    )(proj3, proj3, proj3, kpt, vpt, usn, usp)


def _mix_kernel(yrw_ref, yssm_ref, ymla_ref, ysb_ref, gate_ref, pm_ref, wb_ref, bm_ref, o_ref, acc_ref):
    n = pl.program_id(1)

    @pl.when(n == 0)
    def _():
        acc_ref[...] = jnp.zeros_like(acc_ref)

    def branch(y_ref):
        g = gate_ref[...].astype(F32)
        gated_half = y_ref[...].astype(F32) * (g * (0.5 * _sigmoid(g)))
        up_half = _dot(gated_half.astype(BF16), wb_ref[0, 0].astype(BF16))
        acc_ref[...] += (jnp.tanh(0.5 * (pm_ref[...].astype(F32) + bm_ref[0])) + 1.0) * up_half

    for idx, y_ref in enumerate((yrw_ref, yssm_ref, ymla_ref, ysb_ref)):
        pl.when(n == idx)(functools.partial(branch, y_ref))

    @pl.when(n == N_BRANCH - 1)
    def _():
        o_ref[...] = acc_ref[...].astype(BF16)


def _out_kernel(m_ref, x_ref, wo_ref, fg_ref, o_ref, *, final):
    xn = x_ref[...] + _dot(m_ref[...], wo_ref[0].astype(BF16))
    if final:
        xn = xn * lax.rsqrt(jnp.mean(xn * xn, axis=-1, keepdims=True) + EPS) * fg_ref[...]
    o_ref[...] = xn


def _merge(ys, proj, x2, wb, bm, wo, fg, final, layer):
    n, d = x2.shape
    tm = min(n, 1024)
    ysp = pl.BlockSpec((tm, BRANCH), lambda i, k: (i, 0))
    mixed = pl.pallas_call(
        _mix_kernel,
        out_shape=jax.ShapeDtypeStruct((n, d), BF16),
        grid=(n // tm, N_BRANCH),
        in_specs=[ysp, ysp, ysp, ysp,
                  pl.BlockSpec((tm, BRANCH), lambda i, k: (i, O_GATE // BRANCH + k)),
                  pl.BlockSpec((tm, d), lambda i, k: (i, O_MERGE // d + k)),
                  pl.BlockSpec((1, 1, BRANCH, d), lambda i, k: (layer, k, 0, 0)),
                  pl.BlockSpec((1, 1, d), lambda i, k: (k, 0, 0))],
        out_specs=pl.BlockSpec((tm, d), lambda i, k: (i, 0)),
        scratch_shapes=[pltpu.VMEM((tm, d), F32)],
        compiler_params=_cparams(("parallel", "arbitrary"), 56),
        name="branch_mix",
    )(*ys, proj, proj, wb, bm)
    to = min(n, 512)
    return pl.pallas_call(
        functools.partial(_out_kernel, final=final),
        out_shape=jax.ShapeDtypeStruct((n, d), F32),
        grid=(n // to,),
        in_specs=[pl.BlockSpec((to, d), lambda i: (i, 0)),
                  pl.BlockSpec((to, d), lambda i: (i, 0)),
                  pl.BlockSpec((1, d, d), lambda i: (layer, 0, 0), pipeline_mode=pl.Buffered(1)),
                  pl.BlockSpec((1, d), lambda i: (0, 0))],
        out_specs=pl.BlockSpec((to, d), lambda i: (i, 0)),
        compiler_params=_cparams(("parallel",), 48),
        name="out_proj",
    )(mixed, x2, wo, fg)


_SRC_SSM = RW_IN
_SRC_QLAT = _SRC_SSM + BRANCH
_SRC_KVLAT = _SRC_QLAT + MLA_Q_LORA
_SRC_KPE = _SRC_KVLAT + MLA_KV_LORA
_SRC_SB = _SRC_KPE + MLA_ROPE
_SRC_GATE = _SRC_SB + 3 * BRANCH
_SRC_MERGE = _SRC_GATE + N_BRANCH * BRANCH
_SRC_LORA = 3 * BRANCH
_PW_BLK = 4 * LANES
_PW_NM = N_BRANCH * D_MODEL // _PW_BLK
_PW_NG = _PW_NM + N_BRANCH * BRANCH // _PW_BLK
_PW_ROPE = _PW_NG + 1
_PW_TAIL = {_PW_NG: (_SRC_QLAT, _SRC_QLAT + LANES, _SRC_QLAT + 2 * LANES, _SRC_LORA),
            _PW_ROPE: (_SRC_KVLAT, _SRC_KVLAT + LANES, _SRC_KPE, _SRC_KPE)}


def _permute_src_col(j, k):
    sb0 = _PW_ROPE + 1
    rw0 = sb0 + 3 * BRANCH // _PW_BLK
    ssm0 = rw0 + 3 * BRANCH // _PW_BLK
    u = MLA_ROPE
    col = jnp.where(j < _PW_NM, _SRC_MERGE // u + _PW_BLK // u * j,
                    jnp.where(j < _PW_NG, _SRC_GATE // u + _PW_BLK // u * (j - _PW_NM),
                              jnp.where(j < rw0, _SRC_SB // u + _PW_BLK // u * (j - sb0),
                                        jnp.where(j < ssm0, _PW_BLK // u * (j - rw0), _SRC_SSM // u))))
    col = col + LANES // u * k
    for jj, src in _PW_TAIL.items():
        col = jnp.where(j == jj, src[k] // u, col)
    return col * u


def _permute_kernel(x0_ref, x1_ref, x2_ref, x3_ref, o_ref):
    j = pl.program_id(1)
    wins = (x0_ref, x1_ref, x2_ref, x3_ref)
    half = MLA_ROPE // 2

    def put(k, rows):
        o_ref[0, :, LANES * k:LANES * (k + 1)] = rows.T.astype(BF16)

    @pl.when(j != _PW_ROPE)
    def _():
        for k in range(4):
            put(k, wins[k][0])

    @pl.when(j == _PW_ROPE)
    def _():
        for k in range(2):
            put(k, wins[k][0])
        kpe = wins[2][0]
        row = lax.broadcasted_iota(jnp.int32, kpe.shape, 0)
        put(2, jnp.where(row < MLA_ROPE, kpe, 0.0))
        zeros = jnp.zeros((LANES - MLA_ROPE, kpe.shape[1]), F32)
        put(3, jnp.concatenate([-kpe[half:MLA_ROPE], kpe[:half], zeros], axis=0))


def _permute_w_in(w_in):
    depth, d, _ = w_in.shape
    wt = jnp.swapaxes(w_in, 1, 2)
    win = lambda k: pl.BlockSpec((pl.Element(1), pl.Element(LANES), pl.Element(d)),
                                 lambda l, j, k=k: (l, _permute_src_col(j, k), 0))
    return pl.pallas_call(
        _permute_kernel,
        out_shape=jax.ShapeDtypeStruct((depth, d, PROJ_W), BF16),
        grid=(depth, PROJ_W // _PW_BLK),
        in_specs=[win(k) for k in range(4)],
        out_specs=pl.BlockSpec((1, d, _PW_BLK), lambda l, j: (l, 0, j)),
        compiler_params=_cparams(("parallel", "parallel"), 32),
        name="permute_w_in",
    )(wt, wt, wt, wt)


def _rope_tables(pos):
    half = MLA_ROPE // 2
    inv = ROPE_BASE ** (-jnp.arange(half, dtype=F32) / half)
    ang = pos.astype(F32)[:, None] * inv
    cos, sin = jnp.cos(ang), jnp.sin(ang)
    t = pos.shape[0]
    cc = jnp.concatenate([cos, cos], axis=1)
    ss = jnp.concatenate([sin, sin], axis=1)
    tq = jnp.concatenate([jnp.ones((t, MLA_NOPE), F32), cc, ss], axis=1) * MLA_SCALE
    pad = jnp.zeros((t, LANES - MLA_ROPE), F32)
    return tq, jnp.concatenate([cc, pad], axis=1), jnp.concatenate([ss, pad], axis=1)


def _mla_weights(w_q_up, w_kv_up):
    half = MLA_ROPE // 2
    wq = w_q_up.reshape(MLA_Q_LORA, HEADS, MLA_NOPE + MLA_ROPE)
    x1 = wq[:, :, MLA_NOPE:MLA_NOPE + half]
    x2 = wq[:, :, MLA_NOPE + half:]
    wq_p = jnp.concatenate([wq, -x2, x1], axis=2).reshape(MLA_Q_LORA, HEADS * LANES).astype(BF16)
    wkv = w_kv_up.reshape(MLA_KV_LORA, HEADS, 2 * HEAD)
    wk_p = jnp.concatenate([wkv[:, :, :HEAD], jnp.zeros((MLA_KV_LORA, HEADS, HEAD), F32)], axis=2)
    wk_p = wk_p.reshape(MLA_KV_LORA, HEADS * LANES).astype(BF16)
    wv_p = wkv[:, :, HEAD:].reshape(MLA_KV_LORA, BRANCH).astype(BF16)
    return wq_p, wk_p, wv_p


def _s5_tables(lam_re, lam_im, log_dt, b_re, b_im, c_re, c_im):
    dt = jnp.exp(log_dt)[:, None]
    mag = jnp.exp(lam_re * dt)
    ang = lam_im * dt
    lb_re, lb_im = mag * jnp.cos(ang), mag * jnp.sin(ang)
    nr, ni = lb_re - 1.0, lb_im
    den = lam_re * lam_re + lam_im * lam_im
    f_re = (nr * lam_re + ni * lam_im) / den
    f_im = (ni * lam_re - nr * lam_im) / den
    bb_re = f_re[..., None] * b_re - f_im[..., None] * b_im
    bb_im = f_re[..., None] * b_im + f_im[..., None] * b_re
    eye = jnp.eye(SSM_GROUPS, dtype=F32)
    blk_in = lambda m: jnp.einsum('gpc,gh->gchp', m, eye).reshape(BRANCH, SSM_W)
    blk_out = lambda m: jnp.einsum('gcp,gh->gphc', m, eye).reshape(SSM_W, BRANCH)
    bblk = jnp.concatenate([blk_in(bb_re), blk_in(bb_im)], axis=1).astype(BF16)
    cblk = jnp.concatenate([blk_out(c_re), blk_out(-c_im)], axis=0).astype(BF16)

    def power(j):
        m = jnp.exp(lam_re * dt * j)
        return (m * jnp.cos(ang * j)).reshape(1, SSM_W), (m * jnp.sin(ang * j)).reshape(1, SSM_W)

    lam = jnp.concatenate([p for j in (1, 2, 4) for p in power(j)] + [jnp.zeros((2, SSM_W), F32)], axis=0)
    pws = [power(j) for j in range(1, 9)]
    pw = jnp.concatenate([p[0] for p in pws] + [p[1] for p in pws], axis=0)
    return bblk, cblk, lam, pw


def _layer(x, pos0, st, lw, final_g, final):
    b, t, d = x.shape
    n = b * t
    x2 = x.reshape(n, d)
    proj_b, proj = _norm_proj(x2, lw["norm_g"], lw["w_in_all"], lw["layer"])
    proj3 = proj.reshape(b, t, PF_W)
    past = st["past"]

    chunk = min(t, CHUNK)
    y_rw, wkv_new = _rwkv(proj3, st["sp_r"], st["sp_k"], st["sp_v"], st["sp_l"], st["wkv"],
                        lw["rw_pvec"], lw["rw_mu_l"], lw["rw_w2p"], lw["rw_a2p"], lw["g2"], chunk)
    shift_new = jnp.concatenate([proj3[:, t - 1:, O_RWR:O_RWR + 3 * BRANCH],
                                 proj3[:, t - 1:, O_LORA:O_LORA + 2 * RW_LORA]], axis=-1)

    y_ssm, hr, hi = _s5(proj3, st["ssm_re"], st["ssm_im"], lw["ssm_bblk"], lw["ssm_cblk"], lw["ssm_lam"],
                        lw["ssm_pw"], lw["ssm_d"], lw["ssm_wglu"], lw["ssm_bglu"])
    ssm_re_new = hr.reshape(b, SSM_GROUPS, SSM_STATE)
    ssm_im_new = hi.reshape(b, SSM_GROUPS, SSM_STATE)

    pos = pos0 + jnp.arange(t, dtype=jnp.int32)
    tq_tab, tc_tab, ts_tab = _rope_tables(pos)
    q, ckv, kr = _mla_prep(proj, t, tq_tab, tc_tab, ts_tab, lw["mla_gq"], lw["mla_gkv"], lw["mla_wq"])
    ckv3 = ckv.reshape(b, t, MLA_KV_LORA)
    kr3 = kr.reshape(b, t, MLA_ROPE)
    s_pad = -(-(past + t) // LANES) * LANES
    kc, vv = _mla_kv(ckv[None], kr[None], lw["mla_wk"], lw["mla_wv"], lw["dup"])
    kc3, vv3 = kc.reshape(b, t, HEADS * LANES), vv.reshape(b, t, BRANCH)
    kcp3 = vvp3 = None
    if past:
        kcp, vvp = _mla_kv(st["ckv_all"], st["kpe_all"], lw["mla_wk"], lw["mla_wv"], lw["dup"], lw["layer"])
        kcp3, vvp3 = kcp.reshape(b, past, HEADS * LANES), vvp.reshape(b, past, BRANCH)
    small = t <= 64
    assert small or not past
    tq = min(t, 256)
    hp = 4 if small else 1
    q3 = q.reshape(b, t, HEADS * LANES)
    if small:
        y_mla = _mla_attn(q3, kc3, vv3, kcp3, vvp3, s_pad, tq, s_pad, hp)
    else:
        y_mla = _mla_prompt(q3, kc3, vv3, tq)

    sbk = proj3[:, :, O_SBK:O_SBK + BRANCH]
    sbv = proj3[:, :, O_SBV:O_SBV + BRANCH]
    if past:
        y_sb = _sb_decode(proj3, st["sbk_t"], st["sbv_t"], lw["layer"], min(256, past))
    else:
        y_sb = _sb_attn(proj3, tq, min(256, t))

    ys = [y.reshape(n, BRANCH) for y in (y_rw, y_ssm, y_mla, y_sb)]
    x_new = _merge(ys, proj_b, x2, lw["w_branch"], lw["b_merge"], lw["w_out"], final_g, final,
                   lw["layer"]).reshape(b, t, d)
    new_state = (shift_new, wkv_new, ssm_re_new, ssm_im_new, ckv3, kr3,
                 sbk.reshape(b, t, HEADS, HEAD), sbv.reshape(b, t, HEADS, HEAD))
    return x_new, new_state


def kernel(x_prompt, x_sample, state_rwkv_shift, state_rwkv_wkv, state_ssm_re, state_ssm_im, cache_mla_ckv, cache_mla_kpe, cache_sb_k, cache_sb_v, norm_g, w_in, rw_mu, rw_w0, rw_w2, rw_a0, rw_a2, rw_k_k, rw_k_a, rw_r_k, rw_lnx_g, rw_lnx_b, ssm_lam_re, ssm_lam_im, ssm_log_dt, ssm_b_re, ssm_b_im, ssm_c_re, ssm_c_im, ssm_d, ssm_w_glu, ssm_b_glu, mla_q_norm, mla_w_q_up, mla_kv_norm, mla_w_kv_up, w_branch, b_merge, w_out, final_norm_g):
    depth = w_in.shape[0]
    bp, tp, _ = x_prompt.shape
    bs = x_sample.shape[0]
    past = cache_mla_ckv.shape[2]

    w_in_p = _permute_w_in(w_in)
    lane_i = jnp.arange(LANES)
    g2 = ((lane_i[:, None] // HEAD) == (lane_i[None, :] // HEAD)).astype(BF16)
    rope_i = jnp.arange(MLA_ROPE)
    dup = ((lane_i[None, :] == rope_i[:, None] + MLA_NOPE)
           | (lane_i[None, :] == rope_i[:, None] + MLA_NOPE + MLA_ROPE)).astype(BF16)
    final_g = final_norm_g.reshape(1, D_MODEL)
    zpad = jnp.zeros((RW_LORA, BRANCH), F32)

    layers = []
    for l in range(depth):
        wq_p, wk_p, wv_p = _mla_weights(mla_w_q_up[l], mla_w_kv_up[l])
        bblk, cblk, lam, pw = _s5_tables(ssm_lam_re[l], ssm_lam_im[l], ssm_log_dt[l], ssm_b_re[l], ssm_b_im[l],
                                         ssm_c_re[l], ssm_c_im[l])
        mu = rw_mu[l]
        rows = [mu[:BRANCH], mu[BRANCH:2 * BRANCH], mu[2 * BRANCH:3 * BRANCH], rw_w0[l], rw_a0[l], rw_k_k[l],
                rw_k_a[l], rw_r_k[l].reshape(BRANCH), rw_lnx_g[l], rw_lnx_b[l]]
        pvec = jnp.concatenate([jnp.stack(rows), jnp.zeros((16 - len(rows), BRANCH), F32)], axis=0)
        layers.append(dict(
            norm_g=norm_g[l].reshape(1, D_MODEL), w_in_all=w_in_p, layer=l,
            rw_pvec=pvec, rw_mu_l=mu[3 * BRANCH:].reshape(1, 2 * RW_LORA),
            rw_w2p=jnp.concatenate([rw_w2[l], zpad], axis=0).astype(BF16),
            rw_a2p=jnp.concatenate([zpad, rw_a2[l]], axis=0).astype(BF16),
            g2=g2, dup=dup,
            ssm_bblk=bblk, ssm_cblk=cblk, ssm_lam=lam, ssm_pw=pw,
            ssm_d=ssm_d[l].reshape(1, BRANCH), ssm_wglu=ssm_w_glu[l].astype(BF16),
            ssm_bglu=ssm_b_glu[l].reshape(1, BRANCH),
            mla_gq=mla_q_norm[l].reshape(1, MLA_Q_LORA), mla_gkv=mla_kv_norm[l].reshape(1, MLA_KV_LORA),
            mla_wq=wq_p, mla_wk=wk_p, mla_wv=wv_p,
            w_branch=w_branch, b_merge=b_merge[l].reshape(N_BRANCH, 1, D_MODEL), w_out=w_out))

    def fresh(bn):
        return dict(sp_r=jnp.zeros((bn, 1, BRANCH), F32), sp_k=jnp.zeros((bn, 1, BRANCH), F32),
                    sp_v=jnp.zeros((bn, 1, BRANCH), F32), sp_l=jnp.zeros((bn, 1, 2 * RW_LORA), F32),
                    wkv=jnp.zeros((bn, HEADS, HEAD, HEAD), F32),
                    ssm_re=jnp.zeros((bn, 1, SSM_W), F32), ssm_im=jnp.zeros((bn, 1, SSM_W), F32),
                    past=0)

    nl = cache_mla_ckv.shape[0]
    ckv_all = cache_mla_ckv.reshape(nl, bs * past, MLA_KV_LORA)
    kpe_all = cache_mla_kpe.reshape(nl, bs * past, MLA_ROPE)
    sbk_t = jnp.transpose(cache_sb_k, (0, 1, 3, 4, 2)).reshape(nl, bs, HEADS // 2, LANES, past)
    sbv_t = jnp.transpose(cache_sb_v, (0, 1, 3, 4, 2)).reshape(nl, bs, HEADS // 2, LANES, past)

    def carried(l):
        sh = state_rwkv_shift[l]
        return dict(sp_r=sh[:, :, :BRANCH], sp_k=sh[:, :, BRANCH:2 * BRANCH], sp_v=sh[:, :, 2 * BRANCH:3 * BRANCH],
                    sp_l=sh[:, :, 3 * BRANCH:], wkv=state_rwkv_wkv[l],
                    ssm_re=state_ssm_re[l].reshape(bs, 1, SSM_W), ssm_im=state_ssm_im[l].reshape(bs, 1, SSM_W),
                    past=past, ckv_all=ckv_all, kpe_all=kpe_all, sbk_t=sbk_t, sbv_t=sbv_t)

    xp, xs = x_prompt, x_sample
    new_p, new_s = [], []
    for l in range(depth):
        last = l == depth - 1
        xp, st_p = _layer(xp, 0, fresh(bp), layers[l], final_g, last)
        xs, st_s = _layer(xs, past, carried(l), layers[l], final_g, last)
        new_p.append(st_p)
        new_s.append(st_s)
    stk = lambda lst, i: jnp.stack([s[i] for s in lst], axis=0)
    return (xp, xs) + tuple(stk(new_p, i) for i in range(8)) + tuple(stk(new_s, i) for i in range(8))
```

```python
import functools
import math

import jax
import jax.numpy as jnp
from jax import lax
from jax.experimental import pallas as pl
from jax.experimental.pallas import tpu as pltpu

F32 = jnp.float32
BF16 = jnp.bfloat16

D_MODEL = 2048
BRANCH = 512
N_BRANCH = 4
EPS = 1e-6
CHUNK = 64
HEAD = 64
HEADS = 8
RW_LORA = 64
RW_IN = 3 * BRANCH + 2 * RW_LORA
RW_GN_EPS = 64e-5
SSM_GROUP = 16
SSM_GROUPS = 32
SSM_STATE = 64
SSM_W = SSM_GROUPS * SSM_STATE
MLA_NOPE = 64
MLA_ROPE = 32
MLA_Q_LORA = 384
MLA_KV_LORA = 256
MLA_SCALE = 1.0 / math.sqrt(MLA_NOPE + MLA_ROPE)
ROPE_BASE = 10000.0
SB_SCALE = 1.0 / math.sqrt(HEAD)
LOG2E = math.log2(math.e)
LANES = 128
MIB = 1024 * 1024

O_MERGE = 0
O_GATE = 8192
PB_W = 10240
O_QLAT = 0
O_LORA = 384
O_KVLAT = 512
O_KPE = 768
O_KPESW = 896
O_SBQ = 1024
O_SBK = 1536
O_SBV = 2048
O_RWR = 2560
O_RWK = 3072
O_RWV = 3584
O_SSM = 4096
PF_W = 4608
PROJ_W = PB_W + PF_W


def _cparams(sem, vmem_mib):
    return pltpu.CompilerParams(dimension_semantics=sem, vmem_limit_bytes=vmem_mib * MIB)


def _dot(a, b):
    return jnp.dot(a, b, preferred_element_type=F32)


def _dot_nt(a, b):
    return lax.dot_general(a, b, (((1,), (1,)), ((), ())), preferred_element_type=F32)


def _split3(x):
    h1 = x.astype(BF16)
    r1 = x - h1.astype(F32)
    h2 = r1.astype(BF16)
    h3 = (r1 - h2.astype(F32)).astype(BF16)
    return h1, h2, h3


def _softplus(x):
    return jnp.maximum(x, 0.0) + jnp.log1p(jnp.exp(-jnp.abs(x)))


def _sigmoid(x):
    return 0.5 * jnp.tanh(0.5 * x) + 0.5


def _norm_proj_kernel(x_ref, g_ref, w_ref, ob_ref, of_ref, h_ref, *, nb16):
    j = pl.program_id(1)

    @pl.when(j == 0)
    def _():
        x = x_ref[...]
        ms = jnp.mean(x * x, axis=-1, keepdims=True)
        h_ref[...] = (x * lax.rsqrt(ms + EPS) * g_ref[...]).astype(BF16)

    @pl.when(j < nb16)
    def _():
        ob_ref[...] = _dot(h_ref[...], w_ref[0]).astype(BF16)

    @pl.when(j >= nb16)
    def _():
        of_ref[...] = _dot(h_ref[...], w_ref[0])


def _norm_proj(x2, g, w_all, layer):
    n, d = x2.shape
    tm = min(n, 2048)
    tn = 512
    nb16 = PB_W // tn
    return pl.pallas_call(
        functools.partial(_norm_proj_kernel, nb16=nb16),
        out_shape=(jax.ShapeDtypeStruct((n, PB_W), BF16), jax.ShapeDtypeStruct((n, PF_W), F32)),
        grid=(n // tm, PROJ_W // tn),
        in_specs=[
            pl.BlockSpec((tm, d), lambda i, j: (i, 0), pipeline_mode=pl.Buffered(1)),
            pl.BlockSpec((1, d), lambda i, j: (0, 0)),
            pl.BlockSpec((1, d, tn), lambda i, j: (layer, 0, j)),
        ],
        out_specs=(pl.BlockSpec((tm, tn), lambda i, j: (i, jnp.minimum(j, nb16 - 1))),
                   pl.BlockSpec((tm, tn), lambda i, j: (i, jnp.maximum(j - nb16, 0)))),
        scratch_shapes=[pltpu.VMEM((tm, d), BF16)],
        compiler_params=_cparams(("parallel", "arbitrary"), 48),
        name="norm_proj",
    )(x2, g, w_all)


def _segsum(x, g2):
    outs = []
    for j in range(BRANCH // LANES):
        h1, h2, h3 = _split3(x[:, LANES * j:LANES * (j + 1)])
        outs.append(_dot(h1, g2) + _dot(h2, g2) + _dot(h3, g2))
    return jnp.concatenate(outs, axis=1)


def _rwkv_kernel(pr_ref, pk_ref, pv_ref, plo_ref, spr_ref, spk_ref, spv_ref, spl_ref, s0_ref,
                 pvec_ref, mul_ref, w2_ref, a2_ref, g2_ref,
                 y_ref, sout_ref,
                 st_ref, cr_ref, ck_ref, cv_ref, cl_ref, *, C, nb):
    c = pl.program_id(1)

    @pl.when(c == 0)
    def _():
        zero = jnp.zeros((HEAD, HEAD), F32)
        for bi in range(nb):
            for j in range(HEADS // 2):
                st_ref[bi, j] = jnp.concatenate(
                    [jnp.concatenate([s0_ref[bi, 2 * j], zero], axis=1),
                     jnp.concatenate([zero, s0_ref[bi, 2 * j + 1]], axis=1)], axis=0)
        cr_ref[...] = spr_ref[...]
        ck_ref[...] = spk_ref[...]
        cv_ref[...] = spv_ref[...]
        cl_ref[...] = spl_ref[...]

    pvec = pvec_ref[...]
    prow = lambda i: pvec[i:i + 1, :]
    g2 = g2_ref[...]

    def tshift(p_ref, carry_ref, mu):
        outs = []
        for bi in range(nb):
            p = p_ref[bi]
            rolled = pltpu.roll(p, 1, 0)
            rid = lax.broadcasted_iota(jnp.int32, p.shape, 0)
            prev = jnp.where(rid == 0, carry_ref[bi], rolled)
            carry_ref[bi] = p[C - 1:C, :]
            outs.append(p + (prev - p) * mu)
        return jnp.concatenate(outs, axis=0)

    r = tshift(pr_ref, cr_ref, prow(0))
    k = tshift(pk_ref, ck_ref, prow(1))
    v = tshift(pv_ref, cv_ref, prow(2))
    lo = tshift(plo_ref, cl_ref, mul_ref[...])

    log_w = -_softplus(-(prow(3) + _dot(jnp.tanh(lo).astype(BF16), w2_ref[...]))) - 0.5
    ld = -jnp.exp(log_w)
    a_icl = jax.nn.sigmoid(prow(4) + _dot(lo.astype(BF16), a2_ref[...]))
    kkr = k * prow(5)
    kk = kkr / jnp.maximum(jnp.sqrt(_segsum(kkr * kkr, g2)), 1e-12)
    k2 = k * (1.0 + (a_icl - 1.0) * prow(6))
    av = -kk
    bv = kk * a_icl

    sh = C.bit_length() - 1
    ri = lax.broadcasted_iota(jnp.int32, (nb * C, nb * C), 0)
    ci = lax.broadcasted_iota(jnp.int32, (nb * C, nb * C), 1)
    lincl = jnp.logical_and(ci <= ri, (ci >> sh) == (ri >> sh)).astype(BF16)
    h1, h2, h3 = _split3(ld)
    g = _dot(lincl, h1) + _dot(lincl, h2) + _dot(lincl, h3)
    g_lasts = [g[(bi + 1) * C - 1:(bi + 1) * C, :] for bi in range(nb)]
    g_last = jnp.concatenate([jnp.broadcast_to(gl, (C, BRANCH)) for gl in g_lasts], axis=0)
    eg = jnp.exp(g)
    eng = jnp.exp(-g)
    at = av * jnp.exp(g - ld)
    rt = r * eg
    bt = bv * eng
    kt = k2 * eng
    e_c = jnp.exp(g_last - g)
    b_end = bv * e_c
    k_end = k2 * e_c
    eg_lasts = [jnp.exp(gl) for gl in g_lasts]

    c2 = 2 * C
    npair = BRANCH // LANES
    r_i = lax.broadcasted_iota(jnp.int32, (c2, c2), 0)
    c_i = lax.broadcasted_iota(jnp.int32, (c2, c2), 1)
    c_m = jnp.where(c_i >= C, c_i - C, c_i)
    mhalf = c_m < jnp.where(r_i < C, r_i, r_i - C + 1)
    mfull = jnp.concatenate([mhalf, mhalf], axis=0)
    bd2 = (r_i >> sh) == (c_i >> sh)
    eye2 = (r_i == c_i).astype(F32)
    lane = lax.broadcasted_iota(jnp.int32, (1, LANES), 1)
    m0 = (lane < HEAD).astype(F32)
    m1 = (lane >= HEAD).astype(F32)
    v_i = lax.broadcasted_iota(jnp.int32, (LANES, LANES), 0)
    k_i = lax.broadcasted_iota(jnp.int32, (LANES, LANES), 1)
    bdmask = (v_i >> 6) == (k_i >> 6)
    zf = jnp.zeros((C, LANES), F32)
    zc = jnp.zeros((C, LANES), BF16)
    sls = [slice(LANES * j, LANES * (j + 1)) for j in range(npair)]

    units = [(bi, j) for bi in range(nb) for j in range(npair)]
    tile = lambda x, bi, j: x[bi * C:(bi + 1) * C, LANES * j:LANES * (j + 1)]
    ws, a_ss, s2s = [], [], []
    for bi, j in units:
        atj, rtj = tile(at, bi, j), tile(rt, bi, j)
        a0, r0, a1, r1 = atj * m0, rtj * m0, atj * m1, rtj * m1
        lhs_w = jnp.concatenate([jnp.concatenate([a0, zf], axis=1), jnp.concatenate([r0, zf], axis=1),
                                 jnp.concatenate([zf, a1], axis=1), jnp.concatenate([zf, r1], axis=1)],
                                axis=0).astype(BF16)
        btj, ktj = tile(bt, bi, j), tile(kt, bi, j)
        rhs_w = jnp.concatenate([jnp.concatenate([btj, ktj], axis=0),
                                 jnp.concatenate([ktj, btj], axis=0)], axis=1).astype(BF16)
        ws.append(jnp.where(mfull, _dot_nt(lhs_w, rhs_w), 0.0))
        lhs_s = jnp.concatenate([a0, r0, a1, r1], axis=0).astype(BF16)
        s2 = st_ref[bi, j]
        s2h, s2l, _ = _split3(s2)
        a_ss.append(_dot_nt(lhs_s, s2h) + _dot_nt(lhs_s, s2l))
        s2s.append(s2)

    n_pows = [jnp.where(bd2, jnp.concatenate([w[:C], w[c2:c2 + C]], axis=0), 0.0) for w in ws]
    t_mats = [eye2 + n for n in n_pows]
    lv = 2
    while lv < C:
        n_pows = [_dot(n.astype(BF16), n.astype(BF16)) for n in n_pows]
        t_mats = [t + _dot(t.astype(BF16), n.astype(BF16)) for t, n in zip(t_mats, n_pows)]
        lv *= 2

    wbs = [w.astype(BF16) for w in ws]
    v0s = [(tile(v, bi, j) * m0).astype(BF16) for bi, j in units]
    v1s = [(tile(v, bi, j) * m1).astype(BF16) for bi, j in units]
    xs = [jnp.concatenate([a_s[:C] + _dot(wb[:C], jnp.concatenate([zc, v0], axis=0)),
                           a_s[c2:c2 + C] + _dot(wb[c2:c2 + C], jnp.concatenate([v1, zc], axis=0))], axis=0)
          for a_s, wb, v0, v1 in zip(a_ss, wbs, v0s, v1s)]
    us = [_dot(t.astype(BF16), x.astype(BF16)) for t, x in zip(t_mats, xs)]
    ys = []
    for n, (bi, j) in enumerate(units):
        a_s, wb, u = a_ss[n], wbs[n], us[n]
        u0, u1 = u[:C], u[C:]
        y0 = a_s[C:c2] + _dot(wb[C:c2], jnp.concatenate([u0.astype(BF16), v0s[n]], axis=0))
        y1 = a_s[c2 + C:] + _dot(wb[c2 + C:], jnp.concatenate([v1s[n], u1.astype(BF16)], axis=0))
        ys.append(y0 + y1)
        uv = jnp.concatenate([u0 + u1, tile(v, bi, j)], axis=0)
        bk = jnp.concatenate([tile(b_end, bi, j), tile(k_end, bi, j)], axis=0).astype(BF16)
        upd = _dot(uv.T.astype(BF16), bk)
        st_ref[bi, j] = s2s[n] * eg_lasts[bi][:, sls[j]] + jnp.where(bdmask, upd, 0.0)
    y = jnp.concatenate([jnp.concatenate(ys[bi * npair:(bi + 1) * npair], axis=1) for bi in range(nb)],
                        axis=0)

    inv_n = 1.0 / HEAD
    yc = y - _segsum(y, g2) * inv_n
    yn = yc * lax.rsqrt(_segsum(yc * yc, g2) * inv_n + RW_GN_EPS) * prow(8) + prow(9)
    out = (yn + _segsum(r * k2 * prow(7), g2) * v).astype(y_ref.dtype)
    for bi in range(nb):
        y_ref[bi] = out[bi * C:(bi + 1) * C]

    @pl.when(c == pl.num_programs(1) - 1)
    def _():
        for bi in range(nb):
            for j in range(HEADS // 2):
                s2 = st_ref[bi, j]
                sout_ref[bi, 2 * j] = s2[:HEAD, :HEAD]
                sout_ref[bi, 2 * j + 1] = s2[HEAD:, HEAD:]


def _rwkv(proj3, sp_r, sp_k, sp_v, sp_l, s0, pvec, mu_l, w2p, a2p, g2, chunk):
    b, t, _ = proj3.shape
    nc = t // chunk
    nb = min(b, 4)
    cb = lambda off, wdt: off // wdt
    bspec = lambda off, wdt: pl.BlockSpec((nb, chunk, wdt), lambda i, c, o=cb(off, wdt): (i, c, o))
    row3 = lambda wdt: pl.BlockSpec((nb, 1, wdt), lambda i, c: (i, 0, 0))
    full2 = lambda a: pl.BlockSpec(a.shape, lambda i, c: (0, 0))
    return pl.pallas_call(
        functools.partial(_rwkv_kernel, C=chunk, nb=nb),
        out_shape=(jax.ShapeDtypeStruct((b, t, BRANCH), BF16),
                   jax.ShapeDtypeStruct((b, HEADS, HEAD, HEAD), F32)),
        grid=(b // nb, nc),
        in_specs=[
            bspec(O_RWR, BRANCH), bspec(O_RWK, BRANCH), bspec(O_RWV, BRANCH), bspec(O_LORA, LANES),
            row3(BRANCH), row3(BRANCH), row3(BRANCH), row3(LANES),
            pl.BlockSpec((nb, HEADS, HEAD, HEAD), lambda i, c: (i, 0, 0, 0)),
            full2(pvec), full2(mu_l), full2(w2p), full2(a2p), full2(g2),
        ],
        out_specs=(pl.BlockSpec((nb, chunk, BRANCH), lambda i, c: (i, c, 0)),
                   pl.BlockSpec((nb, HEADS, HEAD, HEAD), lambda i, c: (i, 0, 0, 0))),
        scratch_shapes=[pltpu.VMEM((nb, 4, LANES, LANES), F32),
                        pltpu.VMEM((nb, 1, BRANCH), F32), pltpu.VMEM((nb, 1, BRANCH), F32),
                        pltpu.VMEM((nb, 1, BRANCH), F32), pltpu.VMEM((nb, 1, LANES), F32)],
        compiler_params=_cparams(("parallel", "arbitrary"), 32),
        name="rwkv7",
    )(proj3, proj3, proj3, proj3, sp_r, sp_k, sp_v, sp_l, s0, pvec, mu_l, w2p, a2p, g2)


def _s5_kernel(u_ref, h0r_ref, h0i_ref, bblk_ref, cblk_ref, lam_ref, pw_ref, dsk_ref, wglu_ref, bglu_ref,
               y_ref, hro_ref, hio_ref, x_ref, cr_ref, ci_ref, *, tb):
    @pl.when(pl.program_id(1) == 0)
    def _():
        cr_ref[...] = h0r_ref[0]
        ci_ref[...] = h0i_ref[0]

    u = u_ref[0]
    x = _dot(u.astype(BF16), bblk_ref[...])
    xr = x[:, :SSM_W]
    xi = x[:, SSM_W:]
    for lvl, s in enumerate((1, 2, 4)):
        ar = lam_ref[2 * lvl]
        ai = lam_ref[2 * lvl + 1]
        sr = pltpu.roll(xr, s, 0)
        si = pltpu.roll(xi, s, 0)
        xr, xi = xr + (ar * sr - ai * si), xi + (ar * si + ai * sr)
    x_ref[:, :SSM_W] = xr
    x_ref[:, SSM_W:] = xi
    pwr = pw_ref[0:8, :]
    pwi = pw_ref[8:16, :]

    def body(gi, carry):
        cr, ci = carry
        o = pl.multiple_of(gi * 8, 8)
        br = x_ref[pl.ds(o, 8), 0:SSM_W]
        bi = x_ref[pl.ds(o, 8), SSM_W:2 * SSM_W]
        br = br + pwr * cr - pwi * ci
        bi = bi + pwr * ci + pwi * cr
        x_ref[pl.ds(o, 8), 0:SSM_W] = br
        x_ref[pl.ds(o, 8), SSM_W:2 * SSM_W] = bi
        return br[7:8, :], bi[7:8, :]

    cr, ci = lax.fori_loop(0, tb // 8, body, (cr_ref[...], ci_ref[...]))
    cr_ref[...] = cr
    ci_ref[...] = ci
    hro_ref[0] = cr
    hio_ref[0] = ci
    y = _dot(x_ref[...].astype(BF16), cblk_ref[...]) + dsk_ref[...] * u
    g = jax.nn.gelu(y)
    out = g * jax.nn.sigmoid(_dot(g.astype(BF16), wglu_ref[...]) + bglu_ref[...])
    y_ref[0] = out.astype(y_ref.dtype)


def _s5(proj3, h0r, h0i, bblk, cblk, lam, pw, dsk, wglu, bglu):
    b, t, _ = proj3.shape
    tb = min(t, 128)
    sub = jnp.arange(tb)[:, None] & 7
    lam = jnp.stack([jnp.where(sub >= s, lam[2 * lv + c:2 * lv + c + 1], 0.0)
                     for lv, s in enumerate((1, 2, 4)) for c in (0, 1)])
    full2 = lambda a: pl.BlockSpec(a.shape, lambda i, c: (0,) * a.ndim)
    row3 = pl.BlockSpec((1, 1, SSM_W), lambda i, c: (i, 0, 0))
    return pl.pallas_call(
        functools.partial(_s5_kernel, tb=tb),
        out_shape=(jax.ShapeDtypeStruct((b, t, BRANCH), BF16),
                   jax.ShapeDtypeStruct((b, 1, SSM_W), F32),
                   jax.ShapeDtypeStruct((b, 1, SSM_W), F32)),
        grid=(b, t // tb),
        in_specs=[pl.BlockSpec((1, tb, BRANCH), lambda i, c: (i, c, O_SSM // BRANCH)),
                  row3, row3, full2(bblk), full2(cblk), full2(lam), full2(pw), full2(dsk),
                  full2(wglu), full2(bglu)],
        out_specs=(pl.BlockSpec((1, tb, BRANCH), lambda i, c: (i, c, 0)), row3, row3),
        scratch_shapes=[pltpu.VMEM((tb, 2 * SSM_W), F32),
                        pltpu.VMEM((1, SSM_W), F32), pltpu.VMEM((1, SSM_W), F32)],
        compiler_params=_cparams(("parallel", "arbitrary"), 48),
        name="s5",
    )(proj3, h0r, h0i, bblk, cblk, lam, pw, dsk, wglu, bglu)


def _mla_prep_kernel(ql_ref, kvl_ref, kpe_ref, kpesw_ref, tq_ref, tc_ref, ts_ref,
                     gq_ref, gkv_ref, wq_ref, q_ref, ckv_ref, kr_ref):
    ql = ql_ref[...]
    qn = ql * lax.rsqrt(jnp.mean(ql * ql, axis=-1, keepdims=True) + EPS) * gq_ref[...]
    q = _dot(qn.astype(BF16), wq_ref[...])
    tq = tq_ref[...]
    q_ref[...] = jnp.concatenate(
        [q[:, LANES * h:LANES * (h + 1)] * tq for h in range(HEADS)], axis=1).astype(BF16)
    kvl = kvl_ref[...]
    ckv_ref[...] = kvl * lax.rsqrt(jnp.mean(kvl * kvl, axis=-1, keepdims=True) + EPS) * gkv_ref[...]
    kr = kpe_ref[...] * tc_ref[...] + kpesw_ref[...] * ts_ref[...]
    kr_ref[...] = kr[:, :MLA_ROPE]


def _mla_prep(proj, t, tq_tab, tc_tab, ts_tab, gq, gkv, wq):
    n = proj.shape[0]
    tm = min(t, 512)
    nt = t // tm
    tab = pl.BlockSpec((tm, LANES), lambda i: (i % nt, 0))
    full2 = lambda a: pl.BlockSpec(a.shape, lambda i: (0, 0))
    return pl.pallas_call(
        _mla_prep_kernel,
        out_shape=(jax.ShapeDtypeStruct((n, HEADS * LANES), BF16),
                   jax.ShapeDtypeStruct((n, MLA_KV_LORA), F32),
                   jax.ShapeDtypeStruct((n, MLA_ROPE), F32)),
        grid=(n // tm,),
        in_specs=[pl.BlockSpec((tm, MLA_Q_LORA), lambda i: (i, O_QLAT // MLA_Q_LORA)),
                  pl.BlockSpec((tm, MLA_KV_LORA), lambda i: (i, O_KVLAT // MLA_KV_LORA)),
                  pl.BlockSpec((tm, LANES), lambda i: (i, O_KPE // LANES)),
                  pl.BlockSpec((tm, LANES), lambda i: (i, O_KPESW // LANES)),
                  tab, tab, tab, full2(gq), full2(gkv), full2(wq)],
        out_specs=(pl.BlockSpec((tm, HEADS * LANES), lambda i: (i, 0)),
                   pl.BlockSpec((tm, MLA_KV_LORA), lambda i: (i, 0)),
                   pl.BlockSpec((tm, MLA_ROPE), lambda i: (i, 0))),
        compiler_params=_cparams(("parallel",), 32),
        name="mla_prep",
    )(proj, proj, proj, proj, tq_tab, tc_tab, ts_tab, gq, gkv, wq)


def _mla_kv_kernel(ckv_ref, kr_ref, wk_ref, wv_ref, dup_ref, k_ref, v_ref):
    cb = ckv_ref[0].astype(BF16)
    kn = _dot(cb, wk_ref[...])
    krd = _dot(kr_ref[0].astype(BF16), dup_ref[...])
    k_ref[...] = jnp.concatenate(
        [kn[:, LANES * h:LANES * (h + 1)] + krd for h in range(HEADS)], axis=1).astype(BF16)
    v_ref[...] = _dot(cb, wv_ref[...]).astype(BF16)


def _mla_kv(ckv, kr, wk, wv, dup, layer=0):
    n = ckv.shape[1]
    tm = next((c for c in (512, LANES) if n % c == 0), n)
    full2 = lambda a: pl.BlockSpec(a.shape, lambda i: (0, 0))
    return pl.pallas_call(
        _mla_kv_kernel,
        out_shape=(jax.ShapeDtypeStruct((n, HEADS * LANES), BF16),
                   jax.ShapeDtypeStruct((n, BRANCH), BF16)),
        grid=(n // tm,),
        in_specs=[pl.BlockSpec((1, tm, MLA_KV_LORA), lambda i: (layer, i, 0)),
                  pl.BlockSpec((1, tm, MLA_ROPE), lambda i: (layer, i, 0)),
                  full2(wk), full2(wv), full2(dup)],
        out_specs=(pl.BlockSpec((tm, HEADS * LANES), lambda i: (i, 0)),
                   pl.BlockSpec((tm, BRANCH), lambda i: (i, 0))),
        compiler_params=_cparams(("parallel",), 32),
        name="mla_kv",
    )(ckv, kr, wk, wv, dup)


def _assemble_keys(past_ref, new_ref, all_ref, past, t):
    all_ref[0:past, :] = past_ref[0].astype(BF16)
    all_ref[past:past + t, :] = new_ref[0].astype(BF16)
    pad = all_ref.shape[0] - past - t
    if pad:
        all_ref[past + t:, :] = jnp.zeros((pad, all_ref.shape[1]), BF16)


def _mla_attn_kernel(q_ref, k_ref, v_ref, *rest, tq, tk, q_off, s_valid, hp, past):
    if past:
        kp_ref, vp_ref, o_ref, kall_ref, vall_ref = rest
        _assemble_keys(kp_ref, k_ref, kall_ref, past, tq)
        _assemble_keys(vp_ref, v_ref, vall_ref, past, tq)
        kload = lambda o: kall_ref[pl.ds(o, tk), :]
        vload = lambda o: vall_ref[pl.ds(o, tk), :]
    else:
        (o_ref,) = rest
        kload = lambda o: k_ref[0, pl.ds(o, tk), :]
        vload = lambda o: v_ref[0, pl.ds(o, tk), :]
    i = pl.program_id(2)
    q_lo = q_off + i * tq
    limit = jnp.minimum(q_lo + tq, s_valid)
    nkb = (limit + tk - 1) // tk
    qidx = q_lo + lax.broadcasted_iota(jnp.int32, (tq, tk), 0)
    kloc = lax.broadcasted_iota(jnp.int32, (tq, tk), 1)
    nh = 2 * hp
    q2 = q_ref[0]
    qs = [q2[:, LANES * h:LANES * (h + 1)] for h in range(nh)]

    def body(kb, carry, masked):
        o = pl.multiple_of(kb * tk, tk)
        kblk = kload(o)
        vblk = vload(o)
        ss = [_dot_nt(qs[h], kblk[:, LANES * h:LANES * (h + 1)]) for h in range(nh)]
        if masked:
            kidx = kloc + kb * tk
            vis = jnp.logical_and((kidx >> 6) <= (qidx >> 6), kidx < s_valid)
            ss = [jnp.where(vis, s, -1e30) for s in ss]
        m_new = [jnp.maximum(carry[h][0], jnp.max(ss[h], axis=-1, keepdims=True)) for h in range(nh)]
        ps = [jnp.exp2(ss[h] - m_new[h]) for h in range(nh)]
        pv = [_dot(ps[h].astype(BF16), vblk[:, LANES * (h // 2):LANES * (h // 2 + 1)]) for h in range(nh)]
        new = []
        for h in range(nh):
            m_prev, l_prev, acc = carry[h]
            alpha = jnp.exp2(m_prev - m_new[h])
            new.append((m_new[h], alpha * l_prev + jnp.sum(ps[h], axis=-1, keepdims=True), alpha * acc + pv[h]))
        return tuple(new)

    init = tuple((jnp.full((tq, 1), -1e30, F32), jnp.zeros((tq, 1), F32), jnp.zeros((tq, LANES), F32))
                 for _ in range(nh))
    nfull = jnp.minimum(((q_lo >> 6) + 1) * CHUNK, s_valid) // tk
    res = lax.fori_loop(0, nfull, lambda kb, c: body(kb, c, False), init)
    res = lax.fori_loop(nfull, nkb, lambda kb, c: body(kb, c, True), res)
    lane = lax.broadcasted_iota(jnp.int32, (tq, LANES), 1)
    outs = [jnp.where(lane < HEAD, res[2 * p][2] / res[2 * p][1], res[2 * p + 1][2] / res[2 * p + 1][1])
            for p in range(hp)]
    o_ref[0] = jnp.concatenate(outs, axis=1).astype(o_ref.dtype)


def _mla_prompt_kernel(q_ref, k_ref, v_ref, o_ref, *, tq):
    i = pl.program_id(2)
    q_lo = i * tq
    nkb = i + 1
    w = 2 * tq
    cidx = lax.broadcasted_iota(jnp.int32, (tq, w), 1)
    qidx = q_lo + jnp.where(cidx >= tq, cidx - tq, cidx)
    kloc = lax.broadcasted_iota(jnp.int32, (tq, w), 0)
    q2 = q_ref[0]
    zq = jnp.zeros((tq, LANES), BF16)
    qbd = jnp.concatenate([jnp.concatenate([q2[:, :LANES], zq], axis=1),
                           jnp.concatenate([zq, q2[:, LANES:]], axis=1)], axis=0)

    def group(kb, carry, n):
        m_prev, l_prev, acc_t = carry
        o = pl.multiple_of((kb - (n - 1)) * tq, tq)
        kn = k_ref[0, pl.ds(o, n * tq), :]
        vn_t = v_ref[0, pl.ds(o, n * tq), :].astype(F32).T.astype(BF16)
        s = _dot_nt(kn, qbd)
        vis = ((kloc + kb * tq) >> 6) <= (qidx >> 6)
        s_r = jnp.where(vis, s[(n - 1) * tq:], -1e30)
        s = jnp.concatenate([s[:(n - 1) * tq], s_r], axis=0) if n > 1 else s_r
        m_new = jnp.maximum(m_prev, jnp.max(s, axis=0, keepdims=True))
        alpha = jnp.exp2(m_prev - m_new)
        p = jnp.exp2(s - m_new)
        l_new = alpha * l_prev + jnp.sum(p, axis=0, keepdims=True)
        return m_new, l_new, alpha * acc_t + _dot(vn_t, p.astype(BF16))

    carry = (jnp.full((1, w), -1e30, F32), jnp.zeros((1, w), F32), jnp.zeros((LANES, w), F32))
    n4, n2, n1 = nkb // 4, (nkb % 4) // 2, nkb % 2
    carry = lax.fori_loop(0, n4, lambda s, c: group(nkb - 1 - 4 * s, c, 4), carry)
    carry = lax.fori_loop(0, n2, lambda s, c: group(nkb - 1 - 4 * n4, c, 2), carry)
    carry = lax.fori_loop(0, n1, lambda s, c: group(0, c, 1), carry)
    out_t = carry[2] / carry[1]
    row = lax.broadcasted_iota(jnp.int32, (LANES, tq), 0)
    o_ref[0] = jnp.where(row < HEAD, out_t[:, :tq], out_t[:, tq:]).T.astype(o_ref.dtype)


def _mla_prompt(q3, k3, v3, tq):
    b, t, _ = q3.shape
    assert tq % CHUNK == 0 and t % tq == 0
    return pl.pallas_call(
        functools.partial(_mla_prompt_kernel, tq=tq),
        out_shape=jax.ShapeDtypeStruct((b, t, BRANCH), BF16),
        grid=(b, HEADS // 2, t // tq),
        in_specs=[pl.BlockSpec((1, tq, 2 * LANES), lambda bi, j, i: (bi, i, j)),
                  pl.BlockSpec((1, t, 2 * LANES), lambda bi, j, i: (bi, 0, j)),
                  pl.BlockSpec((1, t, LANES), lambda bi, j, i: (bi, 0, j))],
        out_specs=pl.BlockSpec((1, tq, LANES), lambda bi, j, i: (bi, i, j)),
        compiler_params=_cparams(("parallel", "parallel", "arbitrary"), 40),
        name="mla_prompt",
    )(q3, k3, v3)


def _mla_attn(q3, k3, v3, kp3, vp3, s_pad, tq, tk, hp):
    b, t, _ = q3.shape
    past = 0 if kp3 is None else kp3.shape[1]
    rows = k3.shape[1]
    in_specs = [pl.BlockSpec((1, tq, 2 * LANES * hp), lambda bi, j, i: (bi, i, j)),
                pl.BlockSpec((1, rows, 2 * LANES * hp), lambda bi, j, i: (bi, 0, j)),
                pl.BlockSpec((1, rows, LANES * hp), lambda bi, j, i: (bi, 0, j))]
    args = [q3, k3, v3]
    scratch = []
    if past:
        in_specs += [pl.BlockSpec((1, past, 2 * LANES * hp), lambda bi, j, i: (bi, 0, j)),
                     pl.BlockSpec((1, past, LANES * hp), lambda bi, j, i: (bi, 0, j))]
        args += [kp3, vp3]
        scratch = [pltpu.VMEM((s_pad, 2 * LANES * hp), BF16), pltpu.VMEM((s_pad, LANES * hp), BF16)]
    return pl.pallas_call(
        functools.partial(_mla_attn_kernel, tq=tq, tk=tk, q_off=past, s_valid=past + t, hp=hp, past=past),
        out_shape=jax.ShapeDtypeStruct((b, t, BRANCH), BF16),
        grid=(b, 4 // hp, t // tq),
        in_specs=in_specs,
        out_specs=pl.BlockSpec((1, tq, LANES * hp), lambda bi, j, i: (bi, i, j)),
        scratch_shapes=scratch,
        compiler_params=_cparams(("parallel", "parallel", "arbitrary"), 40),
        name="mla_attn",
    )(*args)


def _sb_block(zs, vis, us, carry, pv):
    nh = len(zs)
    tq = zs[0].shape[0]
    zls = [jnp.minimum(z, 0.0) - jnp.log(1.0 + jnp.exp(-jnp.abs(z))) for z in zs]
    lgs = [zl - z for zl, z in zip(zls, zs)]
    if vis is not None:
        lgs = [jnp.where(vis, lg, 0.0) for lg in lgs]
    his = [lg.astype(BF16) for lg in lgs]
    los = [(lg - hi.astype(F32)).astype(BF16) for lg, hi in zip(lgs, his)]
    later = _dot(jnp.concatenate(his + los, axis=0), us)
    new = []
    for h in range(nh):
        csum, acc = carry[h]
        tot = later[h * tq:(h + 1) * tq] + later[(nh + h) * tq:(nh + h + 1) * tq] + csum
        a = jnp.exp(zls[h] + tot)
        if vis is not None:
            a = jnp.where(vis, a, 0.0)
        new.append((csum + jnp.sum(lgs[h], axis=-1, keepdims=True), acc + pv(h, a.astype(BF16))))
    return tuple(new)


def _sb_heads(q2, nh, scale=SB_SCALE):
    lane1 = lax.broadcasted_iota(jnp.int32, (1, LANES), 1)
    hm = ((lane1 < HEAD).astype(F32) * scale, (lane1 >= HEAD).astype(F32) * scale)
    return [(q2[:, LANES * (h // 2):LANES * (h // 2 + 1)] * hm[h % 2]).astype(BF16) for h in range(nh)]


def _sb_finish(res, o_ref, tq):
    lane = lax.broadcasted_iota(jnp.int32, (tq, LANES), 1)
    outs = [jnp.where(lane < HEAD, res[2 * p][1], res[2 * p + 1][1]) for p in range(len(res) // 2)]
    o_ref[0] = jnp.concatenate(outs, axis=1).astype(o_ref.dtype)


def _sb_attn_kernel(q_ref, k_ref, v_ref, us_ref, o_ref, *, tq, tk):
    i = pl.program_id(2)
    q_lo = i * tq
    nkb = (q_lo + tq - 2) // tk + 1
    cidx = lax.broadcasted_iota(jnp.int32, (tk, 2 * tq), 1)
    qidx = q_lo + jnp.where(cidx >= tq, cidx - tq, cidx)
    kloc = lax.broadcasted_iota(jnp.int32, (tk, 2 * tq), 0)
    qs = jnp.concatenate(_sb_heads(q_ref[0], 2, SB_SCALE * LOG2E), axis=0)
    us = us_ref[...]

    w = 2 * tq

    def group(kb, carry, n):
        csum, acc_t = carry
        o = pl.multiple_of((kb - (n - 1)) * tk, tk)
        kn = k_ref[0, pl.ds(o, n * tk), :].astype(BF16)
        vn_t = v_ref[0, pl.ds(o, n * tk), :].T.astype(BF16)
        z = _dot_nt(kn, qs)
        zl = jnp.minimum(z, 0.0) - jnp.log2(1.0 + jnp.exp2(-jnp.abs(z)))
        lg = zl - z
        vis = (kloc + kb * tk) < qidx
        lgs = [lg[j * tk:(j + 1) * tk] for j in range(n - 1)] + [jnp.where(vis, lg[(n - 1) * tk:], 0.0)]
        his = [x.astype(BF16) for x in lgs]
        los = [(x - h.astype(F32)).astype(BF16) for x, h in zip(lgs, his)]
        later = _dot(us, jnp.concatenate([jnp.concatenate([his[j], los[j]], axis=0) for j in range(n)], axis=1))
        tots = [None] * n
        for j in reversed(range(n)):
            tots[j] = later[:, j * w:(j + 1) * w] + csum
            csum = csum + jnp.sum(lgs[j], axis=0, keepdims=True)
        a = jnp.exp2(zl + jnp.concatenate(tots, axis=0))
        a = jnp.concatenate([a[:(n - 1) * tk], jnp.where(vis, a[(n - 1) * tk:], 0.0)], axis=0) if n > 1 \
            else jnp.where(vis, a, 0.0)
        return csum, acc_t + _dot(vn_t, a.astype(BF16))

    carry = (jnp.zeros((1, 2 * tq), F32), jnp.zeros((LANES, 2 * tq), F32))
    n4, n2, n1 = nkb // 4, (nkb % 4) // 2, nkb % 2
    carry = lax.fori_loop(0, n4, lambda s, c: group(nkb - 1 - 4 * s, c, 4), carry)
    carry = lax.fori_loop(0, n2, lambda s, c: group(nkb - 1 - 4 * n4, c, 2), carry)
    carry = lax.fori_loop(0, n1, lambda s, c: group(0, c, 1), carry)
    acc_t = carry[1]
    row = lax.broadcasted_iota(jnp.int32, (LANES, tq), 0)
    o_ref[0] = jnp.where(row < HEAD, acc_t[:, :tq], acc_t[:, tq:]).T.astype(o_ref.dtype)


def _sb_decode_kernel(q_ref, k_ref, v_ref, kpt_ref, vpt_ref, usn_ref, usp_ref, o_ref, *, t, past, pb):
    qh = _sb_heads(q_ref[0], HEADS)
    kn = k_ref[0].astype(BF16)
    vn = v_ref[0].astype(BF16)
    pair = lambda x, h: x[:, LANES * (h // 2):LANES * (h // 2 + 1)]
    ri = lax.broadcasted_iota(jnp.int32, (t, t), 0)
    ci = lax.broadcasted_iota(jnp.int32, (t, t), 1)
    carry = tuple((jnp.zeros((t, 1), F32), jnp.zeros((t, LANES), F32)) for _ in range(HEADS))
    zs = [_dot_nt(qh[h], pair(kn, h)) for h in range(HEADS)]
    carry = _sb_block(zs, ci < ri, usn_ref[...], carry, lambda h, a: _dot(a, pair(vn, h)))
    usp = usp_ref[...]
    for blk in reversed(range(past // pb)):
        kts = [kpt_ref[0, 0, p, :, blk * pb:(blk + 1) * pb].astype(BF16) for p in range(HEADS // 2)]
        vts = [vpt_ref[0, 0, p, :, blk * pb:(blk + 1) * pb].astype(BF16) for p in range(HEADS // 2)]
        zs = [_dot(qh[h], kts[h // 2]) for h in range(HEADS)]
        carry = _sb_block(zs, None, usp, carry, lambda h, a: _dot_nt(a, vts[h // 2]))
    _sb_finish(carry, o_ref, t)


def _strict_upper(n):
    idx = jnp.arange(n)
    return (idx[:, None] > idx[None, :]).astype(BF16)


def _sb_attn(proj3, tq, tk):
    b, t, _ = proj3.shape
    assert tq == tk
    u = _strict_upper(tk).T
    us = jnp.concatenate([u, u], axis=1)
    return pl.pallas_call(
        functools.partial(_sb_attn_kernel, tq=tq, tk=tk),
        out_shape=jax.ShapeDtypeStruct((b, t, BRANCH), BF16),
        grid=(b, HEADS // 2, t // tq),
        in_specs=[pl.BlockSpec((1, tq, LANES), lambda bi, j, i: (bi, i, O_SBQ // LANES + j)),
                  pl.BlockSpec((1, t, LANES), lambda bi, j, i: (bi, 0, O_SBK // LANES + j)),
                  pl.BlockSpec((1, t, LANES), lambda bi, j, i: (bi, 0, O_SBV // LANES + j)),
                  pl.BlockSpec(us.shape, lambda bi, j, i: (0, 0))],
        out_specs=pl.BlockSpec((1, tq, LANES), lambda bi, j, i: (bi, i, j)),
        compiler_params=_cparams(("parallel", "parallel", "arbitrary"), 48),
        name="sb_attn",
    )(proj3, proj3, proj3, us)


def _sb_decode(proj3, kpt, vpt, layer, pb):
    b, t, _ = proj3.shape
    past = kpt.shape[4]
    usn, usp = _strict_upper(t), _strict_upper(pb)
    new = lambda off: pl.BlockSpec((1, t, BRANCH), lambda bi, o=off // BRANCH: (bi, 0, o))
    cache = pl.BlockSpec((1, 1, HEADS // 2, LANES, past), lambda bi: (layer, bi, 0, 0, 0))
    return pl.pallas_call(
        functools.partial(_sb_decode_kernel, t=t, past=past, pb=pb),
        out_shape=jax.ShapeDtypeStruct((b, t, BRANCH), BF16),
        grid=(b,),
        in_specs=[new(O_SBQ), new(O_SBK), new(O_SBV), cache, cache,
                  pl.BlockSpec(usn.shape, lambda bi: (0, 0)), pl.BlockSpec(usp.shape, lambda bi: (0, 0))],
        out_specs=pl.BlockSpec((1, t, BRANCH), lambda bi: (bi, 0, 0)),
        compiler_params=_cparams(("parallel",), 48),
        name="sb_decode",
    )(proj3, proj3, proj3, kpt, vpt, usn, usp)


def _mix_kernel(yrw_ref, yssm_ref, ymla_ref, ysb_ref, gate_ref, pm_ref, wb_ref, bm_ref, o_ref, acc_ref):
    n = pl.program_id(1)

    @pl.when(n == 0)
    def _():
        acc_ref[...] = jnp.zeros_like(acc_ref)

    def branch(y_ref):
        g = gate_ref[...].astype(F32)
        gated_half = y_ref[...].astype(F32) * (g * (0.5 * _sigmoid(g)))
        up_half = _dot(gated_half.astype(BF16), wb_ref[0, 0].astype(BF16))
        acc_ref[...] += (jnp.tanh(0.5 * (pm_ref[...].astype(F32) + bm_ref[0])) + 1.0) * up_half

    for idx, y_ref in enumerate((yrw_ref, yssm_ref, ymla_ref, ysb_ref)):
        pl.when(n == idx)(functools.partial(branch, y_ref))

    @pl.when(n == N_BRANCH - 1)
    def _():
        o_ref[...] = acc_ref[...].astype(BF16)


def _out_kernel(m_ref, x_ref, wo_ref, fg_ref, o_ref, *, final):
    xn = x_ref[...] + _dot(m_ref[...], wo_ref[0].astype(BF16))
    if final:
        xn = xn * lax.rsqrt(jnp.mean(xn * xn, axis=-1, keepdims=True) + EPS) * fg_ref[...]
    o_ref[...] = xn


def _merge(ys, proj, x2, wb, bm, wo, fg, final, layer):
    n, d = x2.shape
    tm = min(n, 1024)
    ysp = pl.BlockSpec((tm, BRANCH), lambda i, k: (i, 0))
    mixed = pl.pallas_call(
        _mix_kernel,
        out_shape=jax.ShapeDtypeStruct((n, d), BF16),
        grid=(n // tm, N_BRANCH),
        in_specs=[ysp, ysp, ysp, ysp,
                  pl.BlockSpec((tm, BRANCH), lambda i, k: (i, O_GATE // BRANCH + k)),
                  pl.BlockSpec((tm, d), lambda i, k: (i, O_MERGE // d + k)),
                  pl.BlockSpec((1, 1, BRANCH, d), lambda i, k: (layer, k, 0, 0)),
                  pl.BlockSpec((1, 1, d), lambda i, k: (k, 0, 0))],
        out_specs=pl.BlockSpec((tm, d), lambda i, k: (i, 0)),
        scratch_shapes=[pltpu.VMEM((tm, d), F32)],
        compiler_params=_cparams(("parallel", "arbitrary"), 56),
        name="branch_mix",
    )(*ys, proj, proj, wb, bm)
    to = min(n, 512)
    return pl.pallas_call(
        functools.partial(_out_kernel, final=final),
        out_shape=jax.ShapeDtypeStruct((n, d), F32),
        grid=(n // to,),
        in_specs=[pl.BlockSpec((to, d), lambda i: (i, 0)),
                  pl.BlockSpec((to, d), lambda i: (i, 0)),
                  pl.BlockSpec((1, d, d), lambda i: (layer, 0, 0), pipeline_mode=pl.Buffered(1)),
                  pl.BlockSpec((1, d), lambda i: (0, 0))],
        out_specs=pl.BlockSpec((to, d), lambda i: (i, 0)),
        compiler_params=_cparams(("parallel",), 48),
        name="out_proj",
    )(mixed, x2, wo, fg)


_SRC_SSM = RW_IN
_SRC_QLAT = _SRC_SSM + BRANCH
_SRC_KVLAT = _SRC_QLAT + MLA_Q_LORA
_SRC_KPE = _SRC_KVLAT + MLA_KV_LORA
_SRC_SB = _SRC_KPE + MLA_ROPE
_SRC_GATE = _SRC_SB + 3 * BRANCH
_SRC_MERGE = _SRC_GATE + N_BRANCH * BRANCH
_SRC_LORA = 3 * BRANCH
_PW_BLK = 4 * LANES
_PW_NM = N_BRANCH * D_MODEL // _PW_BLK
_PW_NG = _PW_NM + N_BRANCH * BRANCH // _PW_BLK
_PW_ROPE = _PW_NG + 1
_PW_TAIL = {_PW_NG: (_SRC_QLAT, _SRC_QLAT + LANES, _SRC_QLAT + 2 * LANES, _SRC_LORA),
            _PW_ROPE: (_SRC_KVLAT, _SRC_KVLAT + LANES, _SRC_KPE, _SRC_KPE)}


def _permute_src_col(j, k):
    sb0 = _PW_ROPE + 1
    rw0 = sb0 + 3 * BRANCH // _PW_BLK
    ssm0 = rw0 + 3 * BRANCH // _PW_BLK
    u = MLA_ROPE
    col = jnp.where(j < _PW_NM, _SRC_MERGE // u + _PW_BLK // u * j,
                    jnp.where(j < _PW_NG, _SRC_GATE // u + _PW_BLK // u * (j - _PW_NM),
                              jnp.where(j < rw0, _SRC_SB // u + _PW_BLK // u * (j - sb0),
                                        jnp.where(j < ssm0, _PW_BLK // u * (j - rw0), _SRC_SSM // u))))
    col = col + LANES // u * k
    for jj, src in _PW_TAIL.items():
        col = jnp.where(j == jj, src[k] // u, col)
    return col * u


def _permute_kernel(x0_ref, x1_ref, x2_ref, x3_ref, o_ref):
    j = pl.program_id(1)
    wins = (x0_ref, x1_ref, x2_ref, x3_ref)
    half = MLA_ROPE // 2

    def put(k, rows):
        o_ref[0, :, LANES * k:LANES * (k + 1)] = rows.T.astype(BF16)

    @pl.when(j != _PW_ROPE)
    def _():
        for k in range(4):
            put(k, wins[k][0])

    @pl.when(j == _PW_ROPE)
    def _():
        for k in range(2):
            put(k, wins[k][0])
        kpe = wins[2][0]
        row = lax.broadcasted_iota(jnp.int32, kpe.shape, 0)
        put(2, jnp.where(row < MLA_ROPE, kpe, 0.0))
        zeros = jnp.zeros((LANES - MLA_ROPE, kpe.shape[1]), F32)
        put(3, jnp.concatenate([-kpe[half:MLA_ROPE], kpe[:half], zeros], axis=0))


def _permute_w_in(w_in):
    depth, d, _ = w_in.shape
    wt = jnp.swapaxes(w_in, 1, 2)
    win = lambda k: pl.BlockSpec((pl.Element(1), pl.Element(LANES), pl.Element(d)),
                                 lambda l, j, k=k: (l, _permute_src_col(j, k), 0))
    return pl.pallas_call(
        _permute_kernel,
        out_shape=jax.ShapeDtypeStruct((depth, d, PROJ_W), BF16),
        grid=(depth, PROJ_W // _PW_BLK),
        in_specs=[win(k) for k in range(4)],
        out_specs=pl.BlockSpec((1, d, _PW_BLK), lambda l, j: (l, 0, j)),
        compiler_params=_cparams(("parallel", "parallel"), 32),
        name="permute_w_in",
    )(wt, wt, wt, wt)


def _rope_tables(pos):
    half = MLA_ROPE // 2
    inv = ROPE_BASE ** (-jnp.arange(half, dtype=F32) / half)
    ang = pos.astype(F32)[:, None] * inv
    cos, sin = jnp.cos(ang), jnp.sin(ang)
    t = pos.shape[0]
    cc = jnp.concatenate([cos, cos], axis=1)
    ss = jnp.concatenate([sin, sin], axis=1)
    tq = jnp.concatenate([jnp.ones((t, MLA_NOPE), F32), cc, ss], axis=1) * (MLA_SCALE * LOG2E)
    pad = jnp.zeros((t, LANES - MLA_ROPE), F32)
    return tq, jnp.concatenate([cc, pad], axis=1), jnp.concatenate([ss, pad], axis=1)


def _mla_weights(w_q_up, w_kv_up):
    half = MLA_ROPE // 2
    wq = w_q_up.reshape(MLA_Q_LORA, HEADS, MLA_NOPE + MLA_ROPE)
    x1 = wq[:, :, MLA_NOPE:MLA_NOPE + half]
    x2 = wq[:, :, MLA_NOPE + half:]
    wq_p = jnp.concatenate([wq, -x2, x1], axis=2).reshape(MLA_Q_LORA, HEADS * LANES).astype(BF16)
    wkv = w_kv_up.reshape(MLA_KV_LORA, HEADS, 2 * HEAD)
    wk_p = jnp.concatenate([wkv[:, :, :HEAD], jnp.zeros((MLA_KV_LORA, HEADS, HEAD), F32)], axis=2)
    wk_p = wk_p.reshape(MLA_KV_LORA, HEADS * LANES).astype(BF16)
    wv_p = wkv[:, :, HEAD:].reshape(MLA_KV_LORA, BRANCH).astype(BF16)
    return wq_p, wk_p, wv_p


def _s5_tables(lam_re, lam_im, log_dt, b_re, b_im, c_re, c_im):
    dt = jnp.exp(log_dt)[:, None]
    mag = jnp.exp(lam_re * dt)
    ang = lam_im * dt
    lb_re, lb_im = mag * jnp.cos(ang), mag * jnp.sin(ang)
    nr, ni = lb_re - 1.0, lb_im
    den = lam_re * lam_re + lam_im * lam_im
    f_re = (nr * lam_re + ni * lam_im) / den
    f_im = (ni * lam_re - nr * lam_im) / den
    bb_re = f_re[..., None] * b_re - f_im[..., None] * b_im
    bb_im = f_re[..., None] * b_im + f_im[..., None] * b_re
    eye = jnp.eye(SSM_GROUPS, dtype=F32)
    blk_in = lambda m: jnp.einsum('gpc,gh->gchp', m, eye).reshape(BRANCH, SSM_W)
    blk_out = lambda m: jnp.einsum('gcp,gh->gphc', m, eye).reshape(SSM_W, BRANCH)
    bblk = jnp.concatenate([blk_in(bb_re), blk_in(bb_im)], axis=1).astype(BF16)
    cblk = jnp.concatenate([blk_out(c_re), blk_out(-c_im)], axis=0).astype(BF16)

    def power(j):
        m = jnp.exp(lam_re * dt * j)
        return (m * jnp.cos(ang * j)).reshape(1, SSM_W), (m * jnp.sin(ang * j)).reshape(1, SSM_W)

    lam = jnp.concatenate([p for j in (1, 2, 4) for p in power(j)] + [jnp.zeros((2, SSM_W), F32)], axis=0)
    pws = [power(j) for j in range(1, 9)]
    pw = jnp.concatenate([p[0] for p in pws] + [p[1] for p in pws], axis=0)
    return bblk, cblk, lam, pw


def _layer(x, pos0, st, lw, final_g, final):
    b, t, d = x.shape
    n = b * t
    x2 = x.reshape(n, d)
    proj_b, proj = _norm_proj(x2, lw["norm_g"], lw["w_in_all"], lw["layer"])
    proj3 = proj.reshape(b, t, PF_W)
    past = st["past"]

    chunk = min(t, CHUNK)
    y_rw, wkv_new = _rwkv(proj3, st["sp_r"], st["sp_k"], st["sp_v"], st["sp_l"], st["wkv"],
                        lw["rw_pvec"], lw["rw_mu_l"], lw["rw_w2p"], lw["rw_a2p"], lw["g2"], chunk)
    shift_new = jnp.concatenate([proj3[:, t - 1:, O_RWR:O_RWR + 3 * BRANCH],
                                 proj3[:, t - 1:, O_LORA:O_LORA + 2 * RW_LORA]], axis=-1)

    y_ssm, hr, hi = _s5(proj3, st["ssm_re"], st["ssm_im"], lw["ssm_bblk"], lw["ssm_cblk"], lw["ssm_lam"],
                        lw["ssm_pw"], lw["ssm_d"], lw["ssm_wglu"], lw["ssm_bglu"])
    ssm_re_new = hr.reshape(b, SSM_GROUPS, SSM_STATE)
    ssm_im_new = hi.reshape(b, SSM_GROUPS, SSM_STATE)

    pos = pos0 + jnp.arange(t, dtype=jnp.int32)
    tq_tab, tc_tab, ts_tab = _rope_tables(pos)
    q, ckv, kr = _mla_prep(proj, t, tq_tab, tc_tab, ts_tab, lw["mla_gq"], lw["mla_gkv"], lw["mla_wq"])
    ckv3 = ckv.reshape(b, t, MLA_KV_LORA)
    kr3 = kr.reshape(b, t, MLA_ROPE)
    s_pad = -(-(past + t) // LANES) * LANES
    kc, vv = _mla_kv(ckv[None], kr[None], lw["mla_wk"], lw["mla_wv"], lw["dup"])
    kc3, vv3 = kc.reshape(b, t, HEADS * LANES), vv.reshape(b, t, BRANCH)
    kcp3 = vvp3 = None
    if past:
        kcp, vvp = _mla_kv(st["ckv_all"], st["kpe_all"], lw["mla_wk"], lw["mla_wv"], lw["dup"], lw["layer"])
        kcp3, vvp3 = kcp.reshape(b, past, HEADS * LANES), vvp.reshape(b, past, BRANCH)
    small = t <= 64
    assert small or not past
    tq = min(t, 256)
    hp = 4 if small else 1
    q3 = q.reshape(b, t, HEADS * LANES)
    if small:
        y_mla = _mla_attn(q3, kc3, vv3, kcp3, vvp3, s_pad, tq, s_pad, hp)
    else:
        y_mla = _mla_prompt(q3, kc3, vv3, tq)

    sbk = proj3[:, :, O_SBK:O_SBK + BRANCH]
    sbv = proj3[:, :, O_SBV:O_SBV + BRANCH]
    if past:
        y_sb = _sb_decode(proj3, st["sbk_t"], st["sbv_t"], lw["layer"], min(256, past))
    else:
        y_sb = _sb_attn(proj3, tq, min(256, t))

    ys = [y.reshape(n, BRANCH) for y in (y_rw, y_ssm, y_mla, y_sb)]
    x_new = _merge(ys, proj_b, x2, lw["w_branch"], lw["b_merge"], lw["w_out"], final_g, final,
                   lw["layer"]).reshape(b, t, d)
    new_state = (shift_new, wkv_new, ssm_re_new, ssm_im_new, ckv3, kr3,
                 sbk.reshape(b, t, HEADS, HEAD), sbv.reshape(b, t, HEADS, HEAD))
    return x_new, new_state


def kernel(x_prompt, x_sample, state_rwkv_shift, state_rwkv_wkv, state_ssm_re, state_ssm_im, cache_mla_ckv, cache_mla_kpe, cache_sb_k, cache_sb_v, norm_g, w_in, rw_mu, rw_w0, rw_w2, rw_a0, rw_a2, rw_k_k, rw_k_a, rw_r_k, rw_lnx_g, rw_lnx_b, ssm_lam_re, ssm_lam_im, ssm_log_dt, ssm_b_re, ssm_b_im, ssm_c_re, ssm_c_im, ssm_d, ssm_w_glu, ssm_b_glu, mla_q_norm, mla_w_q_up, mla_kv_norm, mla_w_kv_up, w_branch, b_merge, w_out, final_norm_g):
    depth = w_in.shape[0]
    bp, tp, _ = x_prompt.shape
    bs = x_sample.shape[0]
    past = cache_mla_ckv.shape[2]

    w_in_p = _permute_w_in(w_in)
    lane_i = jnp.arange(LANES)
    g2 = ((lane_i[:, None] // HEAD) == (lane_i[None, :] // HEAD)).astype(BF16)
    rope_i = jnp.arange(MLA_ROPE)
    dup = ((lane_i[None, :] == rope_i[:, None] + MLA_NOPE)
           | (lane_i[None, :] == rope_i[:, None] + MLA_NOPE + MLA_ROPE)).astype(BF16)
    final_g = final_norm_g.reshape(1, D_MODEL)
    zpad = jnp.zeros((RW_LORA, BRANCH), F32)

    layers = []
    for l in range(depth):
        wq_p, wk_p, wv_p = _mla_weights(mla_w_q_up[l], mla_w_kv_up[l])
        bblk, cblk, lam, pw = _s5_tables(ssm_lam_re[l], ssm_lam_im[l], ssm_log_dt[l], ssm_b_re[l], ssm_b_im[l],
                                         ssm_c_re[l], ssm_c_im[l])
        mu = rw_mu[l]
        rows = [mu[:BRANCH], mu[BRANCH:2 * BRANCH], mu[2 * BRANCH:3 * BRANCH], rw_w0[l], rw_a0[l], rw_k_k[l],
                rw_k_a[l], rw_r_k[l].reshape(BRANCH), rw_lnx_g[l], rw_lnx_b[l]]
        pvec = jnp.concatenate([jnp.stack(rows), jnp.zeros((16 - len(rows), BRANCH), F32)], axis=0)
        layers.append(dict(
            norm_g=norm_g[l].reshape(1, D_MODEL), w_in_all=w_in_p, layer=l,
            rw_pvec=pvec, rw_mu_l=mu[3 * BRANCH:].reshape(1, 2 * RW_LORA),
            rw_w2p=jnp.concatenate([rw_w2[l], zpad], axis=0).astype(BF16),
            rw_a2p=jnp.concatenate([zpad, rw_a2[l]], axis=0).astype(BF16),
            g2=g2, dup=dup,
            ssm_bblk=bblk, ssm_cblk=cblk, ssm_lam=lam, ssm_pw=pw,
            ssm_d=ssm_d[l].reshape(1, BRANCH), ssm_wglu=ssm_w_glu[l].astype(BF16),
            ssm_bglu=ssm_b_glu[l].reshape(1, BRANCH),
            mla_gq=mla_q_norm[l].reshape(1, MLA_Q_LORA), mla_gkv=mla_kv_norm[l].reshape(1, MLA_KV_LORA),
            mla_wq=wq_p, mla_wk=wk_p, mla_wv=wv_p,
            w_branch=w_branch, b_merge=b_merge[l].reshape(N_BRANCH, 1, D_MODEL), w_out=w_out))

    def fresh(bn):
        return dict(sp_r=jnp.zeros((bn, 1, BRANCH), F32), sp_k=jnp.zeros((bn, 1, BRANCH), F32),
                    sp_v=jnp.zeros((bn, 1, BRANCH), F32), sp_l=jnp.zeros((bn, 1, 2 * RW_LORA), F32),
                    wkv=jnp.zeros((bn, HEADS, HEAD, HEAD), F32),
                    ssm_re=jnp.zeros((bn, 1, SSM_W), F32), ssm_im=jnp.zeros((bn, 1, SSM_W), F32),
                    past=0)

    nl = cache_mla_ckv.shape[0]
    ckv_all = cache_mla_ckv.reshape(nl, bs * past, MLA_KV_LORA)
    kpe_all = cache_mla_kpe.reshape(nl, bs * past, MLA_ROPE)
    sbk_t = jnp.transpose(cache_sb_k, (0, 1, 3, 4, 2)).reshape(nl, bs, HEADS // 2, LANES, past)
    sbv_t = jnp.transpose(cache_sb_v, (0, 1, 3, 4, 2)).reshape(nl, bs, HEADS // 2, LANES, past)

    def carried(l):
        sh = state_rwkv_shift[l]
        return dict(sp_r=sh[:, :, :BRANCH], sp_k=sh[:, :, BRANCH:2 * BRANCH], sp_v=sh[:, :, 2 * BRANCH:3 * BRANCH],
                    sp_l=sh[:, :, 3 * BRANCH:], wkv=state_rwkv_wkv[l],
                    ssm_re=state_ssm_re[l].reshape(bs, 1, SSM_W), ssm_im=state_ssm_im[l].reshape(bs, 1, SSM_W),
                    past=past, ckv_all=ckv_all, kpe_all=kpe_all, sbk_t=sbk_t, sbv_t=sbv_t)

    xp, xs = x_prompt, x_sample
    new_p, new_s = [], []
    for l in range(depth):
        last = l == depth - 1
        xp, st_p = _layer(xp, 0, fresh(bp), layers[l], final_g, last)
        xs, st_s = _layer(xs, past, carried(l), layers[l], final_g, last)
        new_p.append(st_p)
        new_s.append(st_s)
    stk = lambda lst, i: jnp.stack([s[i] for s in lst], axis=0)
    return (xp, xs) + tuple(stk(new_p, i) for i in range(8)) + tuple(stk(new_s, i) for i in range(8))
```

```python
import functools
import math

import jax
import jax.numpy as jnp
from jax import lax
from jax.experimental import pallas as pl
from jax.experimental.pallas import tpu as pltpu

F32 = jnp.float32
BF16 = jnp.bfloat16

D_MODEL = 2048
BRANCH = 512
N_BRANCH = 4
EPS = 1e-6
CHUNK = 64
HEAD = 64
HEADS = 8
RW_LORA = 64
RW_IN = 3 * BRANCH + 2 * RW_LORA
RW_GN_EPS = 64e-5
SSM_GROUP = 16
SSM_GROUPS = 32
SSM_STATE = 64
SSM_W = SSM_GROUPS * SSM_STATE
MLA_NOPE = 64
MLA_ROPE = 32
MLA_Q_LORA = 384
MLA_KV_LORA = 256
MLA_SCALE = 1.0 / math.sqrt(MLA_NOPE + MLA_ROPE)
ROPE_BASE = 10000.0
SB_SCALE = 1.0 / math.sqrt(HEAD)
LOG2E = math.log2(math.e)
LANES = 128
MIB = 1024 * 1024

O_MERGE = 0
O_GATE = 8192
PB_W = 10240
O_QLAT = 0
O_LORA = 384
O_KVLAT = 512
O_KPE = 768
O_KPESW = 896
O_SBQ = 1024
O_SBK = 1536
O_SBV = 2048
O_RWR = 2560
O_RWK = 3072
O_RWV = 3584
O_SSM = 4096
PF_W = 4608
PROJ_W = PB_W + PF_W


def _cparams(sem, vmem_mib):
    return pltpu.CompilerParams(dimension_semantics=sem, vmem_limit_bytes=vmem_mib * MIB)


def _dot(a, b):
    return jnp.dot(a, b, preferred_element_type=F32)


def _dot_nt(a, b):
    return lax.dot_general(a, b, (((1,), (1,)), ((), ())), preferred_element_type=F32)


def _split3(x):
    h1 = x.astype(BF16)
    r1 = x - h1.astype(F32)
    h2 = r1.astype(BF16)
    h3 = (r1 - h2.astype(F32)).astype(BF16)
    return h1, h2, h3


def _softplus(x):
    return jnp.maximum(x, 0.0) + jnp.log1p(jnp.exp(-jnp.abs(x)))


def _sigmoid(x):
    return 0.5 * jnp.tanh(0.5 * x) + 0.5


def _norm_proj_kernel(x_ref, g_ref, w_ref, ob_ref, of_ref, h_ref, *, nb16):
    j = pl.program_id(1)

    @pl.when(j == 0)
    def _():
        x = x_ref[...]
        ms = jnp.mean(x * x, axis=-1, keepdims=True)
        h_ref[...] = (x * lax.rsqrt(ms + EPS) * g_ref[...]).astype(BF16)

    @pl.when(j < nb16)
    def _():
        ob_ref[...] = _dot(h_ref[...], w_ref[0]).astype(BF16)

    @pl.when(j >= nb16)
    def _():
        of_ref[...] = _dot(h_ref[...], w_ref[0])


def _norm_proj(x2, g, w_all, layer):
    n, d = x2.shape
    tm = min(n, 2048)
    tn = 512
    nb16 = PB_W // tn
    return pl.pallas_call(
        functools.partial(_norm_proj_kernel, nb16=nb16),
        out_shape=(jax.ShapeDtypeStruct((n, PB_W), BF16), jax.ShapeDtypeStruct((n, PF_W), F32)),
        grid=(n // tm, PROJ_W // tn),
        in_specs=[
            pl.BlockSpec((tm, d), lambda i, j: (i, 0), pipeline_mode=pl.Buffered(1)),
            pl.BlockSpec((1, d), lambda i, j: (0, 0)),
            pl.BlockSpec((1, d, tn), lambda i, j: (layer, 0, j)),
        ],
        out_specs=(pl.BlockSpec((tm, tn), lambda i, j: (i, jnp.minimum(j, nb16 - 1))),
                   pl.BlockSpec((tm, tn), lambda i, j: (i, jnp.maximum(j - nb16, 0)))),
        scratch_shapes=[pltpu.VMEM((tm, d), BF16)],
        compiler_params=_cparams(("parallel", "arbitrary"), 48),
        name="norm_proj",
    )(x2, g, w_all)


def _segsum(x, g2):
    outs = []
    for j in range(BRANCH // LANES):
        h1, h2, h3 = _split3(x[:, LANES * j:LANES * (j + 1)])
        outs.append(_dot(h1, g2) + _dot(h2, g2) + _dot(h3, g2))
    return jnp.concatenate(outs, axis=1)


def _rwkv_kernel(pr_ref, pk_ref, pv_ref, plo_ref, spr_ref, spk_ref, spv_ref, spl_ref, s0_ref,
                 pvec_ref, mul_ref, w2_ref, a2_ref, g2_ref,
                 y_ref, sout_ref,
                 st_ref, cr_ref, ck_ref, cv_ref, cl_ref, *, C, nb):
    c = pl.program_id(1)

    @pl.when(c == 0)
    def _():
        zero = jnp.zeros((HEAD, HEAD), F32)
        for bi in range(nb):
            for j in range(HEADS // 2):
                st_ref[bi, j] = jnp.concatenate(
                    [jnp.concatenate([s0_ref[bi, 2 * j], zero], axis=1),
                     jnp.concatenate([zero, s0_ref[bi, 2 * j + 1]], axis=1)], axis=0)
        cr_ref[...] = spr_ref[...]
        ck_ref[...] = spk_ref[...]
        cv_ref[...] = spv_ref[...]
        cl_ref[...] = spl_ref[...]

    pvec = pvec_ref[...]
    prow = lambda i: pvec[i:i + 1, :]
    g2 = g2_ref[...]

    def tshift(p_ref, carry_ref, mu):
        outs = []
        for bi in range(nb):
            p = p_ref[bi]
            rolled = pltpu.roll(p, 1, 0)
            rid = lax.broadcasted_iota(jnp.int32, p.shape, 0)
            prev = jnp.where(rid == 0, carry_ref[bi], rolled)
            carry_ref[bi] = p[C - 1:C, :]
            outs.append(p + (prev - p) * mu)
        return jnp.concatenate(outs, axis=0)

    r = tshift(pr_ref, cr_ref, prow(0))
    k = tshift(pk_ref, ck_ref, prow(1))
    v = tshift(pv_ref, cv_ref, prow(2))
    lo = tshift(plo_ref, cl_ref, mul_ref[...])

    log_w = -_softplus(-(prow(3) + _dot(jnp.tanh(lo).astype(BF16), w2_ref[...]))) - 0.5
    ld = -jnp.exp(log_w)
    a_icl = jax.nn.sigmoid(prow(4) + _dot(lo.astype(BF16), a2_ref[...]))
    kkr = k * prow(5)
    kk = kkr / jnp.maximum(jnp.sqrt(_segsum(kkr * kkr, g2)), 1e-12)
    k2 = k * (1.0 + (a_icl - 1.0) * prow(6))
    av = -kk
    bv = kk * a_icl

    sh = C.bit_length() - 1
    ri = lax.broadcasted_iota(jnp.int32, (nb * C, nb * C), 0)
    ci = lax.broadcasted_iota(jnp.int32, (nb * C, nb * C), 1)
    lincl = jnp.logical_and(ci <= ri, (ci >> sh) == (ri >> sh)).astype(BF16)
    h1, h2, h3 = _split3(ld)
    g = _dot(lincl, h1) + _dot(lincl, h2) + _dot(lincl, h3)
    g_lasts = [g[(bi + 1) * C - 1:(bi + 1) * C, :] for bi in range(nb)]
    g_last = jnp.concatenate([jnp.broadcast_to(gl, (C, BRANCH)) for gl in g_lasts], axis=0)
    eg = jnp.exp(g)
    eng = jnp.exp(-g)
    at = av * jnp.exp(g - ld)
    rt = r * eg
    bt = bv * eng
    kt = k2 * eng
    e_c = jnp.exp(g_last - g)
    b_end = bv * e_c
    k_end = k2 * e_c
    eg_lasts = [jnp.exp(gl) for gl in g_lasts]

    c2 = 2 * C
    npair = BRANCH // LANES
    r_i = lax.broadcasted_iota(jnp.int32, (c2, c2), 0)
    c_i = lax.broadcasted_iota(jnp.int32, (c2, c2), 1)
    c_m = jnp.where(c_i >= C, c_i - C, c_i)
    mhalf = c_m < jnp.where(r_i < C, r_i, r_i - C + 1)
    mfull = jnp.concatenate([mhalf, mhalf], axis=0)
    bd2 = (r_i >> sh) == (c_i >> sh)
    eye2 = (r_i == c_i).astype(F32)
    lane = lax.broadcasted_iota(jnp.int32, (1, LANES), 1)
    m0 = (lane < HEAD).astype(F32)
    m1 = (lane >= HEAD).astype(F32)
    v_i = lax.broadcasted_iota(jnp.int32, (LANES, LANES), 0)
    k_i = lax.broadcasted_iota(jnp.int32, (LANES, LANES), 1)
    bdmask = (v_i >> 6) == (k_i >> 6)
    zf = jnp.zeros((C, LANES), F32)
    zc = jnp.zeros((C, LANES), BF16)
    sls = [slice(LANES * j, LANES * (j + 1)) for j in range(npair)]

    units = [(bi, j) for bi in range(nb) for j in range(npair)]
    tile = lambda x, bi, j: x[bi * C:(bi + 1) * C, LANES * j:LANES * (j + 1)]
    ws, a_ss, s2s = [], [], []
    for bi, j in units:
        atj, rtj = tile(at, bi, j), tile(rt, bi, j)
        a0, r0, a1, r1 = atj * m0, rtj * m0, atj * m1, rtj * m1
        lhs_w = jnp.concatenate([jnp.concatenate([a0, zf], axis=1), jnp.concatenate([r0, zf], axis=1),
                                 jnp.concatenate([zf, a1], axis=1), jnp.concatenate([zf, r1], axis=1)],
                                axis=0).astype(BF16)
        btj, ktj = tile(bt, bi, j), tile(kt, bi, j)
        rhs_w = jnp.concatenate([jnp.concatenate([btj, ktj], axis=0),
                                 jnp.concatenate([ktj, btj], axis=0)], axis=1).astype(BF16)
        ws.append(jnp.where(mfull, _dot_nt(lhs_w, rhs_w), 0.0))
        lhs_s = jnp.concatenate([a0, r0, a1, r1], axis=0).astype(BF16)
        s2 = st_ref[bi, j]
        s2h, s2l, _ = _split3(s2)
        a_ss.append(_dot_nt(lhs_s, s2h) + _dot_nt(lhs_s, s2l))
        s2s.append(s2)

    n_pows = [jnp.where(bd2, jnp.concatenate([w[:C], w[c2:c2 + C]], axis=0), 0.0) for w in ws]
    t_mats = [eye2 + n for n in n_pows]
    lv = 2
    while lv < C:
        n_pows = [_dot(n.astype(BF16), n.astype(BF16)) for n in n_pows]
        t_mats = [t + _dot(t.astype(BF16), n.astype(BF16)) for t, n in zip(t_mats, n_pows)]
        lv *= 2

    wbs = [w.astype(BF16) for w in ws]
    v0s = [(tile(v, bi, j) * m0).astype(BF16) for bi, j in units]
    v1s = [(tile(v, bi, j) * m1).astype(BF16) for bi, j in units]
    xs = [jnp.concatenate([a_s[:C] + _dot(wb[:C], jnp.concatenate([zc, v0], axis=0)),
                           a_s[c2:c2 + C] + _dot(wb[c2:c2 + C], jnp.concatenate([v1, zc], axis=0))], axis=0)
          for a_s, wb, v0, v1 in zip(a_ss, wbs, v0s, v1s)]
    us = [_dot(t.astype(BF16), x.astype(BF16)) for t, x in zip(t_mats, xs)]
    ys = []
    for n, (bi, j) in enumerate(units):
        a_s, wb, u = a_ss[n], wbs[n], us[n]
        u0, u1 = u[:C], u[C:]
        y0 = a_s[C:c2] + _dot(wb[C:c2], jnp.concatenate([u0.astype(BF16), v0s[n]], axis=0))
        y1 = a_s[c2 + C:] + _dot(wb[c2 + C:], jnp.concatenate([v1s[n], u1.astype(BF16)], axis=0))
        ys.append(y0 + y1)
        uv = jnp.concatenate([u0 + u1, tile(v, bi, j)], axis=0)
        bk = jnp.concatenate([tile(b_end, bi, j), tile(k_end, bi, j)], axis=0).astype(BF16)
        upd = _dot(uv.T.astype(BF16), bk)
        st_ref[bi, j] = s2s[n] * eg_lasts[bi][:, sls[j]] + jnp.where(bdmask, upd, 0.0)
    y = jnp.concatenate([jnp.concatenate(ys[bi * npair:(bi + 1) * npair], axis=1) for bi in range(nb)],
                        axis=0)

    inv_n = 1.0 / HEAD
    yc = y - _segsum(y, g2) * inv_n
    yn = yc * lax.rsqrt(_segsum(yc * yc, g2) * inv_n + RW_GN_EPS) * prow(8) + prow(9)
    out = (yn + _segsum(r * k2 * prow(7), g2) * v).astype(y_ref.dtype)
    for bi in range(nb):
        y_ref[bi] = out[bi * C:(bi + 1) * C]

    @pl.when(c == pl.num_programs(1) - 1)
    def _():
        for bi in range(nb):
            for j in range(HEADS // 2):
                s2 = st_ref[bi, j]
                sout_ref[bi, 2 * j] = s2[:HEAD, :HEAD]
                sout_ref[bi, 2 * j + 1] = s2[HEAD:, HEAD:]


def _rwkv(proj3, sp_r, sp_k, sp_v, sp_l, s0, pvec, mu_l, w2p, a2p, g2, chunk):
    b, t, _ = proj3.shape
    nc = t // chunk
    nb = min(b, 4)
    cb = lambda off, wdt: off // wdt
    bspec = lambda off, wdt: pl.BlockSpec((nb, chunk, wdt), lambda i, c, o=cb(off, wdt): (i, c, o))
    row3 = lambda wdt: pl.BlockSpec((nb, 1, wdt), lambda i, c: (i, 0, 0))
    full2 = lambda a: pl.BlockSpec(a.shape, lambda i, c: (0, 0))
    return pl.pallas_call(
        functools.partial(_rwkv_kernel, C=chunk, nb=nb),
        out_shape=(jax.ShapeDtypeStruct((b, t, BRANCH), BF16),
                   jax.ShapeDtypeStruct((b, HEADS, HEAD, HEAD), F32)),
        grid=(b // nb, nc),
        in_specs=[
            bspec(O_RWR, BRANCH), bspec(O_RWK, BRANCH), bspec(O_RWV, BRANCH), bspec(O_LORA, LANES),
            row3(BRANCH), row3(BRANCH), row3(BRANCH), row3(LANES),
            pl.BlockSpec((nb, HEADS, HEAD, HEAD), lambda i, c: (i, 0, 0, 0)),
            full2(pvec), full2(mu_l), full2(w2p), full2(a2p), full2(g2),
        ],
        out_specs=(pl.BlockSpec((nb, chunk, BRANCH), lambda i, c: (i, c, 0)),
                   pl.BlockSpec((nb, HEADS, HEAD, HEAD), lambda i, c: (i, 0, 0, 0))),
        scratch_shapes=[pltpu.VMEM((nb, 4, LANES, LANES), F32),
                        pltpu.VMEM((nb, 1, BRANCH), F32), pltpu.VMEM((nb, 1, BRANCH), F32),
                        pltpu.VMEM((nb, 1, BRANCH), F32), pltpu.VMEM((nb, 1, LANES), F32)],
        compiler_params=_cparams(("parallel", "arbitrary"), 32),
        name="rwkv7",
    )(proj3, proj3, proj3, proj3, sp_r, sp_k, sp_v, sp_l, s0, pvec, mu_l, w2p, a2p, g2)


def _s5_kernel(u_ref, h0r_ref, h0i_ref, bblk_ref, cblk_ref, lam_ref, pw_ref, dsk_ref, wglu_ref, bglu_ref,
               y_ref, hro_ref, hio_ref, x_ref, cr_ref, ci_ref, *, tb, nb):
    @pl.when(pl.program_id(1) == 0)
    def _():
        cr_ref[...] = h0r_ref[...]
        ci_ref[...] = h0i_ref[...]

    u = jnp.concatenate([u_ref[bi] for bi in range(nb)], axis=0)
    x = _dot(u.astype(BF16), bblk_ref[...])
    xr = x[:, :SSM_W]
    xi = x[:, SSM_W:]
    for lvl, s in enumerate((1, 2, 4)):
        ar = lam_ref[2 * lvl]
        ai = lam_ref[2 * lvl + 1]
        sr = pltpu.roll(xr, s, 0)
        si = pltpu.roll(xi, s, 0)
        xr, xi = xr + (ar * sr - ai * si), xi + (ar * si + ai * sr)
    x_ref[:, :SSM_W] = xr
    x_ref[:, SSM_W:] = xi
    pwr = pw_ref[0:8, :]
    pwi = pw_ref[8:16, :]

    for b in range(nb):

        def body(gi, carry, b=b):
            cr, ci = carry
            o = pl.multiple_of(b * tb + gi * 8, 8)
            br = x_ref[pl.ds(o, 8), 0:SSM_W]
            bi = x_ref[pl.ds(o, 8), SSM_W:2 * SSM_W]
            br = br + pwr * cr - pwi * ci
            bi = bi + pwr * ci + pwi * cr
            x_ref[pl.ds(o, 8), 0:SSM_W] = br
            x_ref[pl.ds(o, 8), SSM_W:2 * SSM_W] = bi
            return br[7:8, :], bi[7:8, :]

        cr, ci = lax.fori_loop(0, tb // 8, body, (cr_ref[b], ci_ref[b]))
        cr_ref[b] = cr
        ci_ref[b] = ci
        hro_ref[b] = cr
        hio_ref[b] = ci
    y = _dot(x_ref[...].astype(BF16), cblk_ref[...]) + dsk_ref[...] * u
    g = jax.nn.gelu(y)
    out = (g * jax.nn.sigmoid(_dot(g.astype(BF16), wglu_ref[...]) + bglu_ref[...])).astype(y_ref.dtype)
    for b in range(nb):
        y_ref[b] = out[b * tb:(b + 1) * tb]


def _s5(proj3, h0r, h0i, bblk, cblk, lam, pw, dsk, wglu, bglu):
    b, t, _ = proj3.shape
    tb = min(t, 256)
    nb = b if t <= 64 else 1
    rows = nb * tb
    sub = jnp.arange(rows)[:, None] & 7
    lam = jnp.stack([jnp.where(sub >= s, lam[2 * lv + c:2 * lv + c + 1], 0.0)
                     for lv, s in enumerate((1, 2, 4)) for c in (0, 1)])
    full2 = lambda a: pl.BlockSpec(a.shape, lambda i, c: (0,) * a.ndim, pipeline_mode=pl.Buffered(1))
    row3 = pl.BlockSpec((nb, 1, SSM_W), lambda i, c: (i, 0, 0))
    return pl.pallas_call(
        functools.partial(_s5_kernel, tb=tb, nb=nb),
        out_shape=(jax.ShapeDtypeStruct((b, t, BRANCH), BF16),
                   jax.ShapeDtypeStruct((b, 1, SSM_W), F32),
                   jax.ShapeDtypeStruct((b, 1, SSM_W), F32)),
        grid=(b // nb, t // tb),
        in_specs=[pl.BlockSpec((nb, tb, BRANCH), lambda i, c: (i, c, O_SSM // BRANCH)),
                  row3, row3, full2(bblk), full2(cblk), full2(lam), full2(pw), full2(dsk),
                  full2(wglu), full2(bglu)],
        out_specs=(pl.BlockSpec((nb, tb, BRANCH), lambda i, c: (i, c, 0)), row3, row3),
        scratch_shapes=[pltpu.VMEM((rows, 2 * SSM_W), F32),
                        pltpu.VMEM((nb, 1, SSM_W), F32), pltpu.VMEM((nb, 1, SSM_W), F32)],
        compiler_params=_cparams(("parallel", "arbitrary"), 48),
        name="s5",
    )(proj3, h0r, h0i, bblk, cblk, lam, pw, dsk, wglu, bglu)


def _mla_prep_kernel(ql_ref, kvl_ref, kpe_ref, kpesw_ref, tq_ref, tc_ref, ts_ref,
                     gq_ref, gkv_ref, wq_ref, q_ref, ckv_ref, kr_ref):
    ql = ql_ref[...]
    qn = ql * lax.rsqrt(jnp.mean(ql * ql, axis=-1, keepdims=True) + EPS) * gq_ref[...]
    q = _dot(qn.astype(BF16), wq_ref[...])
    tq = tq_ref[...]
    q_ref[...] = jnp.concatenate(
        [q[:, LANES * h:LANES * (h + 1)] * tq for h in range(HEADS)], axis=1).astype(BF16)
    kvl = kvl_ref[...]
    ckv_ref[...] = kvl * lax.rsqrt(jnp.mean(kvl * kvl, axis=-1, keepdims=True) + EPS) * gkv_ref[...]
    kr = kpe_ref[...] * tc_ref[...] + kpesw_ref[...] * ts_ref[...]
    kr_ref[...] = kr[:, :MLA_ROPE]


def _mla_prep(proj, t, tq_tab, tc_tab, ts_tab, gq, gkv, wq):
    n = proj.shape[0]
    tm = min(t, 512)
    nt = t // tm
    tab = pl.BlockSpec((tm, LANES), lambda i: (i % nt, 0))
    full2 = lambda a: pl.BlockSpec(a.shape, lambda i: (0, 0))
    return pl.pallas_call(
        _mla_prep_kernel,
        out_shape=(jax.ShapeDtypeStruct((n, HEADS * LANES), BF16),
                   jax.ShapeDtypeStruct((n, MLA_KV_LORA), F32),
                   jax.ShapeDtypeStruct((n, MLA_ROPE), F32)),
        grid=(n // tm,),
        in_specs=[pl.BlockSpec((tm, MLA_Q_LORA), lambda i: (i, O_QLAT // MLA_Q_LORA)),
                  pl.BlockSpec((tm, MLA_KV_LORA), lambda i: (i, O_KVLAT // MLA_KV_LORA)),
                  pl.BlockSpec((tm, LANES), lambda i: (i, O_KPE // LANES)),
                  pl.BlockSpec((tm, LANES), lambda i: (i, O_KPESW // LANES)),
                  tab, tab, tab, full2(gq), full2(gkv), full2(wq)],
        out_specs=(pl.BlockSpec((tm, HEADS * LANES), lambda i: (i, 0)),
                   pl.BlockSpec((tm, MLA_KV_LORA), lambda i: (i, 0)),
                   pl.BlockSpec((tm, MLA_ROPE), lambda i: (i, 0))),
        compiler_params=_cparams(("parallel",), 32),
        name="mla_prep",
    )(proj, proj, proj, proj, tq_tab, tc_tab, ts_tab, gq, gkv, wq)


def _mla_kv_kernel(ckv_ref, kr_ref, wk_ref, wv_ref, dup_ref, k_ref, v_ref):
    cb = ckv_ref[0].astype(BF16)
    kn = _dot(cb, wk_ref[...])
    krd = _dot(kr_ref[0].astype(BF16), dup_ref[...])
    k_ref[...] = jnp.concatenate(
        [kn[:, LANES * h:LANES * (h + 1)] + krd for h in range(HEADS)], axis=1).astype(BF16)
    v_ref[...] = _dot(cb, wv_ref[...]).astype(BF16)


def _mla_kv(ckv, kr, wk, wv, dup, layer=0):
    n = ckv.shape[1]
    tm = next((c for c in (512, LANES) if n % c == 0), n)
    full2 = lambda a: pl.BlockSpec(a.shape, lambda i: (0, 0))
    return pl.pallas_call(
        _mla_kv_kernel,
        out_shape=(jax.ShapeDtypeStruct((n, HEADS * LANES), BF16),
                   jax.ShapeDtypeStruct((n, BRANCH), BF16)),
        grid=(n // tm,),
        in_specs=[pl.BlockSpec((1, tm, MLA_KV_LORA), lambda i: (layer, i, 0)),
                  pl.BlockSpec((1, tm, MLA_ROPE), lambda i: (layer, i, 0)),
                  full2(wk), full2(wv), full2(dup)],
        out_specs=(pl.BlockSpec((tm, HEADS * LANES), lambda i: (i, 0)),
                   pl.BlockSpec((tm, BRANCH), lambda i: (i, 0))),
        compiler_params=_cparams(("parallel",), 32),
        name="mla_kv",
    )(ckv, kr, wk, wv, dup)


def _assemble_keys(past_ref, new_ref, all_ref, past, t):
    all_ref[0:past, :] = past_ref[0].astype(BF16)
    all_ref[past:past + t, :] = new_ref[0].astype(BF16)
    pad = all_ref.shape[0] - past - t
    if pad:
        all_ref[past + t:, :] = jnp.zeros((pad, all_ref.shape[1]), BF16)


def _mla_attn_kernel(q_ref, k_ref, v_ref, *rest, tq, tk, q_off, s_valid, hp, past):
    if past:
        kp_ref, vp_ref, o_ref, kall_ref, vall_ref = rest
        _assemble_keys(kp_ref, k_ref, kall_ref, past, tq)
        _assemble_keys(vp_ref, v_ref, vall_ref, past, tq)
        kload = lambda o: kall_ref[pl.ds(o, tk), :]
        vload = lambda o: vall_ref[pl.ds(o, tk), :]
    else:
        (o_ref,) = rest
        kload = lambda o: k_ref[0, pl.ds(o, tk), :]
        vload = lambda o: v_ref[0, pl.ds(o, tk), :]
    i = pl.program_id(2)
    q_lo = q_off + i * tq
    limit = jnp.minimum(q_lo + tq, s_valid)
    nkb = (limit + tk - 1) // tk
    qidx = q_lo + lax.broadcasted_iota(jnp.int32, (tq, tk), 0)
    kloc = lax.broadcasted_iota(jnp.int32, (tq, tk), 1)
    nh = 2 * hp
    q2 = q_ref[0]
    qs = [q2[:, LANES * h:LANES * (h + 1)] for h in range(nh)]

    def body(kb, carry, masked):
        o = pl.multiple_of(kb * tk, tk)
        kblk = kload(o)
        vblk = vload(o)
        ss = [_dot_nt(qs[h], kblk[:, LANES * h:LANES * (h + 1)]) for h in range(nh)]
        if masked:
            kidx = kloc + kb * tk
            vis = jnp.logical_and((kidx >> 6) <= (qidx >> 6), kidx < s_valid)
            ss = [jnp.where(vis, s, -1e30) for s in ss]
        m_new = [jnp.maximum(carry[h][0], jnp.max(ss[h], axis=-1, keepdims=True)) for h in range(nh)]
        ps = [jnp.exp2(ss[h] - m_new[h]) for h in range(nh)]
        pv = [_dot(ps[h].astype(BF16), vblk[:, LANES * (h // 2):LANES * (h // 2 + 1)]) for h in range(nh)]
        new = []
        for h in range(nh):
            m_prev, l_prev, acc = carry[h]
            alpha = jnp.exp2(m_prev - m_new[h])
            new.append((m_new[h], alpha * l_prev + jnp.sum(ps[h], axis=-1, keepdims=True), alpha * acc + pv[h]))
        return tuple(new)

    init = tuple((jnp.full((tq, 1), -1e30, F32), jnp.zeros((tq, 1), F32), jnp.zeros((tq, LANES), F32))
                 for _ in range(nh))
    nfull = jnp.minimum(((q_lo >> 6) + 1) * CHUNK, s_valid) // tk
    res = lax.fori_loop(0, nfull, lambda kb, c: body(kb, c, False), init)
    res = lax.fori_loop(nfull, nkb, lambda kb, c: body(kb, c, True), res)
    lane = lax.broadcasted_iota(jnp.int32, (tq, LANES), 1)
    outs = [jnp.where(lane < HEAD, res[2 * p][2] / res[2 * p][1], res[2 * p + 1][2] / res[2 * p + 1][1])
            for p in range(hp)]
    o_ref[0] = jnp.concatenate(outs, axis=1).astype(o_ref.dtype)


def _mla_prompt_kernel(q_ref, k_ref, v_ref, o_ref, *, tq):
    i = pl.program_id(2)
    q_lo = i * tq
    nkb = i + 1
    w = 2 * tq
    cidx = lax.broadcasted_iota(jnp.int32, (tq, w), 1)
    qidx = q_lo + jnp.where(cidx >= tq, cidx - tq, cidx)
    kloc = lax.broadcasted_iota(jnp.int32, (tq, w), 0)
    q2 = q_ref[0]
    zq = jnp.zeros((tq, LANES), BF16)
    qbd = jnp.concatenate([jnp.concatenate([q2[:, :LANES], zq], axis=1),
                           jnp.concatenate([zq, q2[:, LANES:]], axis=1)], axis=0)

    def group(kb, carry, n):
        m_prev, l_prev, acc_t = carry
        o = pl.multiple_of((kb - (n - 1)) * tq, tq)
        kn = k_ref[0, pl.ds(o, n * tq), :]
        vn_t = v_ref[0, pl.ds(o, n * tq), :].astype(F32).T.astype(BF16)
        s = _dot_nt(kn, qbd)
        vis = ((kloc + kb * tq) >> 6) <= (qidx >> 6)
        s_r = jnp.where(vis, s[(n - 1) * tq:], -1e30)
        s = jnp.concatenate([s[:(n - 1) * tq], s_r], axis=0) if n > 1 else s_r
        m_new = jnp.maximum(m_prev, jnp.max(s, axis=0, keepdims=True))
        alpha = jnp.exp2(m_prev - m_new)
        p = jnp.exp2(s - m_new)
        l_new = alpha * l_prev + jnp.sum(p, axis=0, keepdims=True)
        return m_new, l_new, alpha * acc_t + _dot(vn_t, p.astype(BF16))

    carry = (jnp.full((1, w), -1e30, F32), jnp.zeros((1, w), F32), jnp.zeros((LANES, w), F32))
    n4, n2, n1 = nkb // 4, (nkb % 4) // 2, nkb % 2
    carry = lax.fori_loop(0, n4, lambda s, c: group(nkb - 1 - 4 * s, c, 4), carry)
    carry = lax.fori_loop(0, n2, lambda s, c: group(nkb - 1 - 4 * n4, c, 2), carry)
    carry = lax.fori_loop(0, n1, lambda s, c: group(0, c, 1), carry)
    out_t = carry[2] / carry[1]
    row = lax.broadcasted_iota(jnp.int32, (LANES, tq), 0)
    o_ref[0] = jnp.where(row < HEAD, out_t[:, :tq], out_t[:, tq:]).T.astype(o_ref.dtype)


def _mla_prompt(q3, k3, v3, tq):
    b, t, _ = q3.shape
    assert tq % CHUNK == 0 and t % tq == 0
    return pl.pallas_call(
        functools.partial(_mla_prompt_kernel, tq=tq),
        out_shape=jax.ShapeDtypeStruct((b, t, BRANCH), BF16),
        grid=(b, HEADS // 2, t // tq),
        in_specs=[pl.BlockSpec((1, tq, 2 * LANES), lambda bi, j, i: (bi, i, j)),
                  pl.BlockSpec((1, t, 2 * LANES), lambda bi, j, i: (bi, 0, j)),
                  pl.BlockSpec((1, t, LANES), lambda bi, j, i: (bi, 0, j))],
        out_specs=pl.BlockSpec((1, tq, LANES), lambda bi, j, i: (bi, i, j)),
        compiler_params=_cparams(("parallel", "parallel", "arbitrary"), 40),
        name="mla_prompt",
    )(q3, k3, v3)


def _mla_attn(q3, k3, v3, kp3, vp3, s_pad, tq, tk, hp):
    b, t, _ = q3.shape
    past = 0 if kp3 is None else kp3.shape[1]
    rows = k3.shape[1]
    in_specs = [pl.BlockSpec((1, tq, 2 * LANES * hp), lambda bi, j, i: (bi, i, j)),
                pl.BlockSpec((1, rows, 2 * LANES * hp), lambda bi, j, i: (bi, 0, j)),
                pl.BlockSpec((1, rows, LANES * hp), lambda bi, j, i: (bi, 0, j))]
    args = [q3, k3, v3]
    scratch = []
    if past:
        in_specs += [pl.BlockSpec((1, past, 2 * LANES * hp), lambda bi, j, i: (bi, 0, j)),
                     pl.BlockSpec((1, past, LANES * hp), lambda bi, j, i: (bi, 0, j))]
        args += [kp3, vp3]
        scratch = [pltpu.VMEM((s_pad, 2 * LANES * hp), BF16), pltpu.VMEM((s_pad, LANES * hp), BF16)]
    return pl.pallas_call(
        functools.partial(_mla_attn_kernel, tq=tq, tk=tk, q_off=past, s_valid=past + t, hp=hp, past=past),
        out_shape=jax.ShapeDtypeStruct((b, t, BRANCH), BF16),
        grid=(b, 4 // hp, t // tq),
        in_specs=in_specs,
        out_specs=pl.BlockSpec((1, tq, LANES * hp), lambda bi, j, i: (bi, i, j)),
        scratch_shapes=scratch,
        compiler_params=_cparams(("parallel", "parallel", "arbitrary"), 40),
        name="mla_attn",
    )(*args)


def _sb_block(zs, vis, us, carry, pv):
    nh = len(zs)
    tq = zs[0].shape[0]
    zls = [jnp.minimum(z, 0.0) - jnp.log(1.0 + jnp.exp(-jnp.abs(z))) for z in zs]
    lgs = [zl - z for zl, z in zip(zls, zs)]
    if vis is not None:
        lgs = [jnp.where(vis, lg, 0.0) for lg in lgs]
    his = [lg.astype(BF16) for lg in lgs]
    los = [(lg - hi.astype(F32)).astype(BF16) for lg, hi in zip(lgs, his)]
    later = _dot(jnp.concatenate(his + los, axis=0), us)
    new = []
    for h in range(nh):
        csum, acc = carry[h]
        tot = later[h * tq:(h + 1) * tq] + later[(nh + h) * tq:(nh + h + 1) * tq] + csum
        a = jnp.exp(zls[h] + tot)
        if vis is not None:
            a = jnp.where(vis, a, 0.0)
        new.append((csum + jnp.sum(lgs[h], axis=-1, keepdims=True), acc + pv(h, a.astype(BF16))))
    return tuple(new)


def _sb_heads(q2, nh, scale=SB_SCALE):
    lane1 = lax.broadcasted_iota(jnp.int32, (1, LANES), 1)
    hm = ((lane1 < HEAD).astype(F32) * scale, (lane1 >= HEAD).astype(F32) * scale)
    return [(q2[:, LANES * (h // 2):LANES * (h // 2 + 1)] * hm[h % 2]).astype(BF16) for h in range(nh)]


def _sb_finish(res, o_ref, tq):
    lane = lax.broadcasted_iota(jnp.int32, (tq, LANES), 1)
    outs = [jnp.where(lane < HEAD, res[2 * p][1], res[2 * p + 1][1]) for p in range(len(res) // 2)]
    o_ref[0] = jnp.concatenate(outs, axis=1).astype(o_ref.dtype)


def _sb_attn_kernel(q_ref, k_ref, v_ref, us_ref, o_ref, *, tq, tk):
    i = pl.program_id(2)
    q_lo = i * tq
    nkb = (q_lo + tq - 2) // tk + 1
    cidx = lax.broadcasted_iota(jnp.int32, (tk, 2 * tq), 1)
    qidx = q_lo + jnp.where(cidx >= tq, cidx - tq, cidx)
    kloc = lax.broadcasted_iota(jnp.int32, (tk, 2 * tq), 0)
    qs = jnp.concatenate(_sb_heads(q_ref[0], 2, SB_SCALE * LOG2E), axis=0)
    us = us_ref[...]

    w = 2 * tq

    def group(kb, carry, n):
        csum, acc_t = carry
        o = pl.multiple_of((kb - (n - 1)) * tk, tk)
        kn = k_ref[0, pl.ds(o, n * tk), :].astype(BF16)
        vn_t = v_ref[0, pl.ds(o, n * tk), :].T.astype(BF16)
        z = _dot_nt(kn, qs)
        zl = jnp.minimum(z, 0.0) - jnp.log2(1.0 + jnp.exp2(-jnp.abs(z)))
        lg = zl - z
        vis = (kloc + kb * tk) < qidx
        lgs = [lg[j * tk:(j + 1) * tk] for j in range(n - 1)] + [jnp.where(vis, lg[(n - 1) * tk:], 0.0)]
        his = [x.astype(BF16) for x in lgs]
        los = [(x - h.astype(F32)).astype(BF16) for x, h in zip(lgs, his)]
        later = _dot(us, jnp.concatenate([jnp.concatenate([his[j], los[j]], axis=0) for j in range(n)], axis=1))
        tots = [None] * n
        for j in reversed(range(n)):
            tots[j] = later[:, j * w:(j + 1) * w] + csum
            csum = csum + jnp.sum(lgs[j], axis=0, keepdims=True)
        a = jnp.exp2(zl + jnp.concatenate(tots, axis=0))
        a = jnp.concatenate([a[:(n - 1) * tk], jnp.where(vis, a[(n - 1) * tk:], 0.0)], axis=0) if n > 1 \
            else jnp.where(vis, a, 0.0)
        return csum, acc_t + _dot(vn_t, a.astype(BF16))

    carry = (jnp.zeros((1, 2 * tq), F32), jnp.zeros((LANES, 2 * tq), F32))
    n4, n2, n1 = nkb // 4, (nkb % 4) // 2, nkb % 2
    carry = lax.fori_loop(0, n4, lambda s, c: group(nkb - 1 - 4 * s, c, 4), carry)
    carry = lax.fori_loop(0, n2, lambda s, c: group(nkb - 1 - 4 * n4, c, 2), carry)
    carry = lax.fori_loop(0, n1, lambda s, c: group(0, c, 1), carry)
    acc_t = carry[1]
    row = lax.broadcasted_iota(jnp.int32, (LANES, tq), 0)
    o_ref[0] = jnp.where(row < HEAD, acc_t[:, :tq], acc_t[:, tq:]).T.astype(o_ref.dtype)


def _sb_decode_kernel(q_ref, k_ref, v_ref, kpt_ref, vpt_ref, usn_ref, usp_ref, o_ref, *, t, past, pb):
    qh = _sb_heads(q_ref[0], HEADS)
    kn = k_ref[0].astype(BF16)
    vn = v_ref[0].astype(BF16)
    pair = lambda x, h: x[:, LANES * (h // 2):LANES * (h // 2 + 1)]
    ri = lax.broadcasted_iota(jnp.int32, (t, t), 0)
    ci = lax.broadcasted_iota(jnp.int32, (t, t), 1)
    carry = tuple((jnp.zeros((t, 1), F32), jnp.zeros((t, LANES), F32)) for _ in range(HEADS))
    zs = [_dot_nt(qh[h], pair(kn, h)) for h in range(HEADS)]
    carry = _sb_block(zs, ci < ri, usn_ref[...], carry, lambda h, a: _dot(a, pair(vn, h)))
    usp = usp_ref[...]
    for blk in reversed(range(past // pb)):
        kts = [kpt_ref[0, 0, p, :, blk * pb:(blk + 1) * pb].astype(BF16) for p in range(HEADS // 2)]
        vts = [vpt_ref[0, 0, p, :, blk * pb:(blk + 1) * pb].astype(BF16) for p in range(HEADS // 2)]
        zs = [_dot(qh[h], kts[h // 2]) for h in range(HEADS)]
        carry = _sb_block(zs, None, usp, carry, lambda h, a: _dot_nt(a, vts[h // 2]))
    _sb_finish(carry, o_ref, t)


def _strict_upper(n):
    idx = jnp.arange(n)
    return (idx[:, None] > idx[None, :]).astype(BF16)


def _sb_attn(proj3, tq, tk):
    b, t, _ = proj3.shape
    assert tq == tk
    u = _strict_upper(tk).T
    us = jnp.concatenate([u, u], axis=1)
    return pl.pallas_call(
        functools.partial(_sb_attn_kernel, tq=tq, tk=tk),
        out_shape=jax.ShapeDtypeStruct((b, t, BRANCH), BF16),
        grid=(b, HEADS // 2, t // tq),
        in_specs=[pl.BlockSpec((1, tq, LANES), lambda bi, j, i: (bi, i, O_SBQ // LANES + j)),
                  pl.BlockSpec((1, t, LANES), lambda bi, j, i: (bi, 0, O_SBK // LANES + j)),
                  pl.BlockSpec((1, t, LANES), lambda bi, j, i: (bi, 0, O_SBV // LANES + j)),
                  pl.BlockSpec(us.shape, lambda bi, j, i: (0, 0))],
        out_specs=pl.BlockSpec((1, tq, LANES), lambda bi, j, i: (bi, i, j)),
        compiler_params=_cparams(("parallel", "parallel", "arbitrary"), 48),
        name="sb_attn",
    )(proj3, proj3, proj3, us)


def _sb_decode(proj3, kpt, vpt, layer, pb):
    b, t, _ = proj3.shape
    past = kpt.shape[4]
    usn, usp = _strict_upper(t), _strict_upper(pb)
    new = lambda off: pl.BlockSpec((1, t, BRANCH), lambda bi, o=off // BRANCH: (bi, 0, o))
    cache = pl.BlockSpec((1, 1, HEADS // 2, LANES, past), lambda bi: (layer, bi, 0, 0, 0))
    return pl.pallas_call(
        functools.partial(_sb_decode_kernel, t=t, past=past, pb=pb),
        out_shape=jax.ShapeDtypeStruct((b, t, BRANCH), BF16),
        grid=(b,),
        in_specs=[new(O_SBQ), new(O_SBK), new(O_SBV), cache, cache,
                  pl.BlockSpec(usn.shape, lambda bi: (0, 0)), pl.BlockSpec(usp.shape, lambda bi: (0, 0))],
        out_specs=pl.BlockSpec((1, t, BRANCH), lambda bi: (bi, 0, 0)),
        compiler_params=_cparams(("parallel",), 48),
        name="sb_decode",
    )(proj3, proj3, proj3, kpt, vpt, usn, usp)


def _mix_kernel(yrw_ref, yssm_ref, ymla_ref, ysb_ref, gate_ref, pm_ref, wb_ref, bm_ref, o_ref, acc_ref):
    n = pl.program_id(1)

    @pl.when(n == 0)
    def _():
        acc_ref[...] = jnp.zeros_like(acc_ref)

    def branch(y_ref):
        g = gate_ref[...].astype(F32)
        gated_half = y_ref[...].astype(F32) * (g * (0.5 * _sigmoid(g)))
        up_half = _dot(gated_half.astype(BF16), wb_ref[0, 0].astype(BF16))
        acc_ref[...] += (jnp.tanh(0.5 * (pm_ref[...].astype(F32) + bm_ref[0])) + 1.0) * up_half

    for idx, y_ref in enumerate((yrw_ref, yssm_ref, ymla_ref, ysb_ref)):
        pl.when(n == idx)(functools.partial(branch, y_ref))

    @pl.when(n == N_BRANCH - 1)
    def _():
        o_ref[...] = acc_ref[...].astype(BF16)


def _out_kernel(m_ref, x_ref, wo_ref, fg_ref, o_ref, *, final):
    xn = x_ref[...] + _dot(m_ref[...], wo_ref[0].astype(BF16))
    if final:
        xn = xn * lax.rsqrt(jnp.mean(xn * xn, axis=-1, keepdims=True) + EPS) * fg_ref[...]
    o_ref[...] = xn


def _merge(ys, proj, x2, wb, bm, wo, fg, final, layer):
    n, d = x2.shape
    tm = min(n, 1024)
    ysp = pl.BlockSpec((tm, BRANCH), lambda i, k: (i, 0))
    mixed = pl.pallas_call(
        _mix_kernel,
        out_shape=jax.ShapeDtypeStruct((n, d), BF16),
        grid=(n // tm, N_BRANCH),
        in_specs=[ysp, ysp, ysp, ysp,
                  pl.BlockSpec((tm, BRANCH), lambda i, k: (i, O_GATE // BRANCH + k)),
                  pl.BlockSpec((tm, d), lambda i, k: (i, O_MERGE // d + k)),
                  pl.BlockSpec((1, 1, BRANCH, d), lambda i, k: (layer, k, 0, 0)),
                  pl.BlockSpec((1, 1, d), lambda i, k: (k, 0, 0))],
        out_specs=pl.BlockSpec((tm, d), lambda i, k: (i, 0)),
        scratch_shapes=[pltpu.VMEM((tm, d), F32)],
        compiler_params=_cparams(("parallel", "arbitrary"), 56),
        name="branch_mix",
    )(*ys, proj, proj, wb, bm)
    to = min(n, 512)
    return pl.pallas_call(
        functools.partial(_out_kernel, final=final),
        out_shape=jax.ShapeDtypeStruct((n, d), F32),
        grid=(n // to,),
        in_specs=[pl.BlockSpec((to, d), lambda i: (i, 0)),
                  pl.BlockSpec((to, d), lambda i: (i, 0)),
                  pl.BlockSpec((1, d, d), lambda i: (layer, 0, 0), pipeline_mode=pl.Buffered(1)),
                  pl.BlockSpec((1, d), lambda i: (0, 0))],
        out_specs=pl.BlockSpec((to, d), lambda i: (i, 0)),
        compiler_params=_cparams(("parallel",), 48),
        name="out_proj",
    )(mixed, x2, wo, fg)


_SRC_SSM = RW_IN
_SRC_QLAT = _SRC_SSM + BRANCH
_SRC_KVLAT = _SRC_QLAT + MLA_Q_LORA
_SRC_KPE = _SRC_KVLAT + MLA_KV_LORA
_SRC_SB = _SRC_KPE + MLA_ROPE
_SRC_GATE = _SRC_SB + 3 * BRANCH
_SRC_MERGE = _SRC_GATE + N_BRANCH * BRANCH
_SRC_LORA = 3 * BRANCH
_PW_BLK = 4 * LANES
_PW_NM = N_BRANCH * D_MODEL // _PW_BLK
_PW_NG = _PW_NM + N_BRANCH * BRANCH // _PW_BLK
_PW_ROPE = _PW_NG + 1
_PW_TAIL = {_PW_NG: (_SRC_QLAT, _SRC_QLAT + LANES, _SRC_QLAT + 2 * LANES, _SRC_LORA),
            _PW_ROPE: (_SRC_KVLAT, _SRC_KVLAT + LANES, _SRC_KPE, _SRC_KPE)}


def _permute_src_col(j, k):
    sb0 = _PW_ROPE + 1
    rw0 = sb0 + 3 * BRANCH // _PW_BLK
    ssm0 = rw0 + 3 * BRANCH // _PW_BLK
    u = MLA_ROPE
    col = jnp.where(j < _PW_NM, _SRC_MERGE // u + _PW_BLK // u * j,
                    jnp.where(j < _PW_NG, _SRC_GATE // u + _PW_BLK // u * (j - _PW_NM),
                              jnp.where(j < rw0, _SRC_SB // u + _PW_BLK // u * (j - sb0),
                                        jnp.where(j < ssm0, _PW_BLK // u * (j - rw0), _SRC_SSM // u))))
    col = col + LANES // u * k
    for jj, src in _PW_TAIL.items():
        col = jnp.where(j == jj, src[k] // u, col)
    return col * u


def _permute_kernel(x0_ref, x1_ref, x2_ref, x3_ref, o_ref):
    j = pl.program_id(1)
    wins = (x0_ref, x1_ref, x2_ref, x3_ref)
    half = MLA_ROPE // 2

    def put(k, rows):
        o_ref[0, :, LANES * k:LANES * (k + 1)] = rows.T.astype(BF16)

    @pl.when(j != _PW_ROPE)
    def _():
        for k in range(4):
            put(k, wins[k][0])

    @pl.when(j == _PW_ROPE)
    def _():
        for k in range(2):
            put(k, wins[k][0])
        kpe = wins[2][0]
        row = lax.broadcasted_iota(jnp.int32, kpe.shape, 0)
        put(2, jnp.where(row < MLA_ROPE, kpe, 0.0))
        zeros = jnp.zeros((LANES - MLA_ROPE, kpe.shape[1]), F32)
        put(3, jnp.concatenate([-kpe[half:MLA_ROPE], kpe[:half], zeros], axis=0))


def _permute_w_in(w_in):
    depth, d, _ = w_in.shape
    wt = jnp.swapaxes(w_in, 1, 2)
    win = lambda k: pl.BlockSpec((pl.Element(1), pl.Element(LANES), pl.Element(d)),
                                 lambda l, j, k=k: (l, _permute_src_col(j, k), 0))
    return pl.pallas_call(
        _permute_kernel,
        out_shape=jax.ShapeDtypeStruct((depth, d, PROJ_W), BF16),
        grid=(depth, PROJ_W // _PW_BLK),
        in_specs=[win(k) for k in range(4)],
        out_specs=pl.BlockSpec((1, d, _PW_BLK), lambda l, j: (l, 0, j)),
        compiler_params=_cparams(("parallel", "parallel"), 32),
        name="permute_w_in",
    )(wt, wt, wt, wt)


def _rope_tables(pos):
    half = MLA_ROPE // 2
    inv = ROPE_BASE ** (-jnp.arange(half, dtype=F32) / half)
    ang = pos.astype(F32)[:, None] * inv
    cos, sin = jnp.cos(ang), jnp.sin(ang)
    t = pos.shape[0]
    cc = jnp.concatenate([cos, cos], axis=1)
    ss = jnp.concatenate([sin, sin], axis=1)
    tq = jnp.concatenate([jnp.ones((t, MLA_NOPE), F32), cc, ss], axis=1) * (MLA_SCALE * LOG2E)
    pad = jnp.zeros((t, LANES - MLA_ROPE), F32)
    return tq, jnp.concatenate([cc, pad], axis=1), jnp.concatenate([ss, pad], axis=1)


def _mla_weights(w_q_up, w_kv_up):
    half = MLA_ROPE // 2
    wq = w_q_up.reshape(MLA_Q_LORA, HEADS, MLA_NOPE + MLA_ROPE)
    x1 = wq[:, :, MLA_NOPE:MLA_NOPE + half]
    x2 = wq[:, :, MLA_NOPE + half:]
    wq_p = jnp.concatenate([wq, -x2, x1], axis=2).reshape(MLA_Q_LORA, HEADS * LANES).astype(BF16)
    wkv = w_kv_up.reshape(MLA_KV_LORA, HEADS, 2 * HEAD)
    wk_p = jnp.concatenate([wkv[:, :, :HEAD], jnp.zeros((MLA_KV_LORA, HEADS, HEAD), F32)], axis=2)
    wk_p = wk_p.reshape(MLA_KV_LORA, HEADS * LANES).astype(BF16)
    wv_p = wkv[:, :, HEAD:].reshape(MLA_KV_LORA, BRANCH).astype(BF16)
    return wq_p, wk_p, wv_p


def _s5_tables(lam_re, lam_im, log_dt, b_re, b_im, c_re, c_im):
    dt = jnp.exp(log_dt)[:, None]
    mag = jnp.exp(lam_re * dt)
    ang = lam_im * dt
    lb_re, lb_im = mag * jnp.cos(ang), mag * jnp.sin(ang)
    nr, ni = lb_re - 1.0, lb_im
    den = lam_re * lam_re + lam_im * lam_im
    f_re = (nr * lam_re + ni * lam_im) / den
    f_im = (ni * lam_re - nr * lam_im) / den
    bb_re = f_re[..., None] * b_re - f_im[..., None] * b_im
    bb_im = f_re[..., None] * b_im + f_im[..., None] * b_re
    eye = jnp.eye(SSM_GROUPS, dtype=F32)
    blk_in = lambda m: jnp.einsum('gpc,gh->gchp', m, eye).reshape(BRANCH, SSM_W)
    blk_out = lambda m: jnp.einsum('gcp,gh->gphc', m, eye).reshape(SSM_W, BRANCH)
    bblk = jnp.concatenate([blk_in(bb_re), blk_in(bb_im)], axis=1).astype(BF16)
    cblk = jnp.concatenate([blk_out(c_re), blk_out(-c_im)], axis=0).astype(BF16)

    def power(j):
        m = jnp.exp(lam_re * dt * j)
        return (m * jnp.cos(ang * j)).reshape(1, SSM_W), (m * jnp.sin(ang * j)).reshape(1, SSM_W)

    lam = jnp.concatenate([p for j in (1, 2, 4) for p in power(j)] + [jnp.zeros((2, SSM_W), F32)], axis=0)
    pws = [power(j) for j in range(1, 9)]
    pw = jnp.concatenate([p[0] for p in pws] + [p[1] for p in pws], axis=0)
    return bblk, cblk, lam, pw


def _layer(x, pos0, st, lw, final_g, final):
    b, t, d = x.shape
    n = b * t
    x2 = x.reshape(n, d)
    proj_b, proj = _norm_proj(x2, lw["norm_g"], lw["w_in_all"], lw["layer"])
    proj3 = proj.reshape(b, t, PF_W)
    past = st["past"]

    chunk = min(t, CHUNK)
    y_rw, wkv_new = _rwkv(proj3, st["sp_r"], st["sp_k"], st["sp_v"], st["sp_l"], st["wkv"],
                        lw["rw_pvec"], lw["rw_mu_l"], lw["rw_w2p"], lw["rw_a2p"], lw["g2"], chunk)
    shift_new = jnp.concatenate([proj3[:, t - 1:, O_RWR:O_RWR + 3 * BRANCH],
                                 proj3[:, t - 1:, O_LORA:O_LORA + 2 * RW_LORA]], axis=-1)

    y_ssm, hr, hi = _s5(proj3, st["ssm_re"], st["ssm_im"], lw["ssm_bblk"], lw["ssm_cblk"], lw["ssm_lam"],
                        lw["ssm_pw"], lw["ssm_d"], lw["ssm_wglu"], lw["ssm_bglu"])
    ssm_re_new = hr.reshape(b, SSM_GROUPS, SSM_STATE)
    ssm_im_new = hi.reshape(b, SSM_GROUPS, SSM_STATE)

    pos = pos0 + jnp.arange(t, dtype=jnp.int32)
    tq_tab, tc_tab, ts_tab = _rope_tables(pos)
    q, ckv, kr = _mla_prep(proj, t, tq_tab, tc_tab, ts_tab, lw["mla_gq"], lw["mla_gkv"], lw["mla_wq"])
    ckv3 = ckv.reshape(b, t, MLA_KV_LORA)
    kr3 = kr.reshape(b, t, MLA_ROPE)
    s_pad = -(-(past + t) // LANES) * LANES
    kc, vv = _mla_kv(ckv[None], kr[None], lw["mla_wk"], lw["mla_wv"], lw["dup"])
    kc3, vv3 = kc.reshape(b, t, HEADS * LANES), vv.reshape(b, t, BRANCH)
    kcp3 = vvp3 = None
    if past:
        kcp, vvp = _mla_kv(st["ckv_all"], st["kpe_all"], lw["mla_wk"], lw["mla_wv"], lw["dup"], lw["layer"])
        kcp3, vvp3 = kcp.reshape(b, past, HEADS * LANES), vvp.reshape(b, past, BRANCH)
    small = t <= 64
    assert small or not past
    tq = min(t, 256)
    hp = 4 if small else 1
    q3 = q.reshape(b, t, HEADS * LANES)
    if small:
        y_mla = _mla_attn(q3, kc3, vv3, kcp3, vvp3, s_pad, tq, s_pad, hp)
    else:
        y_mla = _mla_prompt(q3, kc3, vv3, tq)

    sbk = proj3[:, :, O_SBK:O_SBK + BRANCH]
    sbv = proj3[:, :, O_SBV:O_SBV + BRANCH]
    if past:
        y_sb = _sb_decode(proj3, st["sbk_t"], st["sbv_t"], lw["layer"], min(256, past))
    else:
        y_sb = _sb_attn(proj3, tq, min(256, t))

    ys = [y.reshape(n, BRANCH) for y in (y_rw, y_ssm, y_mla, y_sb)]
    x_new = _merge(ys, proj_b, x2, lw["w_branch"], lw["b_merge"], lw["w_out"], final_g, final,
                   lw["layer"]).reshape(b, t, d)
    new_state = (shift_new, wkv_new, ssm_re_new, ssm_im_new, ckv3, kr3,
                 sbk.reshape(b, t, HEADS, HEAD), sbv.reshape(b, t, HEADS, HEAD))
    return x_new, new_state


def kernel(x_prompt, x_sample, state_rwkv_shift, state_rwkv_wkv, state_ssm_re, state_ssm_im, cache_mla_ckv, cache_mla_kpe, cache_sb_k, cache_sb_v, norm_g, w_in, rw_mu, rw_w0, rw_w2, rw_a0, rw_a2, rw_k_k, rw_k_a, rw_r_k, rw_lnx_g, rw_lnx_b, ssm_lam_re, ssm_lam_im, ssm_log_dt, ssm_b_re, ssm_b_im, ssm_c_re, ssm_c_im, ssm_d, ssm_w_glu, ssm_b_glu, mla_q_norm, mla_w_q_up, mla_kv_norm, mla_w_kv_up, w_branch, b_merge, w_out, final_norm_g):
    depth = w_in.shape[0]
    bp, tp, _ = x_prompt.shape
    bs = x_sample.shape[0]
    past = cache_mla_ckv.shape[2]

    w_in_p = _permute_w_in(w_in)
    lane_i = jnp.arange(LANES)
    g2 = ((lane_i[:, None] // HEAD) == (lane_i[None, :] // HEAD)).astype(BF16)
    rope_i = jnp.arange(MLA_ROPE)
    dup = ((lane_i[None, :] == rope_i[:, None] + MLA_NOPE)
           | (lane_i[None, :] == rope_i[:, None] + MLA_NOPE + MLA_ROPE)).astype(BF16)
    final_g = final_norm_g.reshape(1, D_MODEL)
    zpad = jnp.zeros((RW_LORA, BRANCH), F32)

    layers = []
    for l in range(depth):
        wq_p, wk_p, wv_p = _mla_weights(mla_w_q_up[l], mla_w_kv_up[l])
        bblk, cblk, lam, pw = _s5_tables(ssm_lam_re[l], ssm_lam_im[l], ssm_log_dt[l], ssm_b_re[l], ssm_b_im[l],
                                         ssm_c_re[l], ssm_c_im[l])
        mu = rw_mu[l]
        rows = [mu[:BRANCH], mu[BRANCH:2 * BRANCH], mu[2 * BRANCH:3 * BRANCH], rw_w0[l], rw_a0[l], rw_k_k[l],
                rw_k_a[l], rw_r_k[l].reshape(BRANCH), rw_lnx_g[l], rw_lnx_b[l]]
        pvec = jnp.concatenate([jnp.stack(rows), jnp.zeros((16 - len(rows), BRANCH), F32)], axis=0)
        layers.append(dict(
            norm_g=norm_g[l].reshape(1, D_MODEL), w_in_all=w_in_p, layer=l,
            rw_pvec=pvec, rw_mu_l=mu[3 * BRANCH:].reshape(1, 2 * RW_LORA),
            rw_w2p=jnp.concatenate([rw_w2[l], zpad], axis=0).astype(BF16),
            rw_a2p=jnp.concatenate([zpad, rw_a2[l]], axis=0).astype(BF16),
            g2=g2, dup=dup,
            ssm_bblk=bblk, ssm_cblk=cblk, ssm_lam=lam, ssm_pw=pw,
            ssm_d=ssm_d[l].reshape(1, BRANCH), ssm_wglu=ssm_w_glu[l].astype(BF16),
            ssm_bglu=ssm_b_glu[l].reshape(1, BRANCH),
            mla_gq=mla_q_norm[l].reshape(1, MLA_Q_LORA), mla_gkv=mla_kv_norm[l].reshape(1, MLA_KV_LORA),
            mla_wq=wq_p, mla_wk=wk_p, mla_wv=wv_p,
            w_branch=w_branch, b_merge=b_merge[l].reshape(N_BRANCH, 1, D_MODEL), w_out=w_out))

    def fresh(bn):
        return dict(sp_r=jnp.zeros((bn, 1, BRANCH), F32), sp_k=jnp.zeros((bn, 1, BRANCH), F32),
                    sp_v=jnp.zeros((bn, 1, BRANCH), F32), sp_l=jnp.zeros((bn, 1, 2 * RW_LORA), F32),
                    wkv=jnp.zeros((bn, HEADS, HEAD, HEAD), F32),
                    ssm_re=jnp.zeros((bn, 1, SSM_W), F32), ssm_im=jnp.zeros((bn, 1, SSM_W), F32),
                    past=0)

    nl = cache_mla_ckv.shape[0]
    ckv_all = cache_mla_ckv.reshape(nl, bs * past, MLA_KV_LORA)
    kpe_all = cache_mla_kpe.reshape(nl, bs * past, MLA_ROPE)
    sbk_t = jnp.transpose(cache_sb_k, (0, 1, 3, 4, 2)).reshape(nl, bs, HEADS // 2, LANES, past)
    sbv_t = jnp.transpose(cache_sb_v, (0, 1, 3, 4, 2)).reshape(nl, bs, HEADS // 2, LANES, past)

    def carried(l):
        sh = state_rwkv_shift[l]
        return dict(sp_r=sh[:, :, :BRANCH], sp_k=sh[:, :, BRANCH:2 * BRANCH], sp_v=sh[:, :, 2 * BRANCH:3 * BRANCH],
                    sp_l=sh[:, :, 3 * BRANCH:], wkv=state_rwkv_wkv[l],
                    ssm_re=state_ssm_re[l].reshape(bs, 1, SSM_W), ssm_im=state_ssm_im[l].reshape(bs, 1, SSM_W),
                    past=past, ckv_all=ckv_all, kpe_all=kpe_all, sbk_t=sbk_t, sbv_t=sbv_t)

    xp, xs = x_prompt, x_sample
    new_p, new_s = [], []
    for l in range(depth):
        last = l == depth - 1
        xp, st_p = _layer(xp, 0, fresh(bp), layers[l], final_g, last)
        xs, st_s = _layer(xs, past, carried(l), layers[l], final_g, last)
        new_p.append(st_p)
        new_s.append(st_s)
    stk = lambda lst, i: jnp.stack([s[i] for s in lst], axis=0)
    return (xp, xs) + tuple(stk(new_p, i) for i in range(8)) + tuple(stk(new_s, i) for i in range(8))
```

```python
import functools
import math

import jax
import jax.numpy as jnp
from jax import lax
from jax.experimental import pallas as pl
from jax.experimental.pallas import tpu as pltpu

F32 = jnp.float32
BF16 = jnp.bfloat16

D_MODEL = 2048
BRANCH = 512
N_BRANCH = 4
EPS = 1e-6
CHUNK = 64
HEAD = 64
HEADS = 8
RW_LORA = 64
RW_IN = 3 * BRANCH + 2 * RW_LORA
RW_GN_EPS = 64e-5
SSM_GROUP = 16
SSM_GROUPS = 32
SSM_STATE = 64
SSM_W = SSM_GROUPS * SSM_STATE
MLA_NOPE = 64
MLA_ROPE = 32
MLA_Q_LORA = 384
MLA_KV_LORA = 256
MLA_SCALE = 1.0 / math.sqrt(MLA_NOPE + MLA_ROPE)
ROPE_BASE = 10000.0
SB_SCALE = 1.0 / math.sqrt(HEAD)
LOG2E = math.log2(math.e)
LANES = 128
MIB = 1024 * 1024

O_MERGE = 0
O_GATE = 8192
PB_W = 10240
O_QLAT = 0
O_LORA = 384
O_KVLAT = 512
O_KPE = 768
O_KPESW = 896
O_SBQ = 1024
O_SBK = 1536
O_SBV = 2048
O_RWR = 2560
O_RWK = 3072
O_RWV = 3584
O_SSM = 4096
PF_W = 4608
PROJ_W = PB_W + PF_W


def _cparams(sem, vmem_mib):
    return pltpu.CompilerParams(dimension_semantics=sem, vmem_limit_bytes=vmem_mib * MIB)


def _dot(a, b):
    return jnp.dot(a, b, preferred_element_type=F32)


def _dot_nt(a, b):
    return lax.dot_general(a, b, (((1,), (1,)), ((), ())), preferred_element_type=F32)


def _split3(x):
    h1 = x.astype(BF16)
    r1 = x - h1.astype(F32)
    h2 = r1.astype(BF16)
    h3 = (r1 - h2.astype(F32)).astype(BF16)
    return h1, h2, h3


def _softplus(x):
    return jnp.maximum(x, 0.0) + jnp.log1p(jnp.exp(-jnp.abs(x)))


def _sigmoid(x):
    return 0.5 * jnp.tanh(0.5 * x) + 0.5


def _norm_proj_kernel(x_ref, g_ref, w_ref, ob_ref, of_ref, h_ref, *, nb16):
    j = pl.program_id(1)

    @pl.when(j == 0)
    def _():
        x = x_ref[...]
        ms = jnp.mean(x * x, axis=-1, keepdims=True)
        h_ref[...] = (x * lax.rsqrt(ms + EPS) * g_ref[...]).astype(BF16)

    @pl.when(j < nb16)
    def _():
        ob_ref[...] = _dot(h_ref[...], w_ref[0]).astype(BF16)

    @pl.when(j >= nb16)
    def _():
        of_ref[...] = _dot(h_ref[...], w_ref[0])


def _norm_proj(x2, g, w_all, layer):
    n, d = x2.shape
    tm = min(n, 2048)
    tn = 512
    nb16 = PB_W // tn
    return pl.pallas_call(
        functools.partial(_norm_proj_kernel, nb16=nb16),
        out_shape=(jax.ShapeDtypeStruct((n, PB_W), BF16), jax.ShapeDtypeStruct((n, PF_W), F32)),
        grid=(n // tm, PROJ_W // tn),
        in_specs=[
            pl.BlockSpec((tm, d), lambda i, j: (i, 0), pipeline_mode=pl.Buffered(1)),
            pl.BlockSpec((1, d), lambda i, j: (0, 0)),
            pl.BlockSpec((1, d, tn), lambda i, j: (layer, 0, j)),
        ],
        out_specs=(pl.BlockSpec((tm, tn), lambda i, j: (i, jnp.minimum(j, nb16 - 1))),
                   pl.BlockSpec((tm, tn), lambda i, j: (i, jnp.maximum(j - nb16, 0)))),
        scratch_shapes=[pltpu.VMEM((tm, d), BF16)],
        compiler_params=_cparams(("parallel", "arbitrary"), 48),
        name="norm_proj",
    )(x2, g, w_all)


def _segsum(x, g2):
    outs = []
    for j in range(BRANCH // LANES):
        h1, h2, h3 = _split3(x[:, LANES * j:LANES * (j + 1)])
        outs.append(_dot(h1, g2) + _dot(h2, g2) + _dot(h3, g2))
    return jnp.concatenate(outs, axis=1)


def _rwkv_kernel(pr_ref, pk_ref, pv_ref, plo_ref, spr_ref, spk_ref, spv_ref, spl_ref, s0_ref,
                 pvec_ref, mul_ref, w2_ref, a2_ref, g2_ref,
                 y_ref, sout_ref,
                 st_ref, cr_ref, ck_ref, cv_ref, cl_ref, *, C, nb):
    c = pl.program_id(1)

    @pl.when(c == 0)
    def _():
        zero = jnp.zeros((HEAD, HEAD), F32)
        for bi in range(nb):
            for j in range(HEADS // 2):
                st_ref[bi, j] = jnp.concatenate(
                    [jnp.concatenate([s0_ref[bi, 2 * j], zero], axis=1),
                     jnp.concatenate([zero, s0_ref[bi, 2 * j + 1]], axis=1)], axis=0)
        cr_ref[...] = spr_ref[...]
        ck_ref[...] = spk_ref[...]
        cv_ref[...] = spv_ref[...]
        cl_ref[...] = spl_ref[...]

    pvec = pvec_ref[...]
    prow = lambda i: pvec[i:i + 1, :]
    g2 = g2_ref[...]

    def tshift(p_ref, carry_ref, mu):
        outs = []
        for bi in range(nb):
            p = p_ref[bi]
            rolled = pltpu.roll(p, 1, 0)
            rid = lax.broadcasted_iota(jnp.int32, p.shape, 0)
            prev = jnp.where(rid == 0, carry_ref[bi], rolled)
            carry_ref[bi] = p[C - 1:C, :]
            outs.append(p + (prev - p) * mu)
        return jnp.concatenate(outs, axis=0)

    r = tshift(pr_ref, cr_ref, prow(0))
    k = tshift(pk_ref, ck_ref, prow(1))
    v = tshift(pv_ref, cv_ref, prow(2))
    lo = tshift(plo_ref, cl_ref, mul_ref[...])

    log_w = -_softplus(-(prow(3) + _dot(jnp.tanh(lo).astype(BF16), w2_ref[...]))) - 0.5
    ld = -jnp.exp(log_w)
    a_icl = jax.nn.sigmoid(prow(4) + _dot(lo.astype(BF16), a2_ref[...]))
    kkr = k * prow(5)
    kk = kkr / jnp.maximum(jnp.sqrt(_segsum(kkr * kkr, g2)), 1e-12)
    k2 = k * (1.0 + (a_icl - 1.0) * prow(6))
    av = -kk
    bv = kk * a_icl

    sh = C.bit_length() - 1
    ri = lax.broadcasted_iota(jnp.int32, (nb * C, nb * C), 0)
    ci = lax.broadcasted_iota(jnp.int32, (nb * C, nb * C), 1)
    lincl = jnp.logical_and(ci <= ri, (ci >> sh) == (ri >> sh)).astype(BF16)
    h1, h2, h3 = _split3(ld)
    g = _dot(lincl, h1) + _dot(lincl, h2) + _dot(lincl, h3)
    g_lasts = [g[(bi + 1) * C - 1:(bi + 1) * C, :] for bi in range(nb)]
    g_last = jnp.concatenate([jnp.broadcast_to(gl, (C, BRANCH)) for gl in g_lasts], axis=0)
    eg = jnp.exp(g)
    eng = jnp.exp(-g)
    at = av * jnp.exp(g - ld)
    rt = r * eg
    bt = bv * eng
    kt = k2 * eng
    e_c = jnp.exp(g_last - g)
    b_end = bv * e_c
    k_end = k2 * e_c
    eg_lasts = [jnp.exp(gl) for gl in g_lasts]

    c2 = 2 * C
    npair = BRANCH // LANES
    r_i = lax.broadcasted_iota(jnp.int32, (c2, c2), 0)
    c_i = lax.broadcasted_iota(jnp.int32, (c2, c2), 1)
    c_m = jnp.where(c_i >= C, c_i - C, c_i)
    mhalf = c_m < jnp.where(r_i < C, r_i, r_i - C + 1)
    mfull = jnp.concatenate([mhalf, mhalf], axis=0)
    bd2 = (r_i >> sh) == (c_i >> sh)
    eye2 = (r_i == c_i).astype(F32)
    lane = lax.broadcasted_iota(jnp.int32, (1, LANES), 1)
    m0 = (lane < HEAD).astype(F32)
    m1 = (lane >= HEAD).astype(F32)
    v_i = lax.broadcasted_iota(jnp.int32, (LANES, LANES), 0)
    k_i = lax.broadcasted_iota(jnp.int32, (LANES, LANES), 1)
    bdmask = (v_i >> 6) == (k_i >> 6)
    zf = jnp.zeros((C, LANES), F32)
    zc = jnp.zeros((C, LANES), BF16)
    sls = [slice(LANES * j, LANES * (j + 1)) for j in range(npair)]

    units = [(bi, j) for bi in range(nb) for j in range(npair)]
    tile = lambda x, bi, j: x[bi * C:(bi + 1) * C, LANES * j:LANES * (j + 1)]
    ws, a_ss, s2s = [], [], []
    for bi, j in units:
        atj, rtj = tile(at, bi, j), tile(rt, bi, j)
        a0, r0, a1, r1 = atj * m0, rtj * m0, atj * m1, rtj * m1
        lhs_w = jnp.concatenate([jnp.concatenate([a0, zf], axis=1), jnp.concatenate([r0, zf], axis=1),
                                 jnp.concatenate([zf, a1], axis=1), jnp.concatenate([zf, r1], axis=1)],
                                axis=0).astype(BF16)
        btj, ktj = tile(bt, bi, j), tile(kt, bi, j)
        rhs_w = jnp.concatenate([jnp.concatenate([btj, ktj], axis=0),
                                 jnp.concatenate([ktj, btj], axis=0)], axis=1).astype(BF16)
        ws.append(jnp.where(mfull, _dot_nt(lhs_w, rhs_w), 0.0))
        lhs_s = jnp.concatenate([a0, r0, a1, r1], axis=0).astype(BF16)
        s2 = st_ref[bi, j]
        s2h, s2l, _ = _split3(s2)
        a_ss.append(_dot_nt(lhs_s, s2h) + _dot_nt(lhs_s, s2l))
        s2s.append(s2)

    n_pows = [jnp.where(bd2, jnp.concatenate([w[:C], w[c2:c2 + C]], axis=0), 0.0) for w in ws]
    t_mats = [eye2 + n for n in n_pows]
    lv = 2
    while lv < C:
        n_pows = [_dot(n.astype(BF16), n.astype(BF16)) for n in n_pows]
        t_mats = [t + _dot(t.astype(BF16), n.astype(BF16)) for t, n in zip(t_mats, n_pows)]
        lv *= 2

    wbs = [w.astype(BF16) for w in ws]
    v0s = [(tile(v, bi, j) * m0).astype(BF16) for bi, j in units]
    v1s = [(tile(v, bi, j) * m1).astype(BF16) for bi, j in units]
    xs = [jnp.concatenate([a_s[:C] + _dot(wb[:C], jnp.concatenate([zc, v0], axis=0)),
                           a_s[c2:c2 + C] + _dot(wb[c2:c2 + C], jnp.concatenate([v1, zc], axis=0))], axis=0)
          for a_s, wb, v0, v1 in zip(a_ss, wbs, v0s, v1s)]
    us = [_dot(t.astype(BF16), x.astype(BF16)) for t, x in zip(t_mats, xs)]
    ys = []
    for n, (bi, j) in enumerate(units):
        a_s, wb, u = a_ss[n], wbs[n], us[n]
        u0, u1 = u[:C], u[C:]
        y0 = a_s[C:c2] + _dot(wb[C:c2], jnp.concatenate([u0.astype(BF16), v0s[n]], axis=0))
        y1 = a_s[c2 + C:] + _dot(wb[c2 + C:], jnp.concatenate([v1s[n], u1.astype(BF16)], axis=0))
        ys.append(y0 + y1)
        uv = jnp.concatenate([u0 + u1, tile(v, bi, j)], axis=0)
        bk = jnp.concatenate([tile(b_end, bi, j), tile(k_end, bi, j)], axis=0).astype(BF16)
        upd = _dot(uv.T.astype(BF16), bk)
        st_ref[bi, j] = s2s[n] * eg_lasts[bi][:, sls[j]] + jnp.where(bdmask, upd, 0.0)
    y = jnp.concatenate([jnp.concatenate(ys[bi * npair:(bi + 1) * npair], axis=1) for bi in range(nb)],
                        axis=0)

    inv_n = 1.0 / HEAD
    yc = y - _segsum(y, g2) * inv_n
    yn = yc * lax.rsqrt(_segsum(yc * yc, g2) * inv_n + RW_GN_EPS) * prow(8) + prow(9)
    out = (yn + _segsum(r * k2 * prow(7), g2) * v).astype(y_ref.dtype)
    for bi in range(nb):
        y_ref[bi] = out[bi * C:(bi + 1) * C]

    @pl.when(c == pl.num_programs(1) - 1)
    def _():
        for bi in range(nb):
            for j in range(HEADS // 2):
                s2 = st_ref[bi, j]
                sout_ref[bi, 2 * j] = s2[:HEAD, :HEAD]
                sout_ref[bi, 2 * j + 1] = s2[HEAD:, HEAD:]


def _rwkv(proj3, sp_r, sp_k, sp_v, sp_l, s0, pvec, mu_l, w2p, a2p, g2, chunk):
    b, t, _ = proj3.shape
    nc = t // chunk
    nb = min(b, 4)
    cb = lambda off, wdt: off // wdt
    bspec = lambda off, wdt: pl.BlockSpec((nb, chunk, wdt), lambda i, c, o=cb(off, wdt): (i, c, o))
    row3 = lambda wdt: pl.BlockSpec((nb, 1, wdt), lambda i, c: (i, 0, 0))
    full2 = lambda a: pl.BlockSpec(a.shape, lambda i, c: (0, 0))
    return pl.pallas_call(
        functools.partial(_rwkv_kernel, C=chunk, nb=nb),
        out_shape=(jax.ShapeDtypeStruct((b, t, BRANCH), BF16),
                   jax.ShapeDtypeStruct((b, HEADS, HEAD, HEAD), F32)),
        grid=(b // nb, nc),
        in_specs=[
            bspec(O_RWR, BRANCH), bspec(O_RWK, BRANCH), bspec(O_RWV, BRANCH), bspec(O_LORA, LANES),
            row3(BRANCH), row3(BRANCH), row3(BRANCH), row3(LANES),
            pl.BlockSpec((nb, HEADS, HEAD, HEAD), lambda i, c: (i, 0, 0, 0)),
            full2(pvec), full2(mu_l), full2(w2p), full2(a2p), full2(g2),
        ],
        out_specs=(pl.BlockSpec((nb, chunk, BRANCH), lambda i, c: (i, c, 0)),
                   pl.BlockSpec((nb, HEADS, HEAD, HEAD), lambda i, c: (i, 0, 0, 0))),
        scratch_shapes=[pltpu.VMEM((nb, 4, LANES, LANES), F32),
                        pltpu.VMEM((nb, 1, BRANCH), F32), pltpu.VMEM((nb, 1, BRANCH), F32),
                        pltpu.VMEM((nb, 1, BRANCH), F32), pltpu.VMEM((nb, 1, LANES), F32)],
        compiler_params=_cparams(("parallel", "arbitrary"), 32),
        name="rwkv7",
    )(proj3, proj3, proj3, proj3, sp_r, sp_k, sp_v, sp_l, s0, pvec, mu_l, w2p, a2p, g2)


def _s5_kernel(u_ref, h0r_ref, h0i_ref, bblk_ref, cblk_ref, lam_ref, pw_ref, dsk_ref, wglu_ref, bglu_ref,
               y_ref, hro_ref, hio_ref, x_ref, cr_ref, ci_ref, *, tb, nb):
    @pl.when(pl.program_id(1) == 0)
    def _():
        cr_ref[...] = h0r_ref[...]
        ci_ref[...] = h0i_ref[...]

    u = jnp.concatenate([u_ref[bi] for bi in range(nb)], axis=0)
    x = _dot(u.astype(BF16), bblk_ref[...])
    xr = x[:, :SSM_W]
    xi = x[:, SSM_W:]
    for lvl, s in enumerate((1, 2, 4)):
        ar = lam_ref[2 * lvl]
        ai = lam_ref[2 * lvl + 1]
        sr = pltpu.roll(xr, s, 0)
        si = pltpu.roll(xi, s, 0)
        xr, xi = xr + (ar * sr - ai * si), xi + (ar * si + ai * sr)
    x_ref[:, :SSM_W] = xr
    x_ref[:, SSM_W:] = xi
    pwr = pw_ref[0:8, :]
    pwi = pw_ref[8:16, :]

    for b in range(nb):

        def body(gi, carry, b=b):
            cr, ci = carry
            o = pl.multiple_of(b * tb + gi * 8, 8)
            br = x_ref[pl.ds(o, 8), 0:SSM_W]
            bi = x_ref[pl.ds(o, 8), SSM_W:2 * SSM_W]
            br = br + pwr * cr - pwi * ci
            bi = bi + pwr * ci + pwi * cr
            x_ref[pl.ds(o, 8), 0:SSM_W] = br
            x_ref[pl.ds(o, 8), SSM_W:2 * SSM_W] = bi
            return br[7:8, :], bi[7:8, :]

        cr, ci = lax.fori_loop(0, tb // 8, body, (cr_ref[b], ci_ref[b]))
        cr_ref[b] = cr
        ci_ref[b] = ci
        hro_ref[b] = cr
        hio_ref[b] = ci
    y = _dot(x_ref[...].astype(BF16), cblk_ref[...]) + dsk_ref[...] * u
    g = jax.nn.gelu(y)
    out = (g * jax.nn.sigmoid(_dot(g.astype(BF16), wglu_ref[...]) + bglu_ref[...])).astype(y_ref.dtype)
    for b in range(nb):
        y_ref[b] = out[b * tb:(b + 1) * tb]


def _s5(proj3, h0r, h0i, bblk, cblk, lam, pw, dsk, wglu, bglu):
    b, t, _ = proj3.shape
    tb = min(t, 256)
    nb = b if t <= 64 else 1
    rows = nb * tb
    sub = jnp.arange(rows)[:, None] & 7
    lam = jnp.stack([jnp.where(sub >= s, lam[2 * lv + c:2 * lv + c + 1], 0.0)
                     for lv, s in enumerate((1, 2, 4)) for c in (0, 1)])
    full2 = lambda a: pl.BlockSpec(a.shape, lambda i, c: (0,) * a.ndim, pipeline_mode=pl.Buffered(1))
    row3 = pl.BlockSpec((nb, 1, SSM_W), lambda i, c: (i, 0, 0))
    return pl.pallas_call(
        functools.partial(_s5_kernel, tb=tb, nb=nb),
        out_shape=(jax.ShapeDtypeStruct((b, t, BRANCH), BF16),
                   jax.ShapeDtypeStruct((b, 1, SSM_W), F32),
                   jax.ShapeDtypeStruct((b, 1, SSM_W), F32)),
        grid=(b // nb, t // tb),
        in_specs=[pl.BlockSpec((nb, tb, BRANCH), lambda i, c: (i, c, O_SSM // BRANCH)),
                  row3, row3, full2(bblk), full2(cblk), full2(lam), full2(pw), full2(dsk),
                  full2(wglu), full2(bglu)],
        out_specs=(pl.BlockSpec((nb, tb, BRANCH), lambda i, c: (i, c, 0)), row3, row3),
        scratch_shapes=[pltpu.VMEM((rows, 2 * SSM_W), F32),
                        pltpu.VMEM((nb, 1, SSM_W), F32), pltpu.VMEM((nb, 1, SSM_W), F32)],
        compiler_params=_cparams(("parallel", "arbitrary"), 48),
        name="s5",
    )(proj3, h0r, h0i, bblk, cblk, lam, pw, dsk, wglu, bglu)


def _mla_prep_kernel(ql_ref, kvl_ref, kpe_ref, kpesw_ref, tq_ref, tc_ref, ts_ref,
                     gq_ref, gkv_ref, wq_ref, q_ref, ckv_ref, kr_ref):
    ql = ql_ref[...]
    qn = ql * lax.rsqrt(jnp.mean(ql * ql, axis=-1, keepdims=True) + EPS) * gq_ref[...]
    q = _dot(qn.astype(BF16), wq_ref[...])
    tq = tq_ref[...]
    q_ref[...] = jnp.concatenate(
        [q[:, LANES * h:LANES * (h + 1)] * tq for h in range(HEADS)], axis=1).astype(BF16)
    kvl = kvl_ref[...]
    ckv_ref[...] = kvl * lax.rsqrt(jnp.mean(kvl * kvl, axis=-1, keepdims=True) + EPS) * gkv_ref[...]
    kr = kpe_ref[...] * tc_ref[...] + kpesw_ref[...] * ts_ref[...]
    kr_ref[...] = kr[:, :MLA_ROPE]


def _mla_prep(proj, t, tq_tab, tc_tab, ts_tab, gq, gkv, wq):
    n = proj.shape[0]
    tm = min(t, 512)
    nt = t // tm
    tab = pl.BlockSpec((tm, LANES), lambda i: (i % nt, 0))
    full2 = lambda a: pl.BlockSpec(a.shape, lambda i: (0, 0))
    return pl.pallas_call(
        _mla_prep_kernel,
        out_shape=(jax.ShapeDtypeStruct((n, HEADS * LANES), BF16),
                   jax.ShapeDtypeStruct((n, MLA_KV_LORA), F32),
                   jax.ShapeDtypeStruct((n, MLA_ROPE), F32)),
        grid=(n // tm,),
        in_specs=[pl.BlockSpec((tm, MLA_Q_LORA), lambda i: (i, O_QLAT // MLA_Q_LORA)),
                  pl.BlockSpec((tm, MLA_KV_LORA), lambda i: (i, O_KVLAT // MLA_KV_LORA)),
                  pl.BlockSpec((tm, LANES), lambda i: (i, O_KPE // LANES)),
                  pl.BlockSpec((tm, LANES), lambda i: (i, O_KPESW // LANES)),
                  tab, tab, tab, full2(gq), full2(gkv), full2(wq)],
        out_specs=(pl.BlockSpec((tm, HEADS * LANES), lambda i: (i, 0)),
                   pl.BlockSpec((tm, MLA_KV_LORA), lambda i: (i, 0)),
                   pl.BlockSpec((tm, MLA_ROPE), lambda i: (i, 0))),
        compiler_params=_cparams(("parallel",), 32),
        name="mla_prep",
    )(proj, proj, proj, proj, tq_tab, tc_tab, ts_tab, gq, gkv, wq)


def _mla_kv_kernel(ckv_ref, kr_ref, wk_ref, wv_ref, dup_ref, k_ref, v_ref):
    cb = ckv_ref[0].astype(BF16)
    kn = _dot(cb, wk_ref[...])
    krd = _dot(kr_ref[0].astype(BF16), dup_ref[...])
    k_ref[...] = jnp.concatenate(
        [kn[:, LANES * h:LANES * (h + 1)] + krd for h in range(HEADS)], axis=1).astype(BF16)
    v_ref[...] = _dot(cb, wv_ref[...]).astype(BF16)


def _mla_kv(ckv, kr, wk, wv, dup, layer=0):
    n = ckv.shape[1]
    tm = next((c for c in (512, LANES) if n % c == 0), n)
    full2 = lambda a: pl.BlockSpec(a.shape, lambda i: (0, 0))
    return pl.pallas_call(
        _mla_kv_kernel,
        out_shape=(jax.ShapeDtypeStruct((n, HEADS * LANES), BF16),
                   jax.ShapeDtypeStruct((n, BRANCH), BF16)),
        grid=(n // tm,),
        in_specs=[pl.BlockSpec((1, tm, MLA_KV_LORA), lambda i: (layer, i, 0)),
                  pl.BlockSpec((1, tm, MLA_ROPE), lambda i: (layer, i, 0)),
                  full2(wk), full2(wv), full2(dup)],
        out_specs=(pl.BlockSpec((tm, HEADS * LANES), lambda i: (i, 0)),
                   pl.BlockSpec((tm, BRANCH), lambda i: (i, 0))),
        compiler_params=_cparams(("parallel",), 32),
        name="mla_kv",
    )(ckv, kr, wk, wv, dup)


def _assemble_keys(past_ref, new_ref, all_ref, past, t):
    all_ref[0:past, :] = past_ref[0].astype(BF16)
    all_ref[past:past + t, :] = new_ref[0].astype(BF16)
    pad = all_ref.shape[0] - past - t
    if pad:
        all_ref[past + t:, :] = jnp.zeros((pad, all_ref.shape[1]), BF16)


def _mla_attn_kernel(q_ref, k_ref, v_ref, *rest, tq, tk, q_off, s_valid, hp, past):
    if past:
        kp_ref, vp_ref, o_ref, kall_ref, vall_ref = rest
        _assemble_keys(kp_ref, k_ref, kall_ref, past, tq)
        _assemble_keys(vp_ref, v_ref, vall_ref, past, tq)
        kload = lambda o: kall_ref[pl.ds(o, tk), :]
        vload = lambda o: vall_ref[pl.ds(o, tk), :]
    else:
        (o_ref,) = rest
        kload = lambda o: k_ref[0, pl.ds(o, tk), :]
        vload = lambda o: v_ref[0, pl.ds(o, tk), :]
    i = pl.program_id(2)
    q_lo = q_off + i * tq
    limit = jnp.minimum(q_lo + tq, s_valid)
    nkb = (limit + tk - 1) // tk
    qidx = q_lo + lax.broadcasted_iota(jnp.int32, (tq, tk), 0)
    kloc = lax.broadcasted_iota(jnp.int32, (tq, tk), 1)
    nh = 2 * hp
    q2 = q_ref[0]
    qs = [q2[:, LANES * h:LANES * (h + 1)] for h in range(nh)]

    def body(kb, carry, masked):
        o = pl.multiple_of(kb * tk, tk)
        kblk = kload(o)
        vblk = vload(o)
        ss = [_dot_nt(qs[h], kblk[:, LANES * h:LANES * (h + 1)]) for h in range(nh)]
        if masked:
            kidx = kloc + kb * tk
            vis = jnp.logical_and((kidx >> 6) <= (qidx >> 6), kidx < s_valid)
            ss = [jnp.where(vis, s, -1e30) for s in ss]
        m_new = [jnp.maximum(carry[h][0], jnp.max(ss[h], axis=-1, keepdims=True)) for h in range(nh)]
        ps = [jnp.exp2(ss[h] - m_new[h]) for h in range(nh)]
        pv = [_dot(ps[h].astype(BF16), vblk[:, LANES * (h // 2):LANES * (h // 2 + 1)]) for h in range(nh)]
        new = []
        for h in range(nh):
            m_prev, l_prev, acc = carry[h]
            alpha = jnp.exp2(m_prev - m_new[h])
            new.append((m_new[h], alpha * l_prev + jnp.sum(ps[h], axis=-1, keepdims=True), alpha * acc + pv[h]))
        return tuple(new)

    init = tuple((jnp.full((tq, 1), -1e30, F32), jnp.zeros((tq, 1), F32), jnp.zeros((tq, LANES), F32))
                 for _ in range(nh))
    nfull = jnp.minimum(((q_lo >> 6) + 1) * CHUNK, s_valid) // tk
    res = lax.fori_loop(0, nfull, lambda kb, c: body(kb, c, False), init)
    res = lax.fori_loop(nfull, nkb, lambda kb, c: body(kb, c, True), res)
    lane = lax.broadcasted_iota(jnp.int32, (tq, LANES), 1)
    outs = [jnp.where(lane < HEAD, res[2 * p][2] / res[2 * p][1], res[2 * p + 1][2] / res[2 * p + 1][1])
            for p in range(hp)]
    o_ref[0] = jnp.concatenate(outs, axis=1).astype(o_ref.dtype)


def _mla_prompt_kernel(q_ref, k_ref, v_ref, o_ref, *, tq):
    i = pl.program_id(2)
    q_lo = i * tq
    nkb = i + 1
    w = 2 * tq
    cidx = lax.broadcasted_iota(jnp.int32, (tq, w), 1)
    qidx = q_lo + jnp.where(cidx >= tq, cidx - tq, cidx)
    kloc = lax.broadcasted_iota(jnp.int32, (tq, w), 0)
    q2 = q_ref[0]
    zq = jnp.zeros((tq, LANES), BF16)
    qbd = jnp.concatenate([jnp.concatenate([q2[:, :LANES], zq], axis=1),
                           jnp.concatenate([zq, q2[:, LANES:]], axis=1)], axis=0)

    def group(kb, carry, n):
        m_prev, l_prev, acc_t = carry
        o = pl.multiple_of((kb - (n - 1)) * tq, tq)
        kn = k_ref[0, pl.ds(o, n * tq), :]
        vn_t = v_ref[0, pl.ds(o, n * tq), :].astype(F32).T.astype(BF16)
        s = _dot_nt(kn, qbd)
        vis = ((kloc + kb * tq) >> 6) <= (qidx >> 6)
        s_r = jnp.where(vis, s[(n - 1) * tq:], -1e30)
        s = jnp.concatenate([s[:(n - 1) * tq], s_r], axis=0) if n > 1 else s_r
        m_new = jnp.maximum(m_prev, jnp.max(s, axis=0, keepdims=True))
        alpha = jnp.exp2(m_prev - m_new)
        p = jnp.exp2(s - m_new)
        l_new = alpha * l_prev + jnp.sum(p, axis=0, keepdims=True)
        return m_new, l_new, alpha * acc_t + _dot(vn_t, p.astype(BF16))

    carry = (jnp.full((1, w), -1e30, F32), jnp.zeros((1, w), F32), jnp.zeros((LANES, w), F32))
    n4, n2, n1 = nkb // 4, (nkb % 4) // 2, nkb % 2
    carry = lax.fori_loop(0, n4, lambda s, c: group(nkb - 1 - 4 * s, c, 4), carry)
    carry = lax.fori_loop(0, n2, lambda s, c: group(nkb - 1 - 4 * n4, c, 2), carry)
    carry = lax.fori_loop(0, n1, lambda s, c: group(0, c, 1), carry)
    out_t = carry[2] / carry[1]
    row = lax.broadcasted_iota(jnp.int32, (LANES, tq), 0)
    o_ref[0] = jnp.where(row < HEAD, out_t[:, :tq], out_t[:, tq:]).T.astype(o_ref.dtype)


def _mla_prompt(q3, k3, v3, tq):
    b, t, _ = q3.shape
    assert tq % CHUNK == 0 and t % tq == 0
    return pl.pallas_call(
        functools.partial(_mla_prompt_kernel, tq=tq),
        out_shape=jax.ShapeDtypeStruct((b, t, BRANCH), BF16),
        grid=(b, HEADS // 2, t // tq),
        in_specs=[pl.BlockSpec((1, tq, 2 * LANES), lambda bi, j, i: (bi, i, j)),
                  pl.BlockSpec((1, t, 2 * LANES), lambda bi, j, i: (bi, 0, j)),
                  pl.BlockSpec((1, t, LANES), lambda bi, j, i: (bi, 0, j))],
        out_specs=pl.BlockSpec((1, tq, LANES), lambda bi, j, i: (bi, i, j)),
        compiler_params=_cparams(("parallel", "parallel", "arbitrary"), 40),
        name="mla_prompt",
    )(q3, k3, v3)


def _mla_attn(q3, k3, v3, kp3, vp3, s_pad, tq, tk, hp):
    b, t, _ = q3.shape
    past = 0 if kp3 is None else kp3.shape[1]
    rows = k3.shape[1]
    in_specs = [pl.BlockSpec((1, tq, 2 * LANES * hp), lambda bi, j, i: (bi, i, j)),
                pl.BlockSpec((1, rows, 2 * LANES * hp), lambda bi, j, i: (bi, 0, j)),
                pl.BlockSpec((1, rows, LANES * hp), lambda bi, j, i: (bi, 0, j))]
    args = [q3, k3, v3]
    scratch = []
    if past:
        in_specs += [pl.BlockSpec((1, past, 2 * LANES * hp), lambda bi, j, i: (bi, 0, j)),
                     pl.BlockSpec((1, past, LANES * hp), lambda bi, j, i: (bi, 0, j))]
        args += [kp3, vp3]
        scratch = [pltpu.VMEM((s_pad, 2 * LANES * hp), BF16), pltpu.VMEM((s_pad, LANES * hp), BF16)]
    return pl.pallas_call(
        functools.partial(_mla_attn_kernel, tq=tq, tk=tk, q_off=past, s_valid=past + t, hp=hp, past=past),
        out_shape=jax.ShapeDtypeStruct((b, t, BRANCH), BF16),
        grid=(b, 4 // hp, t // tq),
        in_specs=in_specs,
        out_specs=pl.BlockSpec((1, tq, LANES * hp), lambda bi, j, i: (bi, i, j)),
        scratch_shapes=scratch,
        compiler_params=_cparams(("parallel", "parallel", "arbitrary"), 40),
        name="mla_attn",
    )(*args)


def _sb_block(zs, vis, us, carry, pv):
    nh = len(zs)
    tq = zs[0].shape[0]
    zls = [jnp.minimum(z, 0.0) - jnp.log(1.0 + jnp.exp(-jnp.abs(z))) for z in zs]
    lgs = [zl - z for zl, z in zip(zls, zs)]
    if vis is not None:
        lgs = [jnp.where(vis, lg, 0.0) for lg in lgs]
    his = [lg.astype(BF16) for lg in lgs]
    los = [(lg - hi.astype(F32)).astype(BF16) for lg, hi in zip(lgs, his)]
    later = _dot(jnp.concatenate(his + los, axis=0), us)
    new = []
    for h in range(nh):
        csum, acc = carry[h]
        tot = later[h * tq:(h + 1) * tq] + later[(nh + h) * tq:(nh + h + 1) * tq] + csum
        a = jnp.exp(zls[h] + tot)
        if vis is not None:
            a = jnp.where(vis, a, 0.0)
        new.append((csum + jnp.sum(lgs[h], axis=-1, keepdims=True), acc + pv(h, a.astype(BF16))))
    return tuple(new)


def _sb_heads(q2, nh, scale=SB_SCALE):
    lane1 = lax.broadcasted_iota(jnp.int32, (1, LANES), 1)
    hm = ((lane1 < HEAD).astype(F32) * scale, (lane1 >= HEAD).astype(F32) * scale)
    return [(q2[:, LANES * (h // 2):LANES * (h // 2 + 1)] * hm[h % 2]).astype(BF16) for h in range(nh)]


def _sb_finish(res, o_ref, tq):
    lane = lax.broadcasted_iota(jnp.int32, (tq, LANES), 1)
    outs = [jnp.where(lane < HEAD, res[2 * p][1], res[2 * p + 1][1]) for p in range(len(res) // 2)]
    o_ref[0] = jnp.concatenate(outs, axis=1).astype(o_ref.dtype)


def _sb_attn_kernel(q_ref, k_ref, v_ref, us_ref, o_ref, *, tq, tk):
    i = pl.program_id(2)
    q_lo = i * tq
    nkb = (q_lo + tq - 2) // tk + 1
    cidx = lax.broadcasted_iota(jnp.int32, (tk, 2 * tq), 1)
    qidx = q_lo + jnp.where(cidx >= tq, cidx - tq, cidx)
    kloc = lax.broadcasted_iota(jnp.int32, (tk, 2 * tq), 0)
    qs = jnp.concatenate(_sb_heads(q_ref[0], 2, SB_SCALE * LOG2E), axis=0)
    us = us_ref[...]

    w = 2 * tq

    def group(kb, carry, n):
        csum, acc_t = carry
        o = pl.multiple_of((kb - (n - 1)) * tk, tk)
        kn = k_ref[0, pl.ds(o, n * tk), :].astype(BF16)
        vn_t = v_ref[0, pl.ds(o, n * tk), :].T.astype(BF16)
        z = _dot_nt(kn, qs)
        zl = jnp.minimum(z, 0.0) - jnp.log2(1.0 + jnp.exp2(-jnp.abs(z)))
        lg = zl - z
        vis = (kloc + kb * tk) < qidx
        lgs = [lg[j * tk:(j + 1) * tk] for j in range(n - 1)] + [jnp.where(vis, lg[(n - 1) * tk:], 0.0)]
        his = [x.astype(BF16) for x in lgs]
        los = [(x - h.astype(F32)).astype(BF16) for x, h in zip(lgs, his)]
        later = _dot(us, jnp.concatenate([jnp.concatenate([his[j], los[j]], axis=0) for j in range(n)], axis=1))
        tots = [None] * n
        for j in reversed(range(n)):
            tots[j] = later[:, j * w:(j + 1) * w] + csum
            csum = csum + jnp.sum(lgs[j], axis=0, keepdims=True)
        a = jnp.exp2(zl + jnp.concatenate(tots, axis=0))
        a = jnp.concatenate([a[:(n - 1) * tk], jnp.where(vis, a[(n - 1) * tk:], 0.0)], axis=0) if n > 1 \
            else jnp.where(vis, a, 0.0)
        return csum, acc_t + _dot(vn_t, a.astype(BF16))

    carry = (jnp.zeros((1, 2 * tq), F32), jnp.zeros((LANES, 2 * tq), F32))
    n4, n2, n1 = nkb // 4, (nkb % 4) // 2, nkb % 2
    carry = lax.fori_loop(0, n4, lambda s, c: group(nkb - 1 - 4 * s, c, 4), carry)
    carry = lax.fori_loop(0, n2, lambda s, c: group(nkb - 1 - 4 * n4, c, 2), carry)
    carry = lax.fori_loop(0, n1, lambda s, c: group(0, c, 1), carry)
    acc_t = carry[1]
    row = lax.broadcasted_iota(jnp.int32, (LANES, tq), 0)
    o_ref[0] = jnp.where(row < HEAD, acc_t[:, :tq], acc_t[:, tq:]).T.astype(o_ref.dtype)


def _sb_decode_kernel(q_ref, k_ref, v_ref, kpt_ref, vpt_ref, usn_ref, usp_ref, o_ref, *, t, past, pb):
    qh = _sb_heads(q_ref[0], HEADS)
    kn = k_ref[0].astype(BF16)
    vn = v_ref[0].astype(BF16)
    pair = lambda x, h: x[:, LANES * (h // 2):LANES * (h // 2 + 1)]
    ri = lax.broadcasted_iota(jnp.int32, (t, t), 0)
    ci = lax.broadcasted_iota(jnp.int32, (t, t), 1)
    carry = tuple((jnp.zeros((t, 1), F32), jnp.zeros((t, LANES), F32)) for _ in range(HEADS))
    zs = [_dot_nt(qh[h], pair(kn, h)) for h in range(HEADS)]
    carry = _sb_block(zs, ci < ri, usn_ref[...], carry, lambda h, a: _dot(a, pair(vn, h)))
    usp = usp_ref[...]
    for blk in reversed(range(past // pb)):
        kts = [kpt_ref[0, 0, p, :, blk * pb:(blk + 1) * pb].astype(BF16) for p in range(HEADS // 2)]
        vts = [vpt_ref[0, 0, p, :, blk * pb:(blk + 1) * pb].astype(BF16) for p in range(HEADS // 2)]
        zs = [_dot(qh[h], kts[h // 2]) for h in range(HEADS)]
        carry = _sb_block(zs, None, usp, carry, lambda h, a: _dot_nt(a, vts[h // 2]))
    _sb_finish(carry, o_ref, t)


def _strict_upper(n):
    idx = jnp.arange(n)
    return (idx[:, None] > idx[None, :]).astype(BF16)


def _sb_attn(proj3, tq, tk):
    b, t, _ = proj3.shape
    assert tq == tk
    u = _strict_upper(tk).T
    us = jnp.concatenate([u, u], axis=1)
    return pl.pallas_call(
        functools.partial(_sb_attn_kernel, tq=tq, tk=tk),
        out_shape=jax.ShapeDtypeStruct((b, t, BRANCH), BF16),
        grid=(b, HEADS // 2, t // tq),
        in_specs=[pl.BlockSpec((1, tq, LANES), lambda bi, j, i: (bi, i, O_SBQ // LANES + j)),
                  pl.BlockSpec((1, t, LANES), lambda bi, j, i: (bi, 0, O_SBK // LANES + j)),
                  pl.BlockSpec((1, t, LANES), lambda bi, j, i: (bi, 0, O_SBV // LANES + j)),
                  pl.BlockSpec(us.shape, lambda bi, j, i: (0, 0))],
        out_specs=pl.BlockSpec((1, tq, LANES), lambda bi, j, i: (bi, i, j)),
        compiler_params=_cparams(("parallel", "parallel", "arbitrary"), 48),
        name="sb_attn",
    )(proj3, proj3, proj3, us)


def _sb_decode(proj3, kpt, vpt, layer, pb):
    b, t, _ = proj3.shape
    past = kpt.shape[4]
    usn, usp = _strict_upper(t), _strict_upper(pb)
    new = lambda off: pl.BlockSpec((1, t, BRANCH), lambda bi, o=off // BRANCH: (bi, 0, o))
    cache = pl.BlockSpec((1, 1, HEADS // 2, LANES, past), lambda bi: (layer, bi, 0, 0, 0))
    return pl.pallas_call(
        functools.partial(_sb_decode_kernel, t=t, past=past, pb=pb),
        out_shape=jax.ShapeDtypeStruct((b, t, BRANCH), BF16),
        grid=(b,),
        in_specs=[new(O_SBQ), new(O_SBK), new(O_SBV), cache, cache,
                  pl.BlockSpec(usn.shape, lambda bi: (0, 0)), pl.BlockSpec(usp.shape, lambda bi: (0, 0))],
        out_specs=pl.BlockSpec((1, t, BRANCH), lambda bi: (bi, 0, 0)),
        compiler_params=_cparams(("parallel",), 48),
        name="sb_decode",
    )(proj3, proj3, proj3, kpt, vpt, usn, usp)


def _mix_kernel(yrw_ref, yssm_ref, ymla_ref, ysb_ref, gate_ref, pm_ref, wb_ref, bm_ref, o_ref, acc_ref):
    n = pl.program_id(1)

    @pl.when(n == 0)
    def _():
        acc_ref[...] = jnp.zeros_like(acc_ref)

    def branch(y_ref):
        g = gate_ref[...].astype(F32)
        gated_half = y_ref[...].astype(F32) * (g * (0.5 * _sigmoid(g)))
        up_half = _dot(gated_half.astype(BF16), wb_ref[0, 0].astype(BF16))
        acc_ref[...] += (jnp.tanh(0.5 * (pm_ref[...].astype(F32) + bm_ref[0])) + 1.0) * up_half

    for idx, y_ref in enumerate((yrw_ref, yssm_ref, ymla_ref, ysb_ref)):
        pl.when(n == idx)(functools.partial(branch, y_ref))

    @pl.when(n == N_BRANCH - 1)
    def _():
        o_ref[...] = acc_ref[...].astype(BF16)


def _out_kernel(m_ref, x_ref, wo_ref, fg_ref, o_ref, *, final):
    xn = x_ref[...] + _dot(m_ref[...], wo_ref[0].astype(BF16))
    if final:
        xn = xn * lax.rsqrt(jnp.mean(xn * xn, axis=-1, keepdims=True) + EPS) * fg_ref[...]
    o_ref[...] = xn


def _merge(ys, proj, x2, wb, bm, wo, fg, final, layer):
    n, d = x2.shape
    tm = min(n, 1024)
    ysp = pl.BlockSpec((tm, BRANCH), lambda i, k: (i, 0))
    mixed = pl.pallas_call(
        _mix_kernel,
        out_shape=jax.ShapeDtypeStruct((n, d), BF16),
        grid=(n // tm, N_BRANCH),
        in_specs=[ysp, ysp, ysp, ysp,
                  pl.BlockSpec((tm, BRANCH), lambda i, k: (i, O_GATE // BRANCH + k)),
                  pl.BlockSpec((tm, d), lambda i, k: (i, O_MERGE // d + k)),
                  pl.BlockSpec((1, 1, BRANCH, d), lambda i, k: (layer, k, 0, 0)),
                  pl.BlockSpec((1, 1, d), lambda i, k: (k, 0, 0))],
        out_specs=pl.BlockSpec((tm, d), lambda i, k: (i, 0)),
        scratch_shapes=[pltpu.VMEM((tm, d), F32)],
        compiler_params=_cparams(("parallel", "arbitrary"), 56),
        name="branch_mix",
    )(*ys, proj, proj, wb, bm)
    to = min(n, 512)
    return pl.pallas_call(
        functools.partial(_out_kernel, final=final),
        out_shape=jax.ShapeDtypeStruct((n, d), F32),
        grid=(n // to,),
        in_specs=[pl.BlockSpec((to, d), lambda i: (i, 0)),
                  pl.BlockSpec((to, d), lambda i: (i, 0)),
                  pl.BlockSpec((1, d, d), lambda i: (layer, 0, 0), pipeline_mode=pl.Buffered(1)),
                  pl.BlockSpec((1, d), lambda i: (0, 0))],
        out_specs=pl.BlockSpec((to, d), lambda i: (i, 0)),
        compiler_params=_cparams(("parallel",), 48),
        name="out_proj",
    )(mixed, x2, wo, fg)


_SRC_SSM = RW_IN
_SRC_QLAT = _SRC_SSM + BRANCH
_SRC_KVLAT = _SRC_QLAT + MLA_Q_LORA
_SRC_KPE = _SRC_KVLAT + MLA_KV_LORA
_SRC_SB = _SRC_KPE + MLA_ROPE
_SRC_GATE = _SRC_SB + 3 * BRANCH
_SRC_MERGE = _SRC_GATE + N_BRANCH * BRANCH
_SRC_LORA = 3 * BRANCH
_PW_BLK = 4 * LANES
_PW_NM = N_BRANCH * D_MODEL // _PW_BLK
_PW_NG = _PW_NM + N_BRANCH * BRANCH // _PW_BLK
_PW_ROPE = _PW_NG + 1
_PW_TAIL = {_PW_NG: (_SRC_QLAT, _SRC_QLAT + LANES, _SRC_QLAT + 2 * LANES, _SRC_LORA),
            _PW_ROPE: (_SRC_KVLAT, _SRC_KVLAT + LANES, _SRC_KPE, _SRC_KPE)}


def _permute_src_col(j, k):
    sb0 = _PW_ROPE + 1
    rw0 = sb0 + 3 * BRANCH // _PW_BLK
    ssm0 = rw0 + 3 * BRANCH // _PW_BLK
    u = MLA_ROPE
    col = jnp.where(j < _PW_NM, _SRC_MERGE // u + _PW_BLK // u * j,
                    jnp.where(j < _PW_NG, _SRC_GATE // u + _PW_BLK // u * (j - _PW_NM),
                              jnp.where(j < rw0, _SRC_SB // u + _PW_BLK // u * (j - sb0),
                                        jnp.where(j < ssm0, _PW_BLK // u * (j - rw0), _SRC_SSM // u))))
    col = col + LANES // u * k
    for jj, src in _PW_TAIL.items():
        col = jnp.where(j == jj, src[k] // u, col)
    return col * u


def _permute_kernel(x0_ref, x1_ref, x2_ref, x3_ref, o_ref):
    j = pl.program_id(1)
    wins = (x0_ref, x1_ref, x2_ref, x3_ref)
    half = MLA_ROPE // 2

    def put(k, rows):
        o_ref[0, :, LANES * k:LANES * (k + 1)] = rows.T.astype(BF16)

    @pl.when(j != _PW_ROPE)
    def _():
        for k in range(4):
            put(k, wins[k][0])

    @pl.when(j == _PW_ROPE)
    def _():
        for k in range(2):
            put(k, wins[k][0])
        kpe = wins[2][0]
        row = lax.broadcasted_iota(jnp.int32, kpe.shape, 0)
        put(2, jnp.where(row < MLA_ROPE, kpe, 0.0))
        zeros = jnp.zeros((LANES - MLA_ROPE, kpe.shape[1]), F32)
        put(3, jnp.concatenate([-kpe[half:MLA_ROPE], kpe[:half], zeros], axis=0))


def _permute_w_in(w_in):
    depth, d, _ = w_in.shape
    wt = jnp.swapaxes(w_in, 1, 2)
    win = lambda k: pl.BlockSpec((pl.Element(1), pl.Element(LANES), pl.Element(d)),
                                 lambda l, j, k=k: (l, _permute_src_col(j, k), 0))
    return pl.pallas_call(
        _permute_kernel,
        out_shape=jax.ShapeDtypeStruct((depth, d, PROJ_W), BF16),
        grid=(depth, PROJ_W // _PW_BLK),
        in_specs=[win(k) for k in range(4)],
        out_specs=pl.BlockSpec((1, d, _PW_BLK), lambda l, j: (l, 0, j)),
        compiler_params=_cparams(("parallel", "parallel"), 32),
        name="permute_w_in",
    )(wt, wt, wt, wt)


def _rope_tables(pos):
    half = MLA_ROPE // 2
    inv = ROPE_BASE ** (-jnp.arange(half, dtype=F32) / half)
    ang = pos.astype(F32)[:, None] * inv
    cos, sin = jnp.cos(ang), jnp.sin(ang)
    t = pos.shape[0]
    cc = jnp.concatenate([cos, cos], axis=1)
    ss = jnp.concatenate([sin, sin], axis=1)
    tq = jnp.concatenate([jnp.ones((t, MLA_NOPE), F32), cc, ss], axis=1) * (MLA_SCALE * LOG2E)
    pad = jnp.zeros((t, LANES - MLA_ROPE), F32)
    return tq, jnp.concatenate([cc, pad], axis=1), jnp.concatenate([ss, pad], axis=1)


def _mla_weights(w_q_up, w_kv_up):
    half = MLA_ROPE // 2
    wq = w_q_up.reshape(MLA_Q_LORA, HEADS, MLA_NOPE + MLA_ROPE)
    x1 = wq[:, :, MLA_NOPE:MLA_NOPE + half]
    x2 = wq[:, :, MLA_NOPE + half:]
    wq_p = jnp.concatenate([wq, -x2, x1], axis=2).reshape(MLA_Q_LORA, HEADS * LANES).astype(BF16)
    wkv = w_kv_up.reshape(MLA_KV_LORA, HEADS, 2 * HEAD)
    wk_p = jnp.concatenate([wkv[:, :, :HEAD], jnp.zeros((MLA_KV_LORA, HEADS, HEAD), F32)], axis=2)
    wk_p = wk_p.reshape(MLA_KV_LORA, HEADS * LANES).astype(BF16)
    wv_p = wkv[:, :, HEAD:].reshape(MLA_KV_LORA, BRANCH).astype(BF16)
    return wq_p, wk_p, wv_p


def _s5_tables(lam_re, lam_im, log_dt, b_re, b_im, c_re, c_im):
    dt = jnp.exp(log_dt)[:, None]
    mag = jnp.exp(lam_re * dt)
    ang = lam_im * dt
    lb_re, lb_im = mag * jnp.cos(ang), mag * jnp.sin(ang)
    nr, ni = lb_re - 1.0, lb_im
    den = lam_re * lam_re + lam_im * lam_im
    f_re = (nr * lam_re + ni * lam_im) / den
    f_im = (ni * lam_re - nr * lam_im) / den
    bb_re = f_re[..., None] * b_re - f_im[..., None] * b_im
    bb_im = f_re[..., None] * b_im + f_im[..., None] * b_re
    rin, cin = jnp.arange(BRANCH)[:, None] // SSM_GROUP, jnp.arange(SSM_W)[None, :] // SSM_STATE
    tile = lambda a: jnp.concatenate([a] * SSM_GROUPS, axis=1)
    blk_in = lambda m: jnp.where(rin == cin, tile(jnp.transpose(m, (0, 2, 1)).reshape(BRANCH, SSM_STATE)), 0.0)
    blk_out = lambda m: jnp.where((rin == cin).T, tile(jnp.transpose(m, (0, 2, 1)).reshape(SSM_W, SSM_GROUP)), 0.0)
    bblk = jnp.concatenate([blk_in(bb_re), blk_in(bb_im)], axis=1).astype(BF16)
    cblk = jnp.concatenate([blk_out(c_re), blk_out(-c_im)], axis=0).astype(BF16)

    def power(j):
        m = jnp.exp(lam_re * dt * j)
        return (m * jnp.cos(ang * j)).reshape(1, SSM_W), (m * jnp.sin(ang * j)).reshape(1, SSM_W)

    lam = jnp.concatenate([p for j in (1, 2, 4) for p in power(j)] + [jnp.zeros((2, SSM_W), F32)], axis=0)
    pws = [power(j) for j in range(1, 9)]
    pw = jnp.concatenate([p[0] for p in pws] + [p[1] for p in pws], axis=0)
    return bblk, cblk, lam, pw


def _layer(x, pos0, st, lw, final_g, final):
    b, t, d = x.shape
    n = b * t
    x2 = x.reshape(n, d)
    proj_b, proj = _norm_proj(x2, lw["norm_g"], lw["w_in_all"], lw["layer"])
    proj3 = proj.reshape(b, t, PF_W)
    past = st["past"]

    chunk = min(t, CHUNK)
    y_rw, wkv_new = _rwkv(proj3, st["sp_r"], st["sp_k"], st["sp_v"], st["sp_l"], st["wkv"],
                        lw["rw_pvec"], lw["rw_mu_l"], lw["rw_w2p"], lw["rw_a2p"], lw["g2"], chunk)
    shift_new = jnp.concatenate([proj3[:, t - 1:, O_RWR:O_RWR + 3 * BRANCH],
                                 proj3[:, t - 1:, O_LORA:O_LORA + 2 * RW_LORA]], axis=-1)

    y_ssm, hr, hi = _s5(proj3, st["ssm_re"], st["ssm_im"], lw["ssm_bblk"], lw["ssm_cblk"], lw["ssm_lam"],
                        lw["ssm_pw"], lw["ssm_d"], lw["ssm_wglu"], lw["ssm_bglu"])
    ssm_re_new = hr.reshape(b, SSM_GROUPS, SSM_STATE)
    ssm_im_new = hi.reshape(b, SSM_GROUPS, SSM_STATE)

    pos = pos0 + jnp.arange(t, dtype=jnp.int32)
    tq_tab, tc_tab, ts_tab = _rope_tables(pos)
    q, ckv, kr = _mla_prep(proj, t, tq_tab, tc_tab, ts_tab, lw["mla_gq"], lw["mla_gkv"], lw["mla_wq"])
    ckv3 = ckv.reshape(b, t, MLA_KV_LORA)
    kr3 = kr.reshape(b, t, MLA_ROPE)
    s_pad = -(-(past + t) // LANES) * LANES
    kc, vv = _mla_kv(ckv[None], kr[None], lw["mla_wk"], lw["mla_wv"], lw["dup"])
    kc3, vv3 = kc.reshape(b, t, HEADS * LANES), vv.reshape(b, t, BRANCH)
    kcp3 = vvp3 = None
    if past:
        kcp, vvp = _mla_kv(st["ckv_all"], st["kpe_all"], lw["mla_wk"], lw["mla_wv"], lw["dup"], lw["layer"])
        kcp3, vvp3 = kcp.reshape(b, past, HEADS * LANES), vvp.reshape(b, past, BRANCH)
    small = t <= 64
    assert small or not past
    tq = min(t, 256)
    hp = 4 if small else 1
    q3 = q.reshape(b, t, HEADS * LANES)
    if small:
        y_mla = _mla_attn(q3, kc3, vv3, kcp3, vvp3, s_pad, tq, s_pad, hp)
    else:
        y_mla = _mla_prompt(q3, kc3, vv3, tq)

    sbk = proj3[:, :, O_SBK:O_SBK + BRANCH]
    sbv = proj3[:, :, O_SBV:O_SBV + BRANCH]
    if past:
        y_sb = _sb_decode(proj3, st["sbk_t"], st["sbv_t"], lw["layer"], min(256, past))
    else:
        y_sb = _sb_attn(proj3, tq, min(256, t))

    ys = [y.reshape(n, BRANCH) for y in (y_rw, y_ssm, y_mla, y_sb)]
    x_new = _merge(ys, proj_b, x2, lw["w_branch"], lw["b_merge"], lw["w_out"], final_g, final,
                   lw["layer"]).reshape(b, t, d)
    new_state = (shift_new, wkv_new, ssm_re_new, ssm_im_new, ckv3, kr3,
                 sbk.reshape(b, t, HEADS, HEAD), sbv.reshape(b, t, HEADS, HEAD))
    return x_new, new_state


def kernel(x_prompt, x_sample, state_rwkv_shift, state_rwkv_wkv, state_ssm_re, state_ssm_im, cache_mla_ckv, cache_mla_kpe, cache_sb_k, cache_sb_v, norm_g, w_in, rw_mu, rw_w0, rw_w2, rw_a0, rw_a2, rw_k_k, rw_k_a, rw_r_k, rw_lnx_g, rw_lnx_b, ssm_lam_re, ssm_lam_im, ssm_log_dt, ssm_b_re, ssm_b_im, ssm_c_re, ssm_c_im, ssm_d, ssm_w_glu, ssm_b_glu, mla_q_norm, mla_w_q_up, mla_kv_norm, mla_w_kv_up, w_branch, b_merge, w_out, final_norm_g):
    depth = w_in.shape[0]
    bp, tp, _ = x_prompt.shape
    bs = x_sample.shape[0]
    past = cache_mla_ckv.shape[2]

    w_in_p = _permute_w_in(w_in)
    lane_i = jnp.arange(LANES)
    g2 = ((lane_i[:, None] // HEAD) == (lane_i[None, :] // HEAD)).astype(BF16)
    rope_i = jnp.arange(MLA_ROPE)
    dup = ((lane_i[None, :] == rope_i[:, None] + MLA_NOPE)
           | (lane_i[None, :] == rope_i[:, None] + MLA_NOPE + MLA_ROPE)).astype(BF16)
    final_g = final_norm_g.reshape(1, D_MODEL)
    zpad = jnp.zeros((RW_LORA, BRANCH), F32)

    layers = []
    for l in range(depth):
        wq_p, wk_p, wv_p = _mla_weights(mla_w_q_up[l], mla_w_kv_up[l])
        bblk, cblk, lam, pw = _s5_tables(ssm_lam_re[l], ssm_lam_im[l], ssm_log_dt[l], ssm_b_re[l], ssm_b_im[l],
                                         ssm_c_re[l], ssm_c_im[l])
        mu = rw_mu[l]
        rows = [mu[:BRANCH], mu[BRANCH:2 * BRANCH], mu[2 * BRANCH:3 * BRANCH], rw_w0[l], rw_a0[l], rw_k_k[l],
                rw_k_a[l], rw_r_k[l].reshape(BRANCH), rw_lnx_g[l], rw_lnx_b[l]]
        pvec = jnp.concatenate([jnp.stack(rows), jnp.zeros((16 - len(rows), BRANCH), F32)], axis=0)
        layers.append(dict(
            norm_g=norm_g[l].reshape(1, D_MODEL), w_in_all=w_in_p, layer=l,
            rw_pvec=pvec, rw_mu_l=mu[3 * BRANCH:].reshape(1, 2 * RW_LORA),
            rw_w2p=jnp.concatenate([rw_w2[l], zpad], axis=0).astype(BF16),
            rw_a2p=jnp.concatenate([zpad, rw_a2[l]], axis=0).astype(BF16),
            g2=g2, dup=dup,
            ssm_bblk=bblk, ssm_cblk=cblk, ssm_lam=lam, ssm_pw=pw,
            ssm_d=ssm_d[l].reshape(1, BRANCH), ssm_wglu=ssm_w_glu[l].astype(BF16),
            ssm_bglu=ssm_b_glu[l].reshape(1, BRANCH),
            mla_gq=mla_q_norm[l].reshape(1, MLA_Q_LORA), mla_gkv=mla_kv_norm[l].reshape(1, MLA_KV_LORA),
            mla_wq=wq_p, mla_wk=wk_p, mla_wv=wv_p,
            w_branch=w_branch, b_merge=b_merge[l].reshape(N_BRANCH, 1, D_MODEL), w_out=w_out))

    def fresh(bn):
        return dict(sp_r=jnp.zeros((bn, 1, BRANCH), F32), sp_k=jnp.zeros((bn, 1, BRANCH), F32),
                    sp_v=jnp.zeros((bn, 1, BRANCH), F32), sp_l=jnp.zeros((bn, 1, 2 * RW_LORA), F32),
                    wkv=jnp.zeros((bn, HEADS, HEAD, HEAD), F32),
                    ssm_re=jnp.zeros((bn, 1, SSM_W), F32), ssm_im=jnp.zeros((bn, 1, SSM_W), F32),
                    past=0)

    nl = cache_mla_ckv.shape[0]
    ckv_all = cache_mla_ckv.reshape(nl, bs * past, MLA_KV_LORA)
    kpe_all = cache_mla_kpe.reshape(nl, bs * past, MLA_ROPE)
    sbk_t = jnp.transpose(cache_sb_k, (0, 1, 3, 4, 2)).reshape(nl, bs, HEADS // 2, LANES, past)
    sbv_t = jnp.transpose(cache_sb_v, (0, 1, 3, 4, 2)).reshape(nl, bs, HEADS // 2, LANES, past)

    def carried(l):
        sh = state_rwkv_shift[l]
        return dict(sp_r=sh[:, :, :BRANCH], sp_k=sh[:, :, BRANCH:2 * BRANCH], sp_v=sh[:, :, 2 * BRANCH:3 * BRANCH],
                    sp_l=sh[:, :, 3 * BRANCH:], wkv=state_rwkv_wkv[l],
                    ssm_re=state_ssm_re[l].reshape(bs, 1, SSM_W), ssm_im=state_ssm_im[l].reshape(bs, 1, SSM_W),
                    past=past, ckv_all=ckv_all, kpe_all=kpe_all, sbk_t=sbk_t, sbv_t=sbv_t)

    xp, xs = x_prompt, x_sample
    new_p, new_s = [], []
    for l in range(depth):
        last = l == depth - 1
        xp, st_p = _layer(xp, 0, fresh(bp), layers[l], final_g, last)
        xs, st_s = _layer(xs, past, carried(l), layers[l], final_g, last)
        new_p.append(st_p)
        new_s.append(st_s)
    stk = lambda lst, i: jnp.stack([s[i] for s in lst], axis=0)
    return (xp, xs) + tuple(stk(new_p, i) for i in range(8)) + tuple(stk(new_s, i) for i in range(8))
```

```python
import functools
import math

import jax
import jax.numpy as jnp
from jax import lax
from jax.experimental import pallas as pl
from jax.experimental.pallas import tpu as pltpu

F32 = jnp.float32
BF16 = jnp.bfloat16

D_MODEL = 2048
BRANCH = 512
N_BRANCH = 4
EPS = 1e-6
CHUNK = 64
HEAD = 64
HEADS = 8
RW_LORA = 64
RW_IN = 3 * BRANCH + 2 * RW_LORA
RW_GN_EPS = 64e-5
SSM_GROUP = 16
SSM_GROUPS = 32
SSM_STATE = 64
SSM_W = SSM_GROUPS * SSM_STATE
MLA_NOPE = 64
MLA_ROPE = 32
MLA_Q_LORA = 384
MLA_KV_LORA = 256
MLA_SCALE = 1.0 / math.sqrt(MLA_NOPE + MLA_ROPE)
ROPE_BASE = 10000.0
SB_SCALE = 1.0 / math.sqrt(HEAD)
LOG2E = math.log2(math.e)
LANES = 128
MIB = 1024 * 1024

O_MERGE = 0
O_GATE = 8192
PB_W = 10240
O_QLAT = 0
O_LORA = 384
O_KVLAT = 512
O_KPE = 768
O_KPESW = 896
O_SBQ = 1024
O_SBK = 1536
O_SBV = 2048
O_RWR = 2560
O_RWK = 3072
O_RWV = 3584
O_SSM = 4096
PF_W = 4608
PROJ_W = PB_W + PF_W


def _cparams(sem, vmem_mib):
    return pltpu.CompilerParams(dimension_semantics=sem, vmem_limit_bytes=vmem_mib * MIB)


def _dot(a, b):
    return jnp.dot(a, b, preferred_element_type=F32)


def _dot_nt(a, b):
    return lax.dot_general(a, b, (((1,), (1,)), ((), ())), preferred_element_type=F32)


def _split3(x):
    h1 = x.astype(BF16)
    r1 = x - h1.astype(F32)
    h2 = r1.astype(BF16)
    h3 = (r1 - h2.astype(F32)).astype(BF16)
    return h1, h2, h3


def _softplus(x):
    return jnp.maximum(x, 0.0) + jnp.log1p(jnp.exp(-jnp.abs(x)))


def _sigmoid(x):
    return 0.5 * jnp.tanh(0.5 * x) + 0.5


def _norm_proj_kernel(x_ref, g_ref, w_ref, ob_ref, of_ref, h_ref, *, nb16):
    j = pl.program_id(1)

    @pl.when(j == 0)
    def _():
        x = x_ref[...]
        ms = jnp.mean(x * x, axis=-1, keepdims=True)
        h_ref[...] = (x * lax.rsqrt(ms + EPS) * g_ref[...]).astype(BF16)

    @pl.when(j < nb16)
    def _():
        ob_ref[...] = _dot(h_ref[...], w_ref[0]).astype(BF16)

    @pl.when(j >= nb16)
    def _():
        of_ref[...] = _dot(h_ref[...], w_ref[0])


def _norm_proj(x2, g, w_all, layer):
    n, d = x2.shape
    tm = min(n, 2048)
    tn = 512
    nb16 = PB_W // tn
    return pl.pallas_call(
        functools.partial(_norm_proj_kernel, nb16=nb16),
        out_shape=(jax.ShapeDtypeStruct((n, PB_W), BF16), jax.ShapeDtypeStruct((n, PF_W), F32)),
        grid=(n // tm, PROJ_W // tn),
        in_specs=[
            pl.BlockSpec((tm, d), lambda i, j: (i, 0), pipeline_mode=pl.Buffered(1)),
            pl.BlockSpec((1, d), lambda i, j: (0, 0)),
            pl.BlockSpec((1, d, tn), lambda i, j: (layer, 0, j)),
        ],
        out_specs=(pl.BlockSpec((tm, tn), lambda i, j: (i, jnp.minimum(j, nb16 - 1))),
                   pl.BlockSpec((tm, tn), lambda i, j: (i, jnp.maximum(j - nb16, 0)))),
        scratch_shapes=[pltpu.VMEM((tm, d), BF16)],
        compiler_params=_cparams(("parallel", "arbitrary"), 48),
        name="norm_proj",
    )(x2, g, w_all)


def _segsum(x, g2):
    outs = []
    for j in range(BRANCH // LANES):
        h1, h2, h3 = _split3(x[:, LANES * j:LANES * (j + 1)])
        outs.append(_dot(h1, g2) + _dot(h2, g2) + _dot(h3, g2))
    return jnp.concatenate(outs, axis=1)


def _rwkv_kernel(pr_ref, pk_ref, pv_ref, plo_ref, spr_ref, spk_ref, spv_ref, spl_ref, s0_ref,
                 pvec_ref, mul_ref, w2_ref, a2_ref, g2_ref,
                 y_ref, sout_ref,
                 st_ref, cr_ref, ck_ref, cv_ref, cl_ref, *, C, nb):
    c = pl.program_id(1)

    @pl.when(c == 0)
    def _():
        zero = jnp.zeros((HEAD, HEAD), F32)
        for bi in range(nb):
            for j in range(HEADS // 2):
                st_ref[bi, j] = jnp.concatenate(
                    [jnp.concatenate([s0_ref[bi, 2 * j], zero], axis=1),
                     jnp.concatenate([zero, s0_ref[bi, 2 * j + 1]], axis=1)], axis=0)
        cr_ref[...] = spr_ref[...]
        ck_ref[...] = spk_ref[...]
        cv_ref[...] = spv_ref[...]
        cl_ref[...] = spl_ref[...]

    pvec = pvec_ref[...]
    prow = lambda i: pvec[i:i + 1, :]
    g2 = g2_ref[...]

    def tshift(p_ref, carry_ref, mu):
        outs = []
        for bi in range(nb):
            p = p_ref[bi]
            rolled = pltpu.roll(p, 1, 0)
            rid = lax.broadcasted_iota(jnp.int32, p.shape, 0)
            prev = jnp.where(rid == 0, carry_ref[bi], rolled)
            carry_ref[bi] = p[C - 1:C, :]
            outs.append(p + (prev - p) * mu)
        return jnp.concatenate(outs, axis=0)

    r = tshift(pr_ref, cr_ref, prow(0))
    k = tshift(pk_ref, ck_ref, prow(1))
    v = tshift(pv_ref, cv_ref, prow(2))
    lo = tshift(plo_ref, cl_ref, mul_ref[...])

    log_w = -_softplus(-(prow(3) + _dot(jnp.tanh(lo).astype(BF16), w2_ref[...]))) - 0.5
    ld = -jnp.exp(log_w)
    a_icl = jax.nn.sigmoid(prow(4) + _dot(lo.astype(BF16), a2_ref[...]))
    kkr = k * prow(5)
    kk = kkr / jnp.maximum(jnp.sqrt(_segsum(kkr * kkr, g2)), 1e-12)
    k2 = k * (1.0 + (a_icl - 1.0) * prow(6))
    av = -kk
    bv = kk * a_icl

    sh = C.bit_length() - 1
    ri = lax.broadcasted_iota(jnp.int32, (nb * C, nb * C), 0)
    ci = lax.broadcasted_iota(jnp.int32, (nb * C, nb * C), 1)
    lincl = jnp.logical_and(ci <= ri, (ci >> sh) == (ri >> sh)).astype(BF16)
    h1, h2, h3 = _split3(ld)
    g = _dot(lincl, h1) + _dot(lincl, h2) + _dot(lincl, h3)
    g_lasts = [g[(bi + 1) * C - 1:(bi + 1) * C, :] for bi in range(nb)]
    g_last = jnp.concatenate([jnp.broadcast_to(gl, (C, BRANCH)) for gl in g_lasts], axis=0)
    eg = jnp.exp(g)
    eng = jnp.exp(-g)
    at = av * jnp.exp(g - ld)
    rt = r * eg
    bt = bv * eng
    kt = k2 * eng
    e_c = jnp.exp(g_last - g)
    b_end = bv * e_c
    k_end = k2 * e_c
    eg_lasts = [jnp.exp(gl) for gl in g_lasts]

    c2 = 2 * C
    npair = BRANCH // LANES
    r_i = lax.broadcasted_iota(jnp.int32, (c2, c2), 0)
    c_i = lax.broadcasted_iota(jnp.int32, (c2, c2), 1)
    c_m = jnp.where(c_i >= C, c_i - C, c_i)
    mhalf = c_m < jnp.where(r_i < C, r_i, r_i - C + 1)
    mfull = jnp.concatenate([mhalf, mhalf], axis=0)
    bd2 = (r_i >> sh) == (c_i >> sh)
    eye2 = (r_i == c_i).astype(F32)
    lane = lax.broadcasted_iota(jnp.int32, (1, LANES), 1)
    m0 = (lane < HEAD).astype(F32)
    m1 = (lane >= HEAD).astype(F32)
    v_i = lax.broadcasted_iota(jnp.int32, (LANES, LANES), 0)
    k_i = lax.broadcasted_iota(jnp.int32, (LANES, LANES), 1)
    bdmask = (v_i >> 6) == (k_i >> 6)
    zf = jnp.zeros((C, LANES), F32)
    zc = jnp.zeros((C, LANES), BF16)
    sls = [slice(LANES * j, LANES * (j + 1)) for j in range(npair)]

    units = [(bi, j) for bi in range(nb) for j in range(npair)]
    tile = lambda x, bi, j: x[bi * C:(bi + 1) * C, LANES * j:LANES * (j + 1)]
    ws, a_ss, s2s = [], [], []
    for bi, j in units:
        atj, rtj = tile(at, bi, j), tile(rt, bi, j)
        a0, r0, a1, r1 = atj * m0, rtj * m0, atj * m1, rtj * m1
        lhs_w = jnp.concatenate([jnp.concatenate([a0, zf], axis=1), jnp.concatenate([r0, zf], axis=1),
                                 jnp.concatenate([zf, a1], axis=1), jnp.concatenate([zf, r1], axis=1)],
                                axis=0).astype(BF16)
        btj, ktj = tile(bt, bi, j), tile(kt, bi, j)
        rhs_w = jnp.concatenate([jnp.concatenate([btj, ktj], axis=0),
                                 jnp.concatenate([ktj, btj], axis=0)], axis=1).astype(BF16)
        ws.append(jnp.where(mfull, _dot_nt(lhs_w, rhs_w), 0.0))
        lhs_s = jnp.concatenate([a0, r0, a1, r1], axis=0).astype(BF16)
        s2 = st_ref[bi, j]
        s2h, s2l, _ = _split3(s2)
        a_ss.append(_dot_nt(lhs_s, s2h) + _dot_nt(lhs_s, s2l))
        s2s.append(s2)

    n_pows = [jnp.where(bd2, jnp.concatenate([w[:C], w[c2:c2 + C]], axis=0), 0.0) for w in ws]
    t_mats = [eye2 + n for n in n_pows]
    lv = 2
    while lv < C:
        n_pows = [_dot(n.astype(BF16), n.astype(BF16)) for n in n_pows]
        t_mats = [t + _dot(t.astype(BF16), n.astype(BF16)) for t, n in zip(t_mats, n_pows)]
        lv *= 2

    wbs = [w.astype(BF16) for w in ws]
    v0s = [(tile(v, bi, j) * m0).astype(BF16) for bi, j in units]
    v1s = [(tile(v, bi, j) * m1).astype(BF16) for bi, j in units]
    xs = [jnp.concatenate([a_s[:C] + _dot(wb[:C], jnp.concatenate([zc, v0], axis=0)),
                           a_s[c2:c2 + C] + _dot(wb[c2:c2 + C], jnp.concatenate([v1, zc], axis=0))], axis=0)
          for a_s, wb, v0, v1 in zip(a_ss, wbs, v0s, v1s)]
    us = [_dot(t.astype(BF16), x.astype(BF16)) for t, x in zip(t_mats, xs)]
    ys = []
    for n, (bi, j) in enumerate(units):
        a_s, wb, u = a_ss[n], wbs[n], us[n]
        u0, u1 = u[:C], u[C:]
        y0 = a_s[C:c2] + _dot(wb[C:c2], jnp.concatenate([u0.astype(BF16), v0s[n]], axis=0))
        y1 = a_s[c2 + C:] + _dot(wb[c2 + C:], jnp.concatenate([v1s[n], u1.astype(BF16)], axis=0))
        ys.append(y0 + y1)
        uv = jnp.concatenate([u0 + u1, tile(v, bi, j)], axis=0)
        bk = jnp.concatenate([tile(b_end, bi, j), tile(k_end, bi, j)], axis=0).astype(BF16)
        upd = _dot(uv.T.astype(BF16), bk)
        st_ref[bi, j] = s2s[n] * eg_lasts[bi][:, sls[j]] + jnp.where(bdmask, upd, 0.0)
    y = jnp.concatenate([jnp.concatenate(ys[bi * npair:(bi + 1) * npair], axis=1) for bi in range(nb)],
                        axis=0)

    inv_n = 1.0 / HEAD
    yc = y - _segsum(y, g2) * inv_n
    yn = yc * lax.rsqrt(_segsum(yc * yc, g2) * inv_n + RW_GN_EPS) * prow(8) + prow(9)
    out = (yn + _segsum(r * k2 * prow(7), g2) * v).astype(y_ref.dtype)
    for bi in range(nb):
        y_ref[bi] = out[bi * C:(bi + 1) * C]

    @pl.when(c == pl.num_programs(1) - 1)
    def _():
        for bi in range(nb):
            for j in range(HEADS // 2):
                s2 = st_ref[bi, j]
                sout_ref[bi, 2 * j] = s2[:HEAD, :HEAD]
                sout_ref[bi, 2 * j + 1] = s2[HEAD:, HEAD:]


def _rwkv(proj3, sp_r, sp_k, sp_v, sp_l, s0, pvec, mu_l, w2p, a2p, g2, chunk):
    b, t, _ = proj3.shape
    nc = t // chunk
    nb = min(b, 4)
    cb = lambda off, wdt: off // wdt
    bspec = lambda off, wdt: pl.BlockSpec((nb, chunk, wdt), lambda i, c, o=cb(off, wdt): (i, c, o))
    row3 = lambda wdt: pl.BlockSpec((nb, 1, wdt), lambda i, c: (i, 0, 0))
    full2 = lambda a: pl.BlockSpec(a.shape, lambda i, c: (0, 0))
    return pl.pallas_call(
        functools.partial(_rwkv_kernel, C=chunk, nb=nb),
        out_shape=(jax.ShapeDtypeStruct((b, t, BRANCH), BF16),
                   jax.ShapeDtypeStruct((b, HEADS, HEAD, HEAD), F32)),
        grid=(b // nb, nc),
        in_specs=[
            bspec(O_RWR, BRANCH), bspec(O_RWK, BRANCH), bspec(O_RWV, BRANCH), bspec(O_LORA, LANES),
            row3(BRANCH), row3(BRANCH), row3(BRANCH), row3(LANES),
            pl.BlockSpec((nb, HEADS, HEAD, HEAD), lambda i, c: (i, 0, 0, 0)),
            full2(pvec), full2(mu_l), full2(w2p), full2(a2p), full2(g2),
        ],
        out_specs=(pl.BlockSpec((nb, chunk, BRANCH), lambda i, c: (i, c, 0)),
                   pl.BlockSpec((nb, HEADS, HEAD, HEAD), lambda i, c: (i, 0, 0, 0))),
        scratch_shapes=[pltpu.VMEM((nb, 4, LANES, LANES), F32),
                        pltpu.VMEM((nb, 1, BRANCH), F32), pltpu.VMEM((nb, 1, BRANCH), F32),
                        pltpu.VMEM((nb, 1, BRANCH), F32), pltpu.VMEM((nb, 1, LANES), F32)],
        compiler_params=_cparams(("parallel", "arbitrary"), 32),
        name="rwkv7",
    )(proj3, proj3, proj3, proj3, sp_r, sp_k, sp_v, sp_l, s0, pvec, mu_l, w2p, a2p, g2)


def _s5_kernel(u_ref, h0r_ref, h0i_ref, bblk_ref, cblk_ref, lam_ref, pw_ref, dsk_ref, wglu_ref, bglu_ref,
               y_ref, hro_ref, hio_ref, x_ref, cr_ref, ci_ref, *, tb, nb):
    @pl.when(pl.program_id(1) == 0)
    def _():
        cr_ref[...] = h0r_ref[...]
        ci_ref[...] = h0i_ref[...]

    u = jnp.concatenate([u_ref[bi] for bi in range(nb)], axis=0)
    x = _dot(u.astype(BF16), bblk_ref[...])
    xr = x[:, :SSM_W]
    xi = x[:, SSM_W:]
    for lvl, s in enumerate((1, 2, 4)):
        ar = lam_ref[2 * lvl]
        ai = lam_ref[2 * lvl + 1]
        sr = pltpu.roll(xr, s, 0)
        si = pltpu.roll(xi, s, 0)
        xr, xi = xr + (ar * sr - ai * si), xi + (ar * si + ai * sr)
    x_ref[:, :SSM_W] = xr
    x_ref[:, SSM_W:] = xi
    pwr = pw_ref[0:8, :]
    pwi = pw_ref[8:16, :]

    for b in range(nb):

        def body(gi, carry, b=b):
            cr, ci = carry
            o = pl.multiple_of(b * tb + gi * 8, 8)
            br = x_ref[pl.ds(o, 8), 0:SSM_W]
            bi = x_ref[pl.ds(o, 8), SSM_W:2 * SSM_W]
            br = br + pwr * cr - pwi * ci
            bi = bi + pwr * ci + pwi * cr
            x_ref[pl.ds(o, 8), 0:SSM_W] = br
            x_ref[pl.ds(o, 8), SSM_W:2 * SSM_W] = bi
            return br[7:8, :], bi[7:8, :]

        cr, ci = lax.fori_loop(0, tb // 8, body, (cr_ref[b], ci_ref[b]))
        cr_ref[b] = cr
        ci_ref[b] = ci
        hro_ref[b] = cr
        hio_ref[b] = ci
    y = _dot(x_ref[...].astype(BF16), cblk_ref[...]) + dsk_ref[...] * u
    g = jax.nn.gelu(y)
    out = (g * jax.nn.sigmoid(_dot(g.astype(BF16), wglu_ref[...]) + bglu_ref[...])).astype(y_ref.dtype)
    for b in range(nb):
        y_ref[b] = out[b * tb:(b + 1) * tb]


def _s5(proj3, h0r, h0i, bblk, cblk, lam, pw, dsk, wglu, bglu):
    b, t, _ = proj3.shape
    tb = min(t, 512)
    nb = b if t <= 64 else 1
    rows = nb * tb
    sub = jnp.arange(rows)[:, None] & 7
    lam = jnp.stack([jnp.where(sub >= s, lam[2 * lv + c:2 * lv + c + 1], 0.0)
                     for lv, s in enumerate((1, 2, 4)) for c in (0, 1)])
    full2 = lambda a: pl.BlockSpec(a.shape, lambda i, c: (0,) * a.ndim, pipeline_mode=pl.Buffered(1))
    row3 = pl.BlockSpec((nb, 1, SSM_W), lambda i, c: (i, 0, 0))
    return pl.pallas_call(
        functools.partial(_s5_kernel, tb=tb, nb=nb),
        out_shape=(jax.ShapeDtypeStruct((b, t, BRANCH), BF16),
                   jax.ShapeDtypeStruct((b, 1, SSM_W), F32),
                   jax.ShapeDtypeStruct((b, 1, SSM_W), F32)),
        grid=(b // nb, t // tb),
        in_specs=[pl.BlockSpec((nb, tb, BRANCH), lambda i, c: (i, c, O_SSM // BRANCH)),
                  row3, row3, full2(bblk), full2(cblk), full2(lam), full2(pw), full2(dsk),
                  full2(wglu), full2(bglu)],
        out_specs=(pl.BlockSpec((nb, tb, BRANCH), lambda i, c: (i, c, 0)), row3, row3),
        scratch_shapes=[pltpu.VMEM((rows, 2 * SSM_W), F32),
                        pltpu.VMEM((nb, 1, SSM_W), F32), pltpu.VMEM((nb, 1, SSM_W), F32)],
        compiler_params=_cparams(("parallel", "arbitrary"), 48),
        name="s5",
    )(proj3, h0r, h0i, bblk, cblk, lam, pw, dsk, wglu, bglu)


def _mla_prep_kernel(ql_ref, kvl_ref, kpe_ref, kpesw_ref, tq_ref, tc_ref, ts_ref,
                     gq_ref, gkv_ref, wq_ref, q_ref, ckv_ref, kr_ref):
    ql = ql_ref[...]
    qn = ql * lax.rsqrt(jnp.mean(ql * ql, axis=-1, keepdims=True) + EPS) * gq_ref[...]
    q = _dot(qn.astype(BF16), wq_ref[...])
    tq = tq_ref[...]
    q_ref[...] = jnp.concatenate(
        [q[:, LANES * h:LANES * (h + 1)] * tq for h in range(HEADS)], axis=1).astype(BF16)
    kvl = kvl_ref[...]
    ckv_ref[...] = kvl * lax.rsqrt(jnp.mean(kvl * kvl, axis=-1, keepdims=True) + EPS) * gkv_ref[...]
    kr = kpe_ref[...] * tc_ref[...] + kpesw_ref[...] * ts_ref[...]
    kr_ref[...] = kr[:, :MLA_ROPE]


def _mla_prep(proj, t, tq_tab, tc_tab, ts_tab, gq, gkv, wq):
    n = proj.shape[0]
    tm = min(t, 512)
    nt = t // tm
    tab = pl.BlockSpec((tm, LANES), lambda i: (i % nt, 0))
    full2 = lambda a: pl.BlockSpec(a.shape, lambda i: (0, 0))
    return pl.pallas_call(
        _mla_prep_kernel,
        out_shape=(jax.ShapeDtypeStruct((n, HEADS * LANES), BF16),
                   jax.ShapeDtypeStruct((n, MLA_KV_LORA), F32),
                   jax.ShapeDtypeStruct((n, MLA_ROPE), F32)),
        grid=(n // tm,),
        in_specs=[pl.BlockSpec((tm, MLA_Q_LORA), lambda i: (i, O_QLAT // MLA_Q_LORA)),
                  pl.BlockSpec((tm, MLA_KV_LORA), lambda i: (i, O_KVLAT // MLA_KV_LORA)),
                  pl.BlockSpec((tm, LANES), lambda i: (i, O_KPE // LANES)),
                  pl.BlockSpec((tm, LANES), lambda i: (i, O_KPESW // LANES)),
                  tab, tab, tab, full2(gq), full2(gkv), full2(wq)],
        out_specs=(pl.BlockSpec((tm, HEADS * LANES), lambda i: (i, 0)),
                   pl.BlockSpec((tm, MLA_KV_LORA), lambda i: (i, 0)),
                   pl.BlockSpec((tm, MLA_ROPE), lambda i: (i, 0))),
        compiler_params=_cparams(("parallel",), 32),
        name="mla_prep",
    )(proj, proj, proj, proj, tq_tab, tc_tab, ts_tab, gq, gkv, wq)


def _mla_kv_kernel(ckv_ref, kr_ref, wk_ref, wv_ref, dup_ref, k_ref, v_ref):
    cb = ckv_ref[0].astype(BF16)
    kn = _dot(cb, wk_ref[...])
    krd = _dot(kr_ref[0].astype(BF16), dup_ref[...])
    k_ref[...] = jnp.concatenate(
        [kn[:, LANES * h:LANES * (h + 1)] + krd for h in range(HEADS)], axis=1).astype(BF16)
    v_ref[...] = _dot(cb, wv_ref[...]).astype(BF16)


def _mla_kv(ckv, kr, wk, wv, dup, layer=0):
    n = ckv.shape[1]
    tm = next((c for c in (512, LANES) if n % c == 0), n)
    full2 = lambda a: pl.BlockSpec(a.shape, lambda i: (0, 0))
    return pl.pallas_call(
        _mla_kv_kernel,
        out_shape=(jax.ShapeDtypeStruct((n, HEADS * LANES), BF16),
                   jax.ShapeDtypeStruct((n, BRANCH), BF16)),
        grid=(n // tm,),
        in_specs=[pl.BlockSpec((1, tm, MLA_KV_LORA), lambda i: (layer, i, 0)),
                  pl.BlockSpec((1, tm, MLA_ROPE), lambda i: (layer, i, 0)),
                  full2(wk), full2(wv), full2(dup)],
        out_specs=(pl.BlockSpec((tm, HEADS * LANES), lambda i: (i, 0)),
                   pl.BlockSpec((tm, BRANCH), lambda i: (i, 0))),
        compiler_params=_cparams(("parallel",), 32),
        name="mla_kv",
    )(ckv, kr, wk, wv, dup)


def _assemble_keys(past_ref, new_ref, all_ref, past, t):
    all_ref[0:past, :] = past_ref[0].astype(BF16)
    all_ref[past:past + t, :] = new_ref[0].astype(BF16)
    pad = all_ref.shape[0] - past - t
    if pad:
        all_ref[past + t:, :] = jnp.zeros((pad, all_ref.shape[1]), BF16)


def _mla_attn_kernel(q_ref, k_ref, v_ref, *rest, tq, tk, q_off, s_valid, hp, past):
    if past:
        kp_ref, vp_ref, o_ref, kall_ref, vall_ref = rest
        _assemble_keys(kp_ref, k_ref, kall_ref, past, tq)
        _assemble_keys(vp_ref, v_ref, vall_ref, past, tq)
        kload = lambda o: kall_ref[pl.ds(o, tk), :]
        vload = lambda o: vall_ref[pl.ds(o, tk), :]
    else:
        (o_ref,) = rest
        kload = lambda o: k_ref[0, pl.ds(o, tk), :]
        vload = lambda o: v_ref[0, pl.ds(o, tk), :]
    i = pl.program_id(2)
    q_lo = q_off + i * tq
    limit = jnp.minimum(q_lo + tq, s_valid)
    nkb = (limit + tk - 1) // tk
    qidx = q_lo + lax.broadcasted_iota(jnp.int32, (tq, tk), 0)
    kloc = lax.broadcasted_iota(jnp.int32, (tq, tk), 1)
    nh = 2 * hp
    q2 = q_ref[0]
    qs = [q2[:, LANES * h:LANES * (h + 1)] for h in range(nh)]

    def body(kb, carry, masked):
        o = pl.multiple_of(kb * tk, tk)
        kblk = kload(o)
        vblk = vload(o)
        ss = [_dot_nt(qs[h], kblk[:, LANES * h:LANES * (h + 1)]) for h in range(nh)]
        if masked:
            kidx = kloc + kb * tk
            vis = jnp.logical_and((kidx >> 6) <= (qidx >> 6), kidx < s_valid)
            ss = [jnp.where(vis, s, -1e30) for s in ss]
        m_new = [jnp.maximum(carry[h][0], jnp.max(ss[h], axis=-1, keepdims=True)) for h in range(nh)]
        ps = [jnp.exp2(ss[h] - m_new[h]) for h in range(nh)]
        pv = [_dot(ps[h].astype(BF16), vblk[:, LANES * (h // 2):LANES * (h // 2 + 1)]) for h in range(nh)]
        new = []
        for h in range(nh):
            m_prev, l_prev, acc = carry[h]
            alpha = jnp.exp2(m_prev - m_new[h])
            new.append((m_new[h], alpha * l_prev + jnp.sum(ps[h], axis=-1, keepdims=True), alpha * acc + pv[h]))
        return tuple(new)

    init = tuple((jnp.full((tq, 1), -1e30, F32), jnp.zeros((tq, 1), F32), jnp.zeros((tq, LANES), F32))
                 for _ in range(nh))
    nfull = jnp.minimum(((q_lo >> 6) + 1) * CHUNK, s_valid) // tk
    res = lax.fori_loop(0, nfull, lambda kb, c: body(kb, c, False), init)
    res = lax.fori_loop(nfull, nkb, lambda kb, c: body(kb, c, True), res)
    lane = lax.broadcasted_iota(jnp.int32, (tq, LANES), 1)
    outs = [jnp.where(lane < HEAD, res[2 * p][2] / res[2 * p][1], res[2 * p + 1][2] / res[2 * p + 1][1])
            for p in range(hp)]
    o_ref[0] = jnp.concatenate(outs, axis=1).astype(o_ref.dtype)


def _mla_prompt_kernel(q_ref, k_ref, v_ref, o_ref, *, tq):
    i = pl.program_id(2)
    q_lo = i * tq
    nkb = i + 1
    w = 2 * tq
    cidx = lax.broadcasted_iota(jnp.int32, (tq, w), 1)
    qidx = q_lo + jnp.where(cidx >= tq, cidx - tq, cidx)
    kloc = lax.broadcasted_iota(jnp.int32, (tq, w), 0)
    q2 = q_ref[0]
    zq = jnp.zeros((tq, LANES), BF16)
    qbd = jnp.concatenate([jnp.concatenate([q2[:, :LANES], zq], axis=1),
                           jnp.concatenate([zq, q2[:, LANES:]], axis=1)], axis=0)

    def group(kb, carry, n):
        m_prev, l_prev, acc_t = carry
        o = pl.multiple_of((kb - (n - 1)) * tq, tq)
        kn = k_ref[0, pl.ds(o, n * tq), :]
        vn_t = v_ref[0, pl.ds(o, n * tq), :].astype(F32).T.astype(BF16)
        s = _dot_nt(kn, qbd)
        vis = ((kloc + kb * tq) >> 6) <= (qidx >> 6)
        s_r = jnp.where(vis, s[(n - 1) * tq:], -1e30)
        s = jnp.concatenate([s[:(n - 1) * tq], s_r], axis=0) if n > 1 else s_r
        m_new = jnp.maximum(m_prev, jnp.max(s, axis=0, keepdims=True))
        alpha = jnp.exp2(m_prev - m_new)
        p = jnp.exp2(s - m_new)
        l_new = alpha * l_prev + jnp.sum(p, axis=0, keepdims=True)
        return m_new, l_new, alpha * acc_t + _dot(vn_t, p.astype(BF16))

    carry = (jnp.full((1, w), -1e30, F32), jnp.zeros((1, w), F32), jnp.zeros((LANES, w), F32))
    n4, n2, n1 = nkb // 4, (nkb % 4) // 2, nkb % 2
    carry = lax.fori_loop(0, n4, lambda s, c: group(nkb - 1 - 4 * s, c, 4), carry)
    carry = lax.fori_loop(0, n2, lambda s, c: group(nkb - 1 - 4 * n4, c, 2), carry)
    carry = lax.fori_loop(0, n1, lambda s, c: group(0, c, 1), carry)
    out_t = carry[2] / carry[1]
    row = lax.broadcasted_iota(jnp.int32, (LANES, tq), 0)
    o_ref[0] = jnp.where(row < HEAD, out_t[:, :tq], out_t[:, tq:]).T.astype(o_ref.dtype)


def _mla_prompt(q3, k3, v3, tq):
    b, t, _ = q3.shape
    assert tq % CHUNK == 0 and t % tq == 0
    return pl.pallas_call(
        functools.partial(_mla_prompt_kernel, tq=tq),
        out_shape=jax.ShapeDtypeStruct((b, t, BRANCH), BF16),
        grid=(b, HEADS // 2, t // tq),
        in_specs=[pl.BlockSpec((1, tq, 2 * LANES), lambda bi, j, i: (bi, i, j)),
                  pl.BlockSpec((1, t, 2 * LANES), lambda bi, j, i: (bi, 0, j)),
                  pl.BlockSpec((1, t, LANES), lambda bi, j, i: (bi, 0, j))],
        out_specs=pl.BlockSpec((1, tq, LANES), lambda bi, j, i: (bi, i, j)),
        compiler_params=_cparams(("parallel", "parallel", "arbitrary"), 40),
        name="mla_prompt",
    )(q3, k3, v3)


def _mla_attn(q3, k3, v3, kp3, vp3, s_pad, tq, tk, hp):
    b, t, _ = q3.shape
    past = 0 if kp3 is None else kp3.shape[1]
    rows = k3.shape[1]
    in_specs = [pl.BlockSpec((1, tq, 2 * LANES * hp), lambda bi, j, i: (bi, i, j)),
                pl.BlockSpec((1, rows, 2 * LANES * hp), lambda bi, j, i: (bi, 0, j)),
                pl.BlockSpec((1, rows, LANES * hp), lambda bi, j, i: (bi, 0, j))]
    args = [q3, k3, v3]
    scratch = []
    if past:
        in_specs += [pl.BlockSpec((1, past, 2 * LANES * hp), lambda bi, j, i: (bi, 0, j)),
                     pl.BlockSpec((1, past, LANES * hp), lambda bi, j, i: (bi, 0, j))]
        args += [kp3, vp3]
        scratch = [pltpu.VMEM((s_pad, 2 * LANES * hp), BF16), pltpu.VMEM((s_pad, LANES * hp), BF16)]
    return pl.pallas_call(
        functools.partial(_mla_attn_kernel, tq=tq, tk=tk, q_off=past, s_valid=past + t, hp=hp, past=past),
        out_shape=jax.ShapeDtypeStruct((b, t, BRANCH), BF16),
        grid=(b, 4 // hp, t // tq),
        in_specs=in_specs,
        out_specs=pl.BlockSpec((1, tq, LANES * hp), lambda bi, j, i: (bi, i, j)),
        scratch_shapes=scratch,
        compiler_params=_cparams(("parallel", "parallel", "arbitrary"), 40),
        name="mla_attn",
    )(*args)


def _sb_block(zs, vis, us, carry, pv):
    nh = len(zs)
    tq = zs[0].shape[0]
    zls = [jnp.minimum(z, 0.0) - jnp.log(1.0 + jnp.exp(-jnp.abs(z))) for z in zs]
    lgs = [zl - z for zl, z in zip(zls, zs)]
    if vis is not None:
        lgs = [jnp.where(vis, lg, 0.0) for lg in lgs]
    his = [lg.astype(BF16) for lg in lgs]
    los = [(lg - hi.astype(F32)).astype(BF16) for lg, hi in zip(lgs, his)]
    later = _dot(jnp.concatenate(his + los, axis=0), us)
    new = []
    for h in range(nh):
        csum, acc = carry[h]
        tot = later[h * tq:(h + 1) * tq] + later[(nh + h) * tq:(nh + h + 1) * tq] + csum
        a = jnp.exp(zls[h] + tot)
        if vis is not None:
            a = jnp.where(vis, a, 0.0)
        new.append((csum + jnp.sum(lgs[h], axis=-1, keepdims=True), acc + pv(h, a.astype(BF16))))
    return tuple(new)


def _sb_heads(q2, nh, scale=SB_SCALE):
    lane1 = lax.broadcasted_iota(jnp.int32, (1, LANES), 1)
    hm = ((lane1 < HEAD).astype(F32) * scale, (lane1 >= HEAD).astype(F32) * scale)
    return [(q2[:, LANES * (h // 2):LANES * (h // 2 + 1)] * hm[h % 2]).astype(BF16) for h in range(nh)]


def _sb_finish(res, o_ref, tq):
    lane = lax.broadcasted_iota(jnp.int32, (tq, LANES), 1)
    outs = [jnp.where(lane < HEAD, res[2 * p][1], res[2 * p + 1][1]) for p in range(len(res) // 2)]
    o_ref[0] = jnp.concatenate(outs, axis=1).astype(o_ref.dtype)


def _sb_attn_kernel(q_ref, k_ref, v_ref, us_ref, o_ref, *, tq, tk):
    i = pl.program_id(2)
    q_lo = i * tq
    nkb = (q_lo + tq - 2) // tk + 1
    cidx = lax.broadcasted_iota(jnp.int32, (tk, 2 * tq), 1)
    qidx = q_lo + jnp.where(cidx >= tq, cidx - tq, cidx)
    kloc = lax.broadcasted_iota(jnp.int32, (tk, 2 * tq), 0)
    qs = jnp.concatenate(_sb_heads(q_ref[0], 2, SB_SCALE * LOG2E), axis=0)
    us = us_ref[...]

    w = 2 * tq

    def group(kb, carry, n):
        csum, acc_t = carry
        o = pl.multiple_of((kb - (n - 1)) * tk, tk)
        kn = k_ref[0, pl.ds(o, n * tk), :].astype(BF16)
        vn_t = v_ref[0, pl.ds(o, n * tk), :].T.astype(BF16)
        z = _dot_nt(kn, qs)
        zl = jnp.minimum(z, 0.0) - jnp.log2(1.0 + jnp.exp2(-jnp.abs(z)))
        lg = zl - z
        vis = (kloc + kb * tk) < qidx
        lgs = [lg[j * tk:(j + 1) * tk] for j in range(n - 1)] + [jnp.where(vis, lg[(n - 1) * tk:], 0.0)]
        his = [x.astype(BF16) for x in lgs]
        los = [(x - h.astype(F32)).astype(BF16) for x, h in zip(lgs, his)]
        later = _dot(us, jnp.concatenate([jnp.concatenate([his[j], los[j]], axis=0) for j in range(n)], axis=1))
        tots = [None] * n
        for j in reversed(range(n)):
            tots[j] = later[:, j * w:(j + 1) * w] + csum
            csum = csum + jnp.sum(lgs[j], axis=0, keepdims=True)
        a = jnp.exp2(zl + jnp.concatenate(tots, axis=0))
        a = jnp.concatenate([a[:(n - 1) * tk], jnp.where(vis, a[(n - 1) * tk:], 0.0)], axis=0) if n > 1 \
            else jnp.where(vis, a, 0.0)
        return csum, acc_t + _dot(vn_t, a.astype(BF16))

    carry = (jnp.zeros((1, 2 * tq), F32), jnp.zeros((LANES, 2 * tq), F32))
    n4, n2, n1 = nkb // 4, (nkb % 4) // 2, nkb % 2
    carry = lax.fori_loop(0, n4, lambda s, c: group(nkb - 1 - 4 * s, c, 4), carry)
    carry = lax.fori_loop(0, n2, lambda s, c: group(nkb - 1 - 4 * n4, c, 2), carry)
    carry = lax.fori_loop(0, n1, lambda s, c: group(0, c, 1), carry)
    acc_t = carry[1]
    row = lax.broadcasted_iota(jnp.int32, (LANES, tq), 0)
    o_ref[0] = jnp.where(row < HEAD, acc_t[:, :tq], acc_t[:, tq:]).T.astype(o_ref.dtype)


def _sb_decode_kernel(q_ref, k_ref, v_ref, kpt_ref, vpt_ref, usn_ref, usp_ref, o_ref, *, t, past, pb):
    qh = _sb_heads(q_ref[0], HEADS)
    kn = k_ref[0].astype(BF16)
    vn = v_ref[0].astype(BF16)
    pair = lambda x, h: x[:, LANES * (h // 2):LANES * (h // 2 + 1)]
    ri = lax.broadcasted_iota(jnp.int32, (t, t), 0)
    ci = lax.broadcasted_iota(jnp.int32, (t, t), 1)
    carry = tuple((jnp.zeros((t, 1), F32), jnp.zeros((t, LANES), F32)) for _ in range(HEADS))
    zs = [_dot_nt(qh[h], pair(kn, h)) for h in range(HEADS)]
    carry = _sb_block(zs, ci < ri, usn_ref[...], carry, lambda h, a: _dot(a, pair(vn, h)))
    usp = usp_ref[...]
    for blk in reversed(range(past // pb)):
        kts = [kpt_ref[0, 0, p, :, blk * pb:(blk + 1) * pb].astype(BF16) for p in range(HEADS // 2)]
        vts = [vpt_ref[0, 0, p, :, blk * pb:(blk + 1) * pb].astype(BF16) for p in range(HEADS // 2)]
        zs = [_dot(qh[h], kts[h // 2]) for h in range(HEADS)]
        carry = _sb_block(zs, None, usp, carry, lambda h, a: _dot_nt(a, vts[h // 2]))
    _sb_finish(carry, o_ref, t)


def _strict_upper(n):
    idx = jnp.arange(n)
    return (idx[:, None] > idx[None, :]).astype(BF16)


def _sb_attn(proj3, tq, tk):
    b, t, _ = proj3.shape
    assert tq == tk
    u = _strict_upper(tk).T
    us = jnp.concatenate([u, u], axis=1)
    return pl.pallas_call(
        functools.partial(_sb_attn_kernel, tq=tq, tk=tk),
        out_shape=jax.ShapeDtypeStruct((b, t, BRANCH), BF16),
        grid=(b, HEADS // 2, t // tq),
        in_specs=[pl.BlockSpec((1, tq, LANES), lambda bi, j, i: (bi, i, O_SBQ // LANES + j)),
                  pl.BlockSpec((1, t, LANES), lambda bi, j, i: (bi, 0, O_SBK // LANES + j)),
                  pl.BlockSpec((1, t, LANES), lambda bi, j, i: (bi, 0, O_SBV // LANES + j)),
                  pl.BlockSpec(us.shape, lambda bi, j, i: (0, 0))],
        out_specs=pl.BlockSpec((1, tq, LANES), lambda bi, j, i: (bi, i, j)),
        compiler_params=_cparams(("parallel", "parallel", "arbitrary"), 48),
        name="sb_attn",
    )(proj3, proj3, proj3, us)


def _sb_decode(proj3, kpt, vpt, layer, pb):
    b, t, _ = proj3.shape
    past = kpt.shape[4]
    usn, usp = _strict_upper(t), _strict_upper(pb)
    new = lambda off: pl.BlockSpec((1, t, BRANCH), lambda bi, o=off // BRANCH: (bi, 0, o))
    cache = pl.BlockSpec((1, 1, HEADS // 2, LANES, past), lambda bi: (layer, bi, 0, 0, 0))
    return pl.pallas_call(
        functools.partial(_sb_decode_kernel, t=t, past=past, pb=pb),
        out_shape=jax.ShapeDtypeStruct((b, t, BRANCH), BF16),
        grid=(b,),
        in_specs=[new(O_SBQ), new(O_SBK), new(O_SBV), cache, cache,
                  pl.BlockSpec(usn.shape, lambda bi: (0, 0)), pl.BlockSpec(usp.shape, lambda bi: (0, 0))],
        out_specs=pl.BlockSpec((1, t, BRANCH), lambda bi: (bi, 0, 0)),
        compiler_params=_cparams(("parallel",), 48),
        name="sb_decode",
    )(proj3, proj3, proj3, kpt, vpt, usn, usp)


def _mix_kernel(yrw_ref, yssm_ref, ymla_ref, ysb_ref, gate_ref, pm_ref, wb_ref, bm_ref, o_ref, acc_ref):
    n = pl.program_id(1)

    @pl.when(n == 0)
    def _():
        acc_ref[...] = jnp.zeros_like(acc_ref)

    def branch(y_ref):
        g = gate_ref[...].astype(F32)
        gated_half = y_ref[...].astype(F32) * (g * (0.5 * _sigmoid(g)))
        up_half = _dot(gated_half.astype(BF16), wb_ref[0, 0].astype(BF16))
        acc_ref[...] += (jnp.tanh(0.5 * (pm_ref[...].astype(F32) + bm_ref[0])) + 1.0) * up_half

    for idx, y_ref in enumerate((yrw_ref, yssm_ref, ymla_ref, ysb_ref)):
        pl.when(n == idx)(functools.partial(branch, y_ref))

    @pl.when(n == N_BRANCH - 1)
    def _():
        o_ref[...] = acc_ref[...].astype(BF16)


def _out_kernel(m_ref, x_ref, wo_ref, fg_ref, o_ref, *, final):
    xn = x_ref[...] + _dot(m_ref[...], wo_ref[0].astype(BF16))
    if final:
        xn = xn * lax.rsqrt(jnp.mean(xn * xn, axis=-1, keepdims=True) + EPS) * fg_ref[...]
    o_ref[...] = xn


def _merge(ys, proj, x2, wb, bm, wo, fg, final, layer):
    n, d = x2.shape
    tm = min(n, 1024)
    ysp = pl.BlockSpec((tm, BRANCH), lambda i, k: (i, 0))
    mixed = pl.pallas_call(
        _mix_kernel,
        out_shape=jax.ShapeDtypeStruct((n, d), BF16),
        grid=(n // tm, N_BRANCH),
        in_specs=[ysp, ysp, ysp, ysp,
                  pl.BlockSpec((tm, BRANCH), lambda i, k: (i, O_GATE // BRANCH + k)),
                  pl.BlockSpec((tm, d), lambda i, k: (i, O_MERGE // d + k)),
                  pl.BlockSpec((1, 1, BRANCH, d), lambda i, k: (layer, k, 0, 0)),
                  pl.BlockSpec((1, 1, d), lambda i, k: (k, 0, 0))],
        out_specs=pl.BlockSpec((tm, d), lambda i, k: (i, 0)),
        scratch_shapes=[pltpu.VMEM((tm, d), F32)],
        compiler_params=_cparams(("parallel", "arbitrary"), 56),
        name="branch_mix",
    )(*ys, proj, proj, wb, bm)
    to = min(n, 512)
    return pl.pallas_call(
        functools.partial(_out_kernel, final=final),
        out_shape=jax.ShapeDtypeStruct((n, d), F32),
        grid=(n // to,),
        in_specs=[pl.BlockSpec((to, d), lambda i: (i, 0)),
                  pl.BlockSpec((to, d), lambda i: (i, 0)),
                  pl.BlockSpec((1, d, d), lambda i: (layer, 0, 0), pipeline_mode=pl.Buffered(1)),
                  pl.BlockSpec((1, d), lambda i: (0, 0))],
        out_specs=pl.BlockSpec((to, d), lambda i: (i, 0)),
        compiler_params=_cparams(("parallel",), 48),
        name="out_proj",
    )(mixed, x2, wo, fg)


_SRC_SSM = RW_IN
_SRC_QLAT = _SRC_SSM + BRANCH
_SRC_KVLAT = _SRC_QLAT + MLA_Q_LORA
_SRC_KPE = _SRC_KVLAT + MLA_KV_LORA
_SRC_SB = _SRC_KPE + MLA_ROPE
_SRC_GATE = _SRC_SB + 3 * BRANCH
_SRC_MERGE = _SRC_GATE + N_BRANCH * BRANCH
_SRC_LORA = 3 * BRANCH
_PW_BLK = 4 * LANES
_PW_NM = N_BRANCH * D_MODEL // _PW_BLK
_PW_NG = _PW_NM + N_BRANCH * BRANCH // _PW_BLK
_PW_ROPE = _PW_NG + 1
_PW_TAIL = {_PW_NG: (_SRC_QLAT, _SRC_QLAT + LANES, _SRC_QLAT + 2 * LANES, _SRC_LORA),
            _PW_ROPE: (_SRC_KVLAT, _SRC_KVLAT + LANES, _SRC_KPE, _SRC_KPE)}


def _permute_src_col(j, k):
    sb0 = _PW_ROPE + 1
    rw0 = sb0 + 3 * BRANCH // _PW_BLK
    ssm0 = rw0 + 3 * BRANCH // _PW_BLK
    u = MLA_ROPE
    col = jnp.where(j < _PW_NM, _SRC_MERGE // u + _PW_BLK // u * j,
                    jnp.where(j < _PW_NG, _SRC_GATE // u + _PW_BLK // u * (j - _PW_NM),
                              jnp.where(j < rw0, _SRC_SB // u + _PW_BLK // u * (j - sb0),
                                        jnp.where(j < ssm0, _PW_BLK // u * (j - rw0), _SRC_SSM // u))))
    col = col + LANES // u * k
    for jj, src in _PW_TAIL.items():
        col = jnp.where(j == jj, src[k] // u, col)
    return col * u


def _permute_kernel(x0_ref, x1_ref, x2_ref, x3_ref, o_ref):
    j = pl.program_id(1)
    wins = (x0_ref, x1_ref, x2_ref, x3_ref)
    half = MLA_ROPE // 2

    def put(k, rows):
        o_ref[0, :, LANES * k:LANES * (k + 1)] = rows.T.astype(BF16)

    @pl.when(j != _PW_ROPE)
    def _():
        for k in range(4):
            put(k, wins[k][0])

    @pl.when(j == _PW_ROPE)
    def _():
        for k in range(2):
            put(k, wins[k][0])
        kpe = wins[2][0]
        row = lax.broadcasted_iota(jnp.int32, kpe.shape, 0)
        put(2, jnp.where(row < MLA_ROPE, kpe, 0.0))
        zeros = jnp.zeros((LANES - MLA_ROPE, kpe.shape[1]), F32)
        put(3, jnp.concatenate([-kpe[half:MLA_ROPE], kpe[:half], zeros], axis=0))


def _permute_w_in(w_in):
    depth, d, _ = w_in.shape
    wt = jnp.swapaxes(w_in, 1, 2)
    win = lambda k: pl.BlockSpec((pl.Element(1), pl.Element(LANES), pl.Element(d)),
                                 lambda l, j, k=k: (l, _permute_src_col(j, k), 0))
    return pl.pallas_call(
        _permute_kernel,
        out_shape=jax.ShapeDtypeStruct((depth, d, PROJ_W), BF16),
        grid=(depth, PROJ_W // _PW_BLK),
        in_specs=[win(k) for k in range(4)],
        out_specs=pl.BlockSpec((1, d, _PW_BLK), lambda l, j: (l, 0, j)),
        compiler_params=_cparams(("parallel", "parallel"), 32),
        name="permute_w_in",
    )(wt, wt, wt, wt)


def _rope_tables(pos):
    half = MLA_ROPE // 2
    inv = ROPE_BASE ** (-jnp.arange(half, dtype=F32) / half)
    ang = pos.astype(F32)[:, None] * inv
    cos, sin = jnp.cos(ang), jnp.sin(ang)
    t = pos.shape[0]
    cc = jnp.concatenate([cos, cos], axis=1)
    ss = jnp.concatenate([sin, sin], axis=1)
    tq = jnp.concatenate([jnp.ones((t, MLA_NOPE), F32), cc, ss], axis=1) * (MLA_SCALE * LOG2E)
    pad = jnp.zeros((t, LANES - MLA_ROPE), F32)
    return tq, jnp.concatenate([cc, pad], axis=1), jnp.concatenate([ss, pad], axis=1)


def _mla_weights(w_q_up, w_kv_up):
    half = MLA_ROPE // 2
    wq = w_q_up.reshape(MLA_Q_LORA, HEADS, MLA_NOPE + MLA_ROPE)
    x1 = wq[:, :, MLA_NOPE:MLA_NOPE + half]
    x2 = wq[:, :, MLA_NOPE + half:]
    wq_p = jnp.concatenate([wq, -x2, x1], axis=2).reshape(MLA_Q_LORA, HEADS * LANES).astype(BF16)
    wkv = w_kv_up.reshape(MLA_KV_LORA, HEADS, 2 * HEAD)
    wk_p = jnp.concatenate([wkv[:, :, :HEAD], jnp.zeros((MLA_KV_LORA, HEADS, HEAD), F32)], axis=2)
    wk_p = wk_p.reshape(MLA_KV_LORA, HEADS * LANES).astype(BF16)
    wv_p = wkv[:, :, HEAD:].reshape(MLA_KV_LORA, BRANCH).astype(BF16)
    return wq_p, wk_p, wv_p


def _s5_tables(lam_re, lam_im, log_dt, b_re, b_im, c_re, c_im):
    dt = jnp.exp(log_dt)[:, None]
    mag = jnp.exp(lam_re * dt)
    ang = lam_im * dt
    lb_re, lb_im = mag * jnp.cos(ang), mag * jnp.sin(ang)
    nr, ni = lb_re - 1.0, lb_im
    den = lam_re * lam_re + lam_im * lam_im
    f_re = (nr * lam_re + ni * lam_im) / den
    f_im = (ni * lam_re - nr * lam_im) / den
    bb_re = f_re[..., None] * b_re - f_im[..., None] * b_im
    bb_im = f_re[..., None] * b_im + f_im[..., None] * b_re
    eye = jnp.eye(SSM_GROUPS, dtype=F32)
    blk_in = lambda m: jnp.einsum('gpc,gh->gchp', m, eye).reshape(BRANCH, SSM_W)
    blk_out = lambda m: jnp.einsum('gcp,gh->gphc', m, eye).reshape(SSM_W, BRANCH)
    bblk = jnp.concatenate([blk_in(bb_re), blk_in(bb_im)], axis=1).astype(BF16)
    cblk = jnp.concatenate([blk_out(c_re), blk_out(-c_im)], axis=0).astype(BF16)

    def power(j):
        m = jnp.exp(lam_re * dt * j)
        return (m * jnp.cos(ang * j)).reshape(1, SSM_W), (m * jnp.sin(ang * j)).reshape(1, SSM_W)

    lam = jnp.concatenate([p for j in (1, 2, 4) for p in power(j)] + [jnp.zeros((2, SSM_W), F32)], axis=0)
    pws = [power(j) for j in range(1, 9)]
    pw = jnp.concatenate([p[0] for p in pws] + [p[1] for p in pws], axis=0)
    return bblk, cblk, lam, pw


def _layer(x, pos0, st, lw, final_g, final):
    b, t, d = x.shape
    n = b * t
    x2 = x.reshape(n, d)
    proj_b, proj = _norm_proj(x2, lw["norm_g"], lw["w_in_all"], lw["layer"])
    proj3 = proj.reshape(b, t, PF_W)
    past = st["past"]

    chunk = min(t, CHUNK)
    y_rw, wkv_new = _rwkv(proj3, st["sp_r"], st["sp_k"], st["sp_v"], st["sp_l"], st["wkv"],
                        lw["rw_pvec"], lw["rw_mu_l"], lw["rw_w2p"], lw["rw_a2p"], lw["g2"], chunk)
    shift_new = jnp.concatenate([proj3[:, t - 1:, O_RWR:O_RWR + 3 * BRANCH],
                                 proj3[:, t - 1:, O_LORA:O_LORA + 2 * RW_LORA]], axis=-1)

    y_ssm, hr, hi = _s5(proj3, st["ssm_re"], st["ssm_im"], lw["ssm_bblk"], lw["ssm_cblk"], lw["ssm_lam"],
                        lw["ssm_pw"], lw["ssm_d"], lw["ssm_wglu"], lw["ssm_bglu"])
    ssm_re_new = hr.reshape(b, SSM_GROUPS, SSM_STATE)
    ssm_im_new = hi.reshape(b, SSM_GROUPS, SSM_STATE)

    pos = pos0 + jnp.arange(t, dtype=jnp.int32)
    tq_tab, tc_tab, ts_tab = _rope_tables(pos)
    q, ckv, kr = _mla_prep(proj, t, tq_tab, tc_tab, ts_tab, lw["mla_gq"], lw["mla_gkv"], lw["mla_wq"])
    ckv3 = ckv.reshape(b, t, MLA_KV_LORA)
    kr3 = kr.reshape(b, t, MLA_ROPE)
    s_pad = -(-(past + t) // LANES) * LANES
    kc, vv = _mla_kv(ckv[None], kr[None], lw["mla_wk"], lw["mla_wv"], lw["dup"])
    kc3, vv3 = kc.reshape(b, t, HEADS * LANES), vv.reshape(b, t, BRANCH)
    kcp3 = vvp3 = None
    if past:
        kcp, vvp = _mla_kv(st["ckv_all"], st["kpe_all"], lw["mla_wk"], lw["mla_wv"], lw["dup"], lw["layer"])
        kcp3, vvp3 = kcp.reshape(b, past, HEADS * LANES), vvp.reshape(b, past, BRANCH)
    small = t <= 64
    assert small or not past
    tq = min(t, 256)
    hp = 4 if small else 1
    q3 = q.reshape(b, t, HEADS * LANES)
    if small:
        y_mla = _mla_attn(q3, kc3, vv3, kcp3, vvp3, s_pad, tq, s_pad, hp)
    else:
        y_mla = _mla_prompt(q3, kc3, vv3, tq)

    sbk = proj3[:, :, O_SBK:O_SBK + BRANCH]
    sbv = proj3[:, :, O_SBV:O_SBV + BRANCH]
    if past:
        y_sb = _sb_decode(proj3, st["sbk_t"], st["sbv_t"], lw["layer"], min(256, past))
    else:
        y_sb = _sb_attn(proj3, tq, min(256, t))

    ys = [y.reshape(n, BRANCH) for y in (y_rw, y_ssm, y_mla, y_sb)]
    x_new = _merge(ys, proj_b, x2, lw["w_branch"], lw["b_merge"], lw["w_out"], final_g, final,
                   lw["layer"]).reshape(b, t, d)
    new_state = (shift_new, wkv_new, ssm_re_new, ssm_im_new, ckv3, kr3,
                 sbk.reshape(b, t, HEADS, HEAD), sbv.reshape(b, t, HEADS, HEAD))
    return x_new, new_state


def kernel(x_prompt, x_sample, state_rwkv_shift, state_rwkv_wkv, state_ssm_re, state_ssm_im, cache_mla_ckv, cache_mla_kpe, cache_sb_k, cache_sb_v, norm_g, w_in, rw_mu, rw_w0, rw_w2, rw_a0, rw_a2, rw_k_k, rw_k_a, rw_r_k, rw_lnx_g, rw_lnx_b, ssm_lam_re, ssm_lam_im, ssm_log_dt, ssm_b_re, ssm_b_im, ssm_c_re, ssm_c_im, ssm_d, ssm_w_glu, ssm_b_glu, mla_q_norm, mla_w_q_up, mla_kv_norm, mla_w_kv_up, w_branch, b_merge, w_out, final_norm_g):
    depth = w_in.shape[0]
    bp, tp, _ = x_prompt.shape
    bs = x_sample.shape[0]
    past = cache_mla_ckv.shape[2]

    w_in_p = _permute_w_in(w_in)
    lane_i = jnp.arange(LANES)
    g2 = ((lane_i[:, None] // HEAD) == (lane_i[None, :] // HEAD)).astype(BF16)
    rope_i = jnp.arange(MLA_ROPE)
    dup = ((lane_i[None, :] == rope_i[:, None] + MLA_NOPE)
           | (lane_i[None, :] == rope_i[:, None] + MLA_NOPE + MLA_ROPE)).astype(BF16)
    final_g = final_norm_g.reshape(1, D_MODEL)
    zpad = jnp.zeros((RW_LORA, BRANCH), F32)

    layers = []
    for l in range(depth):
        wq_p, wk_p, wv_p = _mla_weights(mla_w_q_up[l], mla_w_kv_up[l])
        bblk, cblk, lam, pw = _s5_tables(ssm_lam_re[l], ssm_lam_im[l], ssm_log_dt[l], ssm_b_re[l], ssm_b_im[l],
                                         ssm_c_re[l], ssm_c_im[l])
        mu = rw_mu[l]
        rows = [mu[:BRANCH], mu[BRANCH:2 * BRANCH], mu[2 * BRANCH:3 * BRANCH], rw_w0[l], rw_a0[l], rw_k_k[l],
                rw_k_a[l], rw_r_k[l].reshape(BRANCH), rw_lnx_g[l], rw_lnx_b[l]]
        pvec = jnp.concatenate([jnp.stack(rows), jnp.zeros((16 - len(rows), BRANCH), F32)], axis=0)
        layers.append(dict(
            norm_g=norm_g[l].reshape(1, D_MODEL), w_in_all=w_in_p, layer=l,
            rw_pvec=pvec, rw_mu_l=mu[3 * BRANCH:].reshape(1, 2 * RW_LORA),
            rw_w2p=jnp.concatenate([rw_w2[l], zpad], axis=0).astype(BF16),
            rw_a2p=jnp.concatenate([zpad, rw_a2[l]], axis=0).astype(BF16),
            g2=g2, dup=dup,
            ssm_bblk=bblk, ssm_cblk=cblk, ssm_lam=lam, ssm_pw=pw,
            ssm_d=ssm_d[l].reshape(1, BRANCH), ssm_wglu=ssm_w_glu[l].astype(BF16),
            ssm_bglu=ssm_b_glu[l].reshape(1, BRANCH),
            mla_gq=mla_q_norm[l].reshape(1, MLA_Q_LORA), mla_gkv=mla_kv_norm[l].reshape(1, MLA_KV_LORA),
            mla_wq=wq_p, mla_wk=wk_p, mla_wv=wv_p,
            w_branch=w_branch, b_merge=b_merge[l].reshape(N_BRANCH, 1, D_MODEL), w_out=w_out))

    def fresh(bn):
        return dict(sp_r=jnp.zeros((bn, 1, BRANCH), F32), sp_k=jnp.zeros((bn, 1, BRANCH), F32),
                    sp_v=jnp.zeros((bn, 1, BRANCH), F32), sp_l=jnp.zeros((bn, 1, 2 * RW_LORA), F32),
                    wkv=jnp.zeros((bn, HEADS, HEAD, HEAD), F32),
                    ssm_re=jnp.zeros((bn, 1, SSM_W), F32), ssm_im=jnp.zeros((bn, 1, SSM_W), F32),
                    past=0)

    nl = cache_mla_ckv.shape[0]
    ckv_all = cache_mla_ckv.reshape(nl, bs * past, MLA_KV_LORA)
    kpe_all = cache_mla_kpe.reshape(nl, bs * past, MLA_ROPE)
    sbk_t = jnp.transpose(cache_sb_k, (0, 1, 3, 4, 2)).reshape(nl, bs, HEADS // 2, LANES, past)
    sbv_t = jnp.transpose(cache_sb_v, (0, 1, 3, 4, 2)).reshape(nl, bs, HEADS // 2, LANES, past)

    def carried(l):
        sh = state_rwkv_shift[l]
        return dict(sp_r=sh[:, :, :BRANCH], sp_k=sh[:, :, BRANCH:2 * BRANCH], sp_v=sh[:, :, 2 * BRANCH:3 * BRANCH],
                    sp_l=sh[:, :, 3 * BRANCH:], wkv=state_rwkv_wkv[l],
                    ssm_re=state_ssm_re[l].reshape(bs, 1, SSM_W), ssm_im=state_ssm_im[l].reshape(bs, 1, SSM_W),
                    past=past, ckv_all=ckv_all, kpe_all=kpe_all, sbk_t=sbk_t, sbv_t=sbv_t)

    xp, xs = x_prompt, x_sample
    new_p, new_s = [], []
    for l in range(depth):
        last = l == depth - 1
        xp, st_p = _layer(xp, 0, fresh(bp), layers[l], final_g, last)
        xs, st_s = _layer(xs, past, carried(l), layers[l], final_g, last)
        new_p.append(st_p)
        new_s.append(st_s)
    stk = lambda lst, i: jnp.stack([s[i] for s in lst], axis=0)
    return (xp, xs) + tuple(stk(new_p, i) for i in range(8)) + tuple(stk(new_s, i) for i in range(8))
```
